```python
import math
import jax
import jax.numpy as jnp
from jax import lax
import numpy as np

D_MODEL = 1024
BATCH = 1
SEQ = 16384
DEPTH = 4

HEAD_DIM = 64
GLA_HEADS = 4
GLA_DK = 64
GLA_DV = 128
GLA_RANK = 16
GLA_TAU = 16.0
GLA_CHUNK = 64
DIL_PATTERNS = ((128, 1), (512, 4), (2048, 16))
DIL_GROUPS = 3
DIL_HEADS = 4
DIFF_HEADS = 4
DIFF_DV = 2 * HEAD_DIM
DIFF_QBLOCK = 128
HY_WIDTH = 512
HY_ORDER = 2
HY_EMB = 33
HY_FFN = 64
HY_SHORT = 3
HY_DECAY_TARGET = 1e-2
HY_FAST_DECAY = 0.3
HY_SLOW_DECAY = 1.5
T5_BUCKETS = 32
T5_MAX_DIST = 1024
N_BIAS_HEADS = DIL_GROUPS * DIL_HEADS + DIFF_HEADS
D_FF = 4 * D_MODEL
N_BRANCH = 4
RMS_EPS = 1e-6

GLA_QK = GLA_HEADS * GLA_DK
GLA_VW = GLA_HEADS * GLA_DV
DIL_W = DIL_GROUPS * DIL_HEADS * HEAD_DIM
DIL_OUT = DIL_HEADS * HEAD_DIM
DIFF_QK = DIFF_HEADS * 2 * HEAD_DIM
DIFF_VW = DIFF_HEADS * DIFF_DV
HY_PROJ = (HY_ORDER + 1) * HY_WIDTH
SPLIT_SIZES = (GLA_QK, GLA_QK, GLA_VW, GLA_VW, 2 * GLA_RANK, DIL_W, DIL_W, DIL_W, DIFF_QK, DIFF_QK, DIFF_VW, HY_PROJ, N_BRANCH * D_MODEL)
N_IN = sum(SPLIT_SIZES)

kernel_name = 'hybrid_gla_dilated_diff_hyena_encoder'


def _split_points():
    pts, acc = [], 0
    for n in SPLIT_SIZES[:-1]:
        acc += n
        pts.append(acc)
    return pts


def rmsnorm(x, g):
    xf = x.astype(jnp.float32)
    y = xf * lax.rsqrt(jnp.mean(xf * xf, axis=-1, keepdims=True) + RMS_EPS)
    return y.astype(x.dtype) * g


def t5_bucket(rel):
    half = T5_BUCKETS // 2
    max_exact = half // 2
    ret = jnp.where(rel > 0, half, 0)
    n = jnp.abs(rel)
    nf = jnp.maximum(n, 1).astype(jnp.float32)
    large = max_exact + (jnp.log(nf / max_exact) / math.log(T5_MAX_DIST / max_exact) * (half - max_exact)).astype(jnp.int32)
    large = jnp.minimum(large, half - 1)
    return ret + jnp.where(n < max_exact, n, large)


def gla_chunked(q, k, v, g, include_diag):
    B, S, H, dk = q.shape
    dv = v.shape[-1]
    C = GLA_CHUNK
    N = S // C
    f32 = jnp.float32
    q, k, v, g = [t.astype(f32).reshape(B, N, C, H, t.shape[-1]) for t in (q, k, v, g)]
    b = jnp.cumsum(g, axis=2)
    b_last = b[:, :, -1:]
    qg = q * jnp.exp(b)
    kg = k * jnp.exp(-b)
    A = jnp.einsum('bnthk,bnshk->bnhts', qg, kg)
    mask = jnp.tril(jnp.ones((C, C), dtype=bool), 0 if include_diag else -1)
    A = jnp.where(mask, A, 0.0)
    o = jnp.einsum('bnhts,bnshv->bnthv', A, v)
    dS = jnp.einsum('bnshk,bnshv->nbhkv', k * jnp.exp(b_last - b), v)
    decay = jnp.exp(b_last[:, :, 0]).transpose(1, 0, 2, 3)

    def step(state, inp):
        ds, a = inp
        return a[..., None] * state + ds, state

    _, s_prev = lax.scan(step, jnp.zeros((B, H, dk, dv), f32), (dS, decay))
    o = o + jnp.einsum('bnthk,nbhkv->bnthv', qg, s_prev)
    return o.reshape(B, S, H, dv)


def gla_mixer(q, k, v, r, lr, gate_w, gate_b, norm_g):
    B, S, _ = q.shape
    dtype = q.dtype
    q = q.reshape(B, S, GLA_HEADS, GLA_DK) * (GLA_DK ** -0.5)
    k = k.reshape(B, S, GLA_HEADS, GLA_DK)
    v = v.reshape(B, S, GLA_HEADS, GLA_DV)
    lr = lr.reshape(B, S, 2, GLA_RANK)
    logits = jnp.einsum('bsjr,jrk->bsjk', lr, gate_w) + gate_b
    g = (jax.nn.log_sigmoid(logits.astype(jnp.float32)) / GLA_TAU).reshape(B, S, 2, GLA_HEADS, GLA_DK)
    flip = lambda t: jnp.flip(t, axis=1)
    fwd = gla_chunked(q, k, v, g[:, :, 0], True)
    bwd = flip(gla_chunked(flip(q), flip(k), flip(v), flip(g[:, :, 1]), False))
    o = (fwd + bwd).astype(dtype)
    o = rmsnorm(o, norm_g) * jax.nn.silu(r.reshape(B, S, GLA_HEADS, GLA_DV))
    return o.reshape(B, S, GLA_VW)


def dilated_group(q, k, v, dil, half, bias_g):
    B, S, H, dh = q.shape
    W = half
    M = S // dil
    nb = -(-M // W)
    Mp = nb * W
    f32 = jnp.float32

    def to_sub(t):
        t = t.reshape(B, M, dil, H, dh).transpose(0, 2, 1, 3, 4)
        return jnp.pad(t, ((0, 0), (0, 0), (0, Mp - M), (0, 0), (0, 0)))

    def windows(t):
        t = jnp.pad(to_sub(t), ((0, 0), (0, 0), (W, W), (0, 0), (0, 0))).reshape(B, dil, nb + 2, W, H, dh)
        return jnp.concatenate([t[:, :, :-2], t[:, :, 1:-1], t[:, :, 2:]], axis=3)

    qs = to_sub(q).reshape(B, dil, nb, W, H, dh)
    kw, vw = windows(k), windows(v)
    s = jnp.einsum('brnqhe,brnkhe->brnhqk', qs, kw).astype(f32) * (dh ** -0.5)
    a = jnp.arange(W)[:, None]
    c = jnp.arange(3 * W)[None, :]
    delta = c - W - a
    bias = jnp.transpose(bias_g[t5_bucket(delta * dil)], (2, 0, 1)).astype(f32)
    mq = (jnp.arange(nb) * W)[:, None, None] + a[None]
    mk = mq + delta[None]
    valid = (jnp.abs(delta) <= W)[None] & (mk >= 0) & (mk < M)
    s = jnp.where(valid[:, None], s + bias, -1e30)
    lse = jax.nn.logsumexp(s, axis=-1)
    p = jnp.exp(s - lse[..., None])
    o = jnp.einsum('brnhqk,brnkhe->brnqhe', p, vw.astype(f32))
    o = o.reshape(B, dil, Mp, H, dh)[:, :, :M].transpose(0, 2, 1, 3, 4).reshape(B, S, H, dh)
    lse = lse.transpose(0, 1, 2, 4, 3).reshape(B, dil, Mp, H)[:, :, :M].transpose(0, 2, 1, 3).reshape(B, S, H)
    return o, lse


def dilated_mixer(q, k, v, qn_g, kn_g, bias_table):
    B, S, _ = q.shape
    dtype = q.dtype
    shape = (B, S, DIL_GROUPS, DIL_HEADS, HEAD_DIM)
    q = rmsnorm(q.reshape(shape), qn_g[:, None])
    k = rmsnorm(k.reshape(shape), kn_g[:, None])
    v = v.reshape(shape)
    outs, lses = [], []
    for gi, (win, dil) in enumerate(DIL_PATTERNS):
        o, l = dilated_group(q[:, :, gi], k[:, :, gi], v[:, :, gi], dil, win // (2 * dil),
                             bias_table[:, gi * DIL_HEADS:(gi + 1) * DIL_HEADS])
        outs.append(o)
        lses.append(l)
    w = jax.nn.softmax(jnp.stack(lses), axis=0)
    o = jnp.sum(w[..., None] * jnp.stack(outs), axis=0)
    return o.reshape(B, S, DIL_OUT).astype(dtype)


def diff_mixer(q, k, v, qn_g, kn_g, lam_p, subln_g, bias_c, lam_init):
    B, S, _ = q.shape
    dtype = q.dtype
    H, dh = DIFF_HEADS, HEAD_DIM
    f32 = jnp.float32
    q = rmsnorm(q.reshape(B, S, H, 2, dh), qn_g)
    k = rmsnorm(k.reshape(B, S, H, 2, dh), kn_g)
    vf = v.reshape(B, S, H, DIFF_DV).astype(f32)
    lp = lam_p.astype(f32)
    lam = jnp.exp(jnp.sum(lp[0] * lp[1])) - jnp.exp(jnp.sum(lp[2] * lp[3])) + lam_init
    nq = S // DIFF_QBLOCK
    qb = q.reshape(B, nq, DIFF_QBLOCK, H, 2, dh).transpose(1, 0, 2, 3, 4, 5)
    kpos = jnp.arange(S)
    scale = dh ** -0.5

    def block(args):
        qi, i = args
        s = jnp.einsum('bqhce,bkhce->bhcqk', qi, k).astype(f32) * scale
        qpos = i * DIFF_QBLOCK + jnp.arange(DIFF_QBLOCK)
        bias = bias_c[t5_bucket(kpos[None, :] - qpos[:, None])]
        s = s + jnp.transpose(bias, (2, 0, 1))[None, :, None].astype(f32)
        p = jax.nn.softmax(s, axis=-1)
        att = p[:, :, 0] - lam * p[:, :, 1]
        return jnp.einsum('bhqk,bkhe->bqhe', att, vf)

    o = lax.map(block, (qb, jnp.arange(nq)))
    o = o.transpose(1, 0, 2, 3, 4).reshape(B, S, H, DIFF_DV)
    o = rmsnorm(o, subln_g) * (1.0 - lam_init)
    return o.reshape(B, S, DIFF_VW).astype(dtype)


def hyena_filters(L, w1, b1, fr1, w2, b2, fr2, w3):
    f32 = jnp.float32
    t = jnp.linspace(0.0, 1.0, L, dtype=f32)[:, None]
    bands = (HY_EMB - 1) // 2
    freqs = jnp.linspace(1e-4, bands - 1, bands, dtype=f32)[None]
    w = 2.0 * math.pi * jnp.arange(L, dtype=f32)[:, None] / L
    z = jnp.concatenate([t, jnp.cos(freqs * w), -jnp.sin(freqs * w)], axis=-1)
    h = jnp.sin(fr1 * (z @ w1 + b1))
    h = jnp.sin(fr2 * (h @ w2 + b2))
    h = (h @ w3).astype(f32).reshape(L, 2, HY_WIDTH)
    min_decay = math.log(HY_DECAY_TARGET) / HY_SLOW_DECAY
    max_decay = math.log(HY_DECAY_TARGET) / HY_FAST_DECAY
    deltas = jnp.abs(jnp.linspace(min_decay, max_decay, HY_WIDTH, dtype=f32))
    h = h * jnp.exp(-t * deltas)[:, None]
    return h[:, 0], h[:, 1]


def hyena_mixer(u, conv_w, conv_b, w1, b1, fr1, w2, b2, fr2, w3, skip):
    B, L, _ = u.shape
    f32 = jnp.float32
    pad = HY_SHORT // 2
    up = jnp.pad(u, ((0, 0), (pad, pad), (0, 0)))
    u = conv_b + sum(up[:, j:j + L] * conv_w[j] for j in range(HY_SHORT))
    x0, x1, v = jnp.split(u, HY_ORDER + 1, axis=-1)
    z = (x1 * v).astype(f32)
    hf, hb = hyena_filters(L, w1, b1, fr1, w2, b2, fr2, w3)
    kern = jnp.concatenate([hf, jnp.zeros((1, HY_WIDTH), f32), jnp.flip(hb[1:], axis=0)], axis=0)
    kern = kern / jnp.sum(jnp.abs(kern), axis=0, keepdims=True)
    n = 2 * L
    y = jnp.fft.irfft(jnp.fft.rfft(z, n=n, axis=1) * jnp.fft.rfft(kern, n=n, axis=0)[None], n=n, axis=1)[:, :L]
    y = y + z * skip.astype(f32)
    return x0 * y.astype(x0.dtype)


def setup_inputs(seed: int = 0) -> dict:
    key = jax.random.key(seed)
    ks = iter(jax.random.split(key, 48))

    def nrm(shape, scale):
        return jax.random.normal(next(ks), shape, jnp.float32) * scale

    def gain(shape):
        return 1.0 + nrm(shape, 0.02)

    L = DEPTH
    return {
        'x': nrm((BATCH, SEQ, D_MODEL), 1.0),
        't5_bias': nrm((T5_BUCKETS, N_BIAS_HEADS), 0.2),
        'norm1_g': gain((L, D_MODEL)),
        'w_in': nrm((L, D_MODEL, N_IN), D_MODEL ** -0.5),
        'gla_gate_w': nrm((L, 2, GLA_RANK, GLA_QK), GLA_RANK ** -0.5),
        'gla_gate_b': nrm((L, 2, GLA_QK), 0.1),
        'gla_norm_g': gain((L, GLA_DV)),
        'dil_qnorm_g': gain((L, DIL_GROUPS, HEAD_DIM)),
        'dil_knorm_g': gain((L, DIL_GROUPS, HEAD_DIM)),
        'diff_qnorm_g': gain((L, HEAD_DIM)),
        'diff_knorm_g': gain((L, HEAD_DIM)),
        'diff_lambda': nrm((L, 4, HEAD_DIM), 0.1),
        'diff_subln_g': gain((L, DIFF_DV)),
        'hy_conv_w': nrm((L, HY_SHORT, HY_PROJ), HY_SHORT ** -0.5),
        'hy_conv_b': nrm((L, HY_PROJ), 0.02),
        'hy_w1': nrm((L, HY_EMB, HY_FFN), HY_EMB ** -0.5),
        'hy_b1': nrm((L, HY_FFN), 0.1),
        'hy_freq1': gain((L, HY_FFN)),
        'hy_w2': nrm((L, HY_FFN, HY_FFN), HY_FFN ** -0.5),
        'hy_b2': nrm((L, HY_FFN), 0.1),
        'hy_freq2': gain((L, HY_FFN)),
        'hy_w3': nrm((L, HY_FFN, 2 * HY_WIDTH), HY_FFN ** -0.5),
        'hy_skip': nrm((L, HY_WIDTH), 0.5),
        'proj_a': nrm((L, GLA_VW, D_MODEL), GLA_VW ** -0.5),
        'proj_b': nrm((L, DIL_OUT, D_MODEL), DIL_OUT ** -0.5),
        'proj_c': nrm((L, DIFF_VW, D_MODEL), DIFF_VW ** -0.5),
        'proj_d': nrm((L, HY_WIDTH, D_MODEL), HY_WIDTH ** -0.5),
        'w_out': nrm((L, D_MODEL, D_MODEL), D_MODEL ** -0.5),
        'norm2_g': gain((L, D_MODEL)),
        'mlp_w1': nrm((L, D_MODEL, D_FF), D_MODEL ** -0.5),
        'mlp_w2': nrm((L, D_FF, D_MODEL), D_FF ** -0.5),
    }


def reference(x, t5_bias, norm1_g, w_in, gla_gate_w, gla_gate_b, gla_norm_g, dil_qnorm_g, dil_knorm_g,
              diff_qnorm_g, diff_knorm_g, diff_lambda, diff_subln_g, hy_conv_w, hy_conv_b, hy_w1, hy_b1,
              hy_freq1, hy_w2, hy_b2, hy_freq2, hy_w3, hy_skip, proj_a, proj_b, proj_c, proj_d, w_out,
              norm2_g, mlp_w1, mlp_w2):
    B, S, _ = x.shape
    pts = _split_points()
    n_dil_bias = DIL_GROUPS * DIL_HEADS
    for i in range(DEPTH):
        h = rmsnorm(x, norm1_g[i])
        z = h @ w_in[i]
        (a_q, a_k, a_v, a_r, a_lr, b_q, b_k, b_v, c_q, c_k, c_v, d_u, gates) = jnp.split(z, pts, axis=-1)
        ya = gla_mixer(a_q, a_k, a_v, a_r, a_lr, gla_gate_w[i], gla_gate_b[i], gla_norm_g[i])
        yb = dilated_mixer(b_q, b_k, b_v, dil_qnorm_g[i], dil_knorm_g[i], t5_bias[:, :n_dil_bias])
        lam_init = 0.8 - 0.6 * math.exp(-0.3 * i)
        yc = diff_mixer(c_q, c_k, c_v, diff_qnorm_g[i], diff_knorm_g[i], diff_lambda[i], diff_subln_g[i],
                        t5_bias[:, n_dil_bias:], lam_init)
        yd = hyena_mixer(d_u, hy_conv_w[i], hy_conv_b[i], hy_w1[i], hy_b1[i], hy_freq1[i], hy_w2[i],
                         hy_b2[i], hy_freq2[i], hy_w3[i], hy_skip[i])
        gt = jax.nn.sigmoid(gates.reshape(B, S, N_BRANCH, D_MODEL))
        m = (gt[:, :, 0] * (ya @ proj_a[i]) + gt[:, :, 1] * (yb @ proj_b[i])
             + gt[:, :, 2] * (yc @ proj_c[i]) + gt[:, :, 3] * (yd @ proj_d[i]))
        x = x + m @ w_out[i]
        h2 = rmsnorm(x, norm2_g[i])
        x = x + jnp.square(jax.nn.relu(h2 @ mlp_w1[i])) @ mlp_w2[i]
    return x
```

```python
import functools
import math

import jax
import jax.numpy as jnp
import numpy as np
from jax import lax
from jax.experimental import pallas as pl
from jax.experimental.pallas import tpu as pltpu

F32 = jnp.float32
BF16 = jnp.bfloat16

D_MODEL = 1024
HEAD_DIM = 64
GLA_HEADS = 4
GLA_DK = 64
GLA_DV = 128
GLA_RANK = 16
GLA_TAU = 16.0
GLA_CHUNK = 64
DIL_PATTERNS = ((128, 1), (512, 4), (2048, 16))
DIL_GROUPS = 3
DIL_HEADS = 4
DIL_HALF = 64
DIFF_HEADS = 4
DIFF_DV = 128
HY_WIDTH = 512
HY_EMB = 33
HY_FFN = 64
HY_DECAY_TARGET = 1e-2
HY_FAST_DECAY = 0.3
HY_SLOW_DECAY = 1.5
T5_BUCKETS = 32
T5_MAX_DIST = 1024
N_BIAS_HEADS = 16
D_FF = 4096
RMS_EPS = 1e-6

LANES = 128
VMEM_LIMIT = 48 * 1024 * 1024

Z_DU = 0
Z_BQ, Z_BK, Z_BV = 1536, 2304, 3072
Z_AQ = 3840
Z_CQ, Z_CK, Z_CV = 4096, 4608, 5120
Z_AV, Z_AR = 5632, 6144
Z_AK = 6656
Z_ALR = 6912
Z_GATE = 7168
Z_COLS = 11264

_O_AQ, _O_AK, _O_AV, _O_AR, _O_ALR, _O_B, _O_C, _O_DU, _O_GATE, _O_END = (
    0, 256, 512, 1024, 1536, 1568, 3872, 5408, 6944, 11040)

DFT_N2 = 128


def _cparams(sem):
    return pltpu.CompilerParams(dimension_semantics=sem, vmem_limit_bytes=VMEM_LIMIT)


def _dot(a, b):
    return jnp.dot(a, b, preferred_element_type=F32)


def _dot_nt(a, b):
    return lax.dot_general(a, b, (((1,), (1,)), ((), ())), preferred_element_type=F32)


def _dot_tn(a, b):
    return lax.dot_general(a, b, (((0,), (0,)), ((), ())), preferred_element_type=F32)


def _split(x):
    hi = x.astype(BF16)
    lo = (x - hi.astype(F32)).astype(BF16)
    return hi, lo


def _dot3(a, b):
    ah, al = _split(a)
    bh, bl = _split(b)
    return _dot(ah, bh) + _dot(ah, bl) + _dot(al, bh)


def _rms(x):
    return x * lax.rsqrt(jnp.mean(x * x, axis=-1, keepdims=True) + RMS_EPS)


def _sigmoid(x):
    return 1.0 / (1.0 + jnp.exp(-x))


def _norm_matmul_kernel(x_ref, g_ref, w_ref, o_ref, h_ref):
    @pl.when(pl.program_id(1) == 0)
    def _():
        h_ref[...] = (_rms(x_ref[...]) * g_ref[...]).astype(BF16)

    o_ref[...] = _dot(h_ref[...], w_ref[...])


def _norm_matmul(x, g, w, tm, tn):
    s, d = x.shape
    n = w.shape[1]
    return pl.pallas_call(
        _norm_matmul_kernel,
        grid=(s // tm, n // tn),
        in_specs=[pl.BlockSpec((tm, d), lambda i, j: (i, 0)),
                  pl.BlockSpec((1, d), lambda i, j: (0, 0)),
                  pl.BlockSpec((d, tn), lambda i, j: (0, j))],
        out_specs=pl.BlockSpec((tm, tn), lambda i, j: (i, j)),
        out_shape=jax.ShapeDtypeStruct((s, n), F32),
        scratch_shapes=[pltpu.VMEM((tm, d), BF16)],
        compiler_params=_cparams(("parallel", "arbitrary")),
        name="in_proj",
    )(x, g.reshape(1, d), w)


def _group_norm(x, e, gain):
    hi, lo = _split(x * x)
    ms = (_dot(hi, e) + _dot(lo, e)) * (1.0 / HEAD_DIM)
    return x * lax.rsqrt(ms + RMS_EPS) * gain


def _prep_kernel(bq_ref, bk_ref, bv_ref, cq_ref, ck_ref, cv_ref, eb_ref, ec_ref,
                 gbq_ref, gbk_ref, gcq_ref, gck_ref,
                 obq_ref, obk_ref, obv_ref, ocq_ref, ock_ref, ocv_ref):
    scale = HEAD_DIM ** -0.5
    eb = eb_ref[...]
    ec = ec_ref[...]
    obq_ref[...] = (_group_norm(bq_ref[...], eb, gbq_ref[...]) * scale).astype(BF16)
    obk_ref[...] = _group_norm(bk_ref[...], eb, gbk_ref[...]).astype(BF16)
    obv_ref[...] = bv_ref[...].astype(BF16)
    ocq_ref[...] = (_group_norm(cq_ref[...], ec, gcq_ref[...]) * scale).astype(BF16)
    ock_ref[...] = _group_norm(ck_ref[...], ec, gck_ref[...]).astype(BF16)
    cv = cv_ref[...].astype(BF16)
    ones = jnp.ones((cv.shape[0], DIFF_DV), BF16)
    ocv_ref[...] = jnp.concatenate(
        [t for h in range(DIFF_HEADS) for t in (cv[:, h * DIFF_DV:(h + 1) * DIFF_DV], ones)], axis=1)


def _block_diag_ones(width):
    idx = np.arange(width) // HEAD_DIM
    return jnp.asarray(idx[:, None] == idx[None, :], dtype=BF16)


def _prep(z, gbq, gbk, gcq, gck, tm):
    s = z.shape[0]
    wb, wc = DIL_GROUPS * DIL_HEADS * HEAD_DIM, DIFF_HEADS * 2 * HEAD_DIM
    zspec = lambda w, off: pl.BlockSpec((tm, w), lambda i: (i, off // w))
    cspec = lambda r, c: pl.BlockSpec((r, c), lambda i: (0, 0))
    ospec = lambda w: pl.BlockSpec((tm, w), lambda i: (i, 0))
    return pl.pallas_call(
        _prep_kernel,
        grid=(s // tm,),
        in_specs=[zspec(wb, Z_BQ), zspec(wb, Z_BK), zspec(wb, Z_BV),
                  zspec(wc, Z_CQ), zspec(wc, Z_CK), zspec(wc, Z_CV),
                  cspec(wb, wb), cspec(wc, wc),
                  cspec(1, wb), cspec(1, wb), cspec(1, wc), cspec(1, wc)],
        out_specs=[ospec(wb), ospec(wb), ospec(wb), ospec(wc), ospec(wc), ospec(2 * wc)],
        out_shape=[jax.ShapeDtypeStruct((s, wb), BF16)] * 3
        + [jax.ShapeDtypeStruct((s, wc), BF16)] * 2 + [jax.ShapeDtypeStruct((s, 2 * wc), BF16)],
        compiler_params=_cparams(("parallel",)),
        name="qk_prep",
    )(z, z, z, z, z, z, _block_diag_ones(wb), _block_diag_ones(wc),
      gbq.reshape(1, wb), gbk.reshape(1, wb), gcq.reshape(1, wc), gck.reshape(1, wc))


def _t5_bucket(rel):
    half = T5_BUCKETS // 2
    max_exact = half // 2
    ret = jnp.where(rel > 0, half, 0)
    n = jnp.abs(rel)
    nf = jnp.maximum(n, 1).astype(F32)
    large = max_exact + (jnp.log(nf / max_exact) / math.log(T5_MAX_DIST / max_exact)
                         * (half - max_exact)).astype(jnp.int32)
    large = jnp.minimum(large, half - 1)
    return ret + jnp.where(n < max_exact, n, large)


def _bias_kernel(tab_ref, idx_ref, o_ref, *, head_base, heads_per_tile_group):
    t = pl.program_id(0)
    h = pl.program_id(1)
    col = head_base + (t // heads_per_tile_group[0]) * heads_per_tile_group[1] + h
    idx = idx_ref[0]
    acc = jnp.zeros(idx.shape, F32)
    for b in range(T5_BUCKETS):
        acc = jnp.where(idx == b, tab_ref[b, col], acc)
    o_ref[0, 0] = acc


def _bias_tiles(t5_bias, idx, n_heads, head_base, tiles_per_group, heads_step):
    nt, r, c = idx.shape
    return pl.pallas_call(
        functools.partial(_bias_kernel, head_base=head_base,
                          heads_per_tile_group=(tiles_per_group, heads_step)),
        grid=(nt, n_heads),
        in_specs=[pl.BlockSpec(memory_space=pltpu.SMEM),
                  pl.BlockSpec((1, r, c), lambda t, h: (t, 0, 0))],
        out_specs=pl.BlockSpec((1, 1, r, c), lambda t, h: (t, h, 0, 0)),
        out_shape=jax.ShapeDtypeStruct((nt, n_heads, r, c), F32),
        compiler_params=_cparams(("parallel", "parallel")),
        name="t5_bias_tiles",
    )(t5_bias, idx)


def _diff_attn_kernel(q_ref, k_ref, v_ref, bias_ref, lam_ref, g_ref, o_ref,
                      qa_ref, qb_ref, m_ref, acc_ref, *, lam_init):
    j = pl.program_id(2)

    @pl.when(j == 0)
    def _():
        q = q_ref[...]
        lane = lax.broadcasted_iota(jnp.int32, q.shape, 1)
        qa_ref[...] = jnp.where(lane < HEAD_DIM, q, jnp.zeros_like(q))
        qb_ref[...] = jnp.where(lane >= HEAD_DIM, q, jnp.zeros_like(q))
        m_ref[...] = jnp.full(m_ref.shape, -jnp.inf, F32)
        acc_ref[...] = jnp.zeros(acc_ref.shape, F32)

    k = k_ref[...]
    v = v_ref[...]
    bias = bias_ref[0, 0]
    for c, qr in enumerate((qa_ref, qb_ref)):
        s = _dot_nt(qr[...], k) + bias
        m_old = m_ref[c]
        m_new = jnp.maximum(m_old, jnp.max(s, axis=-1, keepdims=True))
        alpha = jnp.exp(m_old - m_new)
        p = jnp.exp(s - m_new[:, :1])
        acc_ref[c] = alpha[:, :1] * acc_ref[c] + _dot(p.astype(BF16), v)
        m_ref[c] = m_new

    @pl.when(j == pl.num_programs(2) - 1)
    def _():
        lp = lam_ref[...]
        lam = (jnp.exp(jnp.sum(lp[0:1] * lp[1:2], axis=-1, keepdims=True))
               - jnp.exp(jnp.sum(lp[2:3] * lp[3:4], axis=-1, keepdims=True)) + lam_init)
        a0 = acc_ref[0]
        a1 = acc_ref[1]
        o0 = a0[:, :DIFF_DV] / a0[:, DIFF_DV:DIFF_DV + 1]
        o1 = a1[:, :DIFF_DV] / a1[:, DIFF_DV:DIFF_DV + 1]
        att = o0 - lam * o1
        o_ref[...] = _rms(att) * g_ref[...] * (1.0 - lam_init)


def _diff_band(t):
    return -(-(T5_MAX_DIST - 1) // t)


def _diff_attn(cq, ck, cv_aug, bias_tiles, lam_p, subln_g, lam_init, t):
    s = cq.shape[0]
    nb = _diff_band(t) + 1
    w = 2 * HEAD_DIM
    return pl.pallas_call(
        functools.partial(_diff_attn_kernel, lam_init=lam_init),
        grid=(DIFF_HEADS, s // t, s // t),
        in_specs=[pl.BlockSpec((t, w), lambda h, i, j: (i, h)),
                  pl.BlockSpec((t, w), lambda h, i, j: (j, h)),
                  pl.BlockSpec((t, 2 * DIFF_DV), lambda h, i, j: (j, h)),
                  pl.BlockSpec((1, 1, t, t), lambda h, i, j: (jnp.clip(j - i, -nb, nb) + nb, h, 0, 0)),
                  pl.BlockSpec((4, HEAD_DIM), lambda h, i, j: (0, 0)),
                  pl.BlockSpec((1, DIFF_DV), lambda h, i, j: (0, 0))],
        out_specs=pl.BlockSpec((t, DIFF_DV), lambda h, i, j: (i, h)),
        out_shape=jax.ShapeDtypeStruct((s, DIFF_HEADS * DIFF_DV), F32),
        scratch_shapes=[pltpu.VMEM((t, w), BF16), pltpu.VMEM((t, w), BF16),
                        pltpu.VMEM((2, t, LANES), F32), pltpu.VMEM((2, t, 2 * DIFF_DV), F32)],
        compiler_params=_cparams(("parallel", "parallel", "arbitrary")),
        name="diff_attn",
    )(cq, ck, cv_aug, bias_tiles, lam_p, subln_g.reshape(1, DIFF_DV))


def _dil_kernel(q_ref, kp_ref, kc_ref, kn_ref, vp_ref, vc_ref, vn_ref, bias_ref, o_ref, lse_ref,
                *, tq, m_len):
    n = pl.program_id(1)
    q = q_ref[...]
    k = jnp.concatenate([kp_ref[...], kc_ref[...], kn_ref[...]], axis=0)
    v = jnp.concatenate([vp_ref[...], vc_ref[...], vn_ref[...]], axis=0)
    tk = tq + 2 * DIL_HALF
    a = lax.broadcasted_iota(jnp.int32, (tq, tk), 0)
    c = lax.broadcasted_iota(jnp.int32, (tq, tk), 1)
    delta = c - DIL_HALF - a
    kpos = n * tq - DIL_HALF + c
    valid = jnp.where(jnp.abs(delta) <= DIL_HALF, 1, 0) * jnp.where(kpos >= 0, 1, 0) * jnp.where(kpos < m_len, 1, 0)
    lane = lax.broadcasted_iota(jnp.int32, q.shape, 1)
    o = jnp.zeros(q.shape, F32)
    lse_o = jnp.zeros(q.shape, F32)
    for h in range(DIL_HEADS):
        hm = (lane // HEAD_DIM) == h
        s = _dot_nt(jnp.where(hm, q, jnp.zeros_like(q)), k) + bias_ref[0, h]
        s = jnp.where(valid > 0, s, -1e30)
        m = jnp.max(s, axis=-1, keepdims=True)
        e = jnp.exp(s - m)
        l = jnp.sum(e, axis=-1, keepdims=True)
        oh = _dot((e / l).astype(BF16), v)
        o = jnp.where(hm, oh, o)
        lse_o = jnp.where(hm, m + jnp.log(l), lse_o)
    o_ref[...] = o
    lse_ref[...] = lse_o


def _dil_attn(bq, bk, bv, bias, gi, dil, tq):
    s = bq.shape[0]
    m_len = s // dil
    tq = min(tq, m_len)
    gw = DIL_HEADS * HEAD_DIM
    ncol = DIL_GROUPS
    view = lambda x: x.reshape(m_len, dil * DIL_GROUPS * gw)
    hb = tq // DIL_HALF
    last = m_len // DIL_HALF - 1
    col = lambda r: r * ncol + gi
    qspec = pl.BlockSpec((tq, gw), lambda r, n: (n, col(r)))
    pspec = pl.BlockSpec((DIL_HALF, gw), lambda r, n: (jnp.maximum(n * hb - 1, 0), col(r)))
    nspec = pl.BlockSpec((DIL_HALF, gw), lambda r, n: (jnp.minimum((n + 1) * hb, last), col(r)))
    ospec = pl.BlockSpec((tq, gw), lambda r, n: (n, r))
    o, lse = pl.pallas_call(
        functools.partial(_dil_kernel, tq=tq, m_len=m_len),
        grid=(dil, m_len // tq),
        in_specs=[qspec, pspec, qspec, nspec, pspec, qspec, nspec,
                  pl.BlockSpec((1, DIL_HEADS, tq, tq + 2 * DIL_HALF), lambda r, n: (gi, 0, 0, 0))],
        out_specs=[ospec, ospec],
        out_shape=[jax.ShapeDtypeStruct((m_len, dil * gw), F32)] * 2,
        compiler_params=_cparams(("parallel", "parallel")),
        name=f"dil_attn_g{gi}",
    )(view(bq), view(bk), view(bk), view(bk), view(bv), view(bv), view(bv), bias)
    return o.reshape(s, gw), lse.reshape(s, gw)


def _gla_kernel(*refs, reverse, tb, finalize):
    if finalize:
        (q_ref, k_ref, v_ref, lr_ref, wg_ref, gb_ref, tri_ref, ofwd_ref, r_ref, ng_ref,
         o_ref, s_ref, oacc_ref) = refs
    else:
        q_ref, k_ref, v_ref, lr_ref, wg_ref, gb_ref, tri_ref, o_ref, s_ref = refs
        oacc_ref = o_ref
    cw = GLA_CHUNK
    qk = GLA_HEADS * GLA_DK
    vw = GLA_HEADS * GLA_DV

    @pl.when(pl.program_id(0) == 0)
    def _():
        s_ref[...] = jnp.zeros(s_ref.shape, F32)

    logits = _dot(lr_ref[...].astype(BF16), wg_ref[...]) + gb_ref[...]
    g = (jnp.minimum(logits, 0.0) - jnp.log(1.0 + jnp.exp(-jnp.abs(logits)))) * (1.0 / GLA_TAU)
    ghi, glo = _split(g)
    tri = tri_ref[...]
    b = _dot(tri, ghi) + _dot(tri, glo)
    qg = (q_ref[...] * (GLA_DK ** -0.5) * jnp.exp(b)).astype(BF16)
    k = k_ref[...]
    kg = (k * jnp.exp(-b)).astype(BF16)
    v = v_ref[...].astype(BF16)

    ones = jnp.ones((cw, LANES), BF16)
    lane_q = lax.broadcasted_iota(jnp.int32, (cw, qk), 1)
    rr = lax.broadcasted_iota(jnp.int32, (GLA_HEADS * cw, cw), 0)
    cc = lax.broadcasted_iota(jnp.int32, (GLA_HEADS * cw, cw), 1)
    tt = rr % cw
    amask = (cc > tt) if reverse else (cc <= tt)
    srow = lax.broadcasted_iota(jnp.int32, (qk, vw), 0) // GLA_DK
    scol = lax.broadcasted_iota(jnp.int32, (qk, vw), 1) // GLA_DV
    bdmask = srow == scol

    n_chunks = tb // cw
    order = range(n_chunks - 1, -1, -1) if reverse else range(n_chunks)
    for ci in order:
        rows = slice(ci * cw, (ci + 1) * cw)
        bc = b[rows]
        b_end = bc[0:1] if reverse else bc[cw - 1:cw]
        kdec = (k[rows] * jnp.exp(b_end - bc)).astype(BF16)
        btot = _dot_tn(ghi[rows], ones) + _dot_tn(glo[rows], ones)
        qg_c = qg[rows]
        qs = jnp.concatenate(
            [jnp.where((lane_q // GLA_DK) == h, qg_c, jnp.zeros_like(qg_c)) for h in range(GLA_HEADS)], axis=0)
        a = jnp.where(amask, _dot_nt(qs, kg[rows]), 0.0)
        obig = _dot(a.astype(BF16), v[rows])
        o_intra = jnp.concatenate(
            [obig[h * cw:(h + 1) * cw, h * GLA_DV:(h + 1) * GLA_DV] for h in range(GLA_HEADS)], axis=1)
        state = s_ref[...]
        oacc_ref[rows, :] = o_intra + _dot(qg_c, state.astype(BF16))
        ds = _dot_tn(kdec, v[rows])
        decay = jnp.exp(btot)
        decay = jnp.concatenate([decay] * GLA_HEADS, axis=1)
        s_ref[...] = jnp.where(bdmask, decay * state + ds, 0.0)

    if finalize:
        o = ofwd_ref[...] + oacc_ref[...]
        r = r_ref[...]
        outs = []
        for h in range(GLA_HEADS):
            sl = slice(h * GLA_DV, (h + 1) * GLA_DV)
            outs.append(_rms(o[:, sl]) * ng_ref[...] * (r[:, sl] * _sigmoid(r[:, sl])))
        o_ref[...] = jnp.concatenate(outs, axis=1)


def _chunk_tri(tb, reverse):
    i = np.arange(tb)
    same = (i[:, None] // GLA_CHUNK) == (i[None, :] // GLA_CHUNK)
    tri = (i[None, :] >= i[:, None]) if reverse else (i[None, :] <= i[:, None])
    return jnp.asarray(same & tri, dtype=BF16)


def _gla_scan(z, wg, gb, tb, reverse, fin=None):
    s = z.shape[0]
    nb = s // tb
    qk = GLA_HEADS * GLA_DK
    vw = GLA_HEADS * GLA_DV
    blk = (lambda i: nb - 1 - i) if reverse else (lambda i: i)
    zspec = lambda w, off: pl.BlockSpec((tb, w), lambda i: (blk(i), off // w))
    cspec = lambda r, c: pl.BlockSpec((r, c), lambda i: (0, 0))
    in_specs = [zspec(qk, Z_AQ), zspec(qk, Z_AK), zspec(vw, Z_AV), zspec(LANES, Z_ALR),
                cspec(LANES, qk), cspec(1, qk), cspec(tb, tb)]
    args = [z, z, z, z, wg, gb, _chunk_tri(tb, reverse)]
    scratch = [pltpu.VMEM((qk, vw), F32)]
    if fin is not None:
        o_fwd, norm_g = fin
        in_specs += [pl.BlockSpec((tb, vw), lambda i: (blk(i), 0)), zspec(vw, Z_AR), cspec(1, GLA_DV)]
        args += [o_fwd, z, norm_g.reshape(1, GLA_DV)]
        scratch.append(pltpu.VMEM((tb, vw), F32))
    return pl.pallas_call(
        functools.partial(_gla_kernel, reverse=reverse, tb=tb, finalize=fin is not None),
        grid=(nb,),
        in_specs=in_specs,
        out_specs=pl.BlockSpec((tb, vw), lambda i: (blk(i), 0)),
        out_shape=jax.ShapeDtypeStruct((s, vw), F32),
        scratch_shapes=scratch,
        compiler_params=_cparams(("arbitrary",)),
        name="gla_bwd" if reverse else "gla_fwd",
    )(*args)


def _hy_pre_kernel(u_ref, up_ref, un_ref, w_ref, b_ref, x0_ref, z_ref, zb_ref, *, tm):
    i = pl.program_id(0)
    u = u_ref[...]
    row = lax.broadcasted_iota(jnp.int32, u.shape, 0)
    prev_row = jnp.where(i == 0, 0.0, up_ref[7:8, :])
    next_row = jnp.where(i == pl.num_programs(0) - 1, 0.0, un_ref[0:1, :])
    u_prev = jnp.where(row == 0, prev_row, pltpu.roll(u, 1, axis=0))
    u_next = jnp.where(row == tm - 1, next_row, pltpu.roll(u, tm - 1, axis=0))
    y = b_ref[...] + u_prev * w_ref[0:1] + u * w_ref[1:2] + u_next * w_ref[2:3]
    x0_ref[...] = y[:, :HY_WIDTH]
    z = y[:, HY_WIDTH:2 * HY_WIDTH] * y[:, 2 * HY_WIDTH:]
    z_ref[...] = z
    zb_ref[...] = z.astype(BF16)


def _hy_pre(z, conv_w, conv_b, tm):
    s = z.shape[0]
    w = 3 * HY_WIDTH
    nr = s // 8
    ospec = pl.BlockSpec((tm, HY_WIDTH), lambda i: (i, 0))
    return pl.pallas_call(
        functools.partial(_hy_pre_kernel, tm=tm),
        grid=(s // tm,),
        in_specs=[pl.BlockSpec((tm, w), lambda i: (i, 0)),
                  pl.BlockSpec((8, w), lambda i: (jnp.maximum(i * (tm // 8) - 1, 0), 0)),
                  pl.BlockSpec((8, w), lambda i: (jnp.minimum((i + 1) * (tm // 8), nr - 1), 0)),
                  pl.BlockSpec((3, w), lambda i: (0, 0)),
                  pl.BlockSpec((1, w), lambda i: (0, 0))],
        out_specs=[ospec, ospec, ospec],
        out_shape=[jax.ShapeDtypeStruct((s, HY_WIDTH), F32)] * 2 + [jax.ShapeDtypeStruct((s, HY_WIDTH), BF16)],
        compiler_params=_cparams(("parallel",)),
        name="hyena_pre",
    )(z, z, z, conv_w, conv_b.reshape(1, w))


def _hy_filter_kernel(emb_ref, w1_ref, b1_ref, f1_ref, w2_ref, b2_ref, f2_ref, w3_ref, dl_ref,
                      k_ref, kb_ref, norm_ref, *, tl, seq):
    i = pl.program_id(0)
    emb = emb_ref[...]
    h = jnp.sin(f1_ref[...] * (_dot3(emb, w1_ref[...]) + b1_ref[...]))
    h = jnp.sin(f2_ref[...] * (_dot3(h, w2_ref[...]) + b2_ref[...]))
    h = _dot3(h, w3_ref[...])
    back = i * tl >= seq
    h = jnp.where(back, h[:, HY_WIDTH:], h[:, :HY_WIDTH])
    h = h * jnp.exp(-emb[:, 0:1] * dl_ref[...])
    row = i * tl + lax.broadcasted_iota(jnp.int32, h.shape, 0)
    h = jnp.where(row == seq, 0.0, h)
    k_ref[...] = h
    kb_ref[...] = h.astype(BF16)

    @pl.when(i == 0)
    def _():
        norm_ref[...] = jnp.zeros(norm_ref.shape, F32)

    norm_ref[...] += jnp.sum(jnp.abs(h), axis=0, keepdims=True)


def _hy_positions(seq):
    t = jnp.linspace(0.0, 1.0, seq, dtype=F32)[:, None]
    bands = (HY_EMB - 1) // 2
    freqs = jnp.linspace(1e-4, bands - 1, bands, dtype=F32)[None]
    w = 2.0 * math.pi * jnp.arange(seq, dtype=F32)[:, None] / seq
    zf = jnp.concatenate([t, jnp.cos(freqs * w), -jnp.sin(freqs * w)], axis=-1)
    pos = np.concatenate([np.arange(seq), [0], np.arange(seq - 1, 0, -1)])
    emb = zf[pos]
    return jnp.pad(emb, ((0, 0), (0, LANES - HY_EMB)))


def _hy_filter(emb, w1, b1, fr1, w2, b2, fr2, w3, tl):
    n = emb.shape[0]
    seq = n // 2
    pad2 = lambda a, r, c: jnp.pad(a, ((0, r - a.shape[0]), (0, c - a.shape[1])))
    min_decay = math.log(HY_DECAY_TARGET) / HY_SLOW_DECAY
    max_decay = math.log(HY_DECAY_TARGET) / HY_FAST_DECAY
    deltas = jnp.abs(jnp.linspace(min_decay, max_decay, HY_WIDTH, dtype=F32))[None]
    cs = lambda r, c: pl.BlockSpec((r, c), lambda i: (0, 0))
    ospec = pl.BlockSpec((tl, HY_WIDTH), lambda i: (i, 0))
    return pl.pallas_call(
        functools.partial(_hy_filter_kernel, tl=tl, seq=seq),
        grid=(n // tl,),
        in_specs=[pl.BlockSpec((tl, LANES), lambda i: (i, 0)),
                  cs(LANES, LANES), cs(1, LANES), cs(1, LANES),
                  cs(LANES, LANES), cs(1, LANES), cs(1, LANES),
                  cs(LANES, 2 * HY_WIDTH), cs(1, HY_WIDTH)],
        out_specs=[ospec, ospec, pl.BlockSpec((1, HY_WIDTH), lambda i: (0, 0))],
        out_shape=[jax.ShapeDtypeStruct((n, HY_WIDTH), F32), jax.ShapeDtypeStruct((n, HY_WIDTH), BF16),
                   jax.ShapeDtypeStruct((1, HY_WIDTH), F32)],
        compiler_params=_cparams(("arbitrary",)),
        name="hyena_filter",
    )(emb, pad2(w1, LANES, LANES), pad2(b1[None], 1, LANES), pad2(fr1[None], 1, LANES),
      pad2(w2, LANES, LANES), pad2(b2[None], 1, LANES), pad2(fr2[None], 1, LANES),
      pad2(w3, LANES, 2 * HY_WIDTH), deltas)


def _dft_tables(n1):
    n = n1 * DFT_N2
    a = jnp.arange(n1, dtype=jnp.int32)
    ang1 = (2.0 * math.pi / n1) * ((a[:, None] * a[None, :]) % n1).astype(F32)
    c1, s1 = jnp.cos(ang1), jnp.sin(ang1)
    k1 = jnp.arange(n1, dtype=jnp.int32)[:, None, None]
    k2 = jnp.arange(DFT_N2, dtype=jnp.int32)[None, :, None]
    n2 = jnp.arange(DFT_N2, dtype=jnp.int32)[None, None, :]
    ang = (2.0 * math.pi / n) * ((n2 * (k1 + n1 * k2)) % n).astype(F32)
    gc, gs = jnp.cos(ang), jnp.sin(ang)
    tb = lambda x: x.astype(BF16)
    return dict(c1=tb(c1), s1=tb(s1), gc=tb(gc), gs=tb(gs),
                gct=tb(jnp.swapaxes(gc, 1, 2)), gst=tb(jnp.swapaxes(gs, 1, 2)))


def _dft1_kernel(c_ref, s_ref, x_ref, re_ref, im_ref):
    x = x_ref[...]
    re_ref[...] = _dot(c_ref[...], x).astype(BF16)
    im_ref[...] = (-_dot(s_ref[...], x)).astype(BF16)


def _dft1(x, c1, s1, cb):
    k1, w = x.shape
    n1 = c1.shape[0]
    ospec = pl.BlockSpec((n1, cb), lambda j: (0, j))
    return pl.pallas_call(
        _dft1_kernel,
        grid=(w // cb,),
        in_specs=[pl.BlockSpec((n1, k1), lambda j: (0, 0)), pl.BlockSpec((n1, k1), lambda j: (0, 0)),
                  pl.BlockSpec((k1, cb), lambda j: (0, j))],
        out_specs=[ospec, ospec],
        out_shape=[jax.ShapeDtypeStruct((n1, w), BF16)] * 2,
        compiler_params=_cparams(("parallel",)),
        name="dft_stage1",
    )(c1, s1, x)


def _dft2_filter_kernel(gc_ref, gs_ref, are_ref, aim_ref, norm_ref, kre_ref, kim_ref, *, kb):
    inv = 1.0 / norm_ref[...]
    for t in range(kb):
        gc, gs = gc_ref[t], gs_ref[t]
        are, aim = are_ref[t], aim_ref[t]
        kre_ref[t] = (_dot(gc, are) + _dot(gs, aim)) * inv
        kim_ref[t] = (_dot(gc, aim) - _dot(gs, are)) * inv


def _dft2_filter(tabs, are, aim, norm, kb):
    n1, _, c = are.shape
    gspec = pl.BlockSpec((kb, DFT_N2, DFT_N2), lambda i: (i, 0, 0))
    aspec = pl.BlockSpec((kb, DFT_N2, c), lambda i: (i, 0, 0))
    return pl.pallas_call(
        functools.partial(_dft2_filter_kernel, kb=kb),
        grid=(n1 // kb,),
        in_specs=[gspec, gspec, aspec, aspec, pl.BlockSpec((1, c), lambda i: (0, 0))],
        out_specs=[aspec, aspec],
        out_shape=[jax.ShapeDtypeStruct((n1, DFT_N2, c), F32)] * 2,
        compiler_params=_cparams(("parallel",)),
        name="dft_stage2_filter",
    )(tabs["gc"], tabs["gs"], are, aim, norm)


def _conv_mid_kernel(gc_ref, gs_ref, gct_ref, gst_ref, are_ref, aim_ref, kre_ref, kim_ref,
                     bre_ref, bim_ref, *, kb):
    for t in range(kb):
        gc, gs = gc_ref[t], gs_ref[t]
        are, aim = are_ref[t], aim_ref[t]
        xre = _dot(gc, are) + _dot(gs, aim)
        xim = _dot(gc, aim) - _dot(gs, are)
        kre, kim = kre_ref[t], kim_ref[t]
        yre = (xre * kre - xim * kim).astype(BF16)
        yim = (xre * kim + xim * kre).astype(BF16)
        gct, gst = gct_ref[t], gst_ref[t]
        bre_ref[t] = (_dot(gct, yre) - _dot(gst, yim)).astype(BF16)
        bim_ref[t] = (_dot(gct, yim) + _dot(gst, yre)).astype(BF16)


def _conv_mid(tabs, are, aim, kre, kim, kb):
    n1, _, c = are.shape
    gspec = pl.BlockSpec((kb, DFT_N2, DFT_N2), lambda i: (i, 0, 0))
    aspec = pl.BlockSpec((kb, DFT_N2, c), lambda i: (i, 0, 0))
    return pl.pallas_call(
        functools.partial(_conv_mid_kernel, kb=kb),
        grid=(n1 // kb,),
        in_specs=[gspec, gspec, gspec, gspec, aspec, aspec, aspec, aspec],
        out_specs=[aspec, aspec],
        out_shape=[jax.ShapeDtypeStruct((n1, DFT_N2, c), BF16)] * 2,
        compiler_params=_cparams(("parallel",)),
        name="conv_spectral",
    )(tabs["gc"], tabs["gs"], tabs["gct"], tabs["gst"], are, aim, kre, kim)


def _idft1_kernel(c_ref, s_ref, bre_ref, bim_ref, z_ref, x0_ref, skip_ref, o_ref, *, inv_n):
    y = (_dot(c_ref[...], bre_ref[...]) - _dot(s_ref[...], bim_ref[...])) * inv_n
    o_ref[...] = x0_ref[...] * (y + z_ref[...] * skip_ref[...])


def _idft1(c1h, s1h, bre, bim, z, x0, skip_t, cb):
    ko, n1 = c1h.shape
    w = bre.shape[1]
    n = n1 * DFT_N2
    ospec = pl.BlockSpec((ko, cb), lambda j: (0, j))
    return pl.pallas_call(
        functools.partial(_idft1_kernel, inv_n=1.0 / n),
        grid=(w // cb,),
        in_specs=[pl.BlockSpec((ko, n1), lambda j: (0, 0)), pl.BlockSpec((ko, n1), lambda j: (0, 0)),
                  pl.BlockSpec((n1, cb), lambda j: (0, j)), pl.BlockSpec((n1, cb), lambda j: (0, j)),
                  ospec, ospec, pl.BlockSpec((1, cb), lambda j: (0, 0))],
        out_specs=ospec,
        out_shape=jax.ShapeDtypeStruct((ko, w), F32),
        compiler_params=_cparams(("parallel",)),
        name="idft_stage1",
    )(c1h, s1h, bre, bim, z, x0, skip_t)


def _hyena(z_all, tabs, emb, p, i, cb, kb):
    seq = z_all.shape[0]
    n1 = 2 * seq // DFT_N2
    c = HY_WIDTH
    x0, zf, zb = _hy_pre(z_all, p["hy_conv_w"][i], p["hy_conv_b"][i], min(512, seq))
    _, kern_b, norm = _hy_filter(emb, p["hy_w1"][i], p["hy_b1"][i], p["hy_freq1"][i], p["hy_w2"][i],
                                 p["hy_b2"][i], p["hy_freq2"][i], p["hy_w3"][i], min(1024, seq))
    wide = DFT_N2 * c
    kre, kim = _dft1(kern_b.reshape(n1, wide), tabs["c1"], tabs["s1"], cb)
    kre, kim = _dft2_filter(tabs, kre.reshape(n1, DFT_N2, c), kim.reshape(n1, DFT_N2, c), norm, kb)
    are, aim = _dft1(zb.reshape(n1 // 2, wide), tabs["c1"][:, :n1 // 2], tabs["s1"][:, :n1 // 2], cb)
    bre, bim = _conv_mid(tabs, are.reshape(n1, DFT_N2, c), aim.reshape(n1, DFT_N2, c), kre, kim, kb)
    skip_t = jnp.tile(p["hy_skip"][i][None], (1, cb // c))
    yd = _idft1(tabs["c1"][:n1 // 2], tabs["s1"][:n1 // 2], bre.reshape(n1, wide), bim.reshape(n1, wide),
                zf.reshape(n1 // 2, wide), x0.reshape(n1 // 2, wide), skip_t, cb)
    return yd.reshape(seq, c)


def _combine_kernel(x_ref, ya_ref, o0_ref, o1_ref, o2_ref, l0_ref, l1_ref, l2_ref, yc_ref, yd_ref,
                    g0_ref, g1_ref, g2_ref, g3_ref, pa_ref, pb_ref, pc_ref, pd_ref, wo_ref, o_ref):
    l0, l1, l2 = l0_ref[...], l1_ref[...], l2_ref[...]
    mx = jnp.maximum(jnp.maximum(l0, l1), l2)
    e0, e1, e2 = jnp.exp(l0 - mx), jnp.exp(l1 - mx), jnp.exp(l2 - mx)
    yb = (e0 * o0_ref[...] + e1 * o1_ref[...] + e2 * o2_ref[...]) / (e0 + e1 + e2)
    m = (_sigmoid(g0_ref[...]) * _dot(ya_ref[...].astype(BF16), pa_ref[...])
         + _sigmoid(g1_ref[...]) * _dot(yb.astype(BF16), pb_ref[...])
         + _sigmoid(g2_ref[...]) * _dot(yc_ref[...].astype(BF16), pc_ref[...])
         + _sigmoid(g3_ref[...]) * _dot(yd_ref[...].astype(BF16), pd_ref[...]))
    o_ref[...] = x_ref[...] + _dot(m.astype(BF16), wo_ref[...])


def _combine(x, z, ya, dil_outs, yc, yd, pa, pb, pc, pd, wo, tm):
    s, d = x.shape
    rs = lambda w: pl.BlockSpec((tm, w), lambda i: (i, 0))
    gs = lambda b: pl.BlockSpec((tm, d), lambda i: (i, Z_GATE // d + b))
    ws = lambda a: pl.BlockSpec(a.shape, lambda i: (0, 0))
    (o0, l0), (o1, l1), (o2, l2) = dil_outs
    gw = DIL_HEADS * HEAD_DIM
    return pl.pallas_call(
        _combine_kernel,
        grid=(s // tm,),
        in_specs=[rs(d), rs(ya.shape[1]), rs(gw), rs(gw), rs(gw), rs(gw), rs(gw), rs(gw),
                  rs(yc.shape[1]), rs(yd.shape[1]), gs(0), gs(1), gs(2), gs(3),
                  ws(pa), ws(pb), ws(pc), ws(pd), ws(wo)],
        out_specs=rs(d),
        out_shape=jax.ShapeDtypeStruct((s, d), F32),
        compiler_params=_cparams(("parallel",)),
        name="combine",
    )(x, ya, o0, o1, o2, l0, l1, l2, yc, yd, z, z, z, z, pa, pb, pc, pd, wo)


def _mlp_kernel(x_ref, g_ref, w1_ref, w2_ref, o_ref, h_ref, acc_ref):
    j = pl.program_id(1)

    @pl.when(j == 0)
    def _():
        h_ref[...] = (_rms(x_ref[...]) * g_ref[...]).astype(BF16)
        acc_ref[...] = jnp.zeros(acc_ref.shape, F32)

    a = jnp.maximum(_dot(h_ref[...], w1_ref[...]), 0.0)
    acc_ref[...] += _dot((a * a).astype(BF16), w2_ref[...])

    @pl.when(j == pl.num_programs(1) - 1)
    def _():
        o_ref[...] = x_ref[...] + acc_ref[...]


def _mlp(x, g, w1, w2, tm, tf):
    s, d = x.shape
    ff = w1.shape[1]
    return pl.pallas_call(
        _mlp_kernel,
        grid=(s // tm, ff // tf),
        in_specs=[pl.BlockSpec((tm, d), lambda i, j: (i, 0)),
                  pl.BlockSpec((1, d), lambda i, j: (0, 0)),
                  pl.BlockSpec((d, tf), lambda i, j: (0, j)),
                  pl.BlockSpec((tf, d), lambda i, j: (j, 0))],
        out_specs=pl.BlockSpec((tm, d), lambda i, j: (i, 0)),
        out_shape=jax.ShapeDtypeStruct((s, d), F32),
        scratch_shapes=[pltpu.VMEM((tm, d), BF16), pltpu.VMEM((tm, d), F32)],
        compiler_params=_cparams(("parallel", "arbitrary")),
        name="mlp",
    )(x, g.reshape(1, d), w1, w2)


def _permute_w_in(w):
    d = w.shape[0]
    pieces = [w[:, _O_DU:_O_GATE], w[:, _O_B:_O_C], w[:, _O_AQ:_O_AK], w[:, _O_C:_O_DU],
              w[:, _O_AV:_O_AR], w[:, _O_AR:_O_ALR], w[:, _O_AK:_O_AV], w[:, _O_ALR:_O_B],
              jnp.zeros((d, Z_GATE - Z_ALR - 2 * GLA_RANK), w.dtype), w[:, _O_GATE:_O_END]]
    return jnp.concatenate(pieces, axis=1).astype(BF16)


def _dil_bias_idx(tq):
    a = jnp.arange(tq)[:, None]
    c = jnp.arange(tq + 2 * DIL_HALF)[None, :]
    delta = c - DIL_HALF - a
    return jnp.stack([_t5_bucket(delta * dil) for _, dil in DIL_PATTERNS]).astype(jnp.int32)


def _diff_bias_idx(t):
    nb = _diff_band(t) + 1
    a = jnp.arange(t)[:, None]
    c = jnp.arange(t)[None, :]
    return jnp.stack([_t5_bucket(o * t + c - a) for o in range(-nb, nb + 1)]).astype(jnp.int32)


def _forward(x, p, *, t_diff, tq_dil, tb_gla, tm_proj, tn_proj, tm_row, tm_mlp, tf_mlp, cb_dft, kb_dft):
    seq = x.shape[0]
    depth = p["w_in"].shape[0]
    n_dil_bias = DIL_GROUPS * DIL_HEADS
    tq_dil = min(tq_dil, seq // DIL_PATTERNS[-1][1])
    dil_bias = _bias_tiles(p["t5_bias"], _dil_bias_idx(tq_dil), DIL_HEADS, 0, 1, DIL_HEADS)
    diff_bias = _bias_tiles(p["t5_bias"], _diff_bias_idx(t_diff), DIFF_HEADS, n_dil_bias, 2 ** 30, 0)
    tabs = _dft_tables(2 * seq // DFT_N2)
    emb = _hy_positions(seq)
    rep = lambda g, n: jnp.tile(g, n)
    for i in range(depth):
        z = _norm_matmul(x, p["norm1_g"][i], _permute_w_in(p["w_in"][i]), tm_proj, tn_proj)
        wg = [jnp.zeros((LANES, GLA_HEADS * GLA_DK), F32).at[j * GLA_RANK:(j + 1) * GLA_RANK].set(
            p["gla_gate_w"][i, j]).astype(BF16) for j in range(2)]
        gb = p["gla_gate_b"][i]
        o_fwd = _gla_scan(z, wg[0], gb[0:1], tb_gla, False)
        ya = _gla_scan(z, wg[1], gb[1:2], tb_gla, True, fin=(o_fwd, p["gla_norm_g"][i]))
        bq, bk, bv, cq, ck, cv = _prep(
            z, jnp.repeat(p["dil_qnorm_g"][i], DIL_HEADS, axis=0).reshape(-1),
            jnp.repeat(p["dil_knorm_g"][i], DIL_HEADS, axis=0).reshape(-1),
            rep(p["diff_qnorm_g"][i], 2 * DIFF_HEADS), rep(p["diff_knorm_g"][i], 2 * DIFF_HEADS), tm_row)
        dil_outs = [_dil_attn(bq, bk, bv, dil_bias, gi, dil, tq_dil)
                    for gi, (_, dil) in enumerate(DIL_PATTERNS)]
        lam_init = 0.8 - 0.6 * math.exp(-0.3 * i)
        yc = _diff_attn(cq, ck, cv, diff_bias, p["diff_lambda"][i], p["diff_subln_g"][i], lam_init, t_diff)
        yd = _hyena(z, tabs, emb, p, i, cb_dft, kb_dft)
        x = _combine(x, z, ya, dil_outs, yc, yd, p["proj_a"][i].astype(BF16), p["proj_b"][i].astype(BF16),
                     p["proj_c"][i].astype(BF16), p["proj_d"][i].astype(BF16), p["w_out"][i].astype(BF16),
                     tm_row)
        x = _mlp(x, p["norm2_g"][i], p["mlp_w1"][i].astype(BF16), p["mlp_w2"][i].astype(BF16), tm_mlp, tf_mlp)
    return x


def kernel(x, t5_bias, norm1_g, w_in, gla_gate_w, gla_gate_b, gla_norm_g, dil_qnorm_g, dil_knorm_g,
           diff_qnorm_g, diff_knorm_g, diff_lambda, diff_subln_g, hy_conv_w, hy_conv_b, hy_w1, hy_b1,
           hy_freq1, hy_w2, hy_b2, hy_freq2, hy_w3, hy_skip, proj_a, proj_b, proj_c, proj_d, w_out,
           norm2_g, mlp_w1, mlp_w2):
    p = dict(t5_bias=t5_bias, norm1_g=norm1_g, w_in=w_in, gla_gate_w=gla_gate_w, gla_gate_b=gla_gate_b,
             gla_norm_g=gla_norm_g, dil_qnorm_g=dil_qnorm_g, dil_knorm_g=dil_knorm_g,
             diff_qnorm_g=diff_qnorm_g, diff_knorm_g=diff_knorm_g, diff_lambda=diff_lambda,
             diff_subln_g=diff_subln_g, hy_conv_w=hy_conv_w, hy_conv_b=hy_conv_b, hy_w1=hy_w1, hy_b1=hy_b1,
             hy_freq1=hy_freq1, hy_w2=hy_w2, hy_b2=hy_b2, hy_freq2=hy_freq2, hy_w3=hy_w3, hy_skip=hy_skip,
             proj_a=proj_a, proj_b=proj_b, proj_c=proj_c, proj_d=proj_d, w_out=w_out, norm2_g=norm2_g,
             mlp_w1=mlp_w1, mlp_w2=mlp_w2)
    b, s, d = x.shape
    outs = [_forward(x[bi], p, t_diff=512, tq_dil=256, tb_gla=512, tm_proj=1024, tn_proj=512, tm_row=512,
                     tm_mlp=1024, tf_mlp=1024, cb_dft=4096, kb_dft=8) for bi in range(b)]
    return jnp.stack(outs)
```

```python
import functools
import math

import jax
import jax.numpy as jnp
import numpy as np
from jax import lax
from jax.experimental import pallas as pl
from jax.experimental.pallas import tpu as pltpu

F32 = jnp.float32
BF16 = jnp.bfloat16

D_MODEL = 1024
HEAD_DIM = 64
GLA_HEADS = 4
GLA_DK = 64
GLA_DV = 128
GLA_RANK = 16
GLA_TAU = 16.0
GLA_CHUNK = 64
DIL_PATTERNS = ((128, 1), (512, 4), (2048, 16))
DIL_GROUPS = 3
DIL_HEADS = 4
DIL_HALF = 64
DIFF_HEADS = 4
DIFF_DV = 128
DIFF_VT = DIFF_DV + 16
HY_WIDTH = 512
HY_EMB = 33
HY_FFN = 64
HY_DECAY_TARGET = 1e-2
HY_FAST_DECAY = 0.3
HY_SLOW_DECAY = 1.5
T5_BUCKETS = 32
T5_MAX_DIST = 1024
N_BIAS_HEADS = 16
D_FF = 4096
RMS_EPS = 1e-6
LOG2E = math.log2(math.e)

LANES = 128
VMEM_LIMIT = 48 * 1024 * 1024

Z_DU = 0
Z_BQ, Z_BK, Z_BV = 1536, 2304, 3072
Z_AQ = 3840
Z_CQ, Z_CK, Z_CV = 4096, 4608, 5120
Z_AV, Z_AR = 5632, 6144
Z_AK = 6656
Z_ALR = 6912
Z_GATE = 7168
Z_COLS = 11264

_O_AQ, _O_AK, _O_AV, _O_AR, _O_ALR, _O_B, _O_C, _O_DU, _O_GATE, _O_END = (
    0, 256, 512, 1024, 1536, 1568, 3872, 5408, 6944, 11040)

DFT_N2 = 128


def _cparams(sem):
    return pltpu.CompilerParams(dimension_semantics=sem, vmem_limit_bytes=VMEM_LIMIT)


def _dot(a, b):
    return jnp.dot(a, b, preferred_element_type=F32)


def _dot_nt(a, b):
    return lax.dot_general(a, b, (((1,), (1,)), ((), ())), preferred_element_type=F32)


def _dot_tn(a, b):
    return lax.dot_general(a, b, (((0,), (0,)), ((), ())), preferred_element_type=F32)


def _split(x):
    hi = x.astype(BF16)
    lo = (x - hi.astype(F32)).astype(BF16)
    return hi, lo


def _dot3(a, b):
    ah, al = _split(a)
    bh, bl = _split(b)
    return _dot(ah, bh) + _dot(ah, bl) + _dot(al, bh)


def _rms(x):
    return x * lax.rsqrt(jnp.mean(x * x, axis=-1, keepdims=True) + RMS_EPS)


def _sigmoid(x):
    return 1.0 / (1.0 + jnp.exp(-x))


def _norm_matmul_kernel(x_ref, g_ref, w_ref, o_ref, h_ref):
    @pl.when(pl.program_id(1) == 0)
    def _():
        h_ref[...] = (_rms(x_ref[...]) * g_ref[...]).astype(BF16)

    o_ref[...] = _dot(h_ref[...], w_ref[...])


def _norm_matmul(x, g, w, tm, tn):
    s, d = x.shape
    n = w.shape[1]
    return pl.pallas_call(
        _norm_matmul_kernel,
        grid=(s // tm, n // tn),
        in_specs=[pl.BlockSpec((tm, d), lambda i, j: (i, 0)),
                  pl.BlockSpec((1, d), lambda i, j: (0, 0)),
                  pl.BlockSpec((d, tn), lambda i, j: (0, j))],
        out_specs=pl.BlockSpec((tm, tn), lambda i, j: (i, j)),
        out_shape=jax.ShapeDtypeStruct((s, n), F32),
        scratch_shapes=[pltpu.VMEM((tm, d), BF16)],
        compiler_params=_cparams(("parallel", "arbitrary")),
        name="in_proj",
    )(x, g.reshape(1, d), w)


def _group_norm(x, e, gain):
    hi, lo = _split(x * x)
    ms = (_dot(hi, e) + _dot(lo, e)) * (1.0 / HEAD_DIM)
    return x * lax.rsqrt(ms + RMS_EPS) * gain


def _prep_kernel(bq_ref, bk_ref, bv_ref, cq_ref, ck_ref, cv_ref, eb_ref, ec_ref,
                 gbq_ref, gbk_ref, gcq_ref, gck_ref,
                 obq_ref, obk_ref, obv_ref, ocq_ref, ock_ref, ocv_ref):
    scale = HEAD_DIM ** -0.5
    eb = eb_ref[...]
    ec = ec_ref[...]
    obq_ref[...] = (_group_norm(bq_ref[...], eb, gbq_ref[...]) * scale).astype(BF16)
    obk_ref[...] = _group_norm(bk_ref[...], eb, gbk_ref[...]).astype(BF16)
    obv_ref[...] = bv_ref[...].astype(BF16)
    ocq_ref[...] = (_group_norm(cq_ref[...], ec, gcq_ref[...]) * (scale * LOG2E)).T.astype(BF16)
    ock_ref[...] = _group_norm(ck_ref[...], ec, gck_ref[...]).astype(BF16)
    cvt = cv_ref[...].T
    ones = jnp.ones((DIFF_VT - DIFF_DV, cvt.shape[1]), F32)
    ocv_ref[...] = jnp.concatenate(
        [t for h in range(DIFF_HEADS) for t in (cvt[h * DIFF_DV:(h + 1) * DIFF_DV], ones)], axis=0).astype(BF16)


def _block_diag_ones(width):
    idx = np.arange(width) // HEAD_DIM
    return jnp.asarray(idx[:, None] == idx[None, :], dtype=BF16)


def _prep(z, gbq, gbk, gcq, gck, tm):
    s = z.shape[0]
    wb, wc = DIL_GROUPS * DIL_HEADS * HEAD_DIM, DIFF_HEADS * 2 * HEAD_DIM
    zspec = lambda w, off: pl.BlockSpec((tm, w), lambda i: (i, off // w))
    cspec = lambda r, c: pl.BlockSpec((r, c), lambda i: (0, 0))
    ospec = lambda w: pl.BlockSpec((tm, w), lambda i: (i, 0))
    tspec = lambda w: pl.BlockSpec((w, tm), lambda i: (0, i))
    return pl.pallas_call(
        _prep_kernel,
        grid=(s // tm,),
        in_specs=[zspec(wb, Z_BQ), zspec(wb, Z_BK), zspec(wb, Z_BV),
                  zspec(wc, Z_CQ), zspec(wc, Z_CK), zspec(wc, Z_CV),
                  cspec(wb, wb), cspec(wc, wc),
                  cspec(1, wb), cspec(1, wb), cspec(1, wc), cspec(1, wc)],
        out_specs=[ospec(wb), ospec(wb), ospec(wb), tspec(wc), ospec(wc), tspec(DIFF_HEADS * DIFF_VT)],
        out_shape=[jax.ShapeDtypeStruct((s, wb), BF16)] * 3
        + [jax.ShapeDtypeStruct((wc, s), BF16), jax.ShapeDtypeStruct((s, wc), BF16),
           jax.ShapeDtypeStruct((DIFF_HEADS * DIFF_VT, s), BF16)],
        compiler_params=_cparams(("parallel",)),
        name="qk_prep",
    )(z, z, z, z, z, z, _block_diag_ones(wb), _block_diag_ones(wc),
      gbq.reshape(1, wb), gbk.reshape(1, wb), gcq.reshape(1, wc), gck.reshape(1, wc))


def _t5_bucket(rel):
    half = T5_BUCKETS // 2
    max_exact = half // 2
    ret = jnp.where(rel > 0, half, 0)
    n = jnp.abs(rel)
    nf = jnp.maximum(n, 1).astype(F32)
    large = max_exact + (jnp.log(nf / max_exact) / math.log(T5_MAX_DIST / max_exact)
                         * (half - max_exact)).astype(jnp.int32)
    large = jnp.minimum(large, half - 1)
    return ret + jnp.where(n < max_exact, n, large)


def _bias_kernel(tab_ref, idx_ref, o_ref, *, head_base, heads_per_tile_group, out_scale):
    t = pl.program_id(0)
    h = pl.program_id(1)
    col = head_base + (t // heads_per_tile_group[0]) * heads_per_tile_group[1] + h
    idx = idx_ref[0]
    acc = jnp.zeros(idx.shape, F32)
    for b in range(T5_BUCKETS):
        acc = jnp.where(idx == b, tab_ref[b, col], acc)
    o_ref[0, 0] = acc * out_scale


def _bias_tiles(t5_bias, idx, n_heads, head_base, tiles_per_group, heads_step, out_scale=1.0):
    nt, r, c = idx.shape
    return pl.pallas_call(
        functools.partial(_bias_kernel, head_base=head_base,
                          heads_per_tile_group=(tiles_per_group, heads_step), out_scale=out_scale),
        grid=(nt, n_heads),
        in_specs=[pl.BlockSpec(memory_space=pltpu.SMEM),
                  pl.BlockSpec((1, r, c), lambda t, h: (t, 0, 0))],
        out_specs=pl.BlockSpec((1, 1, r, c), lambda t, h: (t, h, 0, 0)),
        out_shape=jax.ShapeDtypeStruct((nt, n_heads, r, c), F32),
        compiler_params=_cparams(("parallel", "parallel")),
        name="t5_bias_tiles",
    )(t5_bias, idx)


def _diff_attn_kernel(qt_ref, k_ref, vt_ref, bias_ref, lam_ref, g_ref, o_ref,
                      qa_ref, qb_ref, m_ref, acc_ref, s_ref, *, lam_init, qc, band):
    j = pl.program_id(2)
    t = qt_ref.shape[1]

    @pl.when(j == 0)
    def _():
        qt = qt_ref[...]
        row = lax.broadcasted_iota(jnp.int32, qt.shape, 0)
        qa_ref[...] = jnp.where(row < HEAD_DIM, qt, jnp.zeros_like(qt))
        qb_ref[...] = jnp.where(row >= HEAD_DIM, qt, jnp.zeros_like(qt))
        m_ref[...] = jnp.full(m_ref.shape, -jnp.inf, F32)
        acc_ref[...] = jnp.zeros(acc_ref.shape, F32)

    chains = [(c, slice(u * qc, (u + 1) * qc)) for c in range(2) for u in range(t // qc)]
    q_refs = (qa_ref, qb_ref)

    def step(far):
        k = k_ref[...]
        vt = vt_ref[...]
        if far:
            const = bias_ref[0, 0, 0:1, 0:1]
            scores = lambda c, cols: _dot(k, q_refs[c][:, cols])
        else:
            const = 0.0
            scores = lambda c, cols: _dot(k, q_refs[c][:, cols]) + bias_ref[0, 0, :, cols]
        s_ref[0] = scores(*chains[0])
        for n, (c, cols) in enumerate(chains):
            if n + 1 < len(chains):
                s_ref[(n + 1) % 2] = scores(*chains[n + 1])
            m_old = m_ref[c, :, cols]
            m_new = jnp.maximum(m_old, jnp.max(s_ref[n % 2], axis=0, keepdims=True) + const)
            alpha = jnp.exp2(m_old - m_new)
            p = jnp.exp2(s_ref[n % 2] - (m_new - const)).astype(BF16)
            acc_ref[c, :, cols] = alpha * acc_ref[c, :, cols] + _dot(vt, p)
            m_ref[c, :, cols] = m_new

    is_far = jnp.abs(j - pl.program_id(1)) > band
    pl.when(is_far)(lambda: step(True))
    pl.when(jnp.logical_not(is_far))(lambda: step(False))

    @pl.when(j == pl.num_programs(2) - 1)
    def _():
        lp = lam_ref[...]
        lam = (jnp.exp(jnp.sum(lp[0:1] * lp[1:2], axis=-1, keepdims=True))
               - jnp.exp(jnp.sum(lp[2:3] * lp[3:4], axis=-1, keepdims=True)) + lam_init)
        a0 = acc_ref[0]
        a1 = acc_ref[1]
        o0 = a0[:DIFF_DV] / a0[DIFF_DV:DIFF_DV + 1]
        o1 = a1[:DIFF_DV] / a1[DIFF_DV:DIFF_DV + 1]
        att = o0 - lam * o1
        y = att * lax.rsqrt(jnp.mean(att * att, axis=0, keepdims=True) + RMS_EPS)
        o_ref[...] = y.T * g_ref[...] * (1.0 - lam_init)


def _diff_band(t):
    return -(-(T5_MAX_DIST - 1) // t)


def _diff_attn(cqt, ck, cvt, bias_tiles, lam_p, subln_g, lam_init, t, qc):
    s = ck.shape[0]
    nb = _diff_band(t) + 1
    w = 2 * HEAD_DIM
    return pl.pallas_call(
        functools.partial(_diff_attn_kernel, lam_init=lam_init, qc=qc, band=nb - 1),
        grid=(DIFF_HEADS, s // t, s // t),
        in_specs=[pl.BlockSpec((w, t), lambda h, i, j: (h, i)),
                  pl.BlockSpec((t, w), lambda h, i, j: (j, h)),
                  pl.BlockSpec((DIFF_VT, t), lambda h, i, j: (h, j)),
                  pl.BlockSpec((1, 1, t, t), lambda h, i, j: (jnp.clip(j - i, -nb, nb) + nb, h, 0, 0)),
                  pl.BlockSpec((4, HEAD_DIM), lambda h, i, j: (0, 0)),
                  pl.BlockSpec((1, DIFF_DV), lambda h, i, j: (0, 0))],
        out_specs=pl.BlockSpec((t, DIFF_DV), lambda h, i, j: (i, h)),
        out_shape=jax.ShapeDtypeStruct((s, DIFF_HEADS * DIFF_DV), F32),
        scratch_shapes=[pltpu.VMEM((w, t), BF16), pltpu.VMEM((w, t), BF16),
                        pltpu.VMEM((2, 1, t), F32), pltpu.VMEM((2, DIFF_VT, t), F32),
                        pltpu.VMEM((2, t, qc), F32)],
        compiler_params=_cparams(("parallel", "parallel", "arbitrary")),
        name="diff_attn",
    )(cqt, ck, cvt, bias_tiles, lam_p, subln_g.reshape(1, DIFF_DV))


def _dil_kernel(q_ref, kp_ref, kc_ref, kn_ref, vp_ref, vc_ref, vn_ref, bias_ref, o_ref, lse_ref,
                *, tq, m_len):
    n = pl.program_id(1)
    q = q_ref[...]
    k = jnp.concatenate([kp_ref[...], kc_ref[...], kn_ref[...]], axis=0)
    v = jnp.concatenate([vp_ref[...], vc_ref[...], vn_ref[...]], axis=0)
    tk = tq + 2 * DIL_HALF
    a = lax.broadcasted_iota(jnp.int32, (tq, tk), 0)
    c = lax.broadcasted_iota(jnp.int32, (tq, tk), 1)
    delta = c - DIL_HALF - a
    kpos = n * tq - DIL_HALF + c
    valid = jnp.where(jnp.abs(delta) <= DIL_HALF, 1, 0) * jnp.where(kpos >= 0, 1, 0) * jnp.where(kpos < m_len, 1, 0)
    lane = lax.broadcasted_iota(jnp.int32, q.shape, 1)
    o = jnp.zeros(q.shape, F32)
    lse_o = jnp.zeros(q.shape, F32)
    for h in range(DIL_HEADS):
        hm = (lane // HEAD_DIM) == h
        s = _dot_nt(jnp.where(hm, q, jnp.zeros_like(q)), k) + bias_ref[0, h]
        s = jnp.where(valid > 0, s, -1e30)
        m = jnp.max(s, axis=-1, keepdims=True)
        e = jnp.exp(s - m)
        l = jnp.sum(e, axis=-1, keepdims=True)
        oh = _dot((e / l).astype(BF16), v)
        o = jnp.where(hm, oh, o)
        lse_o = jnp.where(hm, m + jnp.log(l), lse_o)
    o_ref[...] = o
    lse_ref[...] = lse_o


def _dil_attn(bq, bk, bv, bias, gi, dil, tq):
    s = bq.shape[0]
    m_len = s // dil
    tq = min(tq, m_len)
    gw = DIL_HEADS * HEAD_DIM
    ncol = DIL_GROUPS
    view = lambda x: x.reshape(m_len, dil * DIL_GROUPS * gw)
    hb = tq // DIL_HALF
    last = m_len // DIL_HALF - 1
    col = lambda r: r * ncol + gi
    qspec = pl.BlockSpec((tq, gw), lambda r, n: (n, col(r)))
    pspec = pl.BlockSpec((DIL_HALF, gw), lambda r, n: (jnp.maximum(n * hb - 1, 0), col(r)))
    nspec = pl.BlockSpec((DIL_HALF, gw), lambda r, n: (jnp.minimum((n + 1) * hb, last), col(r)))
    ospec = pl.BlockSpec((tq, gw), lambda r, n: (n, r))
    o, lse = pl.pallas_call(
        functools.partial(_dil_kernel, tq=tq, m_len=m_len),
        grid=(dil, m_len // tq),
        in_specs=[qspec, pspec, qspec, nspec, pspec, qspec, nspec,
                  pl.BlockSpec((1, DIL_HEADS, tq, tq + 2 * DIL_HALF), lambda r, n: (gi, 0, 0, 0))],
        out_specs=[ospec, ospec],
        out_shape=[jax.ShapeDtypeStruct((m_len, dil * gw), F32)] * 2,
        compiler_params=_cparams(("parallel", "parallel")),
        name=f"dil_attn_g{gi}",
    )(view(bq), view(bk), view(bk), view(bk), view(bv), view(bv), view(bv), bias)
    return o.reshape(s, gw), lse.reshape(s, gw)


def _gla_kernel(*refs, reverse, tb, finalize):
    if finalize:
        (q_ref, k_ref, v_ref, lr_ref, wg_ref, gb_ref, tri_ref, ofwd_ref, r_ref, ng_ref,
         o_ref, s_ref, oacc_ref) = refs
    else:
        q_ref, k_ref, v_ref, lr_ref, wg_ref, gb_ref, tri_ref, o_ref, s_ref = refs
        oacc_ref = o_ref
    cw = GLA_CHUNK
    qk = GLA_HEADS * GLA_DK
    vw = GLA_HEADS * GLA_DV

    @pl.when(pl.program_id(0) == 0)
    def _():
        s_ref[...] = jnp.zeros(s_ref.shape, F32)

    logits = _dot(lr_ref[...].astype(BF16), wg_ref[...]) + gb_ref[...]
    g = (jnp.minimum(logits, 0.0) - jnp.log(1.0 + jnp.exp(-jnp.abs(logits)))) * (1.0 / GLA_TAU)
    ghi, glo = _split(g)
    tri = tri_ref[...]
    b = _dot(tri, ghi) + _dot(tri, glo)
    qg = (q_ref[...] * (GLA_DK ** -0.5) * jnp.exp(b)).astype(BF16)
    k = k_ref[...]
    kg = (k * jnp.exp(-b)).astype(BF16)
    v = v_ref[...].astype(BF16)

    ones = jnp.ones((cw, LANES), BF16)
    lane_q = lax.broadcasted_iota(jnp.int32, (cw, qk), 1)
    rr = lax.broadcasted_iota(jnp.int32, (GLA_HEADS * cw, cw), 0)
    cc = lax.broadcasted_iota(jnp.int32, (GLA_HEADS * cw, cw), 1)
    tt = rr % cw
    amask = (cc > tt) if reverse else (cc <= tt)
    srow = lax.broadcasted_iota(jnp.int32, (qk, vw), 0) // GLA_DK
    scol = lax.broadcasted_iota(jnp.int32, (qk, vw), 1) // GLA_DV
    bdmask = srow == scol

    n_chunks = tb // cw
    order = range(n_chunks - 1, -1, -1) if reverse else range(n_chunks)
    for ci in order:
        rows = slice(ci * cw, (ci + 1) * cw)
        bc = b[rows]
        b_end = bc[0:1] if reverse else bc[cw - 1:cw]
        kdec = (k[rows] * jnp.exp(b_end - bc)).astype(BF16)
        btot = _dot_tn(ghi[rows], ones) + _dot_tn(glo[rows], ones)
        qg_c = qg[rows]
        qs = jnp.concatenate(
            [jnp.where((lane_q // GLA_DK) == h, qg_c, jnp.zeros_like(qg_c)) for h in range(GLA_HEADS)], axis=0)
        a = jnp.where(amask, _dot_nt(qs, kg[rows]), 0.0)
        obig = _dot(a.astype(BF16), v[rows])
        o_intra = jnp.concatenate(
            [obig[h * cw:(h + 1) * cw, h * GLA_DV:(h + 1) * GLA_DV] for h in range(GLA_HEADS)], axis=1)
        state = s_ref[...]
        oacc_ref[rows, :] = o_intra + _dot(qg_c, state.astype(BF16))
        ds = _dot_tn(kdec, v[rows])
        decay = jnp.exp(btot)
        decay = jnp.concatenate([decay] * GLA_HEADS, axis=1)
        s_ref[...] = jnp.where(bdmask, decay * state + ds, 0.0)

    if finalize:
        o = ofwd_ref[...] + oacc_ref[...]
        r = r_ref[...]
        outs = []
        for h in range(GLA_HEADS):
            sl = slice(h * GLA_DV, (h + 1) * GLA_DV)
            outs.append(_rms(o[:, sl]) * ng_ref[...] * (r[:, sl] * _sigmoid(r[:, sl])))
        o_ref[...] = jnp.concatenate(outs, axis=1)


def _chunk_tri(tb, reverse):
    i = np.arange(tb)
    same = (i[:, None] // GLA_CHUNK) == (i[None, :] // GLA_CHUNK)
    tri = (i[None, :] >= i[:, None]) if reverse else (i[None, :] <= i[:, None])
    return jnp.asarray(same & tri, dtype=BF16)


def _gla_scan(z, wg, gb, tb, reverse, fin=None):
    s = z.shape[0]
    nb = s // tb
    qk = GLA_HEADS * GLA_DK
    vw = GLA_HEADS * GLA_DV
    blk = (lambda i: nb - 1 - i) if reverse else (lambda i: i)
    zspec = lambda w, off: pl.BlockSpec((tb, w), lambda i: (blk(i), off // w))
    cspec = lambda r, c: pl.BlockSpec((r, c), lambda i: (0, 0))
    in_specs = [zspec(qk, Z_AQ), zspec(qk, Z_AK), zspec(vw, Z_AV), zspec(LANES, Z_ALR),
                cspec(LANES, qk), cspec(1, qk), cspec(tb, tb)]
    args = [z, z, z, z, wg, gb, _chunk_tri(tb, reverse)]
    scratch = [pltpu.VMEM((qk, vw), F32)]
    if fin is not None:
        o_fwd, norm_g = fin
        in_specs += [pl.BlockSpec((tb, vw), lambda i: (blk(i), 0)), zspec(vw, Z_AR), cspec(1, GLA_DV)]
        args += [o_fwd, z, norm_g.reshape(1, GLA_DV)]
        scratch.append(pltpu.VMEM((tb, vw), F32))
    return pl.pallas_call(
        functools.partial(_gla_kernel, reverse=reverse, tb=tb, finalize=fin is not None),
        grid=(nb,),
        in_specs=in_specs,
        out_specs=pl.BlockSpec((tb, vw), lambda i: (blk(i), 0)),
        out_shape=jax.ShapeDtypeStruct((s, vw), F32),
        scratch_shapes=scratch,
        compiler_params=_cparams(("arbitrary",)),
        name="gla_bwd" if reverse else "gla_fwd",
    )(*args)


def _hy_pre_kernel(u_ref, up_ref, un_ref, w_ref, b_ref, x0_ref, z_ref, zb_ref, *, tm):
    i = pl.program_id(0)
    u = u_ref[...]
    row = lax.broadcasted_iota(jnp.int32, u.shape, 0)
    prev_row = jnp.where(i == 0, 0.0, up_ref[7:8, :])
    next_row = jnp.where(i == pl.num_programs(0) - 1, 0.0, un_ref[0:1, :])
    u_prev = jnp.where(row == 0, prev_row, pltpu.roll(u, 1, axis=0))
    u_next = jnp.where(row == tm - 1, next_row, pltpu.roll(u, tm - 1, axis=0))
    y = b_ref[...] + u_prev * w_ref[0:1] + u * w_ref[1:2] + u_next * w_ref[2:3]
    x0_ref[...] = y[:, :HY_WIDTH]
    z = y[:, HY_WIDTH:2 * HY_WIDTH] * y[:, 2 * HY_WIDTH:]
    z_ref[...] = z
    zb_ref[...] = z.astype(BF16)


def _hy_pre(z, conv_w, conv_b, tm):
    s = z.shape[0]
    w = 3 * HY_WIDTH
    nr = s // 8
    ospec = pl.BlockSpec((tm, HY_WIDTH), lambda i: (i, 0))
    return pl.pallas_call(
        functools.partial(_hy_pre_kernel, tm=tm),
        grid=(s // tm,),
        in_specs=[pl.BlockSpec((tm, w), lambda i: (i, 0)),
                  pl.BlockSpec((8, w), lambda i: (jnp.maximum(i * (tm // 8) - 1, 0), 0)),
                  pl.BlockSpec((8, w), lambda i: (jnp.minimum((i + 1) * (tm // 8), nr - 1), 0)),
                  pl.BlockSpec((3, w), lambda i: (0, 0)),
                  pl.BlockSpec((1, w), lambda i: (0, 0))],
        out_specs=[ospec, ospec, ospec],
        out_shape=[jax.ShapeDtypeStruct((s, HY_WIDTH), F32)] * 2 + [jax.ShapeDtypeStruct((s, HY_WIDTH), BF16)],
        compiler_params=_cparams(("parallel",)),
        name="hyena_pre",
    )(z, z, z, conv_w, conv_b.reshape(1, w))


def _hy_filter_kernel(emb_ref, w1_ref, b1_ref, f1_ref, w2_ref, b2_ref, f2_ref, w3_ref, dl_ref,
                      k_ref, kb_ref, norm_ref, *, tl, seq):
    i = pl.program_id(0)
    emb = emb_ref[...]
    h = jnp.sin(f1_ref[...] * (_dot3(emb, w1_ref[...]) + b1_ref[...]))
    h = jnp.sin(f2_ref[...] * (_dot3(h, w2_ref[...]) + b2_ref[...]))
    h = _dot3(h, w3_ref[...])
    back = i * tl >= seq
    h = jnp.where(back, h[:, HY_WIDTH:], h[:, :HY_WIDTH])
    h = h * jnp.exp(-emb[:, 0:1] * dl_ref[...])
    row = i * tl + lax.broadcasted_iota(jnp.int32, h.shape, 0)
    h = jnp.where(row == seq, 0.0, h)
    k_ref[...] = h
    kb_ref[...] = h.astype(BF16)

    @pl.when(i == 0)
    def _():
        norm_ref[...] = jnp.zeros(norm_ref.shape, F32)

    norm_ref[...] += jnp.sum(jnp.abs(h), axis=0, keepdims=True)


def _hy_positions(seq):
    t = jnp.linspace(0.0, 1.0, seq, dtype=F32)[:, None]
    bands = (HY_EMB - 1) // 2
    freqs = jnp.linspace(1e-4, bands - 1, bands, dtype=F32)[None]
    w = 2.0 * math.pi * jnp.arange(seq, dtype=F32)[:, None] / seq
    zf = jnp.concatenate([t, jnp.cos(freqs * w), -jnp.sin(freqs * w)], axis=-1)
    pos = np.concatenate([np.arange(seq), [0], np.arange(seq - 1, 0, -1)])
    emb = zf[pos]
    return jnp.pad(emb, ((0, 0), (0, LANES - HY_EMB)))


def _hy_filter(emb, w1, b1, fr1, w2, b2, fr2, w3, tl):
    n = emb.shape[0]
    seq = n // 2
    pad2 = lambda a, r, c: jnp.pad(a, ((0, r - a.shape[0]), (0, c - a.shape[1])))
    min_decay = math.log(HY_DECAY_TARGET) / HY_SLOW_DECAY
    max_decay = math.log(HY_DECAY_TARGET) / HY_FAST_DECAY
    deltas = jnp.abs(jnp.linspace(min_decay, max_decay, HY_WIDTH, dtype=F32))[None]
    cs = lambda r, c: pl.BlockSpec((r, c), lambda i: (0, 0))
    ospec = pl.BlockSpec((tl, HY_WIDTH), lambda i: (i, 0))
    return pl.pallas_call(
        functools.partial(_hy_filter_kernel, tl=tl, seq=seq),
        grid=(n // tl,),
        in_specs=[pl.BlockSpec((tl, LANES), lambda i: (i, 0)),
                  cs(LANES, LANES), cs(1, LANES), cs(1, LANES),
                  cs(LANES, LANES), cs(1, LANES), cs(1, LANES),
                  cs(LANES, 2 * HY_WIDTH), cs(1, HY_WIDTH)],
        out_specs=[ospec, ospec, pl.BlockSpec((1, HY_WIDTH), lambda i: (0, 0))],
        out_shape=[jax.ShapeDtypeStruct((n, HY_WIDTH), F32), jax.ShapeDtypeStruct((n, HY_WIDTH), BF16),
                   jax.ShapeDtypeStruct((1, HY_WIDTH), F32)],
        compiler_params=_cparams(("arbitrary",)),
        name="hyena_filter",
    )(emb, pad2(w1, LANES, LANES), pad2(b1[None], 1, LANES), pad2(fr1[None], 1, LANES),
      pad2(w2, LANES, LANES), pad2(b2[None], 1, LANES), pad2(fr2[None], 1, LANES),
      pad2(w3, LANES, 2 * HY_WIDTH), deltas)


def _dft_tables(n1):
    n = n1 * DFT_N2
    a = jnp.arange(n1, dtype=jnp.int32)
    ang1 = (2.0 * math.pi / n1) * ((a[:, None] * a[None, :]) % n1).astype(F32)
    c1, s1 = jnp.cos(ang1), jnp.sin(ang1)
    k1 = jnp.arange(n1, dtype=jnp.int32)[:, None, None]
    k2 = jnp.arange(DFT_N2, dtype=jnp.int32)[None, :, None]
    n2 = jnp.arange(DFT_N2, dtype=jnp.int32)[None, None, :]
    ang = (2.0 * math.pi / n) * ((n2 * (k1 + n1 * k2)) % n).astype(F32)
    gc, gs = jnp.cos(ang), jnp.sin(ang)
    tb = lambda x: x.astype(BF16)
    return dict(c1=tb(c1), s1=tb(s1), gc=tb(gc), gs=tb(gs),
                gct=tb(jnp.swapaxes(gc, 1, 2)), gst=tb(jnp.swapaxes(gs, 1, 2)))


def _dft1_kernel(c_ref, s_ref, x_ref, re_ref, im_ref):
    x = x_ref[...]
    re_ref[...] = _dot(c_ref[...], x).astype(BF16)
    im_ref[...] = (-_dot(s_ref[...], x)).astype(BF16)


def _dft1(x, c1, s1, cb):
    k1, w = x.shape
    n1 = c1.shape[0]
    ospec = pl.BlockSpec((n1, cb), lambda j: (0, j))
    return pl.pallas_call(
        _dft1_kernel,
        grid=(w // cb,),
        in_specs=[pl.BlockSpec((n1, k1), lambda j: (0, 0)), pl.BlockSpec((n1, k1), lambda j: (0, 0)),
                  pl.BlockSpec((k1, cb), lambda j: (0, j))],
        out_specs=[ospec, ospec],
        out_shape=[jax.ShapeDtypeStruct((n1, w), BF16)] * 2,
        compiler_params=_cparams(("parallel",)),
        name="dft_stage1",
    )(c1, s1, x)


def _dft2_filter_kernel(gc_ref, gs_ref, are_ref, aim_ref, norm_ref, kre_ref, kim_ref, *, kb):
    inv = 1.0 / norm_ref[...]
    for t in range(kb):
        gc, gs = gc_ref[t], gs_ref[t]
        are, aim = are_ref[t], aim_ref[t]
        kre_ref[t] = (_dot(gc, are) + _dot(gs, aim)) * inv
        kim_ref[t] = (_dot(gc, aim) - _dot(gs, are)) * inv


def _dft2_filter(tabs, are, aim, norm, kb):
    n1, _, c = are.shape
    gspec = pl.BlockSpec((kb, DFT_N2, DFT_N2), lambda i: (i, 0, 0))
    aspec = pl.BlockSpec((kb, DFT_N2, c), lambda i: (i, 0, 0))
    return pl.pallas_call(
        functools.partial(_dft2_filter_kernel, kb=kb),
        grid=(n1 // kb,),
        in_specs=[gspec, gspec, aspec, aspec, pl.BlockSpec((1, c), lambda i: (0, 0))],
        out_specs=[aspec, aspec],
        out_shape=[jax.ShapeDtypeStruct((n1, DFT_N2, c), F32)] * 2,
        compiler_params=_cparams(("parallel",)),
        name="dft_stage2_filter",
    )(tabs["gc"], tabs["gs"], are, aim, norm)


def _conv_mid_kernel(gc_ref, gs_ref, gct_ref, gst_ref, are_ref, aim_ref, kre_ref, kim_ref,
                     bre_ref, bim_ref, *, kb):
    for t in range(kb):
        gc, gs = gc_ref[t], gs_ref[t]
        are, aim = are_ref[t], aim_ref[t]
        xre = _dot(gc, are) + _dot(gs, aim)
        xim = _dot(gc, aim) - _dot(gs, are)
        kre, kim = kre_ref[t], kim_ref[t]
        yre = (xre * kre - xim * kim).astype(BF16)
        yim = (xre * kim + xim * kre).astype(BF16)
        gct, gst = gct_ref[t], gst_ref[t]
        bre_ref[t] = (_dot(gct, yre) - _dot(gst, yim)).astype(BF16)
        bim_ref[t] = (_dot(gct, yim) + _dot(gst, yre)).astype(BF16)


def _conv_mid(tabs, are, aim, kre, kim, kb):
    n1, _, c = are.shape
    gspec = pl.BlockSpec((kb, DFT_N2, DFT_N2), lambda i: (i, 0, 0))
    aspec = pl.BlockSpec((kb, DFT_N2, c), lambda i: (i, 0, 0))
    return pl.pallas_call(
        functools.partial(_conv_mid_kernel, kb=kb),
        grid=(n1 // kb,),
        in_specs=[gspec, gspec, gspec, gspec, aspec, aspec, aspec, aspec],
        out_specs=[aspec, aspec],
        out_shape=[jax.ShapeDtypeStruct((n1, DFT_N2, c), BF16)] * 2,
        compiler_params=_cparams(("parallel",)),
        name="conv_spectral",
    )(tabs["gc"], tabs["gs"], tabs["gct"], tabs["gst"], are, aim, kre, kim)


def _idft1_kernel(c_ref, s_ref, bre_ref, bim_ref, z_ref, x0_ref, skip_ref, o_ref, *, inv_n):
    y = (_dot(c_ref[...], bre_ref[...]) - _dot(s_ref[...], bim_ref[...])) * inv_n
    o_ref[...] = x0_ref[...] * (y + z_ref[...] * skip_ref[...])


def _idft1(c1h, s1h, bre, bim, z, x0, skip_t, cb):
    ko, n1 = c1h.shape
    w = bre.shape[1]
    n = n1 * DFT_N2
    ospec = pl.BlockSpec((ko, cb), lambda j: (0, j))
    return pl.pallas_call(
        functools.partial(_idft1_kernel, inv_n=1.0 / n),
        grid=(w // cb,),
        in_specs=[pl.BlockSpec((ko, n1), lambda j: (0, 0)), pl.BlockSpec((ko, n1), lambda j: (0, 0)),
                  pl.BlockSpec((n1, cb), lambda j: (0, j)), pl.BlockSpec((n1, cb), lambda j: (0, j)),
                  ospec, ospec, pl.BlockSpec((1, cb), lambda j: (0, 0))],
        out_specs=ospec,
        out_shape=jax.ShapeDtypeStruct((ko, w), F32),
        compiler_params=_cparams(("parallel",)),
        name="idft_stage1",
    )(c1h, s1h, bre, bim, z, x0, skip_t)


def _hyena(z_all, tabs, emb, p, i, cb, kb):
    seq = z_all.shape[0]
    n1 = 2 * seq // DFT_N2
    c = HY_WIDTH
    x0, zf, zb = _hy_pre(z_all, p["hy_conv_w"][i], p["hy_conv_b"][i], min(512, seq))
    _, kern_b, norm = _hy_filter(emb, p["hy_w1"][i], p["hy_b1"][i], p["hy_freq1"][i], p["hy_w2"][i],
                                 p["hy_b2"][i], p["hy_freq2"][i], p["hy_w3"][i], min(1024, seq))
    wide = DFT_N2 * c
    kre, kim = _dft1(kern_b.reshape(n1, wide), tabs["c1"], tabs["s1"], cb)
    kre, kim = _dft2_filter(tabs, kre.reshape(n1, DFT_N2, c), kim.reshape(n1, DFT_N2, c), norm, kb)
    are, aim = _dft1(zb.reshape(n1 // 2, wide), tabs["c1"][:, :n1 // 2], tabs["s1"][:, :n1 // 2], cb)
    bre, bim = _conv_mid(tabs, are.reshape(n1, DFT_N2, c), aim.reshape(n1, DFT_N2, c), kre, kim, kb)
    skip_t = jnp.tile(p["hy_skip"][i][None], (1, cb // c))
    yd = _idft1(tabs["c1"][:n1 // 2], tabs["s1"][:n1 // 2], bre.reshape(n1, wide), bim.reshape(n1, wide),
                zf.reshape(n1 // 2, wide), x0.reshape(n1 // 2, wide), skip_t, cb)
    return yd.reshape(seq, c)


def _combine_kernel(x_ref, ya_ref, o0_ref, o1_ref, o2_ref, l0_ref, l1_ref, l2_ref, yc_ref, yd_ref,
                    g0_ref, g1_ref, g2_ref, g3_ref, pa_ref, pb_ref, pc_ref, pd_ref, wo_ref, o_ref):
    l0, l1, l2 = l0_ref[...], l1_ref[...], l2_ref[...]
    mx = jnp.maximum(jnp.maximum(l0, l1), l2)
    e0, e1, e2 = jnp.exp(l0 - mx), jnp.exp(l1 - mx), jnp.exp(l2 - mx)
    yb = (e0 * o0_ref[...] + e1 * o1_ref[...] + e2 * o2_ref[...]) / (e0 + e1 + e2)
    m = (_sigmoid(g0_ref[...]) * _dot(ya_ref[...].astype(BF16), pa_ref[...])
         + _sigmoid(g1_ref[...]) * _dot(yb.astype(BF16), pb_ref[...])
         + _sigmoid(g2_ref[...]) * _dot(yc_ref[...].astype(BF16), pc_ref[...])
         + _sigmoid(g3_ref[...]) * _dot(yd_ref[...].astype(BF16), pd_ref[...]))
    o_ref[...] = x_ref[...] + _dot(m.astype(BF16), wo_ref[...])


def _combine(x, z, ya, dil_outs, yc, yd, pa, pb, pc, pd, wo, tm):
    s, d = x.shape
    rs = lambda w: pl.BlockSpec((tm, w), lambda i: (i, 0))
    gs = lambda b: pl.BlockSpec((tm, d), lambda i: (i, Z_GATE // d + b))
    ws = lambda a: pl.BlockSpec(a.shape, lambda i: (0, 0))
    (o0, l0), (o1, l1), (o2, l2) = dil_outs
    gw = DIL_HEADS * HEAD_DIM
    return pl.pallas_call(
        _combine_kernel,
        grid=(s // tm,),
        in_specs=[rs(d), rs(ya.shape[1]), rs(gw), rs(gw), rs(gw), rs(gw), rs(gw), rs(gw),
                  rs(yc.shape[1]), rs(yd.shape[1]), gs(0), gs(1), gs(2), gs(3),
                  ws(pa), ws(pb), ws(pc), ws(pd), ws(wo)],
        out_specs=rs(d),
        out_shape=jax.ShapeDtypeStruct((s, d), F32),
        compiler_params=_cparams(("parallel",)),
        name="combine",
    )(x, ya, o0, o1, o2, l0, l1, l2, yc, yd, z, z, z, z, pa, pb, pc, pd, wo)


def _mlp_kernel(x_ref, g_ref, w1_ref, w2_ref, o_ref, h_ref, acc_ref):
    j = pl.program_id(1)

    @pl.when(j == 0)
    def _():
        h_ref[...] = (_rms(x_ref[...]) * g_ref[...]).astype(BF16)
        acc_ref[...] = jnp.zeros(acc_ref.shape, F32)

    a = jnp.maximum(_dot(h_ref[...], w1_ref[...]), 0.0)
    acc_ref[...] += _dot((a * a).astype(BF16), w2_ref[...])

    @pl.when(j == pl.num_programs(1) - 1)
    def _():
        o_ref[...] = x_ref[...] + acc_ref[...]


def _mlp(x, g, w1, w2, tm, tf):
    s, d = x.shape
    ff = w1.shape[1]
    return pl.pallas_call(
        _mlp_kernel,
        grid=(s // tm, ff // tf),
        in_specs=[pl.BlockSpec((tm, d), lambda i, j: (i, 0)),
                  pl.BlockSpec((1, d), lambda i, j: (0, 0)),
                  pl.BlockSpec((d, tf), lambda i, j: (0, j)),
                  pl.BlockSpec((tf, d), lambda i, j: (j, 0))],
        out_specs=pl.BlockSpec((tm, d), lambda i, j: (i, 0)),
        out_shape=jax.ShapeDtypeStruct((s, d), F32),
        scratch_shapes=[pltpu.VMEM((tm, d), BF16), pltpu.VMEM((tm, d), F32)],
        compiler_params=_cparams(("parallel", "arbitrary")),
        name="mlp",
    )(x, g.reshape(1, d), w1, w2)


def _permute_w_in(w):
    d = w.shape[0]
    pieces = [w[:, _O_DU:_O_GATE], w[:, _O_B:_O_C], w[:, _O_AQ:_O_AK], w[:, _O_C:_O_DU],
              w[:, _O_AV:_O_AR], w[:, _O_AR:_O_ALR], w[:, _O_AK:_O_AV], w[:, _O_ALR:_O_B],
              jnp.zeros((d, Z_GATE - Z_ALR - 2 * GLA_RANK), w.dtype), w[:, _O_GATE:_O_END]]
    return jnp.concatenate(pieces, axis=1).astype(BF16)


def _dil_bias_idx(tq):
    a = jnp.arange(tq)[:, None]
    c = jnp.arange(tq + 2 * DIL_HALF)[None, :]
    delta = c - DIL_HALF - a
    return jnp.stack([_t5_bucket(delta * dil) for _, dil in DIL_PATTERNS]).astype(jnp.int32)


def _diff_bias_idx(t):
    nb = _diff_band(t) + 1
    kk = jnp.arange(t)[:, None]
    qq = jnp.arange(t)[None, :]
    return jnp.stack([_t5_bucket(o * t + kk - qq) for o in range(-nb, nb + 1)]).astype(jnp.int32)


def _forward(x, p, *, t_diff, qc_diff, tq_dil, tb_gla, tm_proj, tn_proj, tm_row, tm_mlp, tf_mlp, cb_dft, kb_dft):
    seq = x.shape[0]
    depth = p["w_in"].shape[0]
    n_dil_bias = DIL_GROUPS * DIL_HEADS
    tq_dil = min(tq_dil, seq // DIL_PATTERNS[-1][1])
    dil_bias = _bias_tiles(p["t5_bias"], _dil_bias_idx(tq_dil), DIL_HEADS, 0, 1, DIL_HEADS)
    diff_bias = _bias_tiles(p["t5_bias"], _diff_bias_idx(t_diff), DIFF_HEADS, n_dil_bias, 2 ** 30, 0, LOG2E)
    tabs = _dft_tables(2 * seq // DFT_N2)
    emb = _hy_positions(seq)
    rep = lambda g, n: jnp.tile(g, n)
    for i in range(depth):
        z = _norm_matmul(x, p["norm1_g"][i], _permute_w_in(p["w_in"][i]), tm_proj, tn_proj)
        wg = [jnp.zeros((LANES, GLA_HEADS * GLA_DK), F32).at[j * GLA_RANK:(j + 1) * GLA_RANK].set(
            p["gla_gate_w"][i, j]).astype(BF16) for j in range(2)]
        gb = p["gla_gate_b"][i]
        o_fwd = _gla_scan(z, wg[0], gb[0:1], tb_gla, False)
        ya = _gla_scan(z, wg[1], gb[1:2], tb_gla, True, fin=(o_fwd, p["gla_norm_g"][i]))
        bq, bk, bv, cq, ck, cv = _prep(
            z, jnp.repeat(p["dil_qnorm_g"][i], DIL_HEADS, axis=0).reshape(-1),
            jnp.repeat(p["dil_knorm_g"][i], DIL_HEADS, axis=0).reshape(-1),
            rep(p["diff_qnorm_g"][i], 2 * DIFF_HEADS), rep(p["diff_knorm_g"][i], 2 * DIFF_HEADS), tm_row)
        dil_outs = [_dil_attn(bq, bk, bv, dil_bias, gi, dil, tq_dil)
                    for gi, (_, dil) in enumerate(DIL_PATTERNS)]
        lam_init = 0.8 - 0.6 * math.exp(-0.3 * i)
        yc = _diff_attn(cq, ck, cv, diff_bias, p["diff_lambda"][i], p["diff_subln_g"][i], lam_init,
                        t_diff, qc_diff)
        yd = _hyena(z, tabs, emb, p, i, cb_dft, kb_dft)
        x = _combine(x, z, ya, dil_outs, yc, yd, p["proj_a"][i].astype(BF16), p["proj_b"][i].astype(BF16),
                     p["proj_c"][i].astype(BF16), p["proj_d"][i].astype(BF16), p["w_out"][i].astype(BF16),
                     tm_row)
        x = _mlp(x, p["norm2_g"][i], p["mlp_w1"][i].astype(BF16), p["mlp_w2"][i].astype(BF16), tm_mlp, tf_mlp)
    return x


def kernel(x, t5_bias, norm1_g, w_in, gla_gate_w, gla_gate_b, gla_norm_g, dil_qnorm_g, dil_knorm_g,
           diff_qnorm_g, diff_knorm_g, diff_lambda, diff_subln_g, hy_conv_w, hy_conv_b, hy_w1, hy_b1,
           hy_freq1, hy_w2, hy_b2, hy_freq2, hy_w3, hy_skip, proj_a, proj_b, proj_c, proj_d, w_out,
           norm2_g, mlp_w1, mlp_w2):
    p = dict(t5_bias=t5_bias, norm1_g=norm1_g, w_in=w_in, gla_gate_w=gla_gate_w, gla_gate_b=gla_gate_b,
             gla_norm_g=gla_norm_g, dil_qnorm_g=dil_qnorm_g, dil_knorm_g=dil_knorm_g,
             diff_qnorm_g=diff_qnorm_g, diff_knorm_g=diff_knorm_g, diff_lambda=diff_lambda,
             diff_subln_g=diff_subln_g, hy_conv_w=hy_conv_w, hy_conv_b=hy_conv_b, hy_w1=hy_w1, hy_b1=hy_b1,
             hy_freq1=hy_freq1, hy_w2=hy_w2, hy_b2=hy_b2, hy_freq2=hy_freq2, hy_w3=hy_w3, hy_skip=hy_skip,
             proj_a=proj_a, proj_b=proj_b, proj_c=proj_c, proj_d=proj_d, w_out=w_out, norm2_g=norm2_g,
             mlp_w1=mlp_w1, mlp_w2=mlp_w2)
    b, s, d = x.shape
    outs = [_forward(x[bi], p, t_diff=1024, qc_diff=256, tq_dil=256, tb_gla=512, tm_proj=1024, tn_proj=512, tm_row=512,
                     tm_mlp=1024, tf_mlp=1024, cb_dft=4096, kb_dft=8) for bi in range(b)]
    return jnp.stack(outs)
```

```python
import functools
import math

import jax
import jax.numpy as jnp
import numpy as np
from jax import lax
from jax.experimental import pallas as pl
from jax.experimental.pallas import tpu as pltpu

F32 = jnp.float32
BF16 = jnp.bfloat16

D_MODEL = 1024
HEAD_DIM = 64
GLA_HEADS = 4
GLA_DK = 64
GLA_DV = 128
GLA_RANK = 16
GLA_TAU = 16.0
GLA_CHUNK = 64
DIL_PATTERNS = ((128, 1), (512, 4), (2048, 16))
DIL_GROUPS = 3
DIL_HEADS = 4
DIL_HALF = 64
DIFF_HEADS = 4
DIFF_DV = 128
DIFF_VT = DIFF_DV + 16
HY_WIDTH = 512
HY_EMB = 33
HY_FFN = 64
HY_DECAY_TARGET = 1e-2
HY_FAST_DECAY = 0.3
HY_SLOW_DECAY = 1.5
T5_BUCKETS = 32
T5_MAX_DIST = 1024
N_BIAS_HEADS = 16
D_FF = 4096
RMS_EPS = 1e-6
LOG2E = math.log2(math.e)

LANES = 128
HALO_ROWS = 16
VMEM_LIMIT = 48 * 1024 * 1024

Z_DU = 0
Z_BQ, Z_BK, Z_BV = 1536, 2304, 3072
Z_AQ = 3840
Z_CQ, Z_CK, Z_CV = 4096, 4608, 5120
Z_AV, Z_AR = 5632, 6144
Z_AK = 6656
Z_ALR = 6912
Z_GATE = 7168
Z_COLS = 11264

_O_AQ, _O_AK, _O_AV, _O_AR, _O_ALR, _O_B, _O_C, _O_DU, _O_GATE, _O_END = (
    0, 256, 512, 1024, 1536, 1568, 3872, 5408, 6944, 11040)

DFT_N2 = 128


def _cparams(sem):
    return pltpu.CompilerParams(dimension_semantics=sem, vmem_limit_bytes=VMEM_LIMIT)


def _dot(a, b):
    return jnp.dot(a, b, preferred_element_type=F32)


def _dot_nt(a, b):
    return lax.dot_general(a, b, (((1,), (1,)), ((), ())), preferred_element_type=F32)


def _dot_tn(a, b):
    return lax.dot_general(a, b, (((0,), (0,)), ((), ())), preferred_element_type=F32)


def _split(x):
    hi = x.astype(BF16)
    lo = (x - hi.astype(F32)).astype(BF16)
    return hi, lo


def _dot3(a, b):
    ah, al = _split(a)
    bh, bl = _split(b)
    return _dot(ah, bh) + _dot(ah, bl) + _dot(al, bh)


def _rms(x):
    return x * lax.rsqrt(jnp.mean(x * x, axis=-1, keepdims=True) + RMS_EPS)


def _sigmoid(x):
    return 1.0 / (1.0 + jnp.exp(-x))


def _norm_matmul_kernel(x_ref, g_ref, w_ref, o_ref, h_ref):
    @pl.when(pl.program_id(1) == 0)
    def _():
        h_ref[...] = (_rms(x_ref[...]) * g_ref[...]).astype(BF16)

    o_ref[...] = _dot(h_ref[...], w_ref[...]).astype(o_ref.dtype)


def _norm_matmul(x, g, w, tm, tn):
    s, d = x.shape
    n = w.shape[1]
    return pl.pallas_call(
        _norm_matmul_kernel,
        grid=(s // tm, n // tn),
        in_specs=[pl.BlockSpec((tm, d), lambda i, j: (i, 0)),
                  pl.BlockSpec((1, d), lambda i, j: (0, 0)),
                  pl.BlockSpec((d, tn), lambda i, j: (0, j))],
        out_specs=pl.BlockSpec((tm, tn), lambda i, j: (i, j)),
        out_shape=jax.ShapeDtypeStruct((s, n), BF16),
        scratch_shapes=[pltpu.VMEM((tm, d), BF16)],
        compiler_params=_cparams(("parallel", "arbitrary")),
        name="in_proj",
    )(x, g.reshape(1, d), w)


def _group_norm(x, e, gain):
    hi, lo = _split(x * x)
    ms = (_dot(hi, e) + _dot(lo, e)) * (1.0 / HEAD_DIM)
    return x * lax.rsqrt(ms + RMS_EPS) * gain


def _prep_kernel(bq_ref, bk_ref, bv_ref, cq_ref, ck_ref, cv_ref, eb_ref, ec_ref,
                 gbq_ref, gbk_ref, gcq_ref, gck_ref,
                 obq_ref, obk_ref, obv_ref, ocq_ref, ock_ref, ocv_ref):
    scale = HEAD_DIM ** -0.5
    eb = eb_ref[...]
    ec = ec_ref[...]
    f32 = lambda ref: ref[...].astype(F32)
    obq_ref[...] = (_group_norm(f32(bq_ref), eb, gbq_ref[...]) * scale).astype(BF16)
    obk_ref[...] = _group_norm(f32(bk_ref), eb, gbk_ref[...]).astype(BF16)
    obv_ref[...] = bv_ref[...]
    ocq_ref[...] = (_group_norm(f32(cq_ref), ec, gcq_ref[...]) * (scale * LOG2E)).T.astype(BF16)
    ock_ref[...] = _group_norm(f32(ck_ref), ec, gck_ref[...]).astype(BF16)
    cvt = f32(cv_ref).T
    ones = jnp.ones((DIFF_VT - DIFF_DV, cvt.shape[1]), F32)
    ocv_ref[...] = jnp.concatenate(
        [t for h in range(DIFF_HEADS) for t in (cvt[h * DIFF_DV:(h + 1) * DIFF_DV], ones)], axis=0).astype(BF16)


def _block_diag_ones(width):
    idx = np.arange(width) // HEAD_DIM
    return jnp.asarray(idx[:, None] == idx[None, :], dtype=BF16)


def _prep(z, gbq, gbk, gcq, gck, tm):
    s = z.shape[0]
    wb, wc = DIL_GROUPS * DIL_HEADS * HEAD_DIM, DIFF_HEADS * 2 * HEAD_DIM
    zspec = lambda w, off: pl.BlockSpec((tm, w), lambda i: (i, off // w))
    cspec = lambda r, c: pl.BlockSpec((r, c), lambda i: (0, 0))
    ospec = lambda w: pl.BlockSpec((tm, w), lambda i: (i, 0))
    tspec = lambda w: pl.BlockSpec((w, tm), lambda i: (0, i))
    return pl.pallas_call(
        _prep_kernel,
        grid=(s // tm,),
        in_specs=[zspec(wb, Z_BQ), zspec(wb, Z_BK), zspec(wb, Z_BV),
                  zspec(wc, Z_CQ), zspec(wc, Z_CK), zspec(wc, Z_CV),
                  cspec(wb, wb), cspec(wc, wc),
                  cspec(1, wb), cspec(1, wb), cspec(1, wc), cspec(1, wc)],
        out_specs=[ospec(wb), ospec(wb), ospec(wb), tspec(wc), ospec(wc), tspec(DIFF_HEADS * DIFF_VT)],
        out_shape=[jax.ShapeDtypeStruct((s, wb), BF16)] * 3
        + [jax.ShapeDtypeStruct((wc, s), BF16), jax.ShapeDtypeStruct((s, wc), BF16),
           jax.ShapeDtypeStruct((DIFF_HEADS * DIFF_VT, s), BF16)],
        compiler_params=_cparams(("parallel",)),
        name="qk_prep",
    )(z, z, z, z, z, z, _block_diag_ones(wb), _block_diag_ones(wc),
      gbq.reshape(1, wb), gbk.reshape(1, wb), gcq.reshape(1, wc), gck.reshape(1, wc))


def _t5_bucket(rel):
    half = T5_BUCKETS // 2
    max_exact = half // 2
    ret = np.where(rel > 0, half, 0)
    n = np.abs(rel)
    nf = np.maximum(n, 1).astype(np.float64)
    large = max_exact + (np.log(nf / max_exact) / math.log(T5_MAX_DIST / max_exact)
                         * (half - max_exact)).astype(np.int64)
    large = np.minimum(large, half - 1)
    return ret + np.where(n < max_exact, n, large)


def _bias_kernel(tab_ref, idx_ref, o_ref, *, head_base, heads_per_tile_group, out_scale):
    t = pl.program_id(0)
    h = pl.program_id(1)
    col = head_base + (t // heads_per_tile_group[0]) * heads_per_tile_group[1] + h
    idx = idx_ref[0]
    acc = jnp.zeros(idx.shape, F32)
    for b in range(T5_BUCKETS):
        acc = jnp.where(idx == b, tab_ref[b, col], acc)
    o_ref[0, 0] = acc * out_scale


def _bias_tiles(t5_bias, idx, n_heads, head_base, tiles_per_group, heads_step, out_scale=1.0):
    nt, r, c = idx.shape
    return pl.pallas_call(
        functools.partial(_bias_kernel, head_base=head_base,
                          heads_per_tile_group=(tiles_per_group, heads_step), out_scale=out_scale),
        grid=(nt, n_heads),
        in_specs=[pl.BlockSpec(memory_space=pltpu.SMEM),
                  pl.BlockSpec((1, r, c), lambda t, h: (t, 0, 0))],
        out_specs=pl.BlockSpec((1, 1, r, c), lambda t, h: (t, h, 0, 0)),
        out_shape=jax.ShapeDtypeStruct((nt, n_heads, r, c), F32),
        compiler_params=_cparams(("parallel", "parallel")),
        name="t5_bias_tiles",
    )(t5_bias, idx)


def _diff_attn_kernel(qt_ref, k_ref, vt_ref, bias_ref, lam_ref, g_ref, o_ref,
                      qa_ref, qb_ref, m_ref, acc_ref, s_ref, *, lam_init, qc, band):
    j = pl.program_id(2)
    t = qt_ref.shape[1]

    @pl.when(j == 0)
    def _():
        qt = qt_ref[...]
        row = lax.broadcasted_iota(jnp.int32, qt.shape, 0)
        qa_ref[...] = jnp.where(row < HEAD_DIM, qt, jnp.zeros_like(qt))
        qb_ref[...] = jnp.where(row >= HEAD_DIM, qt, jnp.zeros_like(qt))
        m_ref[...] = jnp.full(m_ref.shape, -jnp.inf, F32)
        acc_ref[...] = jnp.zeros(acc_ref.shape, F32)

    chains = [(c, slice(u * qc, (u + 1) * qc)) for c in range(2) for u in range(t // qc)]
    q_refs = (qa_ref, qb_ref)

    def step(far):
        k = k_ref[...]
        vt = vt_ref[...]
        if far:
            const = bias_ref[0, 0, 0:1, 0:1]
            scores = lambda c, cols: _dot(k, q_refs[c][:, cols])
        else:
            const = 0.0
            scores = lambda c, cols: _dot(k, q_refs[c][:, cols]) + bias_ref[0, 0, :, cols]
        s_ref[0] = scores(*chains[0])
        for n, (c, cols) in enumerate(chains):
            if n + 1 < len(chains):
                s_ref[(n + 1) % 2] = scores(*chains[n + 1])
            m_old = m_ref[c, :, cols]
            m_new = jnp.maximum(m_old, jnp.max(s_ref[n % 2], axis=0, keepdims=True) + const)
            alpha = jnp.exp2(m_old - m_new)
            p = jnp.exp2(s_ref[n % 2] - (m_new - const)).astype(BF16)
            acc_ref[c, :, cols] = alpha * acc_ref[c, :, cols] + _dot(vt, p)
            m_ref[c, :, cols] = m_new

    is_far = jnp.abs(j - pl.program_id(1)) > band
    pl.when(is_far)(lambda: step(True))
    pl.when(jnp.logical_not(is_far))(lambda: step(False))

    @pl.when(j == pl.num_programs(2) - 1)
    def _():
        lp = lam_ref[...]
        lam = (jnp.exp(jnp.sum(lp[0:1] * lp[1:2], axis=-1, keepdims=True))
               - jnp.exp(jnp.sum(lp[2:3] * lp[3:4], axis=-1, keepdims=True)) + lam_init)
        a0 = acc_ref[0]
        a1 = acc_ref[1]
        o0 = a0[:DIFF_DV] / a0[DIFF_DV:DIFF_DV + 1]
        o1 = a1[:DIFF_DV] / a1[DIFF_DV:DIFF_DV + 1]
        att = o0 - lam * o1
        y = att * lax.rsqrt(jnp.mean(att * att, axis=0, keepdims=True) + RMS_EPS)
        o_ref[...] = y.T * g_ref[...] * (1.0 - lam_init)


def _diff_band(t):
    return -(-(T5_MAX_DIST - 1) // t)


def _diff_attn(cqt, ck, cvt, bias_tiles, lam_p, subln_g, lam_init, t, qc):
    s = ck.shape[0]
    nb = _diff_band(t) + 1
    w = 2 * HEAD_DIM
    return pl.pallas_call(
        functools.partial(_diff_attn_kernel, lam_init=lam_init, qc=qc, band=nb - 1),
        grid=(DIFF_HEADS, s // t, s // t),
        in_specs=[pl.BlockSpec((w, t), lambda h, i, j: (h, i)),
                  pl.BlockSpec((t, w), lambda h, i, j: (j, h)),
                  pl.BlockSpec((DIFF_VT, t), lambda h, i, j: (h, j)),
                  pl.BlockSpec((1, 1, t, t), lambda h, i, j: (jnp.clip(j - i, -nb, nb) + nb, h, 0, 0)),
                  pl.BlockSpec((4, HEAD_DIM), lambda h, i, j: (0, 0)),
                  pl.BlockSpec((1, DIFF_DV), lambda h, i, j: (0, 0))],
        out_specs=pl.BlockSpec((t, DIFF_DV), lambda h, i, j: (i, h)),
        out_shape=jax.ShapeDtypeStruct((s, DIFF_HEADS * DIFF_DV), F32),
        scratch_shapes=[pltpu.VMEM((w, t), BF16), pltpu.VMEM((w, t), BF16),
                        pltpu.VMEM((2, 1, t), F32), pltpu.VMEM((2, DIFF_VT, t), F32),
                        pltpu.VMEM((2, t, qc), F32)],
        compiler_params=_cparams(("parallel", "parallel", "arbitrary")),
        name="diff_attn",
    )(cqt, ck, cvt, bias_tiles, lam_p, subln_g.reshape(1, DIFF_DV))


def _dil_kernel(q_ref, kp_ref, kc_ref, kn_ref, vp_ref, vc_ref, vn_ref, bias_ref, o_ref, lse_ref,
                qs_ref, ks_ref, vs_ref, os_ref, ls_ref, *, tq, dil, m_len):
    n = pl.program_id(0)
    hd = DIL_HALF * dil
    body_rows = tq * dil
    halves = DIL_HEADS * HEAD_DIM // LANES

    def put(dst, rows, src_ref):
        x = src_ref[...].astype(F32)
        for t in range(halves):
            dst[t, rows, :] = x[:, t * LANES:(t + 1) * LANES]

    def strided(src, r, count):
        return jnp.concatenate([src[t, pl.ds(r, count, stride=dil), :] for t in range(halves)], axis=1)

    put(qs_ref, slice(0, body_rows), q_ref)
    for dst, (p_ref, c_ref, n_ref) in ((ks_ref, (kp_ref, kc_ref, kn_ref)), (vs_ref, (vp_ref, vc_ref, vn_ref))):
        put(dst, slice(0, hd), p_ref)
        put(dst, slice(hd, hd + body_rows), c_ref)
        put(dst, slice(hd + body_rows, 2 * hd + body_rows), n_ref)
    tk = tq + 2 * DIL_HALF
    a = lax.broadcasted_iota(jnp.int32, (tq, tk), 0)
    c = lax.broadcasted_iota(jnp.int32, (tq, tk), 1)
    delta = c - DIL_HALF - a
    kpos = n * tq - DIL_HALF + c
    valid = jnp.where(jnp.abs(delta) <= DIL_HALF, 1, 0) * jnp.where(kpos >= 0, 1, 0) * jnp.where(kpos < m_len, 1, 0)
    lane = lax.broadcasted_iota(jnp.int32, (tq, DIL_HEADS * HEAD_DIM), 1)

    def one_subsequence(r, carry):
        q = strided(qs_ref, r, tq).astype(BF16)
        k = strided(ks_ref, r, tk).astype(BF16)
        v = strided(vs_ref, r, tk).astype(BF16)
        o = jnp.zeros(q.shape, F32)
        lse_o = jnp.zeros(q.shape, F32)
        for h in range(DIL_HEADS):
            hm = (lane // HEAD_DIM) == h
            s = _dot_nt(jnp.where(hm, q, jnp.zeros_like(q)), k) + bias_ref[0, h]
            s = jnp.where(valid > 0, s, -1e30)
            m = jnp.max(s, axis=-1, keepdims=True)
            e = jnp.exp(s - m)
            l = jnp.sum(e, axis=-1, keepdims=True)
            oh = _dot((e / l).astype(BF16), v)
            o = jnp.where(hm, oh, o)
            lse_o = jnp.where(hm, m + jnp.log(l), lse_o)
        for t in range(halves):
            os_ref[t, pl.ds(r, tq, stride=dil), :] = o[:, t * LANES:(t + 1) * LANES]
            ls_ref[t, pl.ds(r, tq, stride=dil), :] = lse_o[:, t * LANES:(t + 1) * LANES]
        return carry

    lax.fori_loop(0, dil, one_subsequence, 0)
    o_ref[...] = jnp.concatenate([os_ref[t] for t in range(halves)], axis=1)
    lse_ref[...] = jnp.concatenate([ls_ref[t] for t in range(halves)], axis=1)


def _dil_attn(bq, bk, bv, bias, gi, dil, tq):
    s = bq.shape[0]
    m_len = s // dil
    gw = DIL_HEADS * HEAD_DIM
    rows = tq * dil
    hd = DIL_HALF * dil
    hb = rows // hd
    last = s // hd - 1
    qspec = pl.BlockSpec((rows, gw), lambda n: (n, gi))
    pspec = pl.BlockSpec((hd, gw), lambda n: (jnp.maximum(n * hb - 1, 0), gi))
    nspec = pl.BlockSpec((hd, gw), lambda n: (jnp.minimum((n + 1) * hb, last), gi))
    ospec = pl.BlockSpec((rows, gw), lambda n: (n, 0))
    return pl.pallas_call(
        functools.partial(_dil_kernel, tq=tq, dil=dil, m_len=m_len),
        grid=(s // rows,),
        in_specs=[qspec, pspec, qspec, nspec, pspec, qspec, nspec,
                  pl.BlockSpec((1, DIL_HEADS, tq, tq + 2 * DIL_HALF), lambda n: (0, 0, 0, 0))],
        out_specs=[ospec, ospec],
        out_shape=[jax.ShapeDtypeStruct((s, gw), F32)] * 2,
        scratch_shapes=[pltpu.VMEM((gw // LANES, rows, LANES), F32),
                        pltpu.VMEM((gw // LANES, rows + 2 * hd, LANES), F32),
                        pltpu.VMEM((gw // LANES, rows + 2 * hd, LANES), F32),
                        pltpu.VMEM((gw // LANES, rows, LANES), F32),
                        pltpu.VMEM((gw // LANES, rows, LANES), F32)],
        compiler_params=_cparams(("parallel",)),
        name=f"dil_attn_g{gi}",
    )(bq, bk, bk, bk, bv, bv, bv, bias)


def _gla_kernel(*refs, reverse, tb, finalize):
    if finalize:
        (q_ref, k_ref, v_ref, lr_ref, wg_ref, gb_ref, tri_ref, ofwd_ref, r_ref, ng_ref,
         o_ref, s_ref, oacc_ref) = refs
    else:
        q_ref, k_ref, v_ref, lr_ref, wg_ref, gb_ref, tri_ref, o_ref, s_ref = refs
        oacc_ref = o_ref
    cw = GLA_CHUNK
    qk = GLA_HEADS * GLA_DK
    vw = GLA_HEADS * GLA_DV

    @pl.when(pl.program_id(0) == 0)
    def _():
        s_ref[...] = jnp.zeros(s_ref.shape, F32)

    logits = _dot(lr_ref[...], wg_ref[...]) + gb_ref[...]
    g = (jnp.minimum(logits, 0.0) - jnp.log(1.0 + jnp.exp(-jnp.abs(logits)))) * (1.0 / GLA_TAU)
    ghi, glo = _split(g)
    tri = tri_ref[...]
    b = _dot(tri, ghi) + _dot(tri, glo)
    qg = (q_ref[...].astype(F32) * (GLA_DK ** -0.5) * jnp.exp(b)).astype(BF16)
    k = k_ref[...].astype(F32)
    kg = (k * jnp.exp(-b)).astype(BF16)
    v = v_ref[...].astype(BF16)

    ones = jnp.ones((cw, LANES), BF16)
    lane_q = lax.broadcasted_iota(jnp.int32, (cw, qk), 1)
    rr = lax.broadcasted_iota(jnp.int32, (GLA_HEADS * cw, cw), 0)
    cc = lax.broadcasted_iota(jnp.int32, (GLA_HEADS * cw, cw), 1)
    tt = rr % cw
    amask = (cc > tt) if reverse else (cc <= tt)
    srow = lax.broadcasted_iota(jnp.int32, (qk, vw), 0) // GLA_DK
    scol = lax.broadcasted_iota(jnp.int32, (qk, vw), 1) // GLA_DV
    bdmask = srow == scol

    n_chunks = tb // cw
    order = range(n_chunks - 1, -1, -1) if reverse else range(n_chunks)
    for ci in order:
        rows = slice(ci * cw, (ci + 1) * cw)
        bc = b[rows]
        b_end = bc[0:1] if reverse else bc[cw - 1:cw]
        kdec = (k[rows] * jnp.exp(b_end - bc)).astype(BF16)
        btot = _dot_tn(ghi[rows], ones) + _dot_tn(glo[rows], ones)
        qg_c = qg[rows]
        qs = jnp.concatenate(
            [jnp.where((lane_q // GLA_DK) == h, qg_c, jnp.zeros_like(qg_c)) for h in range(GLA_HEADS)], axis=0)
        a = jnp.where(amask, _dot_nt(qs, kg[rows]), 0.0)
        obig = _dot(a.astype(BF16), v[rows])
        o_intra = jnp.concatenate(
            [obig[h * cw:(h + 1) * cw, h * GLA_DV:(h + 1) * GLA_DV] for h in range(GLA_HEADS)], axis=1)
        state = s_ref[...]
        oacc_ref[rows, :] = o_intra + _dot(qg_c, state.astype(BF16))
        ds = _dot_tn(kdec, v[rows])
        decay = jnp.exp(btot)
        decay = jnp.concatenate([decay] * GLA_HEADS, axis=1)
        s_ref[...] = jnp.where(bdmask, decay * state + ds, 0.0)

    if finalize:
        o = ofwd_ref[...] + oacc_ref[...]
        r = r_ref[...].astype(F32)
        outs = []
        for h in range(GLA_HEADS):
            sl = slice(h * GLA_DV, (h + 1) * GLA_DV)
            outs.append(_rms(o[:, sl]) * ng_ref[...] * (r[:, sl] * _sigmoid(r[:, sl])))
        o_ref[...] = jnp.concatenate(outs, axis=1)


def _chunk_tri(tb, reverse):
    i = np.arange(tb)
    same = (i[:, None] // GLA_CHUNK) == (i[None, :] // GLA_CHUNK)
    tri = (i[None, :] >= i[:, None]) if reverse else (i[None, :] <= i[:, None])
    return jnp.asarray(same & tri, dtype=BF16)


def _gla_scan(z, wg, gb, tb, reverse, fin=None):
    s = z.shape[0]
    nb = s // tb
    qk = GLA_HEADS * GLA_DK
    vw = GLA_HEADS * GLA_DV
    blk = (lambda i: nb - 1 - i) if reverse else (lambda i: i)
    zspec = lambda w, off: pl.BlockSpec((tb, w), lambda i: (blk(i), off // w))
    cspec = lambda r, c: pl.BlockSpec((r, c), lambda i: (0, 0))
    in_specs = [zspec(qk, Z_AQ), zspec(qk, Z_AK), zspec(vw, Z_AV), zspec(LANES, Z_ALR),
                cspec(LANES, qk), cspec(1, qk), cspec(tb, tb)]
    args = [z, z, z, z, wg, gb, _chunk_tri(tb, reverse)]
    scratch = [pltpu.VMEM((qk, vw), F32)]
    if fin is not None:
        o_fwd, norm_g = fin
        in_specs += [pl.BlockSpec((tb, vw), lambda i: (blk(i), 0)), zspec(vw, Z_AR), cspec(1, GLA_DV)]
        args += [o_fwd, z, norm_g.reshape(1, GLA_DV)]
        scratch.append(pltpu.VMEM((tb, vw), F32))
    return pl.pallas_call(
        functools.partial(_gla_kernel, reverse=reverse, tb=tb, finalize=fin is not None),
        grid=(nb,),
        in_specs=in_specs,
        out_specs=pl.BlockSpec((tb, vw), lambda i: (blk(i), 0)),
        out_shape=jax.ShapeDtypeStruct((s, vw), F32),
        scratch_shapes=scratch,
        compiler_params=_cparams(("arbitrary",)),
        name="gla_bwd" if reverse else "gla_fwd",
    )(*args)


def _hy_pre_kernel(u_ref, up_ref, un_ref, w_ref, b_ref, x0_ref, z_ref, zb_ref, *, tm):
    i = pl.program_id(0)
    u = u_ref[...].astype(F32)
    row = lax.broadcasted_iota(jnp.int32, u.shape, 0)
    prev_row = jnp.where(i == 0, 0.0, up_ref[...].astype(F32)[HALO_ROWS - 1:HALO_ROWS, :])
    next_row = jnp.where(i == pl.num_programs(0) - 1, 0.0, un_ref[...].astype(F32)[0:1, :])
    u_prev = jnp.where(row == 0, prev_row, pltpu.roll(u, 1, axis=0))
    u_next = jnp.where(row == tm - 1, next_row, pltpu.roll(u, tm - 1, axis=0))
    y = b_ref[...] + u_prev * w_ref[0:1] + u * w_ref[1:2] + u_next * w_ref[2:3]
    x0_ref[...] = y[:, :HY_WIDTH]
    z = y[:, HY_WIDTH:2 * HY_WIDTH] * y[:, 2 * HY_WIDTH:]
    z_ref[...] = z
    zb_ref[...] = z.astype(BF16)


def _hy_pre(z, conv_w, conv_b, tm):
    s = z.shape[0]
    w = 3 * HY_WIDTH
    nr = s // HALO_ROWS
    ospec = pl.BlockSpec((tm, HY_WIDTH), lambda i: (i, 0))
    return pl.pallas_call(
        functools.partial(_hy_pre_kernel, tm=tm),
        grid=(s // tm,),
        in_specs=[pl.BlockSpec((tm, w), lambda i: (i, 0)),
                  pl.BlockSpec((HALO_ROWS, w), lambda i: (jnp.maximum(i * (tm // HALO_ROWS) - 1, 0), 0)),
                  pl.BlockSpec((HALO_ROWS, w), lambda i: (jnp.minimum((i + 1) * (tm // HALO_ROWS), nr - 1), 0)),
                  pl.BlockSpec((3, w), lambda i: (0, 0)),
                  pl.BlockSpec((1, w), lambda i: (0, 0))],
        out_specs=[ospec, ospec, ospec],
        out_shape=[jax.ShapeDtypeStruct((s, HY_WIDTH), F32)] * 2 + [jax.ShapeDtypeStruct((s, HY_WIDTH), BF16)],
        compiler_params=_cparams(("parallel",)),
        name="hyena_pre",
    )(z, z, z, conv_w, conv_b.reshape(1, w))


def _hy_filter_kernel(emb_ref, w1_ref, b1_ref, f1_ref, w2_ref, b2_ref, f2_ref, w3_ref, dl_ref,
                      k_ref, kb_ref, norm_ref, *, tl, seq):
    i = pl.program_id(0)
    emb = emb_ref[...]
    h = jnp.sin(f1_ref[...] * (_dot3(emb, w1_ref[...]) + b1_ref[...]))
    h = jnp.sin(f2_ref[...] * (_dot3(h, w2_ref[...]) + b2_ref[...]))
    h = _dot3(h, w3_ref[...])
    back = i * tl >= seq
    h = jnp.where(back, h[:, HY_WIDTH:], h[:, :HY_WIDTH])
    h = h * jnp.exp(-emb[:, 0:1] * dl_ref[...])
    row = i * tl + lax.broadcasted_iota(jnp.int32, h.shape, 0)
    h = jnp.where(row == seq, 0.0, h)
    k_ref[...] = h
    kb_ref[...] = h.astype(BF16)

    @pl.when(i == 0)
    def _():
        norm_ref[...] = jnp.zeros(norm_ref.shape, F32)

    norm_ref[...] += jnp.sum(jnp.abs(h), axis=0, keepdims=True)


def _hy_positions(seq):
    t = np.linspace(0.0, 1.0, seq, dtype=np.float32)[:, None]
    bands = (HY_EMB - 1) // 2
    freqs = np.linspace(1e-4, bands - 1, bands, dtype=np.float32)[None]
    w = (np.float32(2.0 * math.pi) * np.arange(seq, dtype=np.float32)[:, None] / np.float32(seq))
    zf = np.concatenate([t, np.cos(freqs * w), -np.sin(freqs * w)], axis=-1).astype(np.float32)
    pos = np.concatenate([np.arange(seq), [0], np.arange(seq - 1, 0, -1)])
    return jnp.asarray(np.pad(zf[pos], ((0, 0), (0, LANES - HY_EMB))))


def _hy_filter(emb, w1, b1, fr1, w2, b2, fr2, w3, tl):
    n = emb.shape[0]
    seq = n // 2
    pad2 = lambda a, r, c: jnp.pad(a, ((0, r - a.shape[0]), (0, c - a.shape[1])))
    min_decay = math.log(HY_DECAY_TARGET) / HY_SLOW_DECAY
    max_decay = math.log(HY_DECAY_TARGET) / HY_FAST_DECAY
    deltas = jnp.abs(jnp.linspace(min_decay, max_decay, HY_WIDTH, dtype=F32))[None]
    cs = lambda r, c: pl.BlockSpec((r, c), lambda i: (0, 0))
    ospec = pl.BlockSpec((tl, HY_WIDTH), lambda i: (i, 0))
    return pl.pallas_call(
        functools.partial(_hy_filter_kernel, tl=tl, seq=seq),
        grid=(n // tl,),
        in_specs=[pl.BlockSpec((tl, LANES), lambda i: (i, 0)),
                  cs(LANES, LANES), cs(1, LANES), cs(1, LANES),
                  cs(LANES, LANES), cs(1, LANES), cs(1, LANES),
                  cs(LANES, 2 * HY_WIDTH), cs(1, HY_WIDTH)],
        out_specs=[ospec, ospec, pl.BlockSpec((1, HY_WIDTH), lambda i: (0, 0))],
        out_shape=[jax.ShapeDtypeStruct((n, HY_WIDTH), F32), jax.ShapeDtypeStruct((n, HY_WIDTH), BF16),
                   jax.ShapeDtypeStruct((1, HY_WIDTH), F32)],
        compiler_params=_cparams(("arbitrary",)),
        name="hyena_filter",
    )(emb, pad2(w1, LANES, LANES), pad2(b1[None], 1, LANES), pad2(fr1[None], 1, LANES),
      pad2(w2, LANES, LANES), pad2(b2[None], 1, LANES), pad2(fr2[None], 1, LANES),
      pad2(w3, LANES, 2 * HY_WIDTH), deltas)


def _dft_tables(n1):
    n = n1 * DFT_N2
    a = np.arange(n1, dtype=np.int64)
    ang1 = (2.0 * math.pi / n1) * ((a[:, None] * a[None, :]) % n1)
    c1, s1 = np.cos(ang1), np.sin(ang1)
    k1 = np.arange(n1, dtype=np.int64)[:, None, None]
    k2 = np.arange(DFT_N2, dtype=np.int64)[None, :, None]
    n2 = np.arange(DFT_N2, dtype=np.int64)[None, None, :]
    ang = (2.0 * math.pi / n) * ((n2 * (k1 + n1 * k2)) % n)
    gc, gs = np.cos(ang), np.sin(ang)
    tb = lambda x: jnp.asarray(np.ascontiguousarray(x).astype(BF16))
    return dict(c1=tb(c1), s1=tb(s1), gc=tb(gc), gs=tb(gs),
                gct=tb(np.swapaxes(gc, 1, 2)), gst=tb(np.swapaxes(gs, 1, 2)))


def _dft1_kernel(c_ref, s_ref, x_ref, re_ref, im_ref):
    x = x_ref[...]
    re_ref[...] = _dot(c_ref[...], x).astype(BF16)
    im_ref[...] = (-_dot(s_ref[...], x)).astype(BF16)


def _dft1(x, c1, s1, cb):
    k1, w = x.shape
    n1 = c1.shape[0]
    ospec = pl.BlockSpec((n1, cb), lambda j: (0, j))
    return pl.pallas_call(
        _dft1_kernel,
        grid=(w // cb,),
        in_specs=[pl.BlockSpec((n1, k1), lambda j: (0, 0)), pl.BlockSpec((n1, k1), lambda j: (0, 0)),
                  pl.BlockSpec((k1, cb), lambda j: (0, j))],
        out_specs=[ospec, ospec],
        out_shape=[jax.ShapeDtypeStruct((n1, w), BF16)] * 2,
        compiler_params=_cparams(("parallel",)),
        name="dft_stage1",
    )(c1, s1, x)


def _dft2_filter_kernel(gc_ref, gs_ref, are_ref, aim_ref, norm_ref, kre_ref, kim_ref, *, kb):
    inv = 1.0 / norm_ref[...]
    for t in range(kb):
        gc, gs = gc_ref[t], gs_ref[t]
        are, aim = are_ref[t], aim_ref[t]
        kre_ref[t] = (_dot(gc, are) + _dot(gs, aim)) * inv
        kim_ref[t] = (_dot(gc, aim) - _dot(gs, are)) * inv


def _dft2_filter(tabs, are, aim, norm, kb):
    n1, _, c = are.shape
    gspec = pl.BlockSpec((kb, DFT_N2, DFT_N2), lambda i: (i, 0, 0))
    aspec = pl.BlockSpec((kb, DFT_N2, c), lambda i: (i, 0, 0))
    return pl.pallas_call(
        functools.partial(_dft2_filter_kernel, kb=kb),
        grid=(n1 // kb,),
        in_specs=[gspec, gspec, aspec, aspec, pl.BlockSpec((1, c), lambda i: (0, 0))],
        out_specs=[aspec, aspec],
        out_shape=[jax.ShapeDtypeStruct((n1, DFT_N2, c), F32)] * 2,
        compiler_params=_cparams(("parallel",)),
        name="dft_stage2_filter",
    )(tabs["gc"], tabs["gs"], are, aim, norm)


def _conv_mid_kernel(gc_ref, gs_ref, gct_ref, gst_ref, are_ref, aim_ref, kre_ref, kim_ref,
                     bre_ref, bim_ref, *, kb):
    for t in range(kb):
        gc, gs = gc_ref[t], gs_ref[t]
        are, aim = are_ref[t], aim_ref[t]
        xre = _dot(gc, are) + _dot(gs, aim)
        xim = _dot(gc, aim) - _dot(gs, are)
        kre, kim = kre_ref[t], kim_ref[t]
        yre = (xre * kre - xim * kim).astype(BF16)
        yim = (xre * kim + xim * kre).astype(BF16)
        gct, gst = gct_ref[t], gst_ref[t]
        bre_ref[t] = (_dot(gct, yre) - _dot(gst, yim)).astype(BF16)
        bim_ref[t] = (_dot(gct, yim) + _dot(gst, yre)).astype(BF16)


def _conv_mid(tabs, are, aim, kre, kim, kb):
    n1, _, c = are.shape
    gspec = pl.BlockSpec((kb, DFT_N2, DFT_N2), lambda i: (i, 0, 0))
    aspec = pl.BlockSpec((kb, DFT_N2, c), lambda i: (i, 0, 0))
    return pl.pallas_call(
        functools.partial(_conv_mid_kernel, kb=kb),
        grid=(n1 // kb,),
        in_specs=[gspec, gspec, gspec, gspec, aspec, aspec, aspec, aspec],
        out_specs=[aspec, aspec],
        out_shape=[jax.ShapeDtypeStruct((n1, DFT_N2, c), BF16)] * 2,
        compiler_params=_cparams(("parallel",)),
        name="conv_spectral",
    )(tabs["gc"], tabs["gs"], tabs["gct"], tabs["gst"], are, aim, kre, kim)


def _idft1_kernel(c_ref, s_ref, bre_ref, bim_ref, z_ref, x0_ref, skip_ref, o_ref, *, inv_n):
    y = (_dot(c_ref[...], bre_ref[...]) - _dot(s_ref[...], bim_ref[...])) * inv_n
    o_ref[...] = x0_ref[...] * (y + z_ref[...] * skip_ref[...])


def _idft1(c1h, s1h, bre, bim, z, x0, skip_t, cb):
    ko, n1 = c1h.shape
    w = bre.shape[1]
    n = n1 * DFT_N2
    ospec = pl.BlockSpec((ko, cb), lambda j: (0, j))
    return pl.pallas_call(
        functools.partial(_idft1_kernel, inv_n=1.0 / n),
        grid=(w // cb,),
        in_specs=[pl.BlockSpec((ko, n1), lambda j: (0, 0)), pl.BlockSpec((ko, n1), lambda j: (0, 0)),
                  pl.BlockSpec((n1, cb), lambda j: (0, j)), pl.BlockSpec((n1, cb), lambda j: (0, j)),
                  ospec, ospec, pl.BlockSpec((1, cb), lambda j: (0, 0))],
        out_specs=ospec,
        out_shape=jax.ShapeDtypeStruct((ko, w), F32),
        compiler_params=_cparams(("parallel",)),
        name="idft_stage1",
    )(c1h, s1h, bre, bim, z, x0, skip_t)


def _hyena(z_all, tabs, emb, p, i, cb, kb):
    seq = z_all.shape[0]
    n1 = 2 * seq // DFT_N2
    c = HY_WIDTH
    x0, zf, zb = _hy_pre(z_all, p["hy_conv_w"][i], p["hy_conv_b"][i], min(512, seq))
    _, kern_b, norm = _hy_filter(emb, p["hy_w1"][i], p["hy_b1"][i], p["hy_freq1"][i], p["hy_w2"][i],
                                 p["hy_b2"][i], p["hy_freq2"][i], p["hy_w3"][i], min(1024, seq))
    wide = DFT_N2 * c
    kre, kim = _dft1(kern_b.reshape(n1, wide), tabs["c1"], tabs["s1"], cb)
    kre, kim = _dft2_filter(tabs, kre.reshape(n1, DFT_N2, c), kim.reshape(n1, DFT_N2, c), norm, kb)
    are, aim = _dft1(zb.reshape(n1 // 2, wide), tabs["c1"][:, :n1 // 2], tabs["s1"][:, :n1 // 2], cb)
    bre, bim = _conv_mid(tabs, are.reshape(n1, DFT_N2, c), aim.reshape(n1, DFT_N2, c), kre, kim, kb)
    skip_t = jnp.tile(p["hy_skip"][i][None], (1, cb // c))
    yd = _idft1(tabs["c1"][:n1 // 2], tabs["s1"][:n1 // 2], bre.reshape(n1, wide), bim.reshape(n1, wide),
                zf.reshape(n1 // 2, wide), x0.reshape(n1 // 2, wide), skip_t, cb)
    return yd.reshape(seq, c)


def _combine_kernel(x_ref, ya_ref, o0_ref, o1_ref, o2_ref, l0_ref, l1_ref, l2_ref, yc_ref, yd_ref,
                    g0_ref, g1_ref, g2_ref, g3_ref, pa_ref, pb_ref, pc_ref, pd_ref, wo_ref, o_ref):
    l0, l1, l2 = l0_ref[...], l1_ref[...], l2_ref[...]
    mx = jnp.maximum(jnp.maximum(l0, l1), l2)
    e0, e1, e2 = jnp.exp(l0 - mx), jnp.exp(l1 - mx), jnp.exp(l2 - mx)
    yb = (e0 * o0_ref[...] + e1 * o1_ref[...] + e2 * o2_ref[...]) / (e0 + e1 + e2)
    gate = lambda ref: _sigmoid(ref[...].astype(F32))
    m = (gate(g0_ref) * _dot(ya_ref[...].astype(BF16), pa_ref[...])
         + gate(g1_ref) * _dot(yb.astype(BF16), pb_ref[...])
         + gate(g2_ref) * _dot(yc_ref[...].astype(BF16), pc_ref[...])
         + gate(g3_ref) * _dot(yd_ref[...].astype(BF16), pd_ref[...]))
    o_ref[...] = x_ref[...] + _dot(m.astype(BF16), wo_ref[...])


def _combine(x, z, ya, dil_outs, yc, yd, pa, pb, pc, pd, wo, tm):
    s, d = x.shape
    rs = lambda w: pl.BlockSpec((tm, w), lambda i: (i, 0))
    gs = lambda b: pl.BlockSpec((tm, d), lambda i: (i, Z_GATE // d + b))
    ws = lambda a: pl.BlockSpec(a.shape, lambda i: (0, 0))
    (o0, l0), (o1, l1), (o2, l2) = dil_outs
    gw = DIL_HEADS * HEAD_DIM
    return pl.pallas_call(
        _combine_kernel,
        grid=(s // tm,),
        in_specs=[rs(d), rs(ya.shape[1]), rs(gw), rs(gw), rs(gw), rs(gw), rs(gw), rs(gw),
                  rs(yc.shape[1]), rs(yd.shape[1]), gs(0), gs(1), gs(2), gs(3),
                  ws(pa), ws(pb), ws(pc), ws(pd), ws(wo)],
        out_specs=rs(d),
        out_shape=jax.ShapeDtypeStruct((s, d), F32),
        compiler_params=_cparams(("parallel",)),
        name="combine",
    )(x, ya, o0, o1, o2, l0, l1, l2, yc, yd, z, z, z, z, pa, pb, pc, pd, wo)


def _mlp_kernel(x_ref, g_ref, w1_ref, w2_ref, o_ref, h_ref, acc_ref):
    j = pl.program_id(1)

    @pl.when(j == 0)
    def _():
        h_ref[...] = (_rms(x_ref[...]) * g_ref[...]).astype(BF16)
        acc_ref[...] = jnp.zeros(acc_ref.shape, F32)

    a = jnp.maximum(_dot(h_ref[...], w1_ref[...]), 0.0)
    acc_ref[...] += _dot((a * a).astype(BF16), w2_ref[...])

    @pl.when(j == pl.num_programs(1) - 1)
    def _():
        o_ref[...] = x_ref[...] + acc_ref[...]


def _mlp(x, g, w1, w2, tm, tf):
    s, d = x.shape
    ff = w1.shape[1]
    return pl.pallas_call(
        _mlp_kernel,
        grid=(s // tm, ff // tf),
        in_specs=[pl.BlockSpec((tm, d), lambda i, j: (i, 0)),
                  pl.BlockSpec((1, d), lambda i, j: (0, 0)),
                  pl.BlockSpec((d, tf), lambda i, j: (0, j)),
                  pl.BlockSpec((tf, d), lambda i, j: (j, 0))],
        out_specs=pl.BlockSpec((tm, d), lambda i, j: (i, 0)),
        out_shape=jax.ShapeDtypeStruct((s, d), F32),
        scratch_shapes=[pltpu.VMEM((tm, d), BF16), pltpu.VMEM((tm, d), F32)],
        compiler_params=_cparams(("parallel", "arbitrary")),
        name="mlp",
    )(x, g.reshape(1, d), w1, w2)


def _permute_w_in(w):
    d = w.shape[0]
    pieces = [w[:, _O_DU:_O_GATE], w[:, _O_B:_O_C], w[:, _O_AQ:_O_AK], w[:, _O_C:_O_DU],
              w[:, _O_AV:_O_AR], w[:, _O_AR:_O_ALR], w[:, _O_AK:_O_AV], w[:, _O_ALR:_O_B],
              jnp.zeros((d, Z_GATE - Z_ALR - 2 * GLA_RANK), w.dtype), w[:, _O_GATE:_O_END]]
    return jnp.concatenate(pieces, axis=1).astype(BF16)


def _dil_bias_idx(tq, dil):
    a = np.arange(tq)[:, None]
    c = np.arange(tq + 2 * DIL_HALF)[None, :]
    delta = c - DIL_HALF - a
    return jnp.asarray(_t5_bucket(delta * dil)[None].astype(np.int32))


def _diff_bias_idx(t):
    nb = _diff_band(t) + 1
    kk = np.arange(t)[:, None]
    qq = np.arange(t)[None, :]
    return jnp.asarray(np.stack([_t5_bucket(o * t + kk - qq) for o in range(-nb, nb + 1)]).astype(np.int32))


def _forward(x, p, *, t_diff, qc_diff, tq_dil, tb_gla, tm_proj, tn_proj, tm_row, tm_mlp, tf_mlp, cb_dft, kb_dft):
    seq = x.shape[0]
    depth = p["w_in"].shape[0]
    n_dil_bias = DIL_GROUPS * DIL_HEADS
    tq_dil = [min(tq, seq // dil) for tq, (_, dil) in zip(tq_dil, DIL_PATTERNS)]
    dil_bias = [_bias_tiles(p["t5_bias"], _dil_bias_idx(tq, dil), DIL_HEADS, gi * DIL_HEADS, 1, 0)
                for gi, (tq, (_, dil)) in enumerate(zip(tq_dil, DIL_PATTERNS))]
    diff_bias = _bias_tiles(p["t5_bias"], _diff_bias_idx(t_diff), DIFF_HEADS, n_dil_bias, 2 ** 30, 0, LOG2E)
    tabs = _dft_tables(2 * seq // DFT_N2)
    emb = _hy_positions(seq)
    rep = lambda g, n: jnp.tile(g, n)
    for i in range(depth):
        z = _norm_matmul(x, p["norm1_g"][i], _permute_w_in(p["w_in"][i]), tm_proj, tn_proj)
        wg = [jnp.zeros((LANES, GLA_HEADS * GLA_DK), F32).at[j * GLA_RANK:(j + 1) * GLA_RANK].set(
            p["gla_gate_w"][i, j]).astype(BF16) for j in range(2)]
        gb = p["gla_gate_b"][i]
        o_fwd = _gla_scan(z, wg[0], gb[0:1], tb_gla, False)
        ya = _gla_scan(z, wg[1], gb[1:2], tb_gla, True, fin=(o_fwd, p["gla_norm_g"][i]))
        bq, bk, bv, cq, ck, cv = _prep(
            z, jnp.repeat(p["dil_qnorm_g"][i], DIL_HEADS, axis=0).reshape(-1),
            jnp.repeat(p["dil_knorm_g"][i], DIL_HEADS, axis=0).reshape(-1),
            rep(p["diff_qnorm_g"][i], 2 * DIFF_HEADS), rep(p["diff_knorm_g"][i], 2 * DIFF_HEADS), tm_row)
        dil_outs = [_dil_attn(bq, bk, bv, dil_bias[gi], gi, dil, tq_dil[gi])
                    for gi, (_, dil) in enumerate(DIL_PATTERNS)]
        lam_init = 0.8 - 0.6 * math.exp(-0.3 * i)
        yc = _diff_attn(cq, ck, cv, diff_bias, p["diff_lambda"][i], p["diff_subln_g"][i], lam_init,
                        t_diff, qc_diff)
        yd = _hyena(z, tabs, emb, p, i, cb_dft, kb_dft)
        x = _combine(x, z, ya, dil_outs, yc, yd, p["proj_a"][i].astype(BF16), p["proj_b"][i].astype(BF16),
                     p["proj_c"][i].astype(BF16), p["proj_d"][i].astype(BF16), p["w_out"][i].astype(BF16),
                     tm_row)
        x = _mlp(x, p["norm2_g"][i], p["mlp_w1"][i].astype(BF16), p["mlp_w2"][i].astype(BF16), tm_mlp, tf_mlp)
    return x


def kernel(x, t5_bias, norm1_g, w_in, gla_gate_w, gla_gate_b, gla_norm_g, dil_qnorm_g, dil_knorm_g,
           diff_qnorm_g, diff_knorm_g, diff_lambda, diff_subln_g, hy_conv_w, hy_conv_b, hy_w1, hy_b1,
           hy_freq1, hy_w2, hy_b2, hy_freq2, hy_w3, hy_skip, proj_a, proj_b, proj_c, proj_d, w_out,
           norm2_g, mlp_w1, mlp_w2):
    p = dict(t5_bias=t5_bias, norm1_g=norm1_g, w_in=w_in, gla_gate_w=gla_gate_w, gla_gate_b=gla_gate_b,
             gla_norm_g=gla_norm_g, dil_qnorm_g=dil_qnorm_g, dil_knorm_g=dil_knorm_g,
             diff_qnorm_g=diff_qnorm_g, diff_knorm_g=diff_knorm_g, diff_lambda=diff_lambda,
             diff_subln_g=diff_subln_g, hy_conv_w=hy_conv_w, hy_conv_b=hy_conv_b, hy_w1=hy_w1, hy_b1=hy_b1,
             hy_freq1=hy_freq1, hy_w2=hy_w2, hy_b2=hy_b2, hy_freq2=hy_freq2, hy_w3=hy_w3, hy_skip=hy_skip,
             proj_a=proj_a, proj_b=proj_b, proj_c=proj_c, proj_d=proj_d, w_out=w_out, norm2_g=norm2_g,
             mlp_w1=mlp_w1, mlp_w2=mlp_w2)
    b, s, d = x.shape
    outs = [_forward(x[bi], p, t_diff=1024, qc_diff=512, tq_dil=(256, 256, 128), tb_gla=512, tm_proj=2048, tn_proj=512, tm_row=512,
                     tm_mlp=1024, tf_mlp=1024, cb_dft=4096, kb_dft=8) for bi in range(b)]
    return jnp.stack(outs)
```

```python
import functools
import math

import jax
import jax.numpy as jnp
import numpy as np
from jax import lax
from jax.experimental import pallas as pl
from jax.experimental.pallas import tpu as pltpu

F32 = jnp.float32
BF16 = jnp.bfloat16

D_MODEL = 1024
HEAD_DIM = 64
GLA_HEADS = 4
GLA_DK = 64
GLA_DV = 128
GLA_RANK = 16
GLA_TAU = 16.0
GLA_CHUNK = 64
DIL_PATTERNS = ((128, 1), (512, 4), (2048, 16))
DIL_GROUPS = 3
DIL_HEADS = 4
DIL_HALF = 64
DIFF_HEADS = 4
DIFF_DV = 128
DIFF_VT = DIFF_DV + 16
HY_WIDTH = 512
HY_EMB = 33
HY_FFN = 64
HY_DECAY_TARGET = 1e-2
HY_FAST_DECAY = 0.3
HY_SLOW_DECAY = 1.5
T5_BUCKETS = 32
T5_MAX_DIST = 1024
N_BIAS_HEADS = 16
D_FF = 4096
RMS_EPS = 1e-6
LOG2E = math.log2(math.e)

LANES = 128
HALO_ROWS = 16
VMEM_LIMIT = 48 * 1024 * 1024

Z_DU = 0
Z_BQ, Z_BK, Z_BV = 1536, 2304, 3072
Z_AQ = 3840
Z_CQ, Z_CK, Z_CV = 4096, 4608, 5120
Z_AV, Z_AR = 5632, 6144
Z_AK = 6656
Z_ALR = 6912
Z_GATE = 7168
Z_COLS = 11264

_O_AQ, _O_AK, _O_AV, _O_AR, _O_ALR, _O_B, _O_C, _O_DU, _O_GATE, _O_END = (
    0, 256, 512, 1024, 1536, 1568, 3872, 5408, 6944, 11040)

DFT_N2 = 128


def _cparams(sem):
    return pltpu.CompilerParams(dimension_semantics=sem, vmem_limit_bytes=VMEM_LIMIT)


def _dot(a, b):
    return jnp.dot(a, b, preferred_element_type=F32)


def _dot_nt(a, b):
    return lax.dot_general(a, b, (((1,), (1,)), ((), ())), preferred_element_type=F32)


def _dot_tn(a, b):
    return lax.dot_general(a, b, (((0,), (0,)), ((), ())), preferred_element_type=F32)


def _split(x):
    hi = x.astype(BF16)
    lo = (x - hi.astype(F32)).astype(BF16)
    return hi, lo


def _dot3(a, b):
    ah, al = _split(a)
    bh, bl = _split(b)
    return _dot(ah, bh) + _dot(ah, bl) + _dot(al, bh)


def _rms(x):
    return x * lax.rsqrt(jnp.mean(x * x, axis=-1, keepdims=True) + RMS_EPS)


def _sigmoid(x):
    return 1.0 / (1.0 + jnp.exp(-x))


def _norm_matmul_kernel(x_ref, g_ref, w_ref, o_ref, h_ref):
    @pl.when(pl.program_id(1) == 0)
    def _():
        h_ref[...] = (_rms(x_ref[...]) * g_ref[...]).astype(BF16)

    o_ref[...] = _dot(h_ref[...], w_ref[...]).astype(o_ref.dtype)


def _norm_matmul(x, g, w, tm, tn):
    s, d = x.shape
    n = w.shape[1]
    return pl.pallas_call(
        _norm_matmul_kernel,
        grid=(s // tm, n // tn),
        in_specs=[pl.BlockSpec((tm, d), lambda i, j: (i, 0)),
                  pl.BlockSpec((1, d), lambda i, j: (0, 0)),
                  pl.BlockSpec((d, tn), lambda i, j: (0, j))],
        out_specs=pl.BlockSpec((tm, tn), lambda i, j: (i, j)),
        out_shape=jax.ShapeDtypeStruct((s, n), BF16),
        scratch_shapes=[pltpu.VMEM((tm, d), BF16)],
        compiler_params=_cparams(("parallel", "arbitrary")),
        name="in_proj",
    )(x, g.reshape(1, d), w)


def _group_norm(x, e, gain):
    hi, lo = _split(x * x)
    ms = (_dot(hi, e) + _dot(lo, e)) * (1.0 / HEAD_DIM)
    return x * lax.rsqrt(ms + RMS_EPS) * gain


def _prep_kernel(bq_ref, bk_ref, bv_ref, cq_ref, ck_ref, cv_ref, eb_ref, ec_ref,
                 gbq_ref, gbk_ref, gcq_ref, gck_ref,
                 obq_ref, obk_ref, obv_ref, ocq_ref, ock_ref, ocv_ref):
    scale = HEAD_DIM ** -0.5
    eb = eb_ref[...]
    ec = ec_ref[...]
    f32 = lambda ref: ref[...].astype(F32)
    obq_ref[...] = (_group_norm(f32(bq_ref), eb, gbq_ref[...]) * scale).astype(BF16)
    obk_ref[...] = _group_norm(f32(bk_ref), eb, gbk_ref[...]).astype(BF16)
    obv_ref[...] = bv_ref[...]
    ocq_ref[...] = (_group_norm(f32(cq_ref), ec, gcq_ref[...]) * (scale * LOG2E)).T.astype(BF16)
    ock_ref[...] = _group_norm(f32(ck_ref), ec, gck_ref[...]).astype(BF16)
    cvt = f32(cv_ref).T
    ones = jnp.ones((DIFF_VT - DIFF_DV, cvt.shape[1]), F32)
    ocv_ref[...] = jnp.concatenate(
        [t for h in range(DIFF_HEADS) for t in (cvt[h * DIFF_DV:(h + 1) * DIFF_DV], ones)], axis=0).astype(BF16)


def _block_diag_ones(width):
    idx = np.arange(width) // HEAD_DIM
    return jnp.asarray(idx[:, None] == idx[None, :], dtype=BF16)


def _prep(z, gbq, gbk, gcq, gck, tm):
    s = z.shape[0]
    wb, wc = DIL_GROUPS * DIL_HEADS * HEAD_DIM, DIFF_HEADS * 2 * HEAD_DIM
    zspec = lambda w, off: pl.BlockSpec((tm, w), lambda i: (i, off // w))
    cspec = lambda r, c: pl.BlockSpec((r, c), lambda i: (0, 0))
    ospec = lambda w: pl.BlockSpec((tm, w), lambda i: (i, 0))
    tspec = lambda w: pl.BlockSpec((w, tm), lambda i: (0, i))
    return pl.pallas_call(
        _prep_kernel,
        grid=(s // tm,),
        in_specs=[zspec(wb, Z_BQ), zspec(wb, Z_BK), zspec(wb, Z_BV),
                  zspec(wc, Z_CQ), zspec(wc, Z_CK), zspec(wc, Z_CV),
                  cspec(wb, wb), cspec(wc, wc),
                  cspec(1, wb), cspec(1, wb), cspec(1, wc), cspec(1, wc)],
        out_specs=[ospec(wb), ospec(wb), ospec(wb), tspec(wc), ospec(wc), tspec(DIFF_HEADS * DIFF_VT)],
        out_shape=[jax.ShapeDtypeStruct((s, wb), BF16)] * 3
        + [jax.ShapeDtypeStruct((wc, s), BF16), jax.ShapeDtypeStruct((s, wc), BF16),
           jax.ShapeDtypeStruct((DIFF_HEADS * DIFF_VT, s), BF16)],
        compiler_params=_cparams(("parallel",)),
        name="qk_prep",
    )(z, z, z, z, z, z, _block_diag_ones(wb), _block_diag_ones(wc),
      gbq.reshape(1, wb), gbk.reshape(1, wb), gcq.reshape(1, wc), gck.reshape(1, wc))


def _t5_bucket(rel):
    half = T5_BUCKETS // 2
    max_exact = half // 2
    ret = np.where(rel > 0, half, 0)
    n = np.abs(rel)
    nf = np.maximum(n, 1).astype(np.float64)
    large = max_exact + (np.log(nf / max_exact) / math.log(T5_MAX_DIST / max_exact)
                         * (half - max_exact)).astype(np.int64)
    large = np.minimum(large, half - 1)
    return ret + np.where(n < max_exact, n, large)


def _bias_kernel(tab_ref, idx_ref, o_ref, *, head_base, heads_per_tile_group, out_scale):
    t = pl.program_id(0)
    h = pl.program_id(1)
    col = head_base + (t // heads_per_tile_group[0]) * heads_per_tile_group[1] + h
    idx = idx_ref[0]
    acc = jnp.zeros(idx.shape, F32)
    for b in range(T5_BUCKETS):
        acc = jnp.where(idx == b, tab_ref[b, col], acc)
    o_ref[0, 0] = acc * out_scale


def _bias_tiles(t5_bias, idx, n_heads, head_base, tiles_per_group, heads_step, out_scale=1.0):
    nt, r, c = idx.shape
    return pl.pallas_call(
        functools.partial(_bias_kernel, head_base=head_base,
                          heads_per_tile_group=(tiles_per_group, heads_step), out_scale=out_scale),
        grid=(nt, n_heads),
        in_specs=[pl.BlockSpec(memory_space=pltpu.SMEM),
                  pl.BlockSpec((1, r, c), lambda t, h: (t, 0, 0))],
        out_specs=pl.BlockSpec((1, 1, r, c), lambda t, h: (t, h, 0, 0)),
        out_shape=jax.ShapeDtypeStruct((nt, n_heads, r, c), F32),
        compiler_params=_cparams(("parallel", "parallel")),
        name="t5_bias_tiles",
    )(t5_bias, idx)


def _diff_attn_kernel(qt_ref, k_ref, vt_ref, bias_ref, lam_ref, g_ref, o_ref,
                      qa_ref, qb_ref, m_ref, acc_ref, s_ref, *, lam_init, qc, far_lo, far_hi):
    j = pl.program_id(2)
    t = qt_ref.shape[1]

    @pl.when(j == 0)
    def _():
        qt = qt_ref[...]
        row = lax.broadcasted_iota(jnp.int32, qt.shape, 0)
        qa_ref[...] = jnp.where(row < HEAD_DIM, qt, jnp.zeros_like(qt))
        qb_ref[...] = jnp.where(row >= HEAD_DIM, qt, jnp.zeros_like(qt))
        m_ref[...] = jnp.full(m_ref.shape, -jnp.inf, F32)
        acc_ref[...] = jnp.zeros(acc_ref.shape, F32)

    starts = [sum(qc[:u]) for u in range(len(qc))]
    cuts = [slice(a, a + w) for a, w in zip(starts, qc)]
    chains = [(0, cs) for cs in cuts] + [(1, cs) for cs in reversed(cuts)]
    q_refs = (qa_ref, qb_ref)

    def step(far):
        k = k_ref[...]
        vt = vt_ref[...]
        if far:
            const = bias_ref[0, 0, 0:1, 0:1]
            scores = lambda c, cols: _dot(k, q_refs[c][:, cols])
        else:
            const = 0.0
            scores = lambda c, cols: _dot(k, q_refs[c][:, cols]) + bias_ref[0, 0, :, cols]
        width = lambda n: chains[n][1].stop - chains[n][1].start
        s_ref[0, :, :width(0)] = scores(*chains[0])
        for n, (c, cols) in enumerate(chains):
            if n + 1 < len(chains):
                s_ref[(n + 1) % 2, :, :width(n + 1)] = scores(*chains[n + 1])
            s = s_ref[n % 2, :, :width(n)]
            m_old = m_ref[c, :, cols]
            m_new = jnp.maximum(m_old, jnp.max(s, axis=0, keepdims=True) + const)
            alpha = jnp.exp2(m_old - m_new)
            p = jnp.exp2(s - (m_new - const)).astype(BF16)
            acc_ref[c, :, cols] = alpha * acc_ref[c, :, cols] + _dot(vt, p)
            m_ref[c, :, cols] = m_new

    off = j - pl.program_id(1) * (t // k_ref.shape[0])
    is_far = jnp.logical_or(off <= far_lo, off >= far_hi)
    pl.when(is_far)(lambda: step(True))
    pl.when(jnp.logical_not(is_far))(lambda: step(False))

    @pl.when(j == pl.num_programs(2) - 1)
    def _():
        lp = lam_ref[...]
        lam = (jnp.exp(jnp.sum(lp[0:1] * lp[1:2], axis=-1, keepdims=True))
               - jnp.exp(jnp.sum(lp[2:3] * lp[3:4], axis=-1, keepdims=True)) + lam_init)
        a0 = acc_ref[0]
        a1 = acc_ref[1]
        o0 = a0[:DIFF_DV] / a0[DIFF_DV:DIFF_DV + 1]
        o1 = a1[:DIFF_DV] / a1[DIFF_DV:DIFF_DV + 1]
        att = o0 - lam * o1
        y = att * lax.rsqrt(jnp.mean(att * att, axis=0, keepdims=True) + RMS_EPS)
        o_ref[...] = y.T * g_ref[...] * (1.0 - lam_init)


def _diff_far_offsets(tq, tk):
    far_lo = (-T5_MAX_DIST - tk + 1) // tk
    far_hi = -(-(T5_MAX_DIST + tq - 1) // tk)
    return far_lo, far_hi


def _diff_attn(cqt, ck, cvt, bias_tiles, lam_p, subln_g, lam_init, t, tk, qc):
    s = ck.shape[0]
    far_lo, far_hi = _diff_far_offsets(t, tk)
    ratio = t // tk
    w = 2 * HEAD_DIM
    return pl.pallas_call(
        functools.partial(_diff_attn_kernel, lam_init=lam_init, qc=qc, far_lo=far_lo, far_hi=far_hi),
        grid=(DIFF_HEADS, s // t, s // tk),
        in_specs=[pl.BlockSpec((w, t), lambda h, i, j: (h, i)),
                  pl.BlockSpec((tk, w), lambda h, i, j: (j, h)),
                  pl.BlockSpec((DIFF_VT, tk), lambda h, i, j: (h, j)),
                  pl.BlockSpec((1, 1, tk, t),
                               lambda h, i, j: (jnp.clip(j - i * ratio, far_lo, far_hi) - far_lo, h, 0, 0)),
                  pl.BlockSpec((4, HEAD_DIM), lambda h, i, j: (0, 0)),
                  pl.BlockSpec((1, DIFF_DV), lambda h, i, j: (0, 0))],
        out_specs=pl.BlockSpec((t, DIFF_DV), lambda h, i, j: (i, h)),
        out_shape=jax.ShapeDtypeStruct((s, DIFF_HEADS * DIFF_DV), F32),
        scratch_shapes=[pltpu.VMEM((w, t), BF16), pltpu.VMEM((w, t), BF16),
                        pltpu.VMEM((2, 1, t), F32), pltpu.VMEM((2, DIFF_VT, t), F32),
                        pltpu.VMEM((2, tk, max(qc)), F32)],
        compiler_params=_cparams(("parallel", "parallel", "arbitrary")),
        name="diff_attn",
    )(cqt, ck, cvt, bias_tiles, lam_p, subln_g.reshape(1, DIFF_DV))


def _dil_kernel(q_ref, kp_ref, kc_ref, kn_ref, vp_ref, vc_ref, vn_ref, bias_ref, o_ref, lse_ref,
                qs_ref, ks_ref, vs_ref, os_ref, ls_ref, *, tq, dil, m_len):
    n = pl.program_id(0)
    hd = DIL_HALF * dil
    body_rows = tq * dil
    halves = DIL_HEADS * HEAD_DIM // LANES

    def put(dst, rows, src_ref):
        x = src_ref[...].astype(F32)
        for t in range(halves):
            dst[t, rows, :] = x[:, t * LANES:(t + 1) * LANES]

    def strided(src, r, count):
        return jnp.concatenate([src[t, pl.ds(r, count, stride=dil), :] for t in range(halves)], axis=1)

    put(qs_ref, slice(0, body_rows), q_ref)
    for dst, (p_ref, c_ref, n_ref) in ((ks_ref, (kp_ref, kc_ref, kn_ref)), (vs_ref, (vp_ref, vc_ref, vn_ref))):
        put(dst, slice(0, hd), p_ref)
        put(dst, slice(hd, hd + body_rows), c_ref)
        put(dst, slice(hd + body_rows, 2 * hd + body_rows), n_ref)
    tk = tq + 2 * DIL_HALF
    a = lax.broadcasted_iota(jnp.int32, (tq, tk), 0)
    c = lax.broadcasted_iota(jnp.int32, (tq, tk), 1)
    delta = c - DIL_HALF - a
    kpos = n * tq - DIL_HALF + c
    valid = jnp.where(jnp.abs(delta) <= DIL_HALF, 1, 0) * jnp.where(kpos >= 0, 1, 0) * jnp.where(kpos < m_len, 1, 0)
    valid = jnp.concatenate([valid] * DIL_HEADS, axis=0)
    lane = lax.broadcasted_iota(jnp.int32, (tq, DIL_HEADS * HEAD_DIM), 1)
    head_masks = [(lane // HEAD_DIM) == h for h in range(DIL_HEADS)]

    def one_subsequence(r, carry):
        q = strided(qs_ref, r, tq).astype(BF16)
        k = strided(ks_ref, r, tk).astype(BF16)
        v = strided(vs_ref, r, tk).astype(BF16)
        qs = jnp.concatenate([jnp.where(hm, q, jnp.zeros_like(q)) for hm in head_masks], axis=0)
        s = _dot_nt(qs, k) + bias_ref[0].reshape(DIL_HEADS * tq, tk)
        s = jnp.where(valid > 0, s, -1e30)
        m = jnp.max(s, axis=-1, keepdims=True)
        e = jnp.exp(s - m)
        l = jnp.sum(e, axis=-1, keepdims=True)
        oh = _dot((e / l).astype(BF16), v)
        lse = m + jnp.log(l)
        o = jnp.zeros(q.shape, F32)
        lse_o = jnp.zeros(q.shape, F32)
        for h, hm in enumerate(head_masks):
            rows = slice(h * tq, (h + 1) * tq)
            o = jnp.where(hm, oh[rows], o)
            lse_o = jnp.where(hm, lse[rows], lse_o)
        for t in range(halves):
            os_ref[t, pl.ds(r, tq, stride=dil), :] = o[:, t * LANES:(t + 1) * LANES]
            ls_ref[t, pl.ds(r, tq, stride=dil), :] = lse_o[:, t * LANES:(t + 1) * LANES]
        return carry

    lax.fori_loop(0, dil, one_subsequence, 0, unroll=min(dil, 2))
    o_ref[...] = jnp.concatenate([os_ref[t] for t in range(halves)], axis=1)
    lse_ref[...] = jnp.concatenate([ls_ref[t] for t in range(halves)], axis=1)


def _dil_attn(bq, bk, bv, bias, gi, dil, tq):
    s = bq.shape[0]
    m_len = s // dil
    gw = DIL_HEADS * HEAD_DIM
    rows = tq * dil
    hd = DIL_HALF * dil
    hb = rows // hd
    last = s // hd - 1
    qspec = pl.BlockSpec((rows, gw), lambda n: (n, gi))
    pspec = pl.BlockSpec((hd, gw), lambda n: (jnp.maximum(n * hb - 1, 0), gi))
    nspec = pl.BlockSpec((hd, gw), lambda n: (jnp.minimum((n + 1) * hb, last), gi))
    ospec = pl.BlockSpec((rows, gw), lambda n: (n, 0))
    return pl.pallas_call(
        functools.partial(_dil_kernel, tq=tq, dil=dil, m_len=m_len),
        grid=(s // rows,),
        in_specs=[qspec, pspec, qspec, nspec, pspec, qspec, nspec,
                  pl.BlockSpec((1, DIL_HEADS, tq, tq + 2 * DIL_HALF), lambda n: (0, 0, 0, 0))],
        out_specs=[ospec, ospec],
        out_shape=[jax.ShapeDtypeStruct((s, gw), F32)] * 2,
        scratch_shapes=[pltpu.VMEM((gw // LANES, rows, LANES), F32),
                        pltpu.VMEM((gw // LANES, rows + 2 * hd, LANES), F32),
                        pltpu.VMEM((gw // LANES, rows + 2 * hd, LANES), F32),
                        pltpu.VMEM((gw // LANES, rows, LANES), F32),
                        pltpu.VMEM((gw // LANES, rows, LANES), F32)],
        compiler_params=_cparams(("parallel",)),
        name=f"dil_attn_g{gi}",
    )(bq, bk, bk, bk, bv, bv, bv, bias)


def _gla_kernel(*refs, reverse, tb, finalize):
    if finalize:
        (q_ref, k_ref, v_ref, lr_ref, wg_ref, gb_ref, tri_ref, ofwd_ref, r_ref, ng_ref,
         o_ref, s_ref, oacc_ref) = refs
    else:
        q_ref, k_ref, v_ref, lr_ref, wg_ref, gb_ref, tri_ref, o_ref, s_ref = refs
        oacc_ref = o_ref
    cw = GLA_CHUNK
    qk = GLA_HEADS * GLA_DK
    vw = GLA_HEADS * GLA_DV

    @pl.when(pl.program_id(0) == 0)
    def _():
        s_ref[...] = jnp.zeros(s_ref.shape, F32)

    logits = _dot(lr_ref[...], wg_ref[...]) + gb_ref[...]
    g = (jnp.minimum(logits, 0.0) - jnp.log(1.0 + jnp.exp(-jnp.abs(logits)))) * (1.0 / GLA_TAU)
    ghi, glo = _split(g)
    tri = tri_ref[...]
    b = _dot(tri, ghi) + _dot(tri, glo)
    qg = (q_ref[...].astype(F32) * (GLA_DK ** -0.5) * jnp.exp(b)).astype(BF16)
    k = k_ref[...].astype(F32)
    kg = (k * jnp.exp(-b)).astype(BF16)
    v = v_ref[...].astype(BF16)

    ones = jnp.ones((cw, LANES), BF16)
    lane_q = lax.broadcasted_iota(jnp.int32, (cw, qk), 1)
    rr = lax.broadcasted_iota(jnp.int32, (GLA_HEADS * cw, cw), 0)
    cc = lax.broadcasted_iota(jnp.int32, (GLA_HEADS * cw, cw), 1)
    tt = rr % cw
    amask = (cc > tt) if reverse else (cc <= tt)
    srow = lax.broadcasted_iota(jnp.int32, (qk, vw), 0) // GLA_DK
    scol = lax.broadcasted_iota(jnp.int32, (qk, vw), 1) // GLA_DV
    bdmask = srow == scol

    n_chunks = tb // cw
    order = range(n_chunks - 1, -1, -1) if reverse else range(n_chunks)
    for ci in order:
        rows = slice(ci * cw, (ci + 1) * cw)
        bc = b[rows]
        b_end = bc[0:1] if reverse else bc[cw - 1:cw]
        kdec = (k[rows] * jnp.exp(b_end - bc)).astype(BF16)
        btot = _dot_tn(ghi[rows], ones) + _dot_tn(glo[rows], ones)
        qg_c = qg[rows]
        qs = jnp.concatenate(
            [jnp.where((lane_q // GLA_DK) == h, qg_c, jnp.zeros_like(qg_c)) for h in range(GLA_HEADS)], axis=0)
        a = jnp.where(amask, _dot_nt(qs, kg[rows]), 0.0)
        obig = _dot(a.astype(BF16), v[rows])
        o_intra = jnp.concatenate(
            [obig[h * cw:(h + 1) * cw, h * GLA_DV:(h + 1) * GLA_DV] for h in range(GLA_HEADS)], axis=1)
        state = s_ref[...]
        oacc_ref[rows, :] = o_intra + _dot(qg_c, state.astype(BF16))
        ds = _dot_tn(kdec, v[rows])
        decay = jnp.exp(btot)
        decay = jnp.concatenate([decay] * GLA_HEADS, axis=1)
        s_ref[...] = jnp.where(bdmask, decay * state + ds, 0.0)

    if finalize:
        o = ofwd_ref[...] + oacc_ref[...]
        r = r_ref[...].astype(F32)
        outs = []
        for h in range(GLA_HEADS):
            sl = slice(h * GLA_DV, (h + 1) * GLA_DV)
            outs.append(_rms(o[:, sl]) * ng_ref[...] * (r[:, sl] * _sigmoid(r[:, sl])))
        o_ref[...] = jnp.concatenate(outs, axis=1)


def _chunk_tri(tb, reverse):
    i = np.arange(tb)
    same = (i[:, None] // GLA_CHUNK) == (i[None, :] // GLA_CHUNK)
    tri = (i[None, :] >= i[:, None]) if reverse else (i[None, :] <= i[:, None])
    return jnp.asarray(same & tri, dtype=BF16)


def _gla_scan(z, wg, gb, tb, reverse, fin=None):
    s = z.shape[0]
    nb = s // tb
    qk = GLA_HEADS * GLA_DK
    vw = GLA_HEADS * GLA_DV
    blk = (lambda i: nb - 1 - i) if reverse else (lambda i: i)
    zspec = lambda w, off: pl.BlockSpec((tb, w), lambda i: (blk(i), off // w))
    cspec = lambda r, c: pl.BlockSpec((r, c), lambda i: (0, 0))
    in_specs = [zspec(qk, Z_AQ), zspec(qk, Z_AK), zspec(vw, Z_AV), zspec(LANES, Z_ALR),
                cspec(LANES, qk), cspec(1, qk), cspec(tb, tb)]
    args = [z, z, z, z, wg, gb, _chunk_tri(tb, reverse)]
    scratch = [pltpu.VMEM((qk, vw), F32)]
    if fin is not None:
        o_fwd, norm_g = fin
        in_specs += [pl.BlockSpec((tb, vw), lambda i: (blk(i), 0)), zspec(vw, Z_AR), cspec(1, GLA_DV)]
        args += [o_fwd, z, norm_g.reshape(1, GLA_DV)]
        scratch.append(pltpu.VMEM((tb, vw), F32))
    return pl.pallas_call(
        functools.partial(_gla_kernel, reverse=reverse, tb=tb, finalize=fin is not None),
        grid=(nb,),
        in_specs=in_specs,
        out_specs=pl.BlockSpec((tb, vw), lambda i: (blk(i), 0)),
        out_shape=jax.ShapeDtypeStruct((s, vw), F32),
        scratch_shapes=scratch,
        compiler_params=_cparams(("arbitrary",)),
        name="gla_bwd" if reverse else "gla_fwd",
    )(*args)


def _hy_pre_kernel(u_ref, up_ref, un_ref, w_ref, b_ref, x0_ref, z_ref, zb_ref, *, tm):
    i = pl.program_id(0)
    u = u_ref[...].astype(F32)
    row = lax.broadcasted_iota(jnp.int32, u.shape, 0)
    prev_row = jnp.where(i == 0, 0.0, up_ref[...].astype(F32)[HALO_ROWS - 1:HALO_ROWS, :])
    next_row = jnp.where(i == pl.num_programs(0) - 1, 0.0, un_ref[...].astype(F32)[0:1, :])
    u_prev = jnp.where(row == 0, prev_row, pltpu.roll(u, 1, axis=0))
    u_next = jnp.where(row == tm - 1, next_row, pltpu.roll(u, tm - 1, axis=0))
    y = b_ref[...] + u_prev * w_ref[0:1] + u * w_ref[1:2] + u_next * w_ref[2:3]
    x0_ref[...] = y[:, :HY_WIDTH]
    z = y[:, HY_WIDTH:2 * HY_WIDTH] * y[:, 2 * HY_WIDTH:]
    z_ref[...] = z
    zb_ref[...] = z.astype(BF16)


def _hy_pre(z, conv_w, conv_b, tm):
    s = z.shape[0]
    w = 3 * HY_WIDTH
    nr = s // HALO_ROWS
    ospec = pl.BlockSpec((tm, HY_WIDTH), lambda i: (i, 0))
    return pl.pallas_call(
        functools.partial(_hy_pre_kernel, tm=tm),
        grid=(s // tm,),
        in_specs=[pl.BlockSpec((tm, w), lambda i: (i, 0)),
                  pl.BlockSpec((HALO_ROWS, w), lambda i: (jnp.maximum(i * (tm // HALO_ROWS) - 1, 0), 0)),
                  pl.BlockSpec((HALO_ROWS, w), lambda i: (jnp.minimum((i + 1) * (tm // HALO_ROWS), nr - 1), 0)),
                  pl.BlockSpec((3, w), lambda i: (0, 0)),
                  pl.BlockSpec((1, w), lambda i: (0, 0))],
        out_specs=[ospec, ospec, ospec],
        out_shape=[jax.ShapeDtypeStruct((s, HY_WIDTH), F32)] * 2 + [jax.ShapeDtypeStruct((s, HY_WIDTH), BF16)],
        compiler_params=_cparams(("parallel",)),
        name="hyena_pre",
    )(z, z, z, conv_w, conv_b.reshape(1, w))


def _hy_filter_kernel(emb_ref, w1_ref, b1_ref, f1_ref, w2_ref, b2_ref, f2_ref, w3_ref, dl_ref,
                      k_ref, kb_ref, norm_ref, *, tl, seq):
    i = pl.program_id(0)
    emb = emb_ref[...]
    h = jnp.sin(f1_ref[...] * (_dot3(emb, w1_ref[...]) + b1_ref[...]))
    h = jnp.sin(f2_ref[...] * (_dot3(h, w2_ref[...]) + b2_ref[...]))
    h = _dot3(h, w3_ref[...])
    back = i * tl >= seq
    h = jnp.where(back, h[:, HY_WIDTH:], h[:, :HY_WIDTH])
    h = h * jnp.exp(-emb[:, 0:1] * dl_ref[...])
    row = i * tl + lax.broadcasted_iota(jnp.int32, h.shape, 0)
    h = jnp.where(row == seq, 0.0, h)
    k_ref[...] = h
    kb_ref[...] = h.astype(BF16)

    @pl.when(i == 0)
    def _():
        norm_ref[...] = jnp.zeros(norm_ref.shape, F32)

    norm_ref[...] += jnp.sum(jnp.abs(h), axis=0, keepdims=True)


def _hy_positions(seq):
    t = np.linspace(0.0, 1.0, seq, dtype=np.float32)[:, None]
    bands = (HY_EMB - 1) // 2
    freqs = np.linspace(1e-4, bands - 1, bands, dtype=np.float32)[None]
    w = (np.float32(2.0 * math.pi) * np.arange(seq, dtype=np.float32)[:, None] / np.float32(seq))
    zf = np.concatenate([t, np.cos(freqs * w), -np.sin(freqs * w)], axis=-1).astype(np.float32)
    pos = np.concatenate([np.arange(seq), [0], np.arange(seq - 1, 0, -1)])
    return jnp.asarray(np.pad(zf[pos], ((0, 0), (0, LANES - HY_EMB))))


def _hy_filter(emb, w1, b1, fr1, w2, b2, fr2, w3, tl):
    n = emb.shape[0]
    seq = n // 2
    pad2 = lambda a, r, c: jnp.pad(a, ((0, r - a.shape[0]), (0, c - a.shape[1])))
    min_decay = math.log(HY_DECAY_TARGET) / HY_SLOW_DECAY
    max_decay = math.log(HY_DECAY_TARGET) / HY_FAST_DECAY
    deltas = jnp.abs(jnp.linspace(min_decay, max_decay, HY_WIDTH, dtype=F32))[None]
    cs = lambda r, c: pl.BlockSpec((r, c), lambda i: (0, 0))
    ospec = pl.BlockSpec((tl, HY_WIDTH), lambda i: (i, 0))
    return pl.pallas_call(
        functools.partial(_hy_filter_kernel, tl=tl, seq=seq),
        grid=(n // tl,),
        in_specs=[pl.BlockSpec((tl, LANES), lambda i: (i, 0)),
                  cs(LANES, LANES), cs(1, LANES), cs(1, LANES),
                  cs(LANES, LANES), cs(1, LANES), cs(1, LANES),
                  cs(LANES, 2 * HY_WIDTH), cs(1, HY_WIDTH)],
        out_specs=[ospec, ospec, pl.BlockSpec((1, HY_WIDTH), lambda i: (0, 0))],
        out_shape=[jax.ShapeDtypeStruct((n, HY_WIDTH), F32), jax.ShapeDtypeStruct((n, HY_WIDTH), BF16),
                   jax.ShapeDtypeStruct((1, HY_WIDTH), F32)],
        compiler_params=_cparams(("arbitrary",)),
        name="hyena_filter",
    )(emb, pad2(w1, LANES, LANES), pad2(b1[None], 1, LANES), pad2(fr1[None], 1, LANES),
      pad2(w2, LANES, LANES), pad2(b2[None], 1, LANES), pad2(fr2[None], 1, LANES),
      pad2(w3, LANES, 2 * HY_WIDTH), deltas)


def _dft_tables(n1):
    n = n1 * DFT_N2
    a = np.arange(n1, dtype=np.int64)
    ang1 = (2.0 * math.pi / n1) * ((a[:, None] * a[None, :]) % n1)
    c1, s1 = np.cos(ang1), np.sin(ang1)
    k1 = np.arange(n1, dtype=np.int64)[:, None, None]
    k2 = np.arange(DFT_N2, dtype=np.int64)[None, :, None]
    n2 = np.arange(DFT_N2, dtype=np.int64)[None, None, :]
    ang = (2.0 * math.pi / n) * ((n2 * (k1 + n1 * k2)) % n)
    gc, gs = np.cos(ang), np.sin(ang)
    tb = lambda x: jnp.asarray(np.ascontiguousarray(x).astype(BF16))
    return dict(c1=tb(c1), s1=tb(s1), gc=tb(gc), gs=tb(gs),
                gct=tb(np.swapaxes(gc, 1, 2)), gst=tb(np.swapaxes(gs, 1, 2)))


def _dft1_kernel(c_ref, s_ref, x_ref, re_ref, im_ref):
    x = x_ref[...]
    re_ref[...] = _dot(c_ref[...], x).astype(BF16)
    im_ref[...] = (-_dot(s_ref[...], x)).astype(BF16)


def _dft1(x, c1, s1, cb):
    k1, w = x.shape
    n1 = c1.shape[0]
    ospec = pl.BlockSpec((n1, cb), lambda j: (0, j))
    return pl.pallas_call(
        _dft1_kernel,
        grid=(w // cb,),
        in_specs=[pl.BlockSpec((n1, k1), lambda j: (0, 0)), pl.BlockSpec((n1, k1), lambda j: (0, 0)),
                  pl.BlockSpec((k1, cb), lambda j: (0, j))],
        out_specs=[ospec, ospec],
        out_shape=[jax.ShapeDtypeStruct((n1, w), BF16)] * 2,
        compiler_params=_cparams(("parallel",)),
        name="dft_stage1",
    )(c1, s1, x)


def _dft2_filter_kernel(gc_ref, gs_ref, are_ref, aim_ref, norm_ref, kre_ref, kim_ref, *, kb):
    inv = 1.0 / norm_ref[...]
    for t in range(kb):
        gc, gs = gc_ref[t], gs_ref[t]
        are, aim = are_ref[t], aim_ref[t]
        kre_ref[t] = (_dot(gc, are) + _dot(gs, aim)) * inv
        kim_ref[t] = (_dot(gc, aim) - _dot(gs, are)) * inv


def _dft2_filter(tabs, are, aim, norm, kb):
    n1, _, c = are.shape
    gspec = pl.BlockSpec((kb, DFT_N2, DFT_N2), lambda i: (i, 0, 0))
    aspec = pl.BlockSpec((kb, DFT_N2, c), lambda i: (i, 0, 0))
    return pl.pallas_call(
        functools.partial(_dft2_filter_kernel, kb=kb),
        grid=(n1 // kb,),
        in_specs=[gspec, gspec, aspec, aspec, pl.BlockSpec((1, c), lambda i: (0, 0))],
        out_specs=[aspec, aspec],
        out_shape=[jax.ShapeDtypeStruct((n1, DFT_N2, c), F32)] * 2,
        compiler_params=_cparams(("parallel",)),
        name="dft_stage2_filter",
    )(tabs["gc"], tabs["gs"], are, aim, norm)


def _conv_mid_kernel(gc_ref, gs_ref, gct_ref, gst_ref, are_ref, aim_ref, kre_ref, kim_ref,
                     bre_ref, bim_ref, *, kb):
    for t in range(kb):
        gc, gs = gc_ref[t], gs_ref[t]
        are, aim = are_ref[t], aim_ref[t]
        xre = _dot(gc, are) + _dot(gs, aim)
        xim = _dot(gc, aim) - _dot(gs, are)
        kre, kim = kre_ref[t], kim_ref[t]
        yre = (xre * kre - xim * kim).astype(BF16)
        yim = (xre * kim + xim * kre).astype(BF16)
        gct, gst = gct_ref[t], gst_ref[t]
        bre_ref[t] = (_dot(gct, yre) - _dot(gst, yim)).astype(BF16)
        bim_ref[t] = (_dot(gct, yim) + _dot(gst, yre)).astype(BF16)


def _conv_mid(tabs, are, aim, kre, kim, kb):
    n1, _, c = are.shape
    gspec = pl.BlockSpec((kb, DFT_N2, DFT_N2), lambda i: (i, 0, 0))
    aspec = pl.BlockSpec((kb, DFT_N2, c), lambda i: (i, 0, 0))
    return pl.pallas_call(
        functools.partial(_conv_mid_kernel, kb=kb),
        grid=(n1 // kb,),
        in_specs=[gspec, gspec, gspec, gspec, aspec, aspec, aspec, aspec],
        out_specs=[aspec, aspec],
        out_shape=[jax.ShapeDtypeStruct((n1, DFT_N2, c), BF16)] * 2,
        compiler_params=_cparams(("parallel",)),
        name="conv_spectral",
    )(tabs["gc"], tabs["gs"], tabs["gct"], tabs["gst"], are, aim, kre, kim)


def _idft1_kernel(c_ref, s_ref, bre_ref, bim_ref, z_ref, x0_ref, skip_ref, o_ref, *, inv_n):
    y = (_dot(c_ref[...], bre_ref[...]) - _dot(s_ref[...], bim_ref[...])) * inv_n
    o_ref[...] = x0_ref[...] * (y + z_ref[...] * skip_ref[...])


def _idft1(c1h, s1h, bre, bim, z, x0, skip_t, cb):
    ko, n1 = c1h.shape
    w = bre.shape[1]
    n = n1 * DFT_N2
    ospec = pl.BlockSpec((ko, cb), lambda j: (0, j))
    return pl.pallas_call(
        functools.partial(_idft1_kernel, inv_n=1.0 / n),
        grid=(w // cb,),
        in_specs=[pl.BlockSpec((ko, n1), lambda j: (0, 0)), pl.BlockSpec((ko, n1), lambda j: (0, 0)),
                  pl.BlockSpec((n1, cb), lambda j: (0, j)), pl.BlockSpec((n1, cb), lambda j: (0, j)),
                  ospec, ospec, pl.BlockSpec((1, cb), lambda j: (0, 0))],
        out_specs=ospec,
        out_shape=jax.ShapeDtypeStruct((ko, w), F32),
        compiler_params=_cparams(("parallel",)),
        name="idft_stage1",
    )(c1h, s1h, bre, bim, z, x0, skip_t)


def _hyena(z_all, tabs, emb, p, i, cb, kb):
    seq = z_all.shape[0]
    n1 = 2 * seq // DFT_N2
    c = HY_WIDTH
    x0, zf, zb = _hy_pre(z_all, p["hy_conv_w"][i], p["hy_conv_b"][i], min(512, seq))
    _, kern_b, norm = _hy_filter(emb, p["hy_w1"][i], p["hy_b1"][i], p["hy_freq1"][i], p["hy_w2"][i],
                                 p["hy_b2"][i], p["hy_freq2"][i], p["hy_w3"][i], min(1024, seq))
    wide = DFT_N2 * c
    kre, kim = _dft1(kern_b.reshape(n1, wide), tabs["c1"], tabs["s1"], cb)
    kre, kim = _dft2_filter(tabs, kre.reshape(n1, DFT_N2, c), kim.reshape(n1, DFT_N2, c), norm, kb)
    are, aim = _dft1(zb.reshape(n1 // 2, wide), tabs["c1"][:, :n1 // 2], tabs["s1"][:, :n1 // 2], cb)
    bre, bim = _conv_mid(tabs, are.reshape(n1, DFT_N2, c), aim.reshape(n1, DFT_N2, c), kre, kim, kb)
    skip_t = jnp.tile(p["hy_skip"][i][None], (1, cb // c))
    yd = _idft1(tabs["c1"][:n1 // 2], tabs["s1"][:n1 // 2], bre.reshape(n1, wide), bim.reshape(n1, wide),
                zf.reshape(n1 // 2, wide), x0.reshape(n1 // 2, wide), skip_t, cb)
    return yd.reshape(seq, c)


def _combine_kernel(x_ref, ya_ref, o0_ref, o1_ref, o2_ref, l0_ref, l1_ref, l2_ref, yc_ref, yd_ref,
                    g0_ref, g1_ref, g2_ref, g3_ref, pa_ref, pb_ref, pc_ref, pd_ref, wo_ref, o_ref):
    l0, l1, l2 = l0_ref[...], l1_ref[...], l2_ref[...]
    mx = jnp.maximum(jnp.maximum(l0, l1), l2)
    e0, e1, e2 = jnp.exp(l0 - mx), jnp.exp(l1 - mx), jnp.exp(l2 - mx)
    yb = (e0 * o0_ref[...] + e1 * o1_ref[...] + e2 * o2_ref[...]) / (e0 + e1 + e2)
    gate = lambda ref: _sigmoid(ref[...].astype(F32))
    m = (gate(g0_ref) * _dot(ya_ref[...].astype(BF16), pa_ref[...])
         + gate(g1_ref) * _dot(yb.astype(BF16), pb_ref[...])
         + gate(g2_ref) * _dot(yc_ref[...].astype(BF16), pc_ref[...])
         + gate(g3_ref) * _dot(yd_ref[...].astype(BF16), pd_ref[...]))
    o_ref[...] = x_ref[...] + _dot(m.astype(BF16), wo_ref[...])


def _combine(x, z, ya, dil_outs, yc, yd, pa, pb, pc, pd, wo, tm):
    s, d = x.shape
    rs = lambda w: pl.BlockSpec((tm, w), lambda i: (i, 0))
    gs = lambda b: pl.BlockSpec((tm, d), lambda i: (i, Z_GATE // d + b))
    ws = lambda a: pl.BlockSpec(a.shape, lambda i: (0, 0))
    (o0, l0), (o1, l1), (o2, l2) = dil_outs
    gw = DIL_HEADS * HEAD_DIM
    return pl.pallas_call(
        _combine_kernel,
        grid=(s // tm,),
        in_specs=[rs(d), rs(ya.shape[1]), rs(gw), rs(gw), rs(gw), rs(gw), rs(gw), rs(gw),
                  rs(yc.shape[1]), rs(yd.shape[1]), gs(0), gs(1), gs(2), gs(3),
                  ws(pa), ws(pb), ws(pc), ws(pd), ws(wo)],
        out_specs=rs(d),
        out_shape=jax.ShapeDtypeStruct((s, d), F32),
        compiler_params=_cparams(("parallel",)),
        name="combine",
    )(x, ya, o0, o1, o2, l0, l1, l2, yc, yd, z, z, z, z, pa, pb, pc, pd, wo)


def _mlp_kernel(x_ref, g_ref, w1_ref, w2_ref, o_ref, h_ref, acc_ref):
    j = pl.program_id(1)

    @pl.when(j == 0)
    def _():
        h_ref[...] = (_rms(x_ref[...]) * g_ref[...]).astype(BF16)
        acc_ref[...] = jnp.zeros(acc_ref.shape, F32)

    a = jnp.maximum(_dot(h_ref[...], w1_ref[...]), 0.0)
    acc_ref[...] += _dot((a * a).astype(BF16), w2_ref[...])

    @pl.when(j == pl.num_programs(1) - 1)
    def _():
        o_ref[...] = x_ref[...] + acc_ref[...]


def _mlp(x, g, w1, w2, tm, tf):
    s, d = x.shape
    ff = w1.shape[1]
    return pl.pallas_call(
        _mlp_kernel,
        grid=(s // tm, ff // tf),
        in_specs=[pl.BlockSpec((tm, d), lambda i, j: (i, 0)),
                  pl.BlockSpec((1, d), lambda i, j: (0, 0)),
                  pl.BlockSpec((d, tf), lambda i, j: (0, j)),
                  pl.BlockSpec((tf, d), lambda i, j: (j, 0))],
        out_specs=pl.BlockSpec((tm, d), lambda i, j: (i, 0)),
        out_shape=jax.ShapeDtypeStruct((s, d), F32),
        scratch_shapes=[pltpu.VMEM((tm, d), BF16), pltpu.VMEM((tm, d), F32)],
        compiler_params=_cparams(("parallel", "arbitrary")),
        name="mlp",
    )(x, g.reshape(1, d), w1, w2)


def _permute_w_in(w):
    d = w.shape[0]
    pieces = [w[:, _O_DU:_O_GATE], w[:, _O_B:_O_C], w[:, _O_AQ:_O_AK], w[:, _O_C:_O_DU],
              w[:, _O_AV:_O_AR], w[:, _O_AR:_O_ALR], w[:, _O_AK:_O_AV], w[:, _O_ALR:_O_B],
              jnp.zeros((d, Z_GATE - Z_ALR - 2 * GLA_RANK), w.dtype), w[:, _O_GATE:_O_END]]
    return jnp.concatenate(pieces, axis=1).astype(BF16)


def _dil_bias_idx(tq, dil):
    a = np.arange(tq)[:, None]
    c = np.arange(tq + 2 * DIL_HALF)[None, :]
    delta = c - DIL_HALF - a
    return jnp.asarray(_t5_bucket(delta * dil)[None].astype(np.int32))


def _diff_bias_idx(tq, tk):
    far_lo, far_hi = _diff_far_offsets(tq, tk)
    kk = np.arange(tk)[:, None]
    qq = np.arange(tq)[None, :]
    return jnp.asarray(np.stack([_t5_bucket(o * tk + kk - qq)
                                 for o in range(far_lo, far_hi + 1)]).astype(np.int32))


def _forward(x, p, *, t_diff, tk_diff, qc_diff, tq_dil, tb_gla, tm_proj, tn_proj, tm_row, tm_mlp, tf_mlp, cb_dft, kb_dft):
    seq = x.shape[0]
    depth = p["w_in"].shape[0]
    n_dil_bias = DIL_GROUPS * DIL_HEADS
    tq_dil = [min(tq, seq // dil) for tq, (_, dil) in zip(tq_dil, DIL_PATTERNS)]
    dil_bias = [_bias_tiles(p["t5_bias"], _dil_bias_idx(tq, dil), DIL_HEADS, gi * DIL_HEADS, 1, 0)
                for gi, (tq, (_, dil)) in enumerate(zip(tq_dil, DIL_PATTERNS))]
    t_diff, tk_diff = min(t_diff, seq), min(tk_diff, seq)
    diff_bias = _bias_tiles(p["t5_bias"], _diff_bias_idx(t_diff, tk_diff), DIFF_HEADS, n_dil_bias, 2 ** 30, 0,
                            LOG2E)
    tabs = _dft_tables(2 * seq // DFT_N2)
    emb = _hy_positions(seq)
    rep = lambda g, n: jnp.tile(g, n)
    for i in range(depth):
        z = _norm_matmul(x, p["norm1_g"][i], _permute_w_in(p["w_in"][i]), tm_proj, tn_proj)
        wg = [jnp.zeros((LANES, GLA_HEADS * GLA_DK), F32).at[j * GLA_RANK:(j + 1) * GLA_RANK].set(
            p["gla_gate_w"][i, j]).astype(BF16) for j in range(2)]
        gb = p["gla_gate_b"][i]
        o_fwd = _gla_scan(z, wg[0], gb[0:1], tb_gla, False)
        ya = _gla_scan(z, wg[1], gb[1:2], tb_gla, True, fin=(o_fwd, p["gla_norm_g"][i]))
        bq, bk, bv, cq, ck, cv = _prep(
            z, jnp.repeat(p["dil_qnorm_g"][i], DIL_HEADS, axis=0).reshape(-1),
            jnp.repeat(p["dil_knorm_g"][i], DIL_HEADS, axis=0).reshape(-1),
            rep(p["diff_qnorm_g"][i], 2 * DIFF_HEADS), rep(p["diff_knorm_g"][i], 2 * DIFF_HEADS), tm_row)
        dil_outs = [_dil_attn(bq, bk, bv, dil_bias[gi], gi, dil, tq_dil[gi])
                    for gi, (_, dil) in enumerate(DIL_PATTERNS)]
        lam_init = 0.8 - 0.6 * math.exp(-0.3 * i)
        yc = _diff_attn(cq, ck, cv, diff_bias, p["diff_lambda"][i], p["diff_subln_g"][i], lam_init,
                        t_diff, tk_diff, qc_diff)
        yd = _hyena(z, tabs, emb, p, i, cb_dft, kb_dft)
        x = _combine(x, z, ya, dil_outs, yc, yd, p["proj_a"][i].astype(BF16), p["proj_b"][i].astype(BF16),
                     p["proj_c"][i].astype(BF16), p["proj_d"][i].astype(BF16), p["w_out"][i].astype(BF16),
                     tm_row)
        x = _mlp(x, p["norm2_g"][i], p["mlp_w1"][i].astype(BF16), p["mlp_w2"][i].astype(BF16), tm_mlp, tf_mlp)
    return x


def kernel(x, t5_bias, norm1_g, w_in, gla_gate_w, gla_gate_b, gla_norm_g, dil_qnorm_g, dil_knorm_g,
           diff_qnorm_g, diff_knorm_g, diff_lambda, diff_subln_g, hy_conv_w, hy_conv_b, hy_w1, hy_b1,
           hy_freq1, hy_w2, hy_b2, hy_freq2, hy_w3, hy_skip, proj_a, proj_b, proj_c, proj_d, w_out,
           norm2_g, mlp_w1, mlp_w2):
    p = dict(t5_bias=t5_bias, norm1_g=norm1_g, w_in=w_in, gla_gate_w=gla_gate_w, gla_gate_b=gla_gate_b,
             gla_norm_g=gla_norm_g, dil_qnorm_g=dil_qnorm_g, dil_knorm_g=dil_knorm_g,
             diff_qnorm_g=diff_qnorm_g, diff_knorm_g=diff_knorm_g, diff_lambda=diff_lambda,
             diff_subln_g=diff_subln_g, hy_conv_w=hy_conv_w, hy_conv_b=hy_conv_b, hy_w1=hy_w1, hy_b1=hy_b1,
             hy_freq1=hy_freq1, hy_w2=hy_w2, hy_b2=hy_b2, hy_freq2=hy_freq2, hy_w3=hy_w3, hy_skip=hy_skip,
             proj_a=proj_a, proj_b=proj_b, proj_c=proj_c, proj_d=proj_d, w_out=w_out, norm2_g=norm2_g,
             mlp_w1=mlp_w1, mlp_w2=mlp_w2)
    b, s, d = x.shape
    outs = [_forward(x[bi], p, t_diff=2048, tk_diff=1024, qc_diff=(256, 768, 1024), tq_dil=(256, 256, 128), tb_gla=512, tm_proj=2048, tn_proj=512, tm_row=512,
                     tm_mlp=1024, tf_mlp=1024, cb_dft=4096, kb_dft=8) for bi in range(b)]
    return jnp.stack(outs)
```

```python
import functools
import math

import jax
import jax.numpy as jnp
import numpy as np
from jax import lax
from jax.experimental import pallas as pl
from jax.experimental.pallas import tpu as pltpu

F32 = jnp.float32
BF16 = jnp.bfloat16

D_MODEL = 1024
HEAD_DIM = 64
GLA_HEADS = 4
GLA_DK = 64
GLA_DV = 128
GLA_RANK = 16
GLA_TAU = 16.0
GLA_CHUNK = 64
DIL_PATTERNS = ((128, 1), (512, 4), (2048, 16))
DIL_GROUPS = 3
DIL_HEADS = 4
DIL_HALF = 64
DIFF_HEADS = 4
DIFF_DV = 128
DIFF_VT = DIFF_DV + 16
HY_WIDTH = 512
HY_EMB = 33
HY_FFN = 64
HY_DECAY_TARGET = 1e-2
HY_FAST_DECAY = 0.3
HY_SLOW_DECAY = 1.5
T5_BUCKETS = 32
T5_MAX_DIST = 1024
N_BIAS_HEADS = 16
D_FF = 4096
RMS_EPS = 1e-6
LOG2E = math.log2(math.e)

LANES = 128
HALO_ROWS = 16
VMEM_LIMIT = 48 * 1024 * 1024

Z_DU = 0
Z_BQ, Z_BK, Z_BV = 1536, 2304, 3072
Z_AQ = 3840
Z_CQ, Z_CK, Z_CV = 4096, 4608, 5120
Z_AV, Z_AR = 5632, 6144
Z_AK = 6656
Z_ALR = 6912
Z_GATE = 7168
Z_COLS = 11264

_O_AQ, _O_AK, _O_AV, _O_AR, _O_ALR, _O_B, _O_C, _O_DU, _O_GATE, _O_END = (
    0, 256, 512, 1024, 1536, 1568, 3872, 5408, 6944, 11040)

DFT_N2 = 128


def _cparams(sem):
    return pltpu.CompilerParams(dimension_semantics=sem, vmem_limit_bytes=VMEM_LIMIT)


def _dot(a, b):
    return jnp.dot(a, b, preferred_element_type=F32)


def _dot_nt(a, b):
    return lax.dot_general(a, b, (((1,), (1,)), ((), ())), preferred_element_type=F32)


def _dot_tn(a, b):
    return lax.dot_general(a, b, (((0,), (0,)), ((), ())), preferred_element_type=F32)


def _split(x):
    hi = x.astype(BF16)
    lo = (x - hi.astype(F32)).astype(BF16)
    return hi, lo


def _dot3(a, b):
    ah, al = _split(a)
    bh, bl = _split(b)
    return _dot(ah, bh) + _dot(ah, bl) + _dot(al, bh)


def _rms(x):
    return x * lax.rsqrt(jnp.mean(x * x, axis=-1, keepdims=True) + RMS_EPS)


def _sigmoid(x):
    return 1.0 / (1.0 + jnp.exp(-x))


def _norm_matmul_kernel(x_ref, g_ref, w_ref, o_ref, h_ref):
    @pl.when(pl.program_id(1) == 0)
    def _():
        h_ref[...] = (_rms(x_ref[...]) * g_ref[...]).astype(BF16)

    o_ref[...] = _dot(h_ref[...], w_ref[...]).astype(o_ref.dtype)


def _norm_matmul(x, g, w, tm, tn):
    s, d = x.shape
    n = w.shape[1]
    return pl.pallas_call(
        _norm_matmul_kernel,
        grid=(s // tm, n // tn),
        in_specs=[pl.BlockSpec((tm, d), lambda i, j: (i, 0)),
                  pl.BlockSpec((1, d), lambda i, j: (0, 0)),
                  pl.BlockSpec((d, tn), lambda i, j: (0, j))],
        out_specs=pl.BlockSpec((tm, tn), lambda i, j: (i, j)),
        out_shape=jax.ShapeDtypeStruct((s, n), BF16),
        scratch_shapes=[pltpu.VMEM((tm, d), BF16)],
        compiler_params=_cparams(("parallel", "arbitrary")),
        name="in_proj",
    )(x, g.reshape(1, d), w)


def _group_norm(x, e, gain):
    hi, lo = _split(x * x)
    ms = (_dot(hi, e) + _dot(lo, e)) * (1.0 / HEAD_DIM)
    return x * lax.rsqrt(ms + RMS_EPS) * gain


def _prep_kernel(bq_ref, bk_ref, bv_ref, cq_ref, ck_ref, cv_ref, eb_ref, ec_ref,
                 gbq_ref, gbk_ref, gcq_ref, gck_ref,
                 obq_ref, obk_ref, obv_ref, ocq_ref, ock_ref, ocv_ref):
    scale = HEAD_DIM ** -0.5
    eb = eb_ref[...]
    ec = ec_ref[...]
    f32 = lambda ref: ref[...].astype(F32)
    obq_ref[...] = (_group_norm(f32(bq_ref), eb, gbq_ref[...]) * scale).astype(BF16)
    obk_ref[...] = _group_norm(f32(bk_ref), eb, gbk_ref[...]).astype(BF16)
    obv_ref[...] = bv_ref[...]
    ocq_ref[...] = (_group_norm(f32(cq_ref), ec, gcq_ref[...]) * (scale * LOG2E)).T.astype(BF16)
    ock_ref[...] = _group_norm(f32(ck_ref), ec, gck_ref[...]).astype(BF16)
    cvt = f32(cv_ref).T
    ones = jnp.ones((DIFF_VT - DIFF_DV, cvt.shape[1]), F32)
    ocv_ref[...] = jnp.concatenate(
        [t for h in range(DIFF_HEADS) for t in (cvt[h * DIFF_DV:(h + 1) * DIFF_DV], ones)], axis=0).astype(BF16)


def _block_diag_ones(width):
    idx = np.arange(width) // HEAD_DIM
    return jnp.asarray(idx[:, None] == idx[None, :], dtype=BF16)


def _prep(z, gbq, gbk, gcq, gck, tm):
    s = z.shape[0]
    wb, wc = DIL_GROUPS * DIL_HEADS * HEAD_DIM, DIFF_HEADS * 2 * HEAD_DIM
    zspec = lambda w, off: pl.BlockSpec((tm, w), lambda i: (i, off // w))
    cspec = lambda r, c: pl.BlockSpec((r, c), lambda i: (0, 0))
    ospec = lambda w: pl.BlockSpec((tm, w), lambda i: (i, 0))
    tspec = lambda w: pl.BlockSpec((w, tm), lambda i: (0, i))
    return pl.pallas_call(
        _prep_kernel,
        grid=(s // tm,),
        in_specs=[zspec(wb, Z_BQ), zspec(wb, Z_BK), zspec(wb, Z_BV),
                  zspec(wc, Z_CQ), zspec(wc, Z_CK), zspec(wc, Z_CV),
                  cspec(wb, wb), cspec(wc, wc),
                  cspec(1, wb), cspec(1, wb), cspec(1, wc), cspec(1, wc)],
        out_specs=[ospec(wb), ospec(wb), ospec(wb), tspec(wc), ospec(wc), tspec(DIFF_HEADS * DIFF_VT)],
        out_shape=[jax.ShapeDtypeStruct((s, wb), BF16)] * 3
        + [jax.ShapeDtypeStruct((wc, s), BF16), jax.ShapeDtypeStruct((s, wc), BF16),
           jax.ShapeDtypeStruct((DIFF_HEADS * DIFF_VT, s), BF16)],
        compiler_params=_cparams(("parallel",)),
        name="qk_prep",
    )(z, z, z, z, z, z, _block_diag_ones(wb), _block_diag_ones(wc),
      gbq.reshape(1, wb), gbk.reshape(1, wb), gcq.reshape(1, wc), gck.reshape(1, wc))


def _t5_bucket(rel):
    half = T5_BUCKETS // 2
    max_exact = half // 2
    ret = np.where(rel > 0, half, 0)
    n = np.abs(rel)
    nf = np.maximum(n, 1).astype(np.float64)
    large = max_exact + (np.log(nf / max_exact) / math.log(T5_MAX_DIST / max_exact)
                         * (half - max_exact)).astype(np.int64)
    large = np.minimum(large, half - 1)
    return ret + np.where(n < max_exact, n, large)


def _bias_kernel(tab_ref, rng_ref, idx_ref, o_ref, *, head_base, out_scale):
    t = pl.program_id(0)
    col = head_base + pl.program_id(1)
    o_ref[0, 0] = jnp.zeros(o_ref.shape[2:], F32)

    def one_bucket(b, carry):
        o_ref[0, 0] = jnp.where(idx_ref[0] == b, tab_ref[b, col] * out_scale, o_ref[0, 0])
        return carry

    lax.fori_loop(rng_ref[t, 0], rng_ref[t, 1] + 1, one_bucket, 0)


def _bias_tiles(t5_bias, idx, n_heads, head_base, out_scale=1.0):
    nt, r, c = idx.shape
    ranges = np.stack([idx.reshape(nt, -1).min(axis=1), idx.reshape(nt, -1).max(axis=1)], axis=1).astype(np.int32)
    return pl.pallas_call(
        functools.partial(_bias_kernel, head_base=head_base, out_scale=out_scale),
        grid=(nt, n_heads),
        in_specs=[pl.BlockSpec(memory_space=pltpu.SMEM), pl.BlockSpec(memory_space=pltpu.SMEM),
                  pl.BlockSpec((1, r, c), lambda t, h: (t, 0, 0))],
        out_specs=pl.BlockSpec((1, 1, r, c), lambda t, h: (t, h, 0, 0)),
        out_shape=jax.ShapeDtypeStruct((nt, n_heads, r, c), F32),
        compiler_params=_cparams(("parallel", "parallel")),
        name="t5_bias_tiles",
    )(t5_bias, jnp.asarray(ranges), jnp.asarray(idx))


def _diff_attn_kernel(qt_ref, k_ref, vt_ref, bias_ref, lam_ref, g_ref, o_ref,
                      qa_ref, qb_ref, m_ref, acc_ref, s_ref, *, lam_init, qc, far_lo, far_hi):
    j = pl.program_id(2)
    t = qt_ref.shape[1]

    @pl.when(j == 0)
    def _():
        qt = qt_ref[...]
        row = lax.broadcasted_iota(jnp.int32, qt.shape, 0)
        qa_ref[...] = jnp.where(row < HEAD_DIM, qt, jnp.zeros_like(qt))
        qb_ref[...] = jnp.where(row >= HEAD_DIM, qt, jnp.zeros_like(qt))
        m_ref[...] = jnp.full(m_ref.shape, -jnp.inf, F32)
        acc_ref[...] = jnp.zeros(acc_ref.shape, F32)

    starts = [sum(qc[:u]) for u in range(len(qc))]
    cuts = [slice(a, a + w) for a, w in zip(starts, qc)]
    chains = [(0, cs) for cs in cuts] + [(1, cs) for cs in reversed(cuts)]
    q_refs = (qa_ref, qb_ref)

    def step(far):
        k = k_ref[...]
        vt = vt_ref[...]
        if far:
            const = bias_ref[0, 0, 0:1, 0:1]
            scores = lambda c, cols: _dot(k, q_refs[c][:, cols])
        else:
            const = 0.0
            scores = lambda c, cols: _dot(k, q_refs[c][:, cols]) + bias_ref[0, 0, :, cols]
        width = lambda n: chains[n][1].stop - chains[n][1].start
        s_ref[0, :, :width(0)] = scores(*chains[0])
        for n, (c, cols) in enumerate(chains):
            if n + 1 < len(chains):
                s_ref[(n + 1) % 2, :, :width(n + 1)] = scores(*chains[n + 1])
            s = s_ref[n % 2, :, :width(n)]
            m_old = m_ref[c, :, cols]
            m_new = jnp.maximum(m_old, jnp.max(s, axis=0, keepdims=True) + const)
            alpha = jnp.exp2(m_old - m_new)
            p = jnp.exp2(s - (m_new - const)).astype(BF16)
            acc_ref[c, :, cols] = alpha * acc_ref[c, :, cols] + _dot(vt, p)
            m_ref[c, :, cols] = m_new

    off = j - pl.program_id(1) * (t // k_ref.shape[0])
    is_far = jnp.logical_or(off <= far_lo, off >= far_hi)
    pl.when(is_far)(lambda: step(True))
    pl.when(jnp.logical_not(is_far))(lambda: step(False))

    @pl.when(j == pl.num_programs(2) - 1)
    def _():
        lp = lam_ref[...]
        lam = (jnp.exp(jnp.sum(lp[0:1] * lp[1:2], axis=-1, keepdims=True))
               - jnp.exp(jnp.sum(lp[2:3] * lp[3:4], axis=-1, keepdims=True)) + lam_init)
        a0 = acc_ref[0]
        a1 = acc_ref[1]
        o0 = a0[:DIFF_DV] / a0[DIFF_DV:DIFF_DV + 1]
        o1 = a1[:DIFF_DV] / a1[DIFF_DV:DIFF_DV + 1]
        att = o0 - lam * o1
        y = att * lax.rsqrt(jnp.mean(att * att, axis=0, keepdims=True) + RMS_EPS)
        o_ref[...] = y.T * g_ref[...] * (1.0 - lam_init)


def _diff_far_offsets(tq, tk):
    far_lo = (-T5_MAX_DIST - tk + 1) // tk
    far_hi = -(-(T5_MAX_DIST + tq - 1) // tk)
    return far_lo, far_hi


def _diff_attn(cqt, ck, cvt, bias_tiles, lam_p, subln_g, lam_init, t, tk, qc):
    s = ck.shape[0]
    far_lo, far_hi = _diff_far_offsets(t, tk)
    ratio = t // tk
    w = 2 * HEAD_DIM
    return pl.pallas_call(
        functools.partial(_diff_attn_kernel, lam_init=lam_init, qc=qc, far_lo=far_lo, far_hi=far_hi),
        grid=(DIFF_HEADS, s // t, s // tk),
        in_specs=[pl.BlockSpec((w, t), lambda h, i, j: (h, i)),
                  pl.BlockSpec((tk, w), lambda h, i, j: (j, h)),
                  pl.BlockSpec((DIFF_VT, tk), lambda h, i, j: (h, j)),
                  pl.BlockSpec((1, 1, tk, t),
                               lambda h, i, j: (jnp.clip(j - i * ratio, far_lo, far_hi) - far_lo, h, 0, 0)),
                  pl.BlockSpec((4, HEAD_DIM), lambda h, i, j: (0, 0)),
                  pl.BlockSpec((1, DIFF_DV), lambda h, i, j: (0, 0))],
        out_specs=pl.BlockSpec((t, DIFF_DV), lambda h, i, j: (i, h)),
        out_shape=jax.ShapeDtypeStruct((s, DIFF_HEADS * DIFF_DV), F32),
        scratch_shapes=[pltpu.VMEM((w, t), BF16), pltpu.VMEM((w, t), BF16),
                        pltpu.VMEM((2, 1, t), F32), pltpu.VMEM((2, DIFF_VT, t), F32),
                        pltpu.VMEM((2, tk, max(qc)), F32)],
        compiler_params=_cparams(("parallel", "parallel", "arbitrary")),
        name="diff_attn",
    )(cqt, ck, cvt, bias_tiles, lam_p, subln_g.reshape(1, DIFF_DV))


def _dil_kernel(q_ref, kp_ref, kc_ref, kn_ref, vp_ref, vc_ref, vn_ref, bias_ref, o_ref, lse_ref,
                qs_ref, ks_ref, vs_ref, os_ref, ls_ref, *, tq, dil, m_len):
    n = pl.program_id(0)
    hd = DIL_HALF * dil
    body_rows = tq * dil
    halves = DIL_HEADS * HEAD_DIM // LANES

    def put(dst, rows, src_ref):
        x = src_ref[...].astype(F32)
        for t in range(halves):
            dst[t, rows, :] = x[:, t * LANES:(t + 1) * LANES]

    def strided(src, r, count):
        return jnp.concatenate([src[t, pl.ds(r, count, stride=dil), :] for t in range(halves)], axis=1)

    put(qs_ref, slice(0, body_rows), q_ref)
    for dst, (p_ref, c_ref, n_ref) in ((ks_ref, (kp_ref, kc_ref, kn_ref)), (vs_ref, (vp_ref, vc_ref, vn_ref))):
        put(dst, slice(0, hd), p_ref)
        put(dst, slice(hd, hd + body_rows), c_ref)
        put(dst, slice(hd + body_rows, 2 * hd + body_rows), n_ref)
    tk = tq + 2 * DIL_HALF
    a = lax.broadcasted_iota(jnp.int32, (tq, tk), 0)
    c = lax.broadcasted_iota(jnp.int32, (tq, tk), 1)
    delta = c - DIL_HALF - a
    kpos = n * tq - DIL_HALF + c
    valid = jnp.where(jnp.abs(delta) <= DIL_HALF, 1, 0) * jnp.where(kpos >= 0, 1, 0) * jnp.where(kpos < m_len, 1, 0)
    valid = jnp.concatenate([valid] * DIL_HEADS, axis=0)
    lane = lax.broadcasted_iota(jnp.int32, (tq, DIL_HEADS * HEAD_DIM), 1)
    head_masks = [(lane // HEAD_DIM) == h for h in range(DIL_HEADS)]

    def one_subsequence(r, carry):
        q = strided(qs_ref, r, tq).astype(BF16)
        k = strided(ks_ref, r, tk).astype(BF16)
        v = strided(vs_ref, r, tk).astype(BF16)
        qs = jnp.concatenate([jnp.where(hm, q, jnp.zeros_like(q)) for hm in head_masks], axis=0)
        s = _dot_nt(qs, k) + bias_ref[0].reshape(DIL_HEADS * tq, tk)
        s = jnp.where(valid > 0, s, -1e30)
        m = jnp.max(s, axis=-1, keepdims=True)
        e = jnp.exp(s - m)
        l = jnp.sum(e, axis=-1, keepdims=True)
        oh = _dot((e / l).astype(BF16), v)
        lse = m + jnp.log(l)
        o = jnp.zeros(q.shape, F32)
        lse_o = jnp.zeros(q.shape, F32)
        for h, hm in enumerate(head_masks):
            rows = slice(h * tq, (h + 1) * tq)
            o = jnp.where(hm, oh[rows], o)
            lse_o = jnp.where(hm, lse[rows], lse_o)
        for t in range(halves):
            os_ref[t, pl.ds(r, tq, stride=dil), :] = o[:, t * LANES:(t + 1) * LANES]
            ls_ref[t, pl.ds(r, tq, stride=dil), :] = lse_o[:, t * LANES:(t + 1) * LANES]
        return carry

    lax.fori_loop(0, dil, one_subsequence, 0, unroll=min(dil, 4))
    o_ref[...] = jnp.concatenate([os_ref[t] for t in range(halves)], axis=1)
    lse_ref[...] = jnp.concatenate([ls_ref[t] for t in range(halves)], axis=1)


def _dil_attn(bq, bk, bv, bias, gi, dil, tq):
    s = bq.shape[0]
    m_len = s // dil
    gw = DIL_HEADS * HEAD_DIM
    rows = tq * dil
    hd = DIL_HALF * dil
    hb = rows // hd
    last = s // hd - 1
    qspec = pl.BlockSpec((rows, gw), lambda n: (n, gi))
    pspec = pl.BlockSpec((hd, gw), lambda n: (jnp.maximum(n * hb - 1, 0), gi))
    nspec = pl.BlockSpec((hd, gw), lambda n: (jnp.minimum((n + 1) * hb, last), gi))
    ospec = pl.BlockSpec((rows, gw), lambda n: (n, 0))
    return pl.pallas_call(
        functools.partial(_dil_kernel, tq=tq, dil=dil, m_len=m_len),
        grid=(s // rows,),
        in_specs=[qspec, pspec, qspec, nspec, pspec, qspec, nspec,
                  pl.BlockSpec((1, DIL_HEADS, tq, tq + 2 * DIL_HALF), lambda n: (0, 0, 0, 0))],
        out_specs=[ospec, ospec],
        out_shape=[jax.ShapeDtypeStruct((s, gw), F32)] * 2,
        scratch_shapes=[pltpu.VMEM((gw // LANES, rows, LANES), F32),
                        pltpu.VMEM((gw // LANES, rows + 2 * hd, LANES), F32),
                        pltpu.VMEM((gw // LANES, rows + 2 * hd, LANES), F32),
                        pltpu.VMEM((gw // LANES, rows, LANES), F32),
                        pltpu.VMEM((gw // LANES, rows, LANES), F32)],
        compiler_params=_cparams(("parallel",)),
        name=f"dil_attn_g{gi}",
    )(bq, bk, bk, bk, bv, bv, bv, bias)


def _gla_kernel(*refs, reverse, tb, finalize):
    if finalize:
        (q_ref, k_ref, v_ref, lr_ref, wg_ref, gb_ref, tri_ref, ofwd_ref, r_ref, ng_ref,
         o_ref, s_ref, oacc_ref) = refs
    else:
        q_ref, k_ref, v_ref, lr_ref, wg_ref, gb_ref, tri_ref, o_ref, s_ref = refs
        oacc_ref = o_ref
    cw = GLA_CHUNK
    qk = GLA_HEADS * GLA_DK
    vw = GLA_HEADS * GLA_DV

    @pl.when(pl.program_id(0) == 0)
    def _():
        s_ref[...] = jnp.zeros(s_ref.shape, F32)

    logits = _dot(lr_ref[...], wg_ref[...]) + gb_ref[...]
    g = (jnp.minimum(logits, 0.0) - jnp.log(1.0 + jnp.exp(-jnp.abs(logits)))) * (1.0 / GLA_TAU)
    ghi, glo = _split(g)
    tri = tri_ref[...]
    b = _dot(tri, ghi) + _dot(tri, glo)
    qg = (q_ref[...].astype(F32) * (GLA_DK ** -0.5) * jnp.exp(b)).astype(BF16)
    k = k_ref[...].astype(F32)
    kg = (k * jnp.exp(-b)).astype(BF16)
    v = v_ref[...].astype(BF16)

    ones = jnp.ones((cw, LANES), BF16)
    lane_q = lax.broadcasted_iota(jnp.int32, (cw, qk), 1)
    rr = lax.broadcasted_iota(jnp.int32, (GLA_HEADS * cw, cw), 0)
    cc = lax.broadcasted_iota(jnp.int32, (GLA_HEADS * cw, cw), 1)
    tt = rr % cw
    amask = (cc > tt) if reverse else (cc <= tt)
    srow = lax.broadcasted_iota(jnp.int32, (qk, vw), 0) // GLA_DK
    scol = lax.broadcasted_iota(jnp.int32, (qk, vw), 1) // GLA_DV
    bdmask = srow == scol

    n_chunks = tb // cw
    order = range(n_chunks - 1, -1, -1) if reverse else range(n_chunks)
    for ci in order:
        rows = slice(ci * cw, (ci + 1) * cw)
        bc = b[rows]
        b_end = bc[0:1] if reverse else bc[cw - 1:cw]
        kdec = (k[rows] * jnp.exp(b_end - bc)).astype(BF16)
        btot = _dot_tn(ghi[rows], ones) + _dot_tn(glo[rows], ones)
        qg_c = qg[rows]
        qs = jnp.concatenate(
            [jnp.where((lane_q // GLA_DK) == h, qg_c, jnp.zeros_like(qg_c)) for h in range(GLA_HEADS)], axis=0)
        a = jnp.where(amask, _dot_nt(qs, kg[rows]), 0.0)
        obig = _dot(a.astype(BF16), v[rows])
        o_intra = jnp.concatenate(
            [obig[h * cw:(h + 1) * cw, h * GLA_DV:(h + 1) * GLA_DV] for h in range(GLA_HEADS)], axis=1)
        state = s_ref[...]
        oacc_ref[rows, :] = o_intra + _dot(qg_c, state.astype(BF16))
        ds = _dot_tn(kdec, v[rows])
        decay = jnp.exp(btot)
        decay = jnp.concatenate([decay] * GLA_HEADS, axis=1)
        s_ref[...] = jnp.where(bdmask, decay * state + ds, 0.0)

    if finalize:
        o = ofwd_ref[...] + oacc_ref[...]
        r = r_ref[...].astype(F32)
        outs = []
        for h in range(GLA_HEADS):
            sl = slice(h * GLA_DV, (h + 1) * GLA_DV)
            outs.append(_rms(o[:, sl]) * ng_ref[...] * (r[:, sl] * _sigmoid(r[:, sl])))
        o_ref[...] = jnp.concatenate(outs, axis=1)


def _chunk_tri(tb, reverse):
    i = np.arange(tb)
    same = (i[:, None] // GLA_CHUNK) == (i[None, :] // GLA_CHUNK)
    tri = (i[None, :] >= i[:, None]) if reverse else (i[None, :] <= i[:, None])
    return jnp.asarray(same & tri, dtype=BF16)


def _gla_scan(z, wg, gb, tb, reverse, fin=None):
    s = z.shape[0]
    nb = s // tb
    qk = GLA_HEADS * GLA_DK
    vw = GLA_HEADS * GLA_DV
    blk = (lambda i: nb - 1 - i) if reverse else (lambda i: i)
    zspec = lambda w, off: pl.BlockSpec((tb, w), lambda i: (blk(i), off // w))
    cspec = lambda r, c: pl.BlockSpec((r, c), lambda i: (0, 0))
    in_specs = [zspec(qk, Z_AQ), zspec(qk, Z_AK), zspec(vw, Z_AV), zspec(LANES, Z_ALR),
                cspec(LANES, qk), cspec(1, qk), cspec(tb, tb)]
    args = [z, z, z, z, wg, gb, _chunk_tri(tb, reverse)]
    scratch = [pltpu.VMEM((qk, vw), F32)]
    if fin is not None:
        o_fwd, norm_g = fin
        in_specs += [pl.BlockSpec((tb, vw), lambda i: (blk(i), 0)), zspec(vw, Z_AR), cspec(1, GLA_DV)]
        args += [o_fwd, z, norm_g.reshape(1, GLA_DV)]
        scratch.append(pltpu.VMEM((tb, vw), F32))
    return pl.pallas_call(
        functools.partial(_gla_kernel, reverse=reverse, tb=tb, finalize=fin is not None),
        grid=(nb,),
        in_specs=in_specs,
        out_specs=pl.BlockSpec((tb, vw), lambda i: (blk(i), 0)),
        out_shape=jax.ShapeDtypeStruct((s, vw), F32),
        scratch_shapes=scratch,
        compiler_params=_cparams(("arbitrary",)),
        name="gla_bwd" if reverse else "gla_fwd",
    )(*args)


def _hy_pre_kernel(u_ref, up_ref, un_ref, w_ref, b_ref, x0_ref, z_ref, zb_ref, *, tm):
    i = pl.program_id(0)
    u = u_ref[...].astype(F32)
    row = lax.broadcasted_iota(jnp.int32, u.shape, 0)
    prev_row = jnp.where(i == 0, 0.0, up_ref[...].astype(F32)[HALO_ROWS - 1:HALO_ROWS, :])
    next_row = jnp.where(i == pl.num_programs(0) - 1, 0.0, un_ref[...].astype(F32)[0:1, :])
    u_prev = jnp.where(row == 0, prev_row, pltpu.roll(u, 1, axis=0))
    u_next = jnp.where(row == tm - 1, next_row, pltpu.roll(u, tm - 1, axis=0))
    y = b_ref[...] + u_prev * w_ref[0:1] + u * w_ref[1:2] + u_next * w_ref[2:3]
    x0_ref[...] = y[:, :HY_WIDTH]
    z = y[:, HY_WIDTH:2 * HY_WIDTH] * y[:, 2 * HY_WIDTH:]
    z_ref[...] = z
    zb_ref[...] = z.astype(BF16)


def _hy_pre(z, conv_w, conv_b, tm):
    s = z.shape[0]
    w = 3 * HY_WIDTH
    nr = s // HALO_ROWS
    ospec = pl.BlockSpec((tm, HY_WIDTH), lambda i: (i, 0))
    return pl.pallas_call(
        functools.partial(_hy_pre_kernel, tm=tm),
        grid=(s // tm,),
        in_specs=[pl.BlockSpec((tm, w), lambda i: (i, 0)),
                  pl.BlockSpec((HALO_ROWS, w), lambda i: (jnp.maximum(i * (tm // HALO_ROWS) - 1, 0), 0)),
                  pl.BlockSpec((HALO_ROWS, w), lambda i: (jnp.minimum((i + 1) * (tm // HALO_ROWS), nr - 1), 0)),
                  pl.BlockSpec((3, w), lambda i: (0, 0)),
                  pl.BlockSpec((1, w), lambda i: (0, 0))],
        out_specs=[ospec, ospec, ospec],
        out_shape=[jax.ShapeDtypeStruct((s, HY_WIDTH), F32)] * 2 + [jax.ShapeDtypeStruct((s, HY_WIDTH), BF16)],
        compiler_params=_cparams(("parallel",)),
        name="hyena_pre",
    )(z, z, z, conv_w, conv_b.reshape(1, w))


def _hy_filter_kernel(emb_ref, w1_ref, b1_ref, f1_ref, w2_ref, b2_ref, f2_ref, w3_ref, dl_ref,
                      kb_ref, norm_ref, *, tl, seq):
    i = pl.program_id(0)
    half = tl // 2
    emb = emb_ref[...]
    x = jnp.concatenate([emb[:half], emb[half:]], axis=1)
    h = jnp.sin(f1_ref[...] * (_dot3(x, w1_ref[...]) + b1_ref[...]))
    h = jnp.sin(f2_ref[...] * (_dot3(h, w2_ref[...]) + b2_ref[...]))
    h = _dot3(h, w3_ref[...])
    back = i * tl >= seq
    pick = lambda y: jnp.where(back, y[:, HY_WIDTH:], y[:, :HY_WIDTH])
    h = jnp.concatenate([pick(h[:, :2 * HY_WIDTH]), pick(h[:, 2 * HY_WIDTH:])], axis=0)
    h = h * jnp.exp(-emb[:, 0:1] * dl_ref[...])
    row = i * tl + lax.broadcasted_iota(jnp.int32, h.shape, 0)
    h = jnp.where(row == seq, 0.0, h)
    kb_ref[...] = h.astype(BF16)

    @pl.when(i == 0)
    def _():
        norm_ref[...] = jnp.zeros(norm_ref.shape, F32)

    norm_ref[...] += jnp.sum(jnp.abs(h), axis=0, keepdims=True)


def _hy_positions(seq):
    t = np.linspace(0.0, 1.0, seq, dtype=np.float32)[:, None]
    bands = (HY_EMB - 1) // 2
    freqs = np.linspace(1e-4, bands - 1, bands, dtype=np.float32)[None]
    w = (np.float32(2.0 * math.pi) * np.arange(seq, dtype=np.float32)[:, None] / np.float32(seq))
    zf = np.concatenate([t, np.cos(freqs * w), -np.sin(freqs * w)], axis=-1).astype(np.float32)
    pos = np.concatenate([np.arange(seq), [0], np.arange(seq - 1, 0, -1)])
    return jnp.asarray(np.pad(zf[pos], ((0, 0), (0, LANES - HY_EMB))))


def _block_diag2(w):
    z = jnp.zeros_like(w)
    return jnp.concatenate([jnp.concatenate([w, z], axis=1), jnp.concatenate([z, w], axis=1)], axis=0)


def _hy_filter(emb, w1, b1, fr1, w2, b2, fr2, w3, tl):
    n = emb.shape[0]
    seq = n // 2
    w1p = jnp.pad(w1, ((0, LANES - HY_EMB), (0, 0)))
    twice = lambda v: jnp.concatenate([v, v])[None]
    min_decay = math.log(HY_DECAY_TARGET) / HY_SLOW_DECAY
    max_decay = math.log(HY_DECAY_TARGET) / HY_FAST_DECAY
    deltas = jnp.asarray(np.abs(np.linspace(min_decay, max_decay, HY_WIDTH, dtype=np.float32))[None])
    cs = lambda r, c: pl.BlockSpec((r, c), lambda i: (0, 0))
    return pl.pallas_call(
        functools.partial(_hy_filter_kernel, tl=tl, seq=seq),
        grid=(n // tl,),
        in_specs=[pl.BlockSpec((tl, LANES), lambda i: (i, 0)),
                  cs(2 * LANES, LANES), cs(1, LANES), cs(1, LANES),
                  cs(LANES, LANES), cs(1, LANES), cs(1, LANES),
                  cs(LANES, 4 * HY_WIDTH), cs(1, HY_WIDTH)],
        out_specs=[pl.BlockSpec((tl, HY_WIDTH), lambda i: (i, 0)), pl.BlockSpec((1, HY_WIDTH), lambda i: (0, 0))],
        out_shape=[jax.ShapeDtypeStruct((n, HY_WIDTH), BF16), jax.ShapeDtypeStruct((1, HY_WIDTH), F32)],
        compiler_params=_cparams(("arbitrary",)),
        name="hyena_filter",
    )(emb, _block_diag2(w1p), twice(b1), twice(fr1), _block_diag2(w2), twice(b2), twice(fr2),
      _block_diag2(w3), deltas)


def _dft_tables(n1):
    n = n1 * DFT_N2
    a = np.arange(n1, dtype=np.int64)
    ang1 = (2.0 * math.pi / n1) * ((a[:, None] * a[None, :]) % n1)
    c1, s1 = np.cos(ang1), np.sin(ang1)
    k1 = np.arange(n1, dtype=np.int64)[:, None, None]
    k2 = np.arange(DFT_N2, dtype=np.int64)[None, :, None]
    n2 = np.arange(DFT_N2, dtype=np.int64)[None, None, :]
    ang = (2.0 * math.pi / n) * ((n2 * (k1 + n1 * k2)) % n)
    gc, gs = np.cos(ang), np.sin(ang)
    tb = lambda x: jnp.asarray(np.ascontiguousarray(x).astype(BF16))
    return dict(c1=tb(c1), s1=tb(s1), gc=tb(gc), gs=tb(gs),
                gct=tb(np.swapaxes(gc, 1, 2)), gst=tb(np.swapaxes(gs, 1, 2)))


def _dft1_kernel(c_ref, s_ref, x_ref, re_ref, im_ref):
    x = x_ref[...]
    re_ref[...] = _dot(c_ref[...], x).astype(BF16)
    im_ref[...] = (-_dot(s_ref[...], x)).astype(BF16)


def _dft1(x, c1, s1, cb):
    k1, w = x.shape
    n1 = c1.shape[0]
    ospec = pl.BlockSpec((n1, cb), lambda j: (0, j))
    return pl.pallas_call(
        _dft1_kernel,
        grid=(w // cb,),
        in_specs=[pl.BlockSpec((n1, k1), lambda j: (0, 0)), pl.BlockSpec((n1, k1), lambda j: (0, 0)),
                  pl.BlockSpec((k1, cb), lambda j: (0, j))],
        out_specs=[ospec, ospec],
        out_shape=[jax.ShapeDtypeStruct((n1, w), BF16)] * 2,
        compiler_params=_cparams(("parallel",)),
        name="dft_stage1",
    )(c1, s1, x)


def _dft2_filter_kernel(gc_ref, gs_ref, are_ref, aim_ref, norm_ref, kre_ref, kim_ref, *, kb):
    inv = 1.0 / norm_ref[...]
    for t in range(kb):
        gc, gs = gc_ref[t], gs_ref[t]
        are, aim = are_ref[t], aim_ref[t]
        kre_ref[t] = ((_dot(gc, are) + _dot(gs, aim)) * inv).astype(BF16)
        kim_ref[t] = ((_dot(gc, aim) - _dot(gs, are)) * inv).astype(BF16)


def _dft2_filter(tabs, are, aim, norm, kb):
    n1, _, c = are.shape
    gspec = pl.BlockSpec((kb, DFT_N2, DFT_N2), lambda i: (i, 0, 0))
    aspec = pl.BlockSpec((kb, DFT_N2, c), lambda i: (i, 0, 0))
    return pl.pallas_call(
        functools.partial(_dft2_filter_kernel, kb=kb),
        grid=(n1 // kb,),
        in_specs=[gspec, gspec, aspec, aspec, pl.BlockSpec((1, c), lambda i: (0, 0))],
        out_specs=[aspec, aspec],
        out_shape=[jax.ShapeDtypeStruct((n1, DFT_N2, c), BF16)] * 2,
        compiler_params=_cparams(("parallel",)),
        name="dft_stage2_filter",
    )(tabs["gc"], tabs["gs"], are, aim, norm)


def _conv_mid_kernel(gc_ref, gs_ref, gct_ref, gst_ref, are_ref, aim_ref, kre_ref, kim_ref,
                     bre_ref, bim_ref, *, kb):
    for t in range(kb):
        gc, gs = gc_ref[t], gs_ref[t]
        are, aim = are_ref[t], aim_ref[t]
        xre = _dot(gc, are) + _dot(gs, aim)
        xim = _dot(gc, aim) - _dot(gs, are)
        kre, kim = kre_ref[t].astype(F32), kim_ref[t].astype(F32)
        yre = (xre * kre - xim * kim).astype(BF16)
        yim = (xre * kim + xim * kre).astype(BF16)
        gct, gst = gct_ref[t], gst_ref[t]
        bre_ref[t] = (_dot(gct, yre) - _dot(gst, yim)).astype(BF16)
        bim_ref[t] = (_dot(gct, yim) + _dot(gst, yre)).astype(BF16)


def _conv_mid(tabs, are, aim, kre, kim, kb):
    n1, _, c = are.shape
    gspec = pl.BlockSpec((kb, DFT_N2, DFT_N2), lambda i: (i, 0, 0))
    aspec = pl.BlockSpec((kb, DFT_N2, c), lambda i: (i, 0, 0))
    return pl.pallas_call(
        functools.partial(_conv_mid_kernel, kb=kb),
        grid=(n1 // kb,),
        in_specs=[gspec, gspec, gspec, gspec, aspec, aspec, aspec, aspec],
        out_specs=[aspec, aspec],
        out_shape=[jax.ShapeDtypeStruct((n1, DFT_N2, c), BF16)] * 2,
        compiler_params=_cparams(("parallel",)),
        name="conv_spectral",
    )(tabs["gc"], tabs["gs"], tabs["gct"], tabs["gst"], are, aim, kre, kim)


def _idft1_kernel(c_ref, s_ref, bre_ref, bim_ref, z_ref, x0_ref, skip_ref, o_ref, *, inv_n):
    y = (_dot(c_ref[...], bre_ref[...]) - _dot(s_ref[...], bim_ref[...])) * inv_n
    o_ref[...] = x0_ref[...] * (y + z_ref[...] * skip_ref[...])


def _idft1(c1h, s1h, bre, bim, z, x0, skip_t, cb):
    ko, n1 = c1h.shape
    w = bre.shape[1]
    n = n1 * DFT_N2
    ospec = pl.BlockSpec((ko, cb), lambda j: (0, j))
    return pl.pallas_call(
        functools.partial(_idft1_kernel, inv_n=1.0 / n),
        grid=(w // cb,),
        in_specs=[pl.BlockSpec((ko, n1), lambda j: (0, 0)), pl.BlockSpec((ko, n1), lambda j: (0, 0)),
                  pl.BlockSpec((n1, cb), lambda j: (0, j)), pl.BlockSpec((n1, cb), lambda j: (0, j)),
                  ospec, ospec, pl.BlockSpec((1, cb), lambda j: (0, 0))],
        out_specs=ospec,
        out_shape=jax.ShapeDtypeStruct((ko, w), F32),
        compiler_params=_cparams(("parallel",)),
        name="idft_stage1",
    )(c1h, s1h, bre, bim, z, x0, skip_t)


def _hyena(z_all, tabs, emb, p, i, cb, kb):
    seq = z_all.shape[0]
    n1 = 2 * seq // DFT_N2
    c = HY_WIDTH
    x0, zf, zb = _hy_pre(z_all, p["hy_conv_w"][i], p["hy_conv_b"][i], min(512, seq))
    kern_b, norm = _hy_filter(emb, p["hy_w1"][i], p["hy_b1"][i], p["hy_freq1"][i], p["hy_w2"][i],
                                 p["hy_b2"][i], p["hy_freq2"][i], p["hy_w3"][i], min(1024, seq))
    wide = DFT_N2 * c
    kre, kim = _dft1(kern_b.reshape(n1, wide), tabs["c1"], tabs["s1"], cb)
    kre, kim = _dft2_filter(tabs, kre.reshape(n1, DFT_N2, c), kim.reshape(n1, DFT_N2, c), norm, kb)
    are, aim = _dft1(zb.reshape(n1 // 2, wide), tabs["c1"][:, :n1 // 2], tabs["s1"][:, :n1 // 2], cb)
    bre, bim = _conv_mid(tabs, are.reshape(n1, DFT_N2, c), aim.reshape(n1, DFT_N2, c), kre, kim, kb)
    skip_t = jnp.tile(p["hy_skip"][i][None], (1, cb // c))
    yd = _idft1(tabs["c1"][:n1 // 2], tabs["s1"][:n1 // 2], bre.reshape(n1, wide), bim.reshape(n1, wide),
                zf.reshape(n1 // 2, wide), x0.reshape(n1 // 2, wide), skip_t, cb)
    return yd.reshape(seq, c)


def _combine_kernel(x_ref, ya_ref, o0_ref, o1_ref, o2_ref, l0_ref, l1_ref, l2_ref, yc_ref, yd_ref,
                    g0_ref, g1_ref, g2_ref, g3_ref, pa_ref, pb_ref, pc_ref, pd_ref, wo_ref, o_ref):
    l0, l1, l2 = l0_ref[...], l1_ref[...], l2_ref[...]
    mx = jnp.maximum(jnp.maximum(l0, l1), l2)
    e0, e1, e2 = jnp.exp(l0 - mx), jnp.exp(l1 - mx), jnp.exp(l2 - mx)
    yb = (e0 * o0_ref[...] + e1 * o1_ref[...] + e2 * o2_ref[...]) / (e0 + e1 + e2)
    gate = lambda ref: _sigmoid(ref[...].astype(F32))
    m = (gate(g0_ref) * _dot(ya_ref[...].astype(BF16), pa_ref[...])
         + gate(g1_ref) * _dot(yb.astype(BF16), pb_ref[...])
         + gate(g2_ref) * _dot(yc_ref[...].astype(BF16), pc_ref[...])
         + gate(g3_ref) * _dot(yd_ref[...].astype(BF16), pd_ref[...]))
    o_ref[...] = x_ref[...] + _dot(m.astype(BF16), wo_ref[...])


def _combine(x, z, ya, dil_outs, yc, yd, pa, pb, pc, pd, wo, tm):
    s, d = x.shape
    rs = lambda w: pl.BlockSpec((tm, w), lambda i: (i, 0))
    gs = lambda b: pl.BlockSpec((tm, d), lambda i: (i, Z_GATE // d + b))
    ws = lambda a: pl.BlockSpec(a.shape, lambda i: (0, 0))
    (o0, l0), (o1, l1), (o2, l2) = dil_outs
    gw = DIL_HEADS * HEAD_DIM
    return pl.pallas_call(
        _combine_kernel,
        grid=(s // tm,),
        in_specs=[rs(d), rs(ya.shape[1]), rs(gw), rs(gw), rs(gw), rs(gw), rs(gw), rs(gw),
                  rs(yc.shape[1]), rs(yd.shape[1]), gs(0), gs(1), gs(2), gs(3),
                  ws(pa), ws(pb), ws(pc), ws(pd), ws(wo)],
        out_specs=rs(d),
        out_shape=jax.ShapeDtypeStruct((s, d), F32),
        compiler_params=_cparams(("parallel",)),
        name="combine",
    )(x, ya, o0, o1, o2, l0, l1, l2, yc, yd, z, z, z, z, pa, pb, pc, pd, wo)


def _mlp_kernel(x_ref, g_ref, w1_ref, w2_ref, o_ref, h_ref, acc_ref):
    j = pl.program_id(1)

    @pl.when(j == 0)
    def _():
        h_ref[...] = (_rms(x_ref[...]) * g_ref[...]).astype(BF16)
        acc_ref[...] = jnp.zeros(acc_ref.shape, F32)

    a = jnp.maximum(_dot(h_ref[...], w1_ref[...]), 0.0)
    acc_ref[...] += _dot((a * a).astype(BF16), w2_ref[...])

    @pl.when(j == pl.num_programs(1) - 1)
    def _():
        o_ref[...] = x_ref[...] + acc_ref[...]


def _mlp(x, g, w1, w2, tm, tf):
    s, d = x.shape
    ff = w1.shape[1]
    return pl.pallas_call(
        _mlp_kernel,
        grid=(s // tm, ff // tf),
        in_specs=[pl.BlockSpec((tm, d), lambda i, j: (i, 0)),
                  pl.BlockSpec((1, d), lambda i, j: (0, 0)),
                  pl.BlockSpec((d, tf), lambda i, j: (0, j)),
                  pl.BlockSpec((tf, d), lambda i, j: (j, 0))],
        out_specs=pl.BlockSpec((tm, d), lambda i, j: (i, 0)),
        out_shape=jax.ShapeDtypeStruct((s, d), F32),
        scratch_shapes=[pltpu.VMEM((tm, d), BF16), pltpu.VMEM((tm, d), F32)],
        compiler_params=_cparams(("parallel", "arbitrary")),
        name="mlp",
    )(x, g.reshape(1, d), w1, w2)


def _permute_w_in(w):
    d = w.shape[0]
    pieces = [w[:, _O_DU:_O_GATE], w[:, _O_B:_O_C], w[:, _O_AQ:_O_AK], w[:, _O_C:_O_DU],
              w[:, _O_AV:_O_AR], w[:, _O_AR:_O_ALR], w[:, _O_AK:_O_AV], w[:, _O_ALR:_O_B],
              jnp.zeros((d, Z_GATE - Z_ALR - 2 * GLA_RANK), w.dtype), w[:, _O_GATE:_O_END]]
    return jnp.concatenate(pieces, axis=1).astype(BF16)


def _dil_bias_idx(tq, dil):
    a = np.arange(tq)[:, None]
    c = np.arange(tq + 2 * DIL_HALF)[None, :]
    delta = c - DIL_HALF - a
    return _t5_bucket(delta * dil)[None].astype(np.int32)


def _diff_bias_idx(tq, tk):
    far_lo, far_hi = _diff_far_offsets(tq, tk)
    kk = np.arange(tk)[:, None]
    qq = np.arange(tq)[None, :]
    return np.stack([_t5_bucket(o * tk + kk - qq) for o in range(far_lo, far_hi + 1)]).astype(np.int32)


def _forward(x, p, *, t_diff, tk_diff, qc_diff, tq_dil, tb_gla, tm_proj, tn_proj, tm_row, tm_mlp, tf_mlp, cb_dft, kb_dft):
    seq = x.shape[0]
    depth = p["w_in"].shape[0]
    n_dil_bias = DIL_GROUPS * DIL_HEADS
    tq_dil = [min(tq, seq // dil) for tq, (_, dil) in zip(tq_dil, DIL_PATTERNS)]
    dil_bias = [_bias_tiles(p["t5_bias"], _dil_bias_idx(tq, dil), DIL_HEADS, gi * DIL_HEADS)
                for gi, (tq, (_, dil)) in enumerate(zip(tq_dil, DIL_PATTERNS))]
    t_diff, tk_diff = min(t_diff, seq), min(tk_diff, seq)
    diff_bias = _bias_tiles(p["t5_bias"], _diff_bias_idx(t_diff, tk_diff), DIFF_HEADS, n_dil_bias, LOG2E)
    tabs = _dft_tables(2 * seq // DFT_N2)
    emb = _hy_positions(seq)
    rep = lambda g, n: jnp.tile(g, n)
    for i in range(depth):
        z = _norm_matmul(x, p["norm1_g"][i], _permute_w_in(p["w_in"][i]), tm_proj, tn_proj)
        wg = [jnp.zeros((LANES, GLA_HEADS * GLA_DK), F32).at[j * GLA_RANK:(j + 1) * GLA_RANK].set(
            p["gla_gate_w"][i, j]).astype(BF16) for j in range(2)]
        gb = p["gla_gate_b"][i]
        o_fwd = _gla_scan(z, wg[0], gb[0:1], tb_gla, False)
        ya = _gla_scan(z, wg[1], gb[1:2], tb_gla, True, fin=(o_fwd, p["gla_norm_g"][i]))
        bq, bk, bv, cq, ck, cv = _prep(
            z, jnp.repeat(p["dil_qnorm_g"][i], DIL_HEADS, axis=0).reshape(-1),
            jnp.repeat(p["dil_knorm_g"][i], DIL_HEADS, axis=0).reshape(-1),
            rep(p["diff_qnorm_g"][i], 2 * DIFF_HEADS), rep(p["diff_knorm_g"][i], 2 * DIFF_HEADS), tm_row)
        dil_outs = [_dil_attn(bq, bk, bv, dil_bias[gi], gi, dil, tq_dil[gi])
                    for gi, (_, dil) in enumerate(DIL_PATTERNS)]
        lam_init = 0.8 - 0.6 * math.exp(-0.3 * i)
        yc = _diff_attn(cq, ck, cv, diff_bias, p["diff_lambda"][i], p["diff_subln_g"][i], lam_init,
                        t_diff, tk_diff, qc_diff)
        yd = _hyena(z, tabs, emb, p, i, cb_dft, kb_dft)
        x = _combine(x, z, ya, dil_outs, yc, yd, p["proj_a"][i].astype(BF16), p["proj_b"][i].astype(BF16),
                     p["proj_c"][i].astype(BF16), p["proj_d"][i].astype(BF16), p["w_out"][i].astype(BF16),
                     tm_row)
        x = _mlp(x, p["norm2_g"][i], p["mlp_w1"][i].astype(BF16), p["mlp_w2"][i].astype(BF16), tm_mlp, tf_mlp)
    return x


def kernel(x, t5_bias, norm1_g, w_in, gla_gate_w, gla_gate_b, gla_norm_g, dil_qnorm_g, dil_knorm_g,
           diff_qnorm_g, diff_knorm_g, diff_lambda, diff_subln_g, hy_conv_w, hy_conv_b, hy_w1, hy_b1,
           hy_freq1, hy_w2, hy_b2, hy_freq2, hy_w3, hy_skip, proj_a, proj_b, proj_c, proj_d, w_out,
           norm2_g, mlp_w1, mlp_w2):
    p = dict(t5_bias=t5_bias, norm1_g=norm1_g, w_in=w_in, gla_gate_w=gla_gate_w, gla_gate_b=gla_gate_b,
             gla_norm_g=gla_norm_g, dil_qnorm_g=dil_qnorm_g, dil_knorm_g=dil_knorm_g,
             diff_qnorm_g=diff_qnorm_g, diff_knorm_g=diff_knorm_g, diff_lambda=diff_lambda,
             diff_subln_g=diff_subln_g, hy_conv_w=hy_conv_w, hy_conv_b=hy_conv_b, hy_w1=hy_w1, hy_b1=hy_b1,
             hy_freq1=hy_freq1, hy_w2=hy_w2, hy_b2=hy_b2, hy_freq2=hy_freq2, hy_w3=hy_w3, hy_skip=hy_skip,
             proj_a=proj_a, proj_b=proj_b, proj_c=proj_c, proj_d=proj_d, w_out=w_out, norm2_g=norm2_g,
             mlp_w1=mlp_w1, mlp_w2=mlp_w2)
    b, s, d = x.shape
    outs = [_forward(x[bi], p, t_diff=2048, tk_diff=1024, qc_diff=(256, 768, 1024), tq_dil=(256, 256, 128), tb_gla=512, tm_proj=2048, tn_proj=512, tm_row=512,
                     tm_mlp=1024, tf_mlp=1024, cb_dft=4096, kb_dft=8) for bi in range(b)]
    return jnp.stack(outs)
```

```python
import functools
import math

import jax
import jax.numpy as jnp
import numpy as np
from jax import lax
from jax.experimental import pallas as pl
from jax.experimental.pallas import tpu as pltpu

F32 = jnp.float32
BF16 = jnp.bfloat16

D_MODEL = 1024
HEAD_DIM = 64
GLA_HEADS = 4
GLA_DK = 64
GLA_DV = 128
GLA_RANK = 16
GLA_TAU = 16.0
GLA_CHUNK = 64
DIL_PATTERNS = ((128, 1), (512, 4), (2048, 16))
DIL_GROUPS = 3
DIL_HEADS = 4
DIL_HALF = 64
DIFF_HEADS = 4
DIFF_DV = 128
DIFF_VT = DIFF_DV + 16
HY_WIDTH = 512
HY_EMB = 33
HY_FFN = 64
HY_DECAY_TARGET = 1e-2
HY_FAST_DECAY = 0.3
HY_SLOW_DECAY = 1.5
T5_BUCKETS = 32
T5_MAX_DIST = 1024
N_BIAS_HEADS = 16
D_FF = 4096
RMS_EPS = 1e-6
LOG2E = math.log2(math.e)

LANES = 128
HALO_ROWS = 16
VMEM_LIMIT = 48 * 1024 * 1024

Z_DU = 0
Z_BQ, Z_BK, Z_BV = 1536, 2304, 3072
Z_AQ = 3840
Z_CQ, Z_CK, Z_CV = 4096, 4608, 5120
Z_AV, Z_AR = 5632, 6144
Z_AK = 6656
Z_ALR = 6912
Z_GATE = 7168
Z_COLS = 11264

_O_AQ, _O_AK, _O_AV, _O_AR, _O_ALR, _O_B, _O_C, _O_DU, _O_GATE, _O_END = (
    0, 256, 512, 1024, 1536, 1568, 3872, 5408, 6944, 11040)

DFT_N2 = 128


def _cparams(sem):
    return pltpu.CompilerParams(dimension_semantics=sem, vmem_limit_bytes=VMEM_LIMIT)


def _dot(a, b):
    return jnp.dot(a, b, preferred_element_type=F32)


def _dot_nt(a, b):
    return lax.dot_general(a, b, (((1,), (1,)), ((), ())), preferred_element_type=F32)


def _dot_tn(a, b):
    return lax.dot_general(a, b, (((0,), (0,)), ((), ())), preferred_element_type=F32)


def _split(x):
    hi = x.astype(BF16)
    lo = (x - hi.astype(F32)).astype(BF16)
    return hi, lo


def _dot3(a, b):
    ah, al = _split(a)
    bh, bl = _split(b)
    return _dot(ah, bh) + _dot(ah, bl) + _dot(al, bh)


def _rms(x):
    return x * lax.rsqrt(jnp.mean(x * x, axis=-1, keepdims=True) + RMS_EPS)


def _sigmoid(x):
    return 1.0 / (1.0 + jnp.exp(-x))


def _norm_matmul_kernel(x_ref, g_ref, w_ref, o_ref, h_ref):
    @pl.when(pl.program_id(1) == 0)
    def _():
        h_ref[...] = (_rms(x_ref[...]) * g_ref[...]).astype(BF16)

    o_ref[...] = _dot(h_ref[...], w_ref[...]).astype(o_ref.dtype)


def _norm_matmul(x, g, w, tm, tn):
    s, d = x.shape
    n = w.shape[1]
    return pl.pallas_call(
        _norm_matmul_kernel,
        grid=(s // tm, n // tn),
        in_specs=[pl.BlockSpec((tm, d), lambda i, j: (i, 0)),
                  pl.BlockSpec((1, d), lambda i, j: (0, 0)),
                  pl.BlockSpec((d, tn), lambda i, j: (0, j))],
        out_specs=pl.BlockSpec((tm, tn), lambda i, j: (i, j)),
        out_shape=jax.ShapeDtypeStruct((s, n), BF16),
        scratch_shapes=[pltpu.VMEM((tm, d), BF16)],
        compiler_params=_cparams(("parallel", "arbitrary")),
        name="in_proj",
    )(x, g.reshape(1, d), w)


def _group_norm(x, e, gain):
    hi, lo = _split(x * x)
    ms = (_dot(hi, e) + _dot(lo, e)) * (1.0 / HEAD_DIM)
    return x * lax.rsqrt(ms + RMS_EPS) * gain


def _prep_kernel(bq_ref, bk_ref, bv_ref, cq_ref, ck_ref, cv_ref, eb_ref, ec_ref,
                 gbq_ref, gbk_ref, gcq_ref, gck_ref,
                 obq_ref, obk_ref, obv_ref, ocq_ref, ock_ref, ocv_ref):
    scale = HEAD_DIM ** -0.5
    eb = eb_ref[...]
    ec = ec_ref[...]
    f32 = lambda ref: ref[...].astype(F32)
    obq_ref[...] = (_group_norm(f32(bq_ref), eb, gbq_ref[...]) * scale).astype(BF16)
    obk_ref[...] = _group_norm(f32(bk_ref), eb, gbk_ref[...]).astype(BF16)
    obv_ref[...] = bv_ref[...]
    ocq_ref[...] = (_group_norm(f32(cq_ref), ec, gcq_ref[...]) * (scale * LOG2E)).T.astype(BF16)
    ock_ref[...] = _group_norm(f32(ck_ref), ec, gck_ref[...]).astype(BF16)
    cvt = f32(cv_ref).T
    ones = jnp.ones((DIFF_VT - DIFF_DV, cvt.shape[1]), F32)
    ocv_ref[...] = jnp.concatenate(
        [t for h in range(DIFF_HEADS) for t in (cvt[h * DIFF_DV:(h + 1) * DIFF_DV], ones)], axis=0).astype(BF16)


def _block_diag_ones(width):
    idx = np.arange(width) // HEAD_DIM
    return jnp.asarray(idx[:, None] == idx[None, :], dtype=BF16)


def _prep(z, gbq, gbk, gcq, gck, tm):
    s = z.shape[0]
    wb, wc = DIL_GROUPS * DIL_HEADS * HEAD_DIM, DIFF_HEADS * 2 * HEAD_DIM
    zspec = lambda w, off: pl.BlockSpec((tm, w), lambda i: (i, off // w))
    cspec = lambda r, c: pl.BlockSpec((r, c), lambda i: (0, 0))
    ospec = lambda w: pl.BlockSpec((tm, w), lambda i: (i, 0))
    tspec = lambda w: pl.BlockSpec((w, tm), lambda i: (0, i))
    return pl.pallas_call(
        _prep_kernel,
        grid=(s // tm,),
        in_specs=[zspec(wb, Z_BQ), zspec(wb, Z_BK), zspec(wb, Z_BV),
                  zspec(wc, Z_CQ), zspec(wc, Z_CK), zspec(wc, Z_CV),
                  cspec(wb, wb), cspec(wc, wc),
                  cspec(1, wb), cspec(1, wb), cspec(1, wc), cspec(1, wc)],
        out_specs=[ospec(wb), ospec(wb), ospec(wb), tspec(wc), ospec(wc), tspec(DIFF_HEADS * DIFF_VT)],
        out_shape=[jax.ShapeDtypeStruct((s, wb), BF16)] * 3
        + [jax.ShapeDtypeStruct((wc, s), BF16), jax.ShapeDtypeStruct((s, wc), BF16),
           jax.ShapeDtypeStruct((DIFF_HEADS * DIFF_VT, s), BF16)],
        compiler_params=_cparams(("parallel",)),
        name="qk_prep",
    )(z, z, z, z, z, z, _block_diag_ones(wb), _block_diag_ones(wc),
      gbq.reshape(1, wb), gbk.reshape(1, wb), gcq.reshape(1, wc), gck.reshape(1, wc))


def _t5_bucket(rel):
    half = T5_BUCKETS // 2
    max_exact = half // 2
    ret = np.where(rel > 0, half, 0)
    n = np.abs(rel)
    nf = np.maximum(n, 1).astype(np.float64)
    large = max_exact + (np.log(nf / max_exact) / math.log(T5_MAX_DIST / max_exact)
                         * (half - max_exact)).astype(np.int64)
    large = np.minimum(large, half - 1)
    return ret + np.where(n < max_exact, n, large)


def _bias_kernel(tab_ref, rng_ref, idx_ref, o_ref, p_ref, *, head_base, out_scale, chunk):
    t = pl.program_id(0)
    col = head_base + pl.program_id(1)
    rows, width = o_ref.shape[2:]
    slabs = width // LANES
    lo, hi = rng_ref[t, 0], rng_ref[t, 1]

    def one_chunk(ci, carry):
        rs = pl.ds(pl.multiple_of(ci * chunk, chunk), chunk)
        idx = idx_ref[0, rs, :]
        p_ref[rs, :] = lax.fori_loop(
            lo, hi + 1, lambda b, acc: jnp.where(idx == b, tab_ref[b, col] * out_scale, acc),
            jnp.zeros((chunk, LANES), F32))
        return carry

    lax.fori_loop(0, p_ref.shape[0] // chunk, one_chunk, 0)
    for c in range(slabs):
        off = LANES * (slabs - 1 - c)
        o_ref[0, 0, :, c * LANES:(c + 1) * LANES] = p_ref[off:off + rows, :]


def _skewed_rel(rows, width, sign, base):
    slabs = width // LANES
    rho = np.arange(rows + LANES * (slabs - 1))[:, None]
    lane = np.arange(LANES)[None, :]
    return sign * (rho - LANES * (slabs - 1) - lane) + base


def _bias_tiles(t5_bias, idx, rows, width, n_heads, head_base, out_scale=1.0, chunk=64):
    nt, rp, _ = idx.shape
    ranges = np.stack([idx.reshape(nt, -1).min(axis=1), idx.reshape(nt, -1).max(axis=1)], axis=1).astype(np.int32)
    return pl.pallas_call(
        functools.partial(_bias_kernel, head_base=head_base, out_scale=out_scale, chunk=chunk),
        grid=(nt, n_heads),
        in_specs=[pl.BlockSpec(memory_space=pltpu.SMEM), pl.BlockSpec(memory_space=pltpu.SMEM),
                  pl.BlockSpec((1, rp, LANES), lambda t, h: (t, 0, 0))],
        out_specs=pl.BlockSpec((1, 1, rows, width), lambda t, h: (t, h, 0, 0)),
        out_shape=jax.ShapeDtypeStruct((nt, n_heads, rows, width), F32),
        scratch_shapes=[pltpu.VMEM((rp, LANES), F32)],
        compiler_params=_cparams(("parallel", "parallel")),
        name="t5_bias_tiles",
    )(t5_bias, jnp.asarray(ranges), jnp.asarray(idx))


def _diff_attn_kernel(qt_ref, k_ref, vt_ref, bias_ref, lam_ref, g_ref, o_ref,
                      qa_ref, qb_ref, m_ref, acc_ref, s_ref, *, lam_init, qc, far_lo, far_hi):
    j = pl.program_id(2)
    t = qt_ref.shape[1]

    @pl.when(j == 0)
    def _():
        qt = qt_ref[...]
        row = lax.broadcasted_iota(jnp.int32, qt.shape, 0)
        qa_ref[...] = jnp.where(row < HEAD_DIM, qt, jnp.zeros_like(qt))
        qb_ref[...] = jnp.where(row >= HEAD_DIM, qt, jnp.zeros_like(qt))
        m_ref[...] = jnp.full(m_ref.shape, -jnp.inf, F32)
        acc_ref[...] = jnp.zeros(acc_ref.shape, F32)

    starts = [sum(qc[:u]) for u in range(len(qc))]
    cuts = [slice(a, a + w) for a, w in zip(starts, qc)]
    chains = [(0, cs) for cs in cuts] + [(1, cs) for cs in reversed(cuts)]
    q_refs = (qa_ref, qb_ref)

    def step(far):
        k = k_ref[...]
        vt = vt_ref[...]
        if far:
            const = bias_ref[0, 0, 0:1, 0:1]
            scores = lambda c, cols: _dot(k, q_refs[c][:, cols])
        else:
            const = 0.0
            scores = lambda c, cols: _dot(k, q_refs[c][:, cols]) + bias_ref[0, 0, :, cols]
        width = lambda n: chains[n][1].stop - chains[n][1].start
        s_ref[0, :, :width(0)] = scores(*chains[0])
        for n, (c, cols) in enumerate(chains):
            if n + 1 < len(chains):
                s_ref[(n + 1) % 2, :, :width(n + 1)] = scores(*chains[n + 1])
            s = s_ref[n % 2, :, :width(n)]
            m_old = m_ref[c, :, cols]
            m_new = jnp.maximum(m_old, jnp.max(s, axis=0, keepdims=True) + const)
            alpha = jnp.exp2(m_old - m_new)
            p = jnp.exp2(s - (m_new - const)).astype(BF16)
            acc_ref[c, :, cols] = alpha * acc_ref[c, :, cols] + _dot(vt, p)
            m_ref[c, :, cols] = m_new

    off = j - pl.program_id(1) * (t // k_ref.shape[0])
    is_far = jnp.logical_or(off <= far_lo, off >= far_hi)
    pl.when(is_far)(lambda: step(True))
    pl.when(jnp.logical_not(is_far))(lambda: step(False))

    @pl.when(j == pl.num_programs(2) - 1)
    def _():
        lp = lam_ref[...]
        lam = (jnp.exp(jnp.sum(lp[0:1] * lp[1:2], axis=-1, keepdims=True))
               - jnp.exp(jnp.sum(lp[2:3] * lp[3:4], axis=-1, keepdims=True)) + lam_init)
        a0 = acc_ref[0]
        a1 = acc_ref[1]
        o0 = a0[:DIFF_DV] / a0[DIFF_DV:DIFF_DV + 1]
        o1 = a1[:DIFF_DV] / a1[DIFF_DV:DIFF_DV + 1]
        att = o0 - lam * o1
        y = att * lax.rsqrt(jnp.mean(att * att, axis=0, keepdims=True) + RMS_EPS)
        o_ref[...] = y.T * g_ref[...] * (1.0 - lam_init)


def _diff_far_offsets(tq, tk):
    far_lo = (-T5_MAX_DIST - tk + 1) // tk
    far_hi = -(-(T5_MAX_DIST + tq - 1) // tk)
    return far_lo, far_hi


def _diff_attn(cqt, ck, cvt, bias_tiles, lam_p, subln_g, lam_init, t, tk, qc):
    s = ck.shape[0]
    far_lo, far_hi = _diff_far_offsets(t, tk)
    ratio = t // tk
    w = 2 * HEAD_DIM
    return pl.pallas_call(
        functools.partial(_diff_attn_kernel, lam_init=lam_init, qc=qc, far_lo=far_lo, far_hi=far_hi),
        grid=(DIFF_HEADS, s // t, s // tk),
        in_specs=[pl.BlockSpec((w, t), lambda h, i, j: (h, i)),
                  pl.BlockSpec((tk, w), lambda h, i, j: (j, h)),
                  pl.BlockSpec((DIFF_VT, tk), lambda h, i, j: (h, j)),
                  pl.BlockSpec((1, 1, tk, t),
                               lambda h, i, j: (jnp.clip(j - i * ratio, far_lo, far_hi) - far_lo, h, 0, 0)),
                  pl.BlockSpec((4, HEAD_DIM), lambda h, i, j: (0, 0)),
                  pl.BlockSpec((1, DIFF_DV), lambda h, i, j: (0, 0))],
        out_specs=pl.BlockSpec((t, DIFF_DV), lambda h, i, j: (i, h)),
        out_shape=jax.ShapeDtypeStruct((s, DIFF_HEADS * DIFF_DV), F32),
        scratch_shapes=[pltpu.VMEM((w, t), BF16), pltpu.VMEM((w, t), BF16),
                        pltpu.VMEM((2, 1, t), F32), pltpu.VMEM((2, DIFF_VT, t), F32),
                        pltpu.VMEM((2, tk, max(qc)), F32)],
        compiler_params=_cparams(("parallel", "parallel", "arbitrary")),
        name="diff_attn",
    )(cqt, ck, cvt, bias_tiles, lam_p, subln_g.reshape(1, DIFF_DV))


def _dil_kernel(q_ref, kp_ref, kc_ref, kn_ref, vp_ref, vc_ref, vn_ref, bias_ref, o_ref, lse_ref,
                qs_ref, ks_ref, vs_ref, os_ref, ls_ref, *, tq, dil, m_len):
    n = pl.program_id(0)
    hd = DIL_HALF * dil
    body_rows = tq * dil
    halves = DIL_HEADS * HEAD_DIM // LANES

    def put(dst, rows, src_ref):
        x = src_ref[...].astype(F32)
        for t in range(halves):
            dst[t, rows, :] = x[:, t * LANES:(t + 1) * LANES]

    def strided(src, r, count):
        return jnp.concatenate([src[t, pl.ds(r, count, stride=dil), :] for t in range(halves)], axis=1)

    put(qs_ref, slice(0, body_rows), q_ref)
    for dst, (p_ref, c_ref, n_ref) in ((ks_ref, (kp_ref, kc_ref, kn_ref)), (vs_ref, (vp_ref, vc_ref, vn_ref))):
        put(dst, slice(0, hd), p_ref)
        put(dst, slice(hd, hd + body_rows), c_ref)
        put(dst, slice(hd + body_rows, 2 * hd + body_rows), n_ref)
    tk = tq + 2 * DIL_HALF
    a = lax.broadcasted_iota(jnp.int32, (tq, tk), 0)
    c = lax.broadcasted_iota(jnp.int32, (tq, tk), 1)
    delta = c - DIL_HALF - a
    kpos = n * tq - DIL_HALF + c
    valid = jnp.where(jnp.abs(delta) <= DIL_HALF, 1, 0) * jnp.where(kpos >= 0, 1, 0) * jnp.where(kpos < m_len, 1, 0)
    valid = jnp.concatenate([valid] * DIL_HEADS, axis=0)
    lane = lax.broadcasted_iota(jnp.int32, (tq, DIL_HEADS * HEAD_DIM), 1)
    head_masks = [(lane // HEAD_DIM) == h for h in range(DIL_HEADS)]

    def one_subsequence(r, carry):
        q = strided(qs_ref, r, tq).astype(BF16)
        k = strided(ks_ref, r, tk).astype(BF16)
        v = strided(vs_ref, r, tk).astype(BF16)
        qs = jnp.concatenate([jnp.where(hm, q, jnp.zeros_like(q)) for hm in head_masks], axis=0)
        s = _dot_nt(qs, k) + bias_ref[0].reshape(DIL_HEADS * tq, tk)
        s = jnp.where(valid > 0, s, -1e30)
        m = jnp.max(s, axis=-1, keepdims=True)
        e = jnp.exp(s - m)
        l = jnp.sum(e, axis=-1, keepdims=True)
        oh = _dot((e / l).astype(BF16), v)
        lse = m + jnp.log(l)
        o = jnp.zeros(q.shape, F32)
        lse_o = jnp.zeros(q.shape, F32)
        for h, hm in enumerate(head_masks):
            rows = slice(h * tq, (h + 1) * tq)
            o = jnp.where(hm, oh[rows], o)
            lse_o = jnp.where(hm, lse[rows], lse_o)
        for t in range(halves):
            os_ref[t, pl.ds(r, tq, stride=dil), :] = o[:, t * LANES:(t + 1) * LANES]
            ls_ref[t, pl.ds(r, tq, stride=dil), :] = lse_o[:, t * LANES:(t + 1) * LANES]
        return carry

    lax.fori_loop(0, dil, one_subsequence, 0, unroll=min(dil, 4))
    o_ref[...] = jnp.concatenate([os_ref[t] for t in range(halves)], axis=1)
    lse_ref[...] = jnp.concatenate([ls_ref[t] for t in range(halves)], axis=1)


def _dil_attn(bq, bk, bv, bias, gi, dil, tq):
    s = bq.shape[0]
    m_len = s // dil
    gw = DIL_HEADS * HEAD_DIM
    rows = tq * dil
    hd = DIL_HALF * dil
    hb = rows // hd
    last = s // hd - 1
    qspec = pl.BlockSpec((rows, gw), lambda n: (n, gi))
    pspec = pl.BlockSpec((hd, gw), lambda n: (jnp.maximum(n * hb - 1, 0), gi))
    nspec = pl.BlockSpec((hd, gw), lambda n: (jnp.minimum((n + 1) * hb, last), gi))
    ospec = pl.BlockSpec((rows, gw), lambda n: (n, 0))
    return pl.pallas_call(
        functools.partial(_dil_kernel, tq=tq, dil=dil, m_len=m_len),
        grid=(s // rows,),
        in_specs=[qspec, pspec, qspec, nspec, pspec, qspec, nspec,
                  pl.BlockSpec((1, DIL_HEADS, tq, tq + 2 * DIL_HALF), lambda n: (0, 0, 0, 0))],
        out_specs=[ospec, ospec],
        out_shape=[jax.ShapeDtypeStruct((s, gw), F32)] * 2,
        scratch_shapes=[pltpu.VMEM((gw // LANES, rows, LANES), F32),
                        pltpu.VMEM((gw // LANES, rows + 2 * hd, LANES), F32),
                        pltpu.VMEM((gw // LANES, rows + 2 * hd, LANES), F32),
                        pltpu.VMEM((gw // LANES, rows, LANES), F32),
                        pltpu.VMEM((gw // LANES, rows, LANES), F32)],
        compiler_params=_cparams(("parallel",)),
        name=f"dil_attn_g{gi}",
    )(bq, bk, bk, bk, bv, bv, bv, bias)


def _gla_kernel(*refs, reverse, tb, finalize):
    if finalize:
        (q_ref, k_ref, v_ref, lr_ref, wg_ref, gb_ref, tri_ref, ofwd_ref, r_ref, ng_ref,
         o_ref, s_ref, oacc_ref) = refs
    else:
        q_ref, k_ref, v_ref, lr_ref, wg_ref, gb_ref, tri_ref, o_ref, s_ref = refs
        oacc_ref = o_ref
    cw = GLA_CHUNK
    qk = GLA_HEADS * GLA_DK
    vw = GLA_HEADS * GLA_DV

    @pl.when(pl.program_id(0) == 0)
    def _():
        s_ref[...] = jnp.zeros(s_ref.shape, F32)

    logits = _dot(lr_ref[...], wg_ref[...]) + gb_ref[...]
    g = (jnp.minimum(logits, 0.0) - jnp.log(1.0 + jnp.exp(-jnp.abs(logits)))) * (1.0 / GLA_TAU)
    ghi, glo = _split(g)
    tri = tri_ref[...]
    b = _dot(tri, ghi) + _dot(tri, glo)
    qg = (q_ref[...].astype(F32) * (GLA_DK ** -0.5) * jnp.exp(b)).astype(BF16)
    k = k_ref[...].astype(F32)
    kg = (k * jnp.exp(-b)).astype(BF16)
    v = v_ref[...].astype(BF16)

    lane_q = lax.broadcasted_iota(jnp.int32, (cw, qk), 1)
    rr = lax.broadcasted_iota(jnp.int32, (GLA_HEADS * cw, cw), 0)
    cc = lax.broadcasted_iota(jnp.int32, (GLA_HEADS * cw, cw), 1)
    tt = rr % cw
    amask = (cc > tt) if reverse else (cc <= tt)
    srow = lax.broadcasted_iota(jnp.int32, (vw, qk), 0) // GLA_DV
    scol = lax.broadcasted_iota(jnp.int32, (vw, qk), 1) // GLA_DK
    bdmask = srow == scol

    n_chunks = tb // cw
    order = range(n_chunks - 1, -1, -1) if reverse else range(n_chunks)
    for ci in order:
        rows = slice(ci * cw, (ci + 1) * cw)
        bc = b[rows]
        b_end = bc[0:1] if reverse else bc[cw - 1:cw]
        kdec = (k[rows] * jnp.exp(b_end - bc)).astype(BF16)
        qg_c = qg[rows]
        qs = jnp.concatenate(
            [jnp.where((lane_q // GLA_DK) == h, qg_c, jnp.zeros_like(qg_c)) for h in range(GLA_HEADS)], axis=0)
        a = jnp.where(amask, _dot_nt(qs, kg[rows]), 0.0)
        obig = _dot(a.astype(BF16), v[rows])
        o_intra = jnp.concatenate(
            [obig[h * cw:(h + 1) * cw, h * GLA_DV:(h + 1) * GLA_DV] for h in range(GLA_HEADS)], axis=1)
        state = s_ref[...]
        oacc_ref[rows, :] = o_intra + _dot_nt(qg_c, state.astype(BF16))
        ds = _dot_tn(v[rows], kdec)
        s_ref[...] = jnp.where(bdmask, jnp.exp(b_end) * state + ds, 0.0)

    if finalize:
        o = ofwd_ref[...] + oacc_ref[...]
        r = r_ref[...].astype(F32)
        outs = []
        for h in range(GLA_HEADS):
            sl = slice(h * GLA_DV, (h + 1) * GLA_DV)
            outs.append(_rms(o[:, sl]) * ng_ref[...] * (r[:, sl] * _sigmoid(r[:, sl])))
        o_ref[...] = jnp.concatenate(outs, axis=1)


def _chunk_tri(tb, reverse):
    i = np.arange(tb)
    same = (i[:, None] // GLA_CHUNK) == (i[None, :] // GLA_CHUNK)
    tri = (i[None, :] >= i[:, None]) if reverse else (i[None, :] <= i[:, None])
    return jnp.asarray(same & tri, dtype=BF16)


def _gla_scan(z, wg, gb, tb, reverse, fin=None):
    s = z.shape[0]
    nb = s // tb
    qk = GLA_HEADS * GLA_DK
    vw = GLA_HEADS * GLA_DV
    blk = (lambda i: nb - 1 - i) if reverse else (lambda i: i)
    zspec = lambda w, off: pl.BlockSpec((tb, w), lambda i: (blk(i), off // w))
    cspec = lambda r, c: pl.BlockSpec((r, c), lambda i: (0, 0))
    in_specs = [zspec(qk, Z_AQ), zspec(qk, Z_AK), zspec(vw, Z_AV), zspec(LANES, Z_ALR),
                cspec(LANES, qk), cspec(1, qk), cspec(tb, tb)]
    args = [z, z, z, z, wg, gb, _chunk_tri(tb, reverse)]
    scratch = [pltpu.VMEM((vw, qk), F32)]
    if fin is not None:
        o_fwd, norm_g = fin
        in_specs += [pl.BlockSpec((tb, vw), lambda i: (blk(i), 0)), zspec(vw, Z_AR), cspec(1, GLA_DV)]
        args += [o_fwd, z, norm_g.reshape(1, GLA_DV)]
        scratch.append(pltpu.VMEM((tb, vw), F32))
    return pl.pallas_call(
        functools.partial(_gla_kernel, reverse=reverse, tb=tb, finalize=fin is not None),
        grid=(nb,),
        in_specs=in_specs,
        out_specs=pl.BlockSpec((tb, vw), lambda i: (blk(i), 0)),
        out_shape=jax.ShapeDtypeStruct((s, vw), F32),
        scratch_shapes=scratch,
        compiler_params=_cparams(("arbitrary",)),
        name="gla_bwd" if reverse else "gla_fwd",
    )(*args)


def _hy_pre_kernel(u_ref, up_ref, un_ref, w_ref, b_ref, x0_ref, z_ref, zb_ref, *, tm):
    i = pl.program_id(0)
    u = u_ref[...].astype(F32)
    row = lax.broadcasted_iota(jnp.int32, u.shape, 0)
    prev_row = jnp.where(i == 0, 0.0, up_ref[...].astype(F32)[HALO_ROWS - 1:HALO_ROWS, :])
    next_row = jnp.where(i == pl.num_programs(0) - 1, 0.0, un_ref[...].astype(F32)[0:1, :])
    u_prev = jnp.where(row == 0, prev_row, pltpu.roll(u, 1, axis=0))
    u_next = jnp.where(row == tm - 1, next_row, pltpu.roll(u, tm - 1, axis=0))
    y = b_ref[...] + u_prev * w_ref[0:1] + u * w_ref[1:2] + u_next * w_ref[2:3]
    x0_ref[...] = y[:, :HY_WIDTH]
    z = y[:, HY_WIDTH:2 * HY_WIDTH] * y[:, 2 * HY_WIDTH:]
    z_ref[...] = z
    zb_ref[...] = z.astype(BF16)


def _hy_pre(z, conv_w, conv_b, tm):
    s = z.shape[0]
    w = 3 * HY_WIDTH
    nr = s // HALO_ROWS
    ospec = pl.BlockSpec((tm, HY_WIDTH), lambda i: (i, 0))
    return pl.pallas_call(
        functools.partial(_hy_pre_kernel, tm=tm),
        grid=(s // tm,),
        in_specs=[pl.BlockSpec((tm, w), lambda i: (i, 0)),
                  pl.BlockSpec((HALO_ROWS, w), lambda i: (jnp.maximum(i * (tm // HALO_ROWS) - 1, 0), 0)),
                  pl.BlockSpec((HALO_ROWS, w), lambda i: (jnp.minimum((i + 1) * (tm // HALO_ROWS), nr - 1), 0)),
                  pl.BlockSpec((3, w), lambda i: (0, 0)),
                  pl.BlockSpec((1, w), lambda i: (0, 0))],
        out_specs=[ospec, ospec, ospec],
        out_shape=[jax.ShapeDtypeStruct((s, HY_WIDTH), F32)] * 2 + [jax.ShapeDtypeStruct((s, HY_WIDTH), BF16)],
        compiler_params=_cparams(("parallel",)),
        name="hyena_pre",
    )(z, z, z, conv_w, conv_b.reshape(1, w))


def _hy_filter_kernel(emb_ref, w1_ref, b1_ref, f1_ref, w2_ref, b2_ref, f2_ref, w3_ref, dl_ref,
                      kb_ref, norm_ref, *, tl, seq):
    i = pl.program_id(0)
    half = tl // 2
    emb = emb_ref[...]
    x = jnp.concatenate([emb[:half], emb[half:]], axis=1)
    h = jnp.sin(f1_ref[...] * (_dot3(x, w1_ref[...]) + b1_ref[...]))
    h = jnp.sin(f2_ref[...] * (_dot3(h, w2_ref[...]) + b2_ref[...]))
    h = _dot3(h, w3_ref[...])
    back = i * tl >= seq
    pick = lambda y: jnp.where(back, y[:, HY_WIDTH:], y[:, :HY_WIDTH])
    h = jnp.concatenate([pick(h[:, :2 * HY_WIDTH]), pick(h[:, 2 * HY_WIDTH:])], axis=0)
    h = h * jnp.exp(-emb[:, 0:1] * dl_ref[...])
    row = i * tl + lax.broadcasted_iota(jnp.int32, h.shape, 0)
    h = jnp.where(row == seq, 0.0, h)
    kb_ref[...] = h.astype(BF16)

    @pl.when(i == 0)
    def _():
        norm_ref[...] = jnp.zeros(norm_ref.shape, F32)

    norm_ref[...] += jnp.sum(jnp.abs(h), axis=0, keepdims=True)


def _hy_positions(seq):
    t = np.linspace(0.0, 1.0, seq, dtype=np.float32)[:, None]
    bands = (HY_EMB - 1) // 2
    freqs = np.linspace(1e-4, bands - 1, bands, dtype=np.float32)[None]
    w = (np.float32(2.0 * math.pi) * np.arange(seq, dtype=np.float32)[:, None] / np.float32(seq))
    zf = np.concatenate([t, np.cos(freqs * w), -np.sin(freqs * w)], axis=-1).astype(np.float32)
    pos = np.concatenate([np.arange(seq), [0], np.arange(seq - 1, 0, -1)])
    return jnp.asarray(np.pad(zf[pos], ((0, 0), (0, LANES - HY_EMB))))


def _block_diag2(w):
    z = jnp.zeros_like(w)
    return jnp.concatenate([jnp.concatenate([w, z], axis=1), jnp.concatenate([z, w], axis=1)], axis=0)


def _hy_filter(emb, w1, b1, fr1, w2, b2, fr2, w3, tl):
    n = emb.shape[0]
    seq = n // 2
    w1p = jnp.pad(w1, ((0, LANES - HY_EMB), (0, 0)))
    twice = lambda v: jnp.concatenate([v, v])[None]
    min_decay = math.log(HY_DECAY_TARGET) / HY_SLOW_DECAY
    max_decay = math.log(HY_DECAY_TARGET) / HY_FAST_DECAY
    deltas = jnp.asarray(np.abs(np.linspace(min_decay, max_decay, HY_WIDTH, dtype=np.float32))[None])
    cs = lambda r, c: pl.BlockSpec((r, c), lambda i: (0, 0))
    return pl.pallas_call(
        functools.partial(_hy_filter_kernel, tl=tl, seq=seq),
        grid=(n // tl,),
        in_specs=[pl.BlockSpec((tl, LANES), lambda i: (i, 0)),
                  cs(2 * LANES, LANES), cs(1, LANES), cs(1, LANES),
                  cs(LANES, LANES), cs(1, LANES), cs(1, LANES),
                  cs(LANES, 4 * HY_WIDTH), cs(1, HY_WIDTH)],
        out_specs=[pl.BlockSpec((tl, HY_WIDTH), lambda i: (i, 0)), pl.BlockSpec((1, HY_WIDTH), lambda i: (0, 0))],
        out_shape=[jax.ShapeDtypeStruct((n, HY_WIDTH), BF16), jax.ShapeDtypeStruct((1, HY_WIDTH), F32)],
        compiler_params=_cparams(("arbitrary",)),
        name="hyena_filter",
    )(emb, _block_diag2(w1p), twice(b1), twice(fr1), _block_diag2(w2), twice(b2), twice(fr2),
      _block_diag2(w3), deltas)


def _dft_tables(n1):
    n = n1 * DFT_N2
    a = np.arange(n1, dtype=np.int64)
    ang1 = (2.0 * math.pi / n1) * ((a[:, None] * a[None, :]) % n1)
    c1, s1 = np.cos(ang1), np.sin(ang1)
    k1 = np.arange(n1, dtype=np.int64)[:, None, None]
    k2 = np.arange(DFT_N2, dtype=np.int64)[None, :, None]
    n2 = np.arange(DFT_N2, dtype=np.int64)[None, None, :]
    ang = (2.0 * math.pi / n) * ((n2 * (k1 + n1 * k2)) % n)
    gc, gs = np.cos(ang), np.sin(ang)
    tb = lambda x: jnp.asarray(np.ascontiguousarray(x).astype(BF16))
    gc, gs = tb(gc), tb(gs)
    gct, gst = jnp.swapaxes(gc, 1, 2), jnp.swapaxes(gs, 1, 2)
    blk = lambda a, b, c, d: jnp.concatenate([jnp.concatenate([a, b], axis=2), jnp.concatenate([c, d], axis=2)], axis=1)
    return dict(c1=tb(c1), s1=tb(s1), g2=blk(gc, gs, -gs, gc), h2=blk(gct, -gst, gst, gct))


def _dft1_kernel(c_ref, s_ref, x_ref, re_ref, im_ref):
    x = x_ref[...]
    re_ref[...] = _dot(c_ref[...], x).astype(BF16)
    im_ref[...] = (-_dot(s_ref[...], x)).astype(BF16)


def _dft1(x, c1, s1, cb):
    k1, w = x.shape
    n1 = c1.shape[0]
    ospec = pl.BlockSpec((n1, cb), lambda j: (0, j))
    return pl.pallas_call(
        _dft1_kernel,
        grid=(w // cb,),
        in_specs=[pl.BlockSpec((n1, k1), lambda j: (0, 0)), pl.BlockSpec((n1, k1), lambda j: (0, 0)),
                  pl.BlockSpec((k1, cb), lambda j: (0, j))],
        out_specs=[ospec, ospec],
        out_shape=[jax.ShapeDtypeStruct((n1, w), BF16)] * 2,
        compiler_params=_cparams(("parallel",)),
        name="dft_stage1",
    )(c1, s1, x)


def _stack(re, im):
    return jnp.concatenate([re, im], axis=0)


def _dft2_filter_kernel(g2_ref, are_ref, aim_ref, norm_ref, kre_ref, kim_ref, *, kb):
    inv = 1.0 / norm_ref[...]
    for t in range(kb):
        x = _dot(g2_ref[t], _stack(are_ref[t], aim_ref[t])) * inv
        kre_ref[t] = x[:DFT_N2].astype(BF16)
        kim_ref[t] = x[DFT_N2:].astype(BF16)


def _dft2_filter(tabs, are, aim, norm, kb):
    n1, _, c = are.shape
    gspec = pl.BlockSpec((kb, 2 * DFT_N2, 2 * DFT_N2), lambda i: (i, 0, 0))
    aspec = pl.BlockSpec((kb, DFT_N2, c), lambda i: (i, 0, 0))
    return pl.pallas_call(
        functools.partial(_dft2_filter_kernel, kb=kb),
        grid=(n1 // kb,),
        in_specs=[gspec, aspec, aspec, pl.BlockSpec((1, c), lambda i: (0, 0))],
        out_specs=[aspec, aspec],
        out_shape=[jax.ShapeDtypeStruct((n1, DFT_N2, c), BF16)] * 2,
        compiler_params=_cparams(("parallel",)),
        name="dft_stage2_filter",
    )(tabs["g2"], are, aim, norm)


def _conv_mid_kernel(g2_ref, h2_ref, are_ref, aim_ref, kre_ref, kim_ref, bre_ref, bim_ref, *, kb):
    for t in range(kb):
        x = _dot(g2_ref[t], _stack(are_ref[t], aim_ref[t]))
        xre, xim = x[:DFT_N2], x[DFT_N2:]
        kre, kim = kre_ref[t].astype(F32), kim_ref[t].astype(F32)
        yre = (xre * kre - xim * kim).astype(BF16)
        yim = (xre * kim + xim * kre).astype(BF16)
        b = _dot(h2_ref[t], _stack(yre, yim))
        bre_ref[t] = b[:DFT_N2].astype(BF16)
        bim_ref[t] = b[DFT_N2:].astype(BF16)


def _conv_mid(tabs, are, aim, kre, kim, kb):
    n1, _, c = are.shape
    gspec = pl.BlockSpec((kb, 2 * DFT_N2, 2 * DFT_N2), lambda i: (i, 0, 0))
    aspec = pl.BlockSpec((kb, DFT_N2, c), lambda i: (i, 0, 0))
    return pl.pallas_call(
        functools.partial(_conv_mid_kernel, kb=kb),
        grid=(n1 // kb,),
        in_specs=[gspec, gspec, aspec, aspec, aspec, aspec],
        out_specs=[aspec, aspec],
        out_shape=[jax.ShapeDtypeStruct((n1, DFT_N2, c), BF16)] * 2,
        compiler_params=_cparams(("parallel",)),
        name="conv_spectral",
    )(tabs["g2"], tabs["h2"], are, aim, kre, kim)


def _idft1_kernel(c_ref, s_ref, bre_ref, bim_ref, z_ref, x0_ref, skip_ref, o_ref, *, inv_n):
    y = (_dot(c_ref[...], bre_ref[...]) - _dot(s_ref[...], bim_ref[...])) * inv_n
    o_ref[...] = x0_ref[...] * (y + z_ref[...] * skip_ref[...])


def _idft1(c1h, s1h, bre, bim, z, x0, skip_t, cb):
    ko, n1 = c1h.shape
    w = bre.shape[1]
    n = n1 * DFT_N2
    ospec = pl.BlockSpec((ko, cb), lambda j: (0, j))
    return pl.pallas_call(
        functools.partial(_idft1_kernel, inv_n=1.0 / n),
        grid=(w // cb,),
        in_specs=[pl.BlockSpec((ko, n1), lambda j: (0, 0)), pl.BlockSpec((ko, n1), lambda j: (0, 0)),
                  pl.BlockSpec((n1, cb), lambda j: (0, j)), pl.BlockSpec((n1, cb), lambda j: (0, j)),
                  ospec, ospec, pl.BlockSpec((1, cb), lambda j: (0, 0))],
        out_specs=ospec,
        out_shape=jax.ShapeDtypeStruct((ko, w), F32),
        compiler_params=_cparams(("parallel",)),
        name="idft_stage1",
    )(c1h, s1h, bre, bim, z, x0, skip_t)


def _hyena(z_all, tabs, emb, p, i, cb, kb):
    seq = z_all.shape[0]
    n1 = 2 * seq // DFT_N2
    c = HY_WIDTH
    x0, zf, zb = _hy_pre(z_all, p["hy_conv_w"][i], p["hy_conv_b"][i], min(512, seq))
    kern_b, norm = _hy_filter(emb, p["hy_w1"][i], p["hy_b1"][i], p["hy_freq1"][i], p["hy_w2"][i],
                                 p["hy_b2"][i], p["hy_freq2"][i], p["hy_w3"][i], min(1024, seq))
    wide = DFT_N2 * c
    kre, kim = _dft1(kern_b.reshape(n1, wide), tabs["c1"], tabs["s1"], cb)
    kre, kim = _dft2_filter(tabs, kre.reshape(n1, DFT_N2, c), kim.reshape(n1, DFT_N2, c), norm, kb)
    are, aim = _dft1(zb.reshape(n1 // 2, wide), tabs["c1"][:, :n1 // 2], tabs["s1"][:, :n1 // 2], cb)
    bre, bim = _conv_mid(tabs, are.reshape(n1, DFT_N2, c), aim.reshape(n1, DFT_N2, c), kre, kim, kb)
    skip_t = jnp.tile(p["hy_skip"][i][None], (1, cb // c))
    yd = _idft1(tabs["c1"][:n1 // 2], tabs["s1"][:n1 // 2], bre.reshape(n1, wide), bim.reshape(n1, wide),
                zf.reshape(n1 // 2, wide), x0.reshape(n1 // 2, wide), skip_t, cb)
    return yd.reshape(seq, c)


def _combine_kernel(x_ref, ya_ref, o0_ref, o1_ref, o2_ref, l0_ref, l1_ref, l2_ref, yc_ref, yd_ref,
                    g0_ref, g1_ref, g2_ref, g3_ref, pa_ref, pb_ref, pc_ref, pd_ref, wo_ref, o_ref):
    l0, l1, l2 = l0_ref[...], l1_ref[...], l2_ref[...]
    mx = jnp.maximum(jnp.maximum(l0, l1), l2)
    e0, e1, e2 = jnp.exp(l0 - mx), jnp.exp(l1 - mx), jnp.exp(l2 - mx)
    yb = (e0 * o0_ref[...] + e1 * o1_ref[...] + e2 * o2_ref[...]) / (e0 + e1 + e2)
    gate = lambda ref: _sigmoid(ref[...].astype(F32))
    m = (gate(g0_ref) * _dot(ya_ref[...].astype(BF16), pa_ref[...])
         + gate(g1_ref) * _dot(yb.astype(BF16), pb_ref[...])
         + gate(g2_ref) * _dot(yc_ref[...].astype(BF16), pc_ref[...])
         + gate(g3_ref) * _dot(yd_ref[...].astype(BF16), pd_ref[...]))
    o_ref[...] = x_ref[...] + _dot(m.astype(BF16), wo_ref[...])


def _combine(x, z, ya, dil_outs, yc, yd, pa, pb, pc, pd, wo, tm):
    s, d = x.shape
    rs = lambda w: pl.BlockSpec((tm, w), lambda i: (i, 0))
    gs = lambda b: pl.BlockSpec((tm, d), lambda i: (i, Z_GATE // d + b))
    ws = lambda a: pl.BlockSpec(a.shape, lambda i: (0, 0))
    (o0, l0), (o1, l1), (o2, l2) = dil_outs
    gw = DIL_HEADS * HEAD_DIM
    return pl.pallas_call(
        _combine_kernel,
        grid=(s // tm,),
        in_specs=[rs(d), rs(ya.shape[1]), rs(gw), rs(gw), rs(gw), rs(gw), rs(gw), rs(gw),
                  rs(yc.shape[1]), rs(yd.shape[1]), gs(0), gs(1), gs(2), gs(3),
                  ws(pa), ws(pb), ws(pc), ws(pd), ws(wo)],
        out_specs=rs(d),
        out_shape=jax.ShapeDtypeStruct((s, d), F32),
        compiler_params=_cparams(("parallel",)),
        name="combine",
    )(x, ya, o0, o1, o2, l0, l1, l2, yc, yd, z, z, z, z, pa, pb, pc, pd, wo)


def _mlp_kernel(x_ref, g_ref, w1_ref, w2_ref, o_ref, h_ref, acc_ref):
    j = pl.program_id(1)

    @pl.when(j == 0)
    def _():
        h_ref[...] = (_rms(x_ref[...]) * g_ref[...]).astype(BF16)
        acc_ref[...] = jnp.zeros(acc_ref.shape, F32)

    a = jnp.maximum(_dot(h_ref[...], w1_ref[...]), 0.0)
    acc_ref[...] += _dot((a * a).astype(BF16), w2_ref[...])

    @pl.when(j == pl.num_programs(1) - 1)
    def _():
        o_ref[...] = x_ref[...] + acc_ref[...]


def _mlp(x, g, w1, w2, tm, tf):
    s, d = x.shape
    ff = w1.shape[1]
    return pl.pallas_call(
        _mlp_kernel,
        grid=(s // tm, ff // tf),
        in_specs=[pl.BlockSpec((tm, d), lambda i, j: (i, 0)),
                  pl.BlockSpec((1, d), lambda i, j: (0, 0)),
                  pl.BlockSpec((d, tf), lambda i, j: (0, j)),
                  pl.BlockSpec((tf, d), lambda i, j: (j, 0))],
        out_specs=pl.BlockSpec((tm, d), lambda i, j: (i, 0)),
        out_shape=jax.ShapeDtypeStruct((s, d), F32),
        scratch_shapes=[pltpu.VMEM((tm, d), BF16), pltpu.VMEM((tm, d), F32)],
        compiler_params=_cparams(("parallel", "arbitrary")),
        name="mlp",
    )(x, g.reshape(1, d), w1, w2)


def _permute_w_in(w):
    d = w.shape[0]
    pieces = [w[:, _O_DU:_O_GATE], w[:, _O_B:_O_C], w[:, _O_AQ:_O_AK], w[:, _O_C:_O_DU],
              w[:, _O_AV:_O_AR], w[:, _O_AR:_O_ALR], w[:, _O_AK:_O_AV], w[:, _O_ALR:_O_B],
              jnp.zeros((d, Z_GATE - Z_ALR - 2 * GLA_RANK), w.dtype), w[:, _O_GATE:_O_END]]
    return jnp.concatenate(pieces, axis=1).astype(BF16)


def _dil_bias_idx(tq, dil):
    rel = _skewed_rel(tq, tq + 2 * DIL_HALF, -1, -DIL_HALF)
    return _t5_bucket(rel * dil)[None].astype(np.int32)


def _diff_bias_idx(tq, tk):
    far_lo, far_hi = _diff_far_offsets(tq, tk)
    return np.stack([_t5_bucket(_skewed_rel(tk, tq, 1, o * tk)) for o in range(far_lo, far_hi + 1)]).astype(np.int32)


def _forward(x, p, *, t_diff, tk_diff, qc_diff, tq_dil, tb_gla, tm_proj, tn_proj, tm_row, tm_mlp, tf_mlp, cb_dft, kb_dft):
    seq = x.shape[0]
    depth = p["w_in"].shape[0]
    n_dil_bias = DIL_GROUPS * DIL_HEADS
    tq_dil = [min(tq, seq // dil) for tq, (_, dil) in zip(tq_dil, DIL_PATTERNS)]
    dil_bias = [_bias_tiles(p["t5_bias"], _dil_bias_idx(tq, dil), tq, tq + 2 * DIL_HALF, DIL_HEADS, gi * DIL_HEADS)
                for gi, (tq, (_, dil)) in enumerate(zip(tq_dil, DIL_PATTERNS))]
    t_diff, tk_diff = min(t_diff, seq), min(tk_diff, seq)
    diff_bias = _bias_tiles(p["t5_bias"], _diff_bias_idx(t_diff, tk_diff), tk_diff, t_diff, DIFF_HEADS,
                            n_dil_bias, LOG2E)
    tabs = _dft_tables(2 * seq // DFT_N2)
    emb = _hy_positions(seq)
    rep = lambda g, n: jnp.tile(g, n)
    for i in range(depth):
        z = _norm_matmul(x, p["norm1_g"][i], _permute_w_in(p["w_in"][i]), tm_proj, tn_proj)
        wg = [jnp.zeros((LANES, GLA_HEADS * GLA_DK), F32).at[j * GLA_RANK:(j + 1) * GLA_RANK].set(
            p["gla_gate_w"][i, j]).astype(BF16) for j in range(2)]
        gb = p["gla_gate_b"][i]
        o_fwd = _gla_scan(z, wg[0], gb[0:1], tb_gla, False)
        ya = _gla_scan(z, wg[1], gb[1:2], tb_gla, True, fin=(o_fwd, p["gla_norm_g"][i]))
        bq, bk, bv, cq, ck, cv = _prep(
            z, jnp.repeat(p["dil_qnorm_g"][i], DIL_HEADS, axis=0).reshape(-1),
            jnp.repeat(p["dil_knorm_g"][i], DIL_HEADS, axis=0).reshape(-1),
            rep(p["diff_qnorm_g"][i], 2 * DIFF_HEADS), rep(p["diff_knorm_g"][i], 2 * DIFF_HEADS), tm_row)
        dil_outs = [_dil_attn(bq, bk, bv, dil_bias[gi], gi, dil, tq_dil[gi])
                    for gi, (_, dil) in enumerate(DIL_PATTERNS)]
        lam_init = 0.8 - 0.6 * math.exp(-0.3 * i)
        yc = _diff_attn(cq, ck, cv, diff_bias, p["diff_lambda"][i], p["diff_subln_g"][i], lam_init,
                        t_diff, tk_diff, qc_diff)
        yd = _hyena(z, tabs, emb, p, i, cb_dft, kb_dft)
        x = _combine(x, z, ya, dil_outs, yc, yd, p["proj_a"][i].astype(BF16), p["proj_b"][i].astype(BF16),
                     p["proj_c"][i].astype(BF16), p["proj_d"][i].astype(BF16), p["w_out"][i].astype(BF16),
                     tm_row)
        x = _mlp(x, p["norm2_g"][i], p["mlp_w1"][i].astype(BF16), p["mlp_w2"][i].astype(BF16), tm_mlp, tf_mlp)
    return x


def kernel(x, t5_bias, norm1_g, w_in, gla_gate_w, gla_gate_b, gla_norm_g, dil_qnorm_g, dil_knorm_g,
           diff_qnorm_g, diff_knorm_g, diff_lambda, diff_subln_g, hy_conv_w, hy_conv_b, hy_w1, hy_b1,
           hy_freq1, hy_w2, hy_b2, hy_freq2, hy_w3, hy_skip, proj_a, proj_b, proj_c, proj_d, w_out,
           norm2_g, mlp_w1, mlp_w2):
    p = dict(t5_bias=t5_bias, norm1_g=norm1_g, w_in=w_in, gla_gate_w=gla_gate_w, gla_gate_b=gla_gate_b,
             gla_norm_g=gla_norm_g, dil_qnorm_g=dil_qnorm_g, dil_knorm_g=dil_knorm_g,
             diff_qnorm_g=diff_qnorm_g, diff_knorm_g=diff_knorm_g, diff_lambda=diff_lambda,
             diff_subln_g=diff_subln_g, hy_conv_w=hy_conv_w, hy_conv_b=hy_conv_b, hy_w1=hy_w1, hy_b1=hy_b1,
             hy_freq1=hy_freq1, hy_w2=hy_w2, hy_b2=hy_b2, hy_freq2=hy_freq2, hy_w3=hy_w3, hy_skip=hy_skip,
             proj_a=proj_a, proj_b=proj_b, proj_c=proj_c, proj_d=proj_d, w_out=w_out, norm2_g=norm2_g,
             mlp_w1=mlp_w1, mlp_w2=mlp_w2)
    b, s, d = x.shape
    outs = [_forward(x[bi], p, t_diff=2048, tk_diff=1024, qc_diff=(256, 768, 1024), tq_dil=(256, 256, 128), tb_gla=512, tm_proj=2048, tn_proj=512, tm_row=512,
                     tm_mlp=1024, tf_mlp=1024, cb_dft=4096, kb_dft=8) for bi in range(b)]
    return jnp.stack(outs)
```

```python
import functools
import math

import jax
import jax.numpy as jnp
import numpy as np
from jax import lax
from jax.experimental import pallas as pl
from jax.experimental.pallas import tpu as pltpu

F32 = jnp.float32
BF16 = jnp.bfloat16

D_MODEL = 1024
HEAD_DIM = 64
GLA_HEADS = 4
GLA_DK = 64
GLA_DV = 128
GLA_RANK = 16
GLA_TAU = 16.0
GLA_CHUNK = 64
DIL_PATTERNS = ((128, 1), (512, 4), (2048, 16))
DIL_GROUPS = 3
DIL_HEADS = 4
DIL_HALF = 64
DIFF_HEADS = 4
DIFF_DV = 128
DIFF_VT = DIFF_DV + 16
HY_WIDTH = 512
HY_EMB = 33
HY_FFN = 64
HY_DECAY_TARGET = 1e-2
HY_FAST_DECAY = 0.3
HY_SLOW_DECAY = 1.5
T5_BUCKETS = 32
T5_MAX_DIST = 1024
N_BIAS_HEADS = 16
D_FF = 4096
RMS_EPS = 1e-6
LOG2E = math.log2(math.e)

LANES = 128
HALO_ROWS = 16
VMEM_LIMIT = 48 * 1024 * 1024

Z_DU = 0
Z_BQ, Z_BK, Z_BV = 1536, 2304, 3072
Z_AQ = 3840
Z_CQ, Z_CK, Z_CV = 4096, 4608, 5120
Z_AV, Z_AR = 5632, 6144
Z_AK = 6656
Z_ALR = 6912
Z_GATE = 7168
Z_COLS = 11264

_O_AQ, _O_AK, _O_AV, _O_AR, _O_ALR, _O_B, _O_C, _O_DU, _O_GATE, _O_END = (
    0, 256, 512, 1024, 1536, 1568, 3872, 5408, 6944, 11040)

DFT_N2 = 128


def _cparams(sem):
    return pltpu.CompilerParams(dimension_semantics=sem, vmem_limit_bytes=VMEM_LIMIT)


def _dot(a, b):
    return jnp.dot(a, b, preferred_element_type=F32)


def _dot_nt(a, b):
    return lax.dot_general(a, b, (((1,), (1,)), ((), ())), preferred_element_type=F32)


def _dot_tn(a, b):
    return lax.dot_general(a, b, (((0,), (0,)), ((), ())), preferred_element_type=F32)


def _split(x):
    hi = x.astype(BF16)
    lo = (x - hi.astype(F32)).astype(BF16)
    return hi, lo


def _dot3(a, b):
    ah, al = _split(a)
    bh, bl = _split(b)
    return _dot(ah, bh) + _dot(ah, bl) + _dot(al, bh)


def _rms(x):
    return x * lax.rsqrt(jnp.mean(x * x, axis=-1, keepdims=True) + RMS_EPS)


def _sigmoid(x):
    return 0.5 * jnp.tanh(0.5 * x) + 0.5


def _norm_matmul_kernel(x_ref, g_ref, w_ref, o_ref, h_ref):
    @pl.when(pl.program_id(1) == 0)
    def _():
        h_ref[...] = (_rms(x_ref[...]) * g_ref[...]).astype(BF16)

    o_ref[...] = _dot(h_ref[...], w_ref[...]).astype(o_ref.dtype)


def _norm_matmul(x, g, w, tm, tn):
    s, d = x.shape
    n = w.shape[1]
    return pl.pallas_call(
        _norm_matmul_kernel,
        grid=(s // tm, n // tn),
        in_specs=[pl.BlockSpec((tm, d), lambda i, j: (i, 0)),
                  pl.BlockSpec((1, d), lambda i, j: (0, 0)),
                  pl.BlockSpec((d, tn), lambda i, j: (0, j))],
        out_specs=pl.BlockSpec((tm, tn), lambda i, j: (i, j)),
        out_shape=jax.ShapeDtypeStruct((s, n), BF16),
        scratch_shapes=[pltpu.VMEM((tm, d), BF16)],
        compiler_params=_cparams(("parallel", "arbitrary")),
        name="in_proj",
    )(x, g.reshape(1, d), w)


def _group_norm(x, e, gain):
    hi, lo = _split(x * x)
    ms = (_dot(hi, e) + _dot(lo, e)) * (1.0 / HEAD_DIM)
    return x * lax.rsqrt(ms + RMS_EPS) * gain


def _prep_kernel(bq_ref, bk_ref, bv_ref, cq_ref, ck_ref, cv_ref, eb_ref, ec_ref,
                 gbq_ref, gbk_ref, gcq_ref, gck_ref,
                 obq_ref, obk_ref, obv_ref, ocq_ref, ock_ref, ocv_ref):
    scale = HEAD_DIM ** -0.5
    eb = eb_ref[...]
    ec = ec_ref[...]
    f32 = lambda ref: ref[...].astype(F32)
    obq_ref[...] = (_group_norm(f32(bq_ref), eb, gbq_ref[...]) * scale).astype(BF16)
    obk_ref[...] = _group_norm(f32(bk_ref), eb, gbk_ref[...]).astype(BF16)
    obv_ref[...] = bv_ref[...]
    ocq_ref[...] = (_group_norm(f32(cq_ref), ec, gcq_ref[...]) * (scale * LOG2E)).T.astype(BF16)
    ock_ref[...] = _group_norm(f32(ck_ref), ec, gck_ref[...]).astype(BF16)
    cvt = f32(cv_ref).T
    ones = jnp.ones((DIFF_VT - DIFF_DV, cvt.shape[1]), F32)
    ocv_ref[...] = jnp.concatenate(
        [t for h in range(DIFF_HEADS) for t in (cvt[h * DIFF_DV:(h + 1) * DIFF_DV], ones)], axis=0).astype(BF16)


def _block_diag_ones(width):
    idx = np.arange(width) // HEAD_DIM
    return jnp.asarray(idx[:, None] == idx[None, :], dtype=BF16)


def _prep(z, gbq, gbk, gcq, gck, tm):
    s = z.shape[0]
    wb, wc = DIL_GROUPS * DIL_HEADS * HEAD_DIM, DIFF_HEADS * 2 * HEAD_DIM
    zspec = lambda w, off: pl.BlockSpec((tm, w), lambda i: (i, off // w))
    cspec = lambda r, c: pl.BlockSpec((r, c), lambda i: (0, 0))
    ospec = lambda w: pl.BlockSpec((tm, w), lambda i: (i, 0))
    tspec = lambda w: pl.BlockSpec((w, tm), lambda i: (0, i))
    return pl.pallas_call(
        _prep_kernel,
        grid=(s // tm,),
        in_specs=[zspec(wb, Z_BQ), zspec(wb, Z_BK), zspec(wb, Z_BV),
                  zspec(wc, Z_CQ), zspec(wc, Z_CK), zspec(wc, Z_CV),
                  cspec(wb, wb), cspec(wc, wc),
                  cspec(1, wb), cspec(1, wb), cspec(1, wc), cspec(1, wc)],
        out_specs=[ospec(wb), ospec(wb), ospec(wb), tspec(wc), ospec(wc), tspec(DIFF_HEADS * DIFF_VT)],
        out_shape=[jax.ShapeDtypeStruct((s, wb), BF16)] * 3
        + [jax.ShapeDtypeStruct((wc, s), BF16), jax.ShapeDtypeStruct((s, wc), BF16),
           jax.ShapeDtypeStruct((DIFF_HEADS * DIFF_VT, s), BF16)],
        compiler_params=_cparams(("parallel",)),
        name="qk_prep",
    )(z, z, z, z, z, z, _block_diag_ones(wb), _block_diag_ones(wc),
      gbq.reshape(1, wb), gbk.reshape(1, wb), gcq.reshape(1, wc), gck.reshape(1, wc))


def _t5_bucket(rel):
    half = T5_BUCKETS // 2
    max_exact = half // 2
    ret = np.where(rel > 0, half, 0)
    n = np.abs(rel)
    nf = np.maximum(n, 1).astype(np.float64)
    large = max_exact + (np.log(nf / max_exact) / math.log(T5_MAX_DIST / max_exact)
                         * (half - max_exact)).astype(np.int64)
    large = np.minimum(large, half - 1)
    return ret + np.where(n < max_exact, n, large)


def _bias_kernel(tab_ref, rng_ref, idx_ref, o_ref, p_ref, *, head_base, out_scale, chunk):
    t = pl.program_id(0)
    col = head_base + pl.program_id(1)
    rows, width = o_ref.shape[2:]
    slabs = width // LANES
    lo, hi = rng_ref[t, 0], rng_ref[t, 1]

    def one_chunk(ci, carry):
        rs = pl.ds(pl.multiple_of(ci * chunk, chunk), chunk)
        idx = idx_ref[0, rs, :]
        p_ref[rs, :] = lax.fori_loop(
            lo, hi + 1, lambda b, acc: jnp.where(idx == b, tab_ref[b, col] * out_scale, acc),
            jnp.zeros((chunk, LANES), F32))
        return carry

    lax.fori_loop(0, p_ref.shape[0] // chunk, one_chunk, 0)
    for c in range(slabs):
        off = LANES * (slabs - 1 - c)
        o_ref[0, 0, :, c * LANES:(c + 1) * LANES] = p_ref[off:off + rows, :]


def _skewed_rel(rows, width, sign, base):
    slabs = width // LANES
    rho = np.arange(rows + LANES * (slabs - 1))[:, None]
    lane = np.arange(LANES)[None, :]
    return sign * (rho - LANES * (slabs - 1) - lane) + base


def _bias_tiles(t5_bias, idx, rows, width, n_heads, head_base, out_scale=1.0, chunk=64):
    nt, rp, _ = idx.shape
    ranges = np.stack([idx.reshape(nt, -1).min(axis=1), idx.reshape(nt, -1).max(axis=1)], axis=1).astype(np.int32)
    return pl.pallas_call(
        functools.partial(_bias_kernel, head_base=head_base, out_scale=out_scale, chunk=chunk),
        grid=(nt, n_heads),
        in_specs=[pl.BlockSpec(memory_space=pltpu.SMEM), pl.BlockSpec(memory_space=pltpu.SMEM),
                  pl.BlockSpec((1, rp, LANES), lambda t, h: (t, 0, 0))],
        out_specs=pl.BlockSpec((1, 1, rows, width), lambda t, h: (t, h, 0, 0)),
        out_shape=jax.ShapeDtypeStruct((nt, n_heads, rows, width), F32),
        scratch_shapes=[pltpu.VMEM((rp, LANES), F32)],
        compiler_params=_cparams(("parallel", "parallel")),
        name="t5_bias_tiles",
    )(t5_bias, jnp.asarray(ranges), jnp.asarray(idx))


def _diff_attn_kernel(qt_ref, k_ref, vt_ref, bias_ref, lam_ref, g_ref, o_ref,
                      qa_ref, qb_ref, m_ref, acc_ref, s_ref, *, lam_init, qc, far_lo, far_hi):
    j = pl.program_id(2)
    t = qt_ref.shape[1]

    @pl.when(j == 0)
    def _():
        qt = qt_ref[...]
        row = lax.broadcasted_iota(jnp.int32, qt.shape, 0)
        qa_ref[...] = jnp.where(row < HEAD_DIM, qt, jnp.zeros_like(qt))
        qb_ref[...] = jnp.where(row >= HEAD_DIM, qt, jnp.zeros_like(qt))
        m_ref[...] = jnp.full(m_ref.shape, -jnp.inf, F32)
        acc_ref[...] = jnp.zeros(acc_ref.shape, F32)

    starts = [sum(qc[:u]) for u in range(len(qc))]
    cuts = [slice(a, a + w) for a, w in zip(starts, qc)]
    chains = [(0, cs) for cs in cuts] + [(1, cs) for cs in reversed(cuts)]
    q_refs = (qa_ref, qb_ref)

    def step(far):
        k = k_ref[...]
        vt = vt_ref[...]
        if far:
            const = bias_ref[0, 0, 0:1, 0:1]
            scores = lambda c, cols: _dot(k, q_refs[c][:, cols])
        else:
            const = 0.0
            scores = lambda c, cols: _dot(k, q_refs[c][:, cols]) + bias_ref[0, 0, :, cols]
        width = lambda n: chains[n][1].stop - chains[n][1].start
        s_ref[0, :, :width(0)] = scores(*chains[0])
        for n, (c, cols) in enumerate(chains):
            if n + 1 < len(chains):
                s_ref[(n + 1) % 2, :, :width(n + 1)] = scores(*chains[n + 1])
            s = s_ref[n % 2, :, :width(n)]
            m_old = m_ref[c, :, cols]
            m_new = jnp.maximum(m_old, jnp.max(s, axis=0, keepdims=True) + const)
            alpha = jnp.exp2(m_old - m_new)
            p = jnp.exp2(s - (m_new - const)).astype(BF16)
            acc_ref[c, :, cols] = alpha * acc_ref[c, :, cols] + _dot(vt, p)
            m_ref[c, :, cols] = m_new

    off = j - pl.program_id(1) * (t // k_ref.shape[0])
    is_far = jnp.logical_or(off <= far_lo, off >= far_hi)
    pl.when(is_far)(lambda: step(True))
    pl.when(jnp.logical_not(is_far))(lambda: step(False))

    @pl.when(j == pl.num_programs(2) - 1)
    def _():
        lp = lam_ref[...]
        lam = (jnp.exp(jnp.sum(lp[0:1] * lp[1:2], axis=-1, keepdims=True))
               - jnp.exp(jnp.sum(lp[2:3] * lp[3:4], axis=-1, keepdims=True)) + lam_init)
        a0 = acc_ref[0]
        a1 = acc_ref[1]
        o0 = a0[:DIFF_DV] / a0[DIFF_DV:DIFF_DV + 1]
        o1 = a1[:DIFF_DV] / a1[DIFF_DV:DIFF_DV + 1]
        att = o0 - lam * o1
        y = att * lax.rsqrt(jnp.mean(att * att, axis=0, keepdims=True) + RMS_EPS)
        o_ref[...] = y.T * g_ref[...] * (1.0 - lam_init)


def _diff_far_offsets(tq, tk):
    far_lo = (-T5_MAX_DIST - tk + 1) // tk
    far_hi = -(-(T5_MAX_DIST + tq - 1) // tk)
    return far_lo, far_hi


def _diff_attn(cqt, ck, cvt, bias_tiles, lam_p, subln_g, lam_init, t, tk, qc):
    s = ck.shape[0]
    far_lo, far_hi = _diff_far_offsets(t, tk)
    ratio = t // tk
    w = 2 * HEAD_DIM
    return pl.pallas_call(
        functools.partial(_diff_attn_kernel, lam_init=lam_init, qc=qc, far_lo=far_lo, far_hi=far_hi),
        grid=(DIFF_HEADS, s // t, s // tk),
        in_specs=[pl.BlockSpec((w, t), lambda h, i, j: (h, i)),
                  pl.BlockSpec((tk, w), lambda h, i, j: (j, h)),
                  pl.BlockSpec((DIFF_VT, tk), lambda h, i, j: (h, j)),
                  pl.BlockSpec((1, 1, tk, t),
                               lambda h, i, j: (jnp.clip(j - i * ratio, far_lo, far_hi) - far_lo, h, 0, 0)),
                  pl.BlockSpec((4, HEAD_DIM), lambda h, i, j: (0, 0)),
                  pl.BlockSpec((1, DIFF_DV), lambda h, i, j: (0, 0))],
        out_specs=pl.BlockSpec((t, DIFF_DV), lambda h, i, j: (i, h)),
        out_shape=jax.ShapeDtypeStruct((s, DIFF_HEADS * DIFF_DV), F32),
        scratch_shapes=[pltpu.VMEM((w, t), BF16), pltpu.VMEM((w, t), BF16),
                        pltpu.VMEM((2, 1, t), F32), pltpu.VMEM((2, DIFF_VT, t), F32),
                        pltpu.VMEM((2, tk, max(qc)), F32)],
        compiler_params=_cparams(("parallel", "parallel", "arbitrary")),
        name="diff_attn",
    )(cqt, ck, cvt, bias_tiles, lam_p, subln_g.reshape(1, DIFF_DV))


def _dil_kernel(q_ref, kp_ref, kc_ref, kn_ref, vp_ref, vc_ref, vn_ref, bias_ref, o_ref, lse_ref,
                qs_ref, ks_ref, vs_ref, os_ref, ls_ref, *, tq, dil, m_len):
    n = pl.program_id(0)
    hd = DIL_HALF * dil
    body_rows = tq * dil
    halves = DIL_HEADS * HEAD_DIM // LANES

    def put(dst, rows, src_ref):
        x = src_ref[...].astype(F32)
        for t in range(halves):
            dst[t, rows, :] = x[:, t * LANES:(t + 1) * LANES]

    def strided(src, r, count):
        return jnp.concatenate([src[t, pl.ds(r, count, stride=dil), :] for t in range(halves)], axis=1)

    put(qs_ref, slice(0, body_rows), q_ref)
    for dst, (p_ref, c_ref, n_ref) in ((ks_ref, (kp_ref, kc_ref, kn_ref)), (vs_ref, (vp_ref, vc_ref, vn_ref))):
        put(dst, slice(0, hd), p_ref)
        put(dst, slice(hd, hd + body_rows), c_ref)
        put(dst, slice(hd + body_rows, 2 * hd + body_rows), n_ref)
    tk = tq + 2 * DIL_HALF
    a = lax.broadcasted_iota(jnp.int32, (tq, tk), 0)
    c = lax.broadcasted_iota(jnp.int32, (tq, tk), 1)
    delta = c - DIL_HALF - a
    kpos = n * tq - DIL_HALF + c
    valid = jnp.where(jnp.abs(delta) <= DIL_HALF, 1, 0) * jnp.where(kpos >= 0, 1, 0) * jnp.where(kpos < m_len, 1, 0)
    valid = jnp.concatenate([valid] * DIL_HEADS, axis=0)
    lane = lax.broadcasted_iota(jnp.int32, (tq, DIL_HEADS * HEAD_DIM), 1)
    head_masks = [(lane // HEAD_DIM) == h for h in range(DIL_HEADS)]

    def one_subsequence(r, carry):
        q = strided(qs_ref, r, tq).astype(BF16)
        k = strided(ks_ref, r, tk).astype(BF16)
        v = strided(vs_ref, r, tk).astype(BF16)
        qs = jnp.concatenate([jnp.where(hm, q, jnp.zeros_like(q)) for hm in head_masks], axis=0)
        s = _dot_nt(qs, k) + bias_ref[0].reshape(DIL_HEADS * tq, tk)
        s = jnp.where(valid > 0, s, -1e30)
        m = jnp.max(s, axis=-1, keepdims=True)
        e = jnp.exp(s - m)
        l = jnp.sum(e, axis=-1, keepdims=True)
        oh = _dot((e / l).astype(BF16), v)
        lse = m + jnp.log(l)
        o = jnp.zeros(q.shape, F32)
        lse_o = jnp.zeros(q.shape, F32)
        for h, hm in enumerate(head_masks):
            rows = slice(h * tq, (h + 1) * tq)
            o = jnp.where(hm, oh[rows], o)
            lse_o = jnp.where(hm, lse[rows], lse_o)
        for t in range(halves):
            os_ref[t, pl.ds(r, tq, stride=dil), :] = o[:, t * LANES:(t + 1) * LANES]
            ls_ref[t, pl.ds(r, tq, stride=dil), :] = lse_o[:, t * LANES:(t + 1) * LANES]
        return carry

    lax.fori_loop(0, dil, one_subsequence, 0, unroll=min(dil, 4))
    o_ref[...] = jnp.concatenate([os_ref[t] for t in range(halves)], axis=1)
    lse_ref[...] = jnp.concatenate([ls_ref[t] for t in range(halves)], axis=1)


def _dil_attn(bq, bk, bv, bias, gi, dil, tq):
    s = bq.shape[0]
    m_len = s // dil
    gw = DIL_HEADS * HEAD_DIM
    rows = tq * dil
    hd = DIL_HALF * dil
    hb = rows // hd
    last = s // hd - 1
    qspec = pl.BlockSpec((rows, gw), lambda n: (n, gi))
    pspec = pl.BlockSpec((hd, gw), lambda n: (jnp.maximum(n * hb - 1, 0), gi))
    nspec = pl.BlockSpec((hd, gw), lambda n: (jnp.minimum((n + 1) * hb, last), gi))
    ospec = pl.BlockSpec((rows, gw), lambda n: (n, 0))
    return pl.pallas_call(
        functools.partial(_dil_kernel, tq=tq, dil=dil, m_len=m_len),
        grid=(s // rows,),
        in_specs=[qspec, pspec, qspec, nspec, pspec, qspec, nspec,
                  pl.BlockSpec((1, DIL_HEADS, tq, tq + 2 * DIL_HALF), lambda n: (0, 0, 0, 0))],
        out_specs=[ospec, ospec],
        out_shape=[jax.ShapeDtypeStruct((s, gw), F32)] * 2,
        scratch_shapes=[pltpu.VMEM((gw // LANES, rows, LANES), F32),
                        pltpu.VMEM((gw // LANES, rows + 2 * hd, LANES), F32),
                        pltpu.VMEM((gw // LANES, rows + 2 * hd, LANES), F32),
                        pltpu.VMEM((gw // LANES, rows, LANES), F32),
                        pltpu.VMEM((gw // LANES, rows, LANES), F32)],
        compiler_params=_cparams(("parallel",)),
        name=f"dil_attn_g{gi}",
    )(bq, bk, bk, bk, bv, bv, bv, bias)


def _gla_kernel(*refs, reverse, tb, finalize):
    if finalize:
        (q_ref, k_ref, v_ref, lr_ref, wg_ref, gb_ref, tri_ref, ofwd_ref, r_ref, ng_ref,
         o_ref, s_ref, oacc_ref) = refs
    else:
        q_ref, k_ref, v_ref, lr_ref, wg_ref, gb_ref, tri_ref, o_ref, s_ref = refs
        oacc_ref = o_ref
    cw = GLA_CHUNK
    qk = GLA_HEADS * GLA_DK
    vw = GLA_HEADS * GLA_DV

    @pl.when(pl.program_id(0) == 0)
    def _():
        s_ref[...] = jnp.zeros(s_ref.shape, F32)

    logits = _dot(lr_ref[...], wg_ref[...]) + gb_ref[...]
    g = (jnp.minimum(logits, 0.0) - jnp.log(1.0 + jnp.exp(-jnp.abs(logits)))) * (1.0 / GLA_TAU)
    ghi, glo = _split(g)
    tri = tri_ref[...]
    b = _dot(tri, ghi) + _dot(tri, glo)
    qg = (q_ref[...].astype(F32) * (GLA_DK ** -0.5) * jnp.exp(b)).astype(BF16)
    k = k_ref[...].astype(F32)
    kg = (k * jnp.exp(-b)).astype(BF16)
    v = v_ref[...].astype(BF16)

    lane_q = lax.broadcasted_iota(jnp.int32, (cw, qk), 1)
    rr = lax.broadcasted_iota(jnp.int32, (GLA_HEADS * cw, cw), 0)
    cc = lax.broadcasted_iota(jnp.int32, (GLA_HEADS * cw, cw), 1)
    tt = rr % cw
    amask = (cc > tt) if reverse else (cc <= tt)
    srow = lax.broadcasted_iota(jnp.int32, (vw, qk), 0) // GLA_DV
    scol = lax.broadcasted_iota(jnp.int32, (vw, qk), 1) // GLA_DK
    bdmask = srow == scol

    n_chunks = tb // cw
    order = range(n_chunks - 1, -1, -1) if reverse else range(n_chunks)
    for ci in order:
        rows = slice(ci * cw, (ci + 1) * cw)
        bc = b[rows]
        b_end = bc[0:1] if reverse else bc[cw - 1:cw]
        kdec = (k[rows] * jnp.exp(b_end - bc)).astype(BF16)
        qg_c = qg[rows]
        qs = jnp.concatenate(
            [jnp.where((lane_q // GLA_DK) == h, qg_c, jnp.zeros_like(qg_c)) for h in range(GLA_HEADS)], axis=0)
        a = jnp.where(amask, _dot_nt(qs, kg[rows]), 0.0)
        obig = _dot(a.astype(BF16), v[rows])
        o_intra = jnp.concatenate(
            [obig[h * cw:(h + 1) * cw, h * GLA_DV:(h + 1) * GLA_DV] for h in range(GLA_HEADS)], axis=1)
        state = s_ref[...]
        oacc_ref[rows, :] = o_intra + _dot_nt(qg_c, state.astype(BF16))
        ds = _dot_tn(v[rows], kdec)
        s_ref[...] = jnp.where(bdmask, jnp.exp(b_end) * state + ds, 0.0)

    if finalize:
        o = ofwd_ref[...] + oacc_ref[...]
        r = r_ref[...].astype(F32)
        outs = []
        for h in range(GLA_HEADS):
            sl = slice(h * GLA_DV, (h + 1) * GLA_DV)
            outs.append(_rms(o[:, sl]) * ng_ref[...] * (r[:, sl] * _sigmoid(r[:, sl])))
        o_ref[...] = jnp.concatenate(outs, axis=1)


def _chunk_tri(tb, reverse):
    i = np.arange(tb)
    same = (i[:, None] // GLA_CHUNK) == (i[None, :] // GLA_CHUNK)
    tri = (i[None, :] >= i[:, None]) if reverse else (i[None, :] <= i[:, None])
    return jnp.asarray(same & tri, dtype=BF16)


def _gla_scan(z, wg, gb, tb, reverse, fin=None):
    s = z.shape[0]
    nb = s // tb
    qk = GLA_HEADS * GLA_DK
    vw = GLA_HEADS * GLA_DV
    blk = (lambda i: nb - 1 - i) if reverse else (lambda i: i)
    zspec = lambda w, off: pl.BlockSpec((tb, w), lambda i: (blk(i), off // w))
    cspec = lambda r, c: pl.BlockSpec((r, c), lambda i: (0, 0))
    in_specs = [zspec(qk, Z_AQ), zspec(qk, Z_AK), zspec(vw, Z_AV), zspec(LANES, Z_ALR),
                cspec(LANES, qk), cspec(1, qk), cspec(tb, tb)]
    args = [z, z, z, z, wg, gb, _chunk_tri(tb, reverse)]
    scratch = [pltpu.VMEM((vw, qk), F32)]
    if fin is not None:
        o_fwd, norm_g = fin
        in_specs += [pl.BlockSpec((tb, vw), lambda i: (blk(i), 0)), zspec(vw, Z_AR), cspec(1, GLA_DV)]
        args += [o_fwd, z, norm_g.reshape(1, GLA_DV)]
        scratch.append(pltpu.VMEM((tb, vw), F32))
    return pl.pallas_call(
        functools.partial(_gla_kernel, reverse=reverse, tb=tb, finalize=fin is not None),
        grid=(nb,),
        in_specs=in_specs,
        out_specs=pl.BlockSpec((tb, vw), lambda i: (blk(i), 0)),
        out_shape=jax.ShapeDtypeStruct((s, vw), F32),
        scratch_shapes=scratch,
        compiler_params=_cparams(("arbitrary",)),
        name="gla_bwd" if reverse else "gla_fwd",
    )(*args)


def _hy_pre_kernel(u_ref, up_ref, un_ref, w_ref, b_ref, x0_ref, z_ref, zb_ref, *, tm):
    i = pl.program_id(0)
    u = u_ref[...].astype(F32)
    row = lax.broadcasted_iota(jnp.int32, u.shape, 0)
    prev_row = jnp.where(i == 0, 0.0, up_ref[...].astype(F32)[HALO_ROWS - 1:HALO_ROWS, :])
    next_row = jnp.where(i == pl.num_programs(0) - 1, 0.0, un_ref[...].astype(F32)[0:1, :])
    u_prev = jnp.where(row == 0, prev_row, pltpu.roll(u, 1, axis=0))
    u_next = jnp.where(row == tm - 1, next_row, pltpu.roll(u, tm - 1, axis=0))
    y = b_ref[...] + u_prev * w_ref[0:1] + u * w_ref[1:2] + u_next * w_ref[2:3]
    x0_ref[...] = y[:, :HY_WIDTH]
    z = y[:, HY_WIDTH:2 * HY_WIDTH] * y[:, 2 * HY_WIDTH:]
    z_ref[...] = z
    zb_ref[...] = z.astype(BF16)


def _hy_pre(z, conv_w, conv_b, tm):
    s = z.shape[0]
    w = 3 * HY_WIDTH
    nr = s // HALO_ROWS
    ospec = pl.BlockSpec((tm, HY_WIDTH), lambda i: (i, 0))
    return pl.pallas_call(
        functools.partial(_hy_pre_kernel, tm=tm),
        grid=(s // tm,),
        in_specs=[pl.BlockSpec((tm, w), lambda i: (i, 0)),
                  pl.BlockSpec((HALO_ROWS, w), lambda i: (jnp.maximum(i * (tm // HALO_ROWS) - 1, 0), 0)),
                  pl.BlockSpec((HALO_ROWS, w), lambda i: (jnp.minimum((i + 1) * (tm // HALO_ROWS), nr - 1), 0)),
                  pl.BlockSpec((3, w), lambda i: (0, 0)),
                  pl.BlockSpec((1, w), lambda i: (0, 0))],
        out_specs=[ospec, ospec, ospec],
        out_shape=[jax.ShapeDtypeStruct((s, HY_WIDTH), F32)] * 2 + [jax.ShapeDtypeStruct((s, HY_WIDTH), BF16)],
        compiler_params=_cparams(("parallel",)),
        name="hyena_pre",
    )(z, z, z, conv_w, conv_b.reshape(1, w))


def _hy_filter_kernel(emb_ref, w1_ref, b1_ref, f1_ref, w2_ref, b2_ref, f2_ref, w3_ref, dl_ref,
                      kb_ref, norm_ref, *, tl, seq):
    i = pl.program_id(0)
    half = tl // 2
    emb = emb_ref[...]
    x = jnp.concatenate([emb[:half], emb[half:]], axis=1)
    h = jnp.sin(f1_ref[...] * (_dot3(x, w1_ref[...]) + b1_ref[...]))
    h = jnp.sin(f2_ref[...] * (_dot3(h, w2_ref[...]) + b2_ref[...]))
    h = _dot3(h, w3_ref[...])
    back = i * tl >= seq
    pick = lambda y: jnp.where(back, y[:, HY_WIDTH:], y[:, :HY_WIDTH])
    h = jnp.concatenate([pick(h[:, :2 * HY_WIDTH]), pick(h[:, 2 * HY_WIDTH:])], axis=0)
    h = h * jnp.exp(-emb[:, 0:1] * dl_ref[...])
    row = i * tl + lax.broadcasted_iota(jnp.int32, h.shape, 0)
    h = jnp.where(row == seq, 0.0, h)
    kb_ref[...] = h.astype(BF16)

    @pl.when(i == 0)
    def _():
        norm_ref[...] = jnp.zeros(norm_ref.shape, F32)

    norm_ref[...] += jnp.sum(jnp.abs(h), axis=0, keepdims=True)


def _hy_positions(seq):
    t = np.linspace(0.0, 1.0, seq, dtype=np.float32)[:, None]
    bands = (HY_EMB - 1) // 2
    freqs = np.linspace(1e-4, bands - 1, bands, dtype=np.float32)[None]
    w = (np.float32(2.0 * math.pi) * np.arange(seq, dtype=np.float32)[:, None] / np.float32(seq))
    zf = np.concatenate([t, np.cos(freqs * w), -np.sin(freqs * w)], axis=-1).astype(np.float32)
    pos = np.concatenate([np.arange(seq), [0], np.arange(seq - 1, 0, -1)])
    return jnp.asarray(np.pad(zf[pos], ((0, 0), (0, LANES - HY_EMB))))


def _block_diag2(w):
    z = jnp.zeros_like(w)
    return jnp.concatenate([jnp.concatenate([w, z], axis=1), jnp.concatenate([z, w], axis=1)], axis=0)


def _hy_filter(emb, w1, b1, fr1, w2, b2, fr2, w3, tl):
    n = emb.shape[0]
    seq = n // 2
    w1p = jnp.pad(w1, ((0, LANES - HY_EMB), (0, 0)))
    twice = lambda v: jnp.concatenate([v, v])[None]
    min_decay = math.log(HY_DECAY_TARGET) / HY_SLOW_DECAY
    max_decay = math.log(HY_DECAY_TARGET) / HY_FAST_DECAY
    deltas = jnp.asarray(np.abs(np.linspace(min_decay, max_decay, HY_WIDTH, dtype=np.float32))[None])
    cs = lambda r, c: pl.BlockSpec((r, c), lambda i: (0, 0))
    return pl.pallas_call(
        functools.partial(_hy_filter_kernel, tl=tl, seq=seq),
        grid=(n // tl,),
        in_specs=[pl.BlockSpec((tl, LANES), lambda i: (i, 0)),
                  cs(2 * LANES, LANES), cs(1, LANES), cs(1, LANES),
                  cs(LANES, LANES), cs(1, LANES), cs(1, LANES),
                  cs(LANES, 4 * HY_WIDTH), cs(1, HY_WIDTH)],
        out_specs=[pl.BlockSpec((tl, HY_WIDTH), lambda i: (i, 0)), pl.BlockSpec((1, HY_WIDTH), lambda i: (0, 0))],
        out_shape=[jax.ShapeDtypeStruct((n, HY_WIDTH), BF16), jax.ShapeDtypeStruct((1, HY_WIDTH), F32)],
        compiler_params=_cparams(("arbitrary",)),
        name="hyena_filter",
    )(emb, _block_diag2(w1p), twice(b1), twice(fr1), _block_diag2(w2), twice(b2), twice(fr2),
      _block_diag2(w3), deltas)


def _dft_tables(n1):
    n = n1 * DFT_N2
    a = np.arange(n1, dtype=np.int64)
    ang1 = (2.0 * math.pi / n1) * ((a[:, None] * a[None, :]) % n1)
    c1, s1 = np.cos(ang1), np.sin(ang1)
    k1 = np.arange(n1, dtype=np.int64)[:, None, None]
    k2 = np.arange(DFT_N2, dtype=np.int64)[None, :, None]
    n2 = np.arange(DFT_N2, dtype=np.int64)[None, None, :]
    ang = (2.0 * math.pi / n) * ((n2 * (k1 + n1 * k2)) % n)
    gc, gs = np.cos(ang), np.sin(ang)
    tb = lambda x: jnp.asarray(np.ascontiguousarray(x).astype(BF16))
    gc, gs = tb(gc), tb(gs)
    gct, gst = jnp.swapaxes(gc, 1, 2), jnp.swapaxes(gs, 1, 2)
    blk = lambda a, b, c, d: jnp.concatenate([jnp.concatenate([a, b], axis=2), jnp.concatenate([c, d], axis=2)], axis=1)
    return dict(c1=tb(c1), s1=tb(s1), g2=blk(gc, gs, -gs, gc), h2=blk(gct, -gst, gst, gct))


def _dft1_kernel(c_ref, s_ref, x_ref, re_ref, im_ref):
    x = x_ref[...]
    re_ref[...] = _dot(c_ref[...], x).astype(BF16)
    im_ref[...] = (-_dot(s_ref[...], x)).astype(BF16)


def _dft1(x, c1, s1, cb):
    k1, w = x.shape
    n1 = c1.shape[0]
    ospec = pl.BlockSpec((n1, cb), lambda j: (0, j))
    return pl.pallas_call(
        _dft1_kernel,
        grid=(w // cb,),
        in_specs=[pl.BlockSpec((n1, k1), lambda j: (0, 0)), pl.BlockSpec((n1, k1), lambda j: (0, 0)),
                  pl.BlockSpec((k1, cb), lambda j: (0, j))],
        out_specs=[ospec, ospec],
        out_shape=[jax.ShapeDtypeStruct((n1, w), BF16)] * 2,
        compiler_params=_cparams(("parallel",)),
        name="dft_stage1",
    )(c1, s1, x)


def _stack(re, im):
    return jnp.concatenate([re, im], axis=0)


def _dft2_filter_kernel(g2_ref, are_ref, aim_ref, norm_ref, kre_ref, kim_ref, *, kb):
    inv = 1.0 / norm_ref[...]
    for t in range(kb):
        x = _dot(g2_ref[t], _stack(are_ref[t], aim_ref[t])) * inv
        kre_ref[t] = x[:DFT_N2].astype(BF16)
        kim_ref[t] = x[DFT_N2:].astype(BF16)


def _dft2_filter(tabs, are, aim, norm, kb):
    n1, _, c = are.shape
    gspec = pl.BlockSpec((kb, 2 * DFT_N2, 2 * DFT_N2), lambda i: (i, 0, 0))
    aspec = pl.BlockSpec((kb, DFT_N2, c), lambda i: (i, 0, 0))
    return pl.pallas_call(
        functools.partial(_dft2_filter_kernel, kb=kb),
        grid=(n1 // kb,),
        in_specs=[gspec, aspec, aspec, pl.BlockSpec((1, c), lambda i: (0, 0))],
        out_specs=[aspec, aspec],
        out_shape=[jax.ShapeDtypeStruct((n1, DFT_N2, c), BF16)] * 2,
        compiler_params=_cparams(("parallel",)),
        name="dft_stage2_filter",
    )(tabs["g2"], are, aim, norm)


def _conv_mid_kernel(g2_ref, h2_ref, are_ref, aim_ref, kre_ref, kim_ref, bre_ref, bim_ref, *, kb):
    for t in range(kb):
        x = _dot(g2_ref[t], _stack(are_ref[t], aim_ref[t]))
        xre, xim = x[:DFT_N2], x[DFT_N2:]
        kre, kim = kre_ref[t].astype(F32), kim_ref[t].astype(F32)
        yre = (xre * kre - xim * kim).astype(BF16)
        yim = (xre * kim + xim * kre).astype(BF16)
        b = _dot(h2_ref[t], _stack(yre, yim))
        bre_ref[t] = b[:DFT_N2].astype(BF16)
        bim_ref[t] = b[DFT_N2:].astype(BF16)


def _conv_mid(tabs, are, aim, kre, kim, kb):
    n1, _, c = are.shape
    gspec = pl.BlockSpec((kb, 2 * DFT_N2, 2 * DFT_N2), lambda i: (i, 0, 0))
    aspec = pl.BlockSpec((kb, DFT_N2, c), lambda i: (i, 0, 0))
    return pl.pallas_call(
        functools.partial(_conv_mid_kernel, kb=kb),
        grid=(n1 // kb,),
        in_specs=[gspec, gspec, aspec, aspec, aspec, aspec],
        out_specs=[aspec, aspec],
        out_shape=[jax.ShapeDtypeStruct((n1, DFT_N2, c), BF16)] * 2,
        compiler_params=_cparams(("parallel",)),
        name="conv_spectral",
    )(tabs["g2"], tabs["h2"], are, aim, kre, kim)


def _idft1_kernel(c_ref, s_ref, bre_ref, bim_ref, z_ref, x0_ref, skip_ref, o_ref, *, inv_n):
    y = (_dot(c_ref[...], bre_ref[...]) - _dot(s_ref[...], bim_ref[...])) * inv_n
    o_ref[...] = x0_ref[...] * (y + z_ref[...] * skip_ref[...])


def _idft1(c1h, s1h, bre, bim, z, x0, skip_t, cb):
    ko, n1 = c1h.shape
    w = bre.shape[1]
    n = n1 * DFT_N2
    ospec = pl.BlockSpec((ko, cb), lambda j: (0, j))
    return pl.pallas_call(
        functools.partial(_idft1_kernel, inv_n=1.0 / n),
        grid=(w // cb,),
        in_specs=[pl.BlockSpec((ko, n1), lambda j: (0, 0)), pl.BlockSpec((ko, n1), lambda j: (0, 0)),
                  pl.BlockSpec((n1, cb), lambda j: (0, j)), pl.BlockSpec((n1, cb), lambda j: (0, j)),
                  ospec, ospec, pl.BlockSpec((1, cb), lambda j: (0, 0))],
        out_specs=ospec,
        out_shape=jax.ShapeDtypeStruct((ko, w), F32),
        compiler_params=_cparams(("parallel",)),
        name="idft_stage1",
    )(c1h, s1h, bre, bim, z, x0, skip_t)


def _hyena(z_all, tabs, emb, p, i, cb, kb):
    seq = z_all.shape[0]
    n1 = 2 * seq // DFT_N2
    c = HY_WIDTH
    x0, zf, zb = _hy_pre(z_all, p["hy_conv_w"][i], p["hy_conv_b"][i], min(512, seq))
    kern_b, norm = _hy_filter(emb, p["hy_w1"][i], p["hy_b1"][i], p["hy_freq1"][i], p["hy_w2"][i],
                                 p["hy_b2"][i], p["hy_freq2"][i], p["hy_w3"][i], min(1024, seq))
    wide = DFT_N2 * c
    kre, kim = _dft1(kern_b.reshape(n1, wide), tabs["c1"], tabs["s1"], cb)
    kre, kim = _dft2_filter(tabs, kre.reshape(n1, DFT_N2, c), kim.reshape(n1, DFT_N2, c), norm, kb)
    are, aim = _dft1(zb.reshape(n1 // 2, wide), tabs["c1"][:, :n1 // 2], tabs["s1"][:, :n1 // 2], cb)
    bre, bim = _conv_mid(tabs, are.reshape(n1, DFT_N2, c), aim.reshape(n1, DFT_N2, c), kre, kim, kb)
    cbi = cb // 2
    skip_t = jnp.tile(p["hy_skip"][i][None], (1, cbi // c))
    yd = _idft1(tabs["c1"][:n1 // 2], tabs["s1"][:n1 // 2], bre.reshape(n1, wide), bim.reshape(n1, wide),
                zf.reshape(n1 // 2, wide), x0.reshape(n1 // 2, wide), skip_t, cbi)
    return yd.reshape(seq, c)


def _combine_kernel(x_ref, ya_ref, o0_ref, o1_ref, o2_ref, l0_ref, l1_ref, l2_ref, yc_ref, yd_ref,
                    g0_ref, g1_ref, g2_ref, g3_ref, pa_ref, pb_ref, pc_ref, pd_ref, wo_ref, o_ref):
    l0, l1, l2 = l0_ref[...], l1_ref[...], l2_ref[...]
    mx = jnp.maximum(jnp.maximum(l0, l1), l2)
    e0, e1, e2 = jnp.exp(l0 - mx), jnp.exp(l1 - mx), jnp.exp(l2 - mx)
    yb = (e0 * o0_ref[...] + e1 * o1_ref[...] + e2 * o2_ref[...]) / (e0 + e1 + e2)
    gate = lambda ref: _sigmoid(ref[...].astype(F32))
    m = (gate(g0_ref) * _dot(ya_ref[...].astype(BF16), pa_ref[...])
         + gate(g1_ref) * _dot(yb.astype(BF16), pb_ref[...])
         + gate(g2_ref) * _dot(yc_ref[...].astype(BF16), pc_ref[...])
         + gate(g3_ref) * _dot(yd_ref[...].astype(BF16), pd_ref[...]))
    o_ref[...] = x_ref[...] + _dot(m.astype(BF16), wo_ref[...])


def _combine(x, z, ya, dil_outs, yc, yd, pa, pb, pc, pd, wo, tm):
    s, d = x.shape
    rs = lambda w: pl.BlockSpec((tm, w), lambda i: (i, 0))
    gs = lambda b: pl.BlockSpec((tm, d), lambda i: (i, Z_GATE // d + b))
    ws = lambda a: pl.BlockSpec(a.shape, lambda i: (0, 0))
    (o0, l0), (o1, l1), (o2, l2) = dil_outs
    gw = DIL_HEADS * HEAD_DIM
    return pl.pallas_call(
        _combine_kernel,
        grid=(s // tm,),
        in_specs=[rs(d), rs(ya.shape[1]), rs(gw), rs(gw), rs(gw), rs(gw), rs(gw), rs(gw),
                  rs(yc.shape[1]), rs(yd.shape[1]), gs(0), gs(1), gs(2), gs(3),
                  ws(pa), ws(pb), ws(pc), ws(pd), ws(wo)],
        out_specs=rs(d),
        out_shape=jax.ShapeDtypeStruct((s, d), F32),
        compiler_params=_cparams(("parallel",)),
        name="combine",
    )(x, ya, o0, o1, o2, l0, l1, l2, yc, yd, z, z, z, z, pa, pb, pc, pd, wo)


def _mlp_kernel(x_ref, g_ref, w1_ref, w2_ref, o_ref, h_ref, acc_ref):
    j = pl.program_id(1)

    @pl.when(j == 0)
    def _():
        h_ref[...] = (_rms(x_ref[...]) * g_ref[...]).astype(BF16)
        acc_ref[...] = jnp.zeros(acc_ref.shape, F32)

    a = jnp.maximum(_dot(h_ref[...], w1_ref[...]), 0.0)
    acc_ref[...] += _dot((a * a).astype(BF16), w2_ref[...])

    @pl.when(j == pl.num_programs(1) - 1)
    def _():
        o_ref[...] = x_ref[...] + acc_ref[...]


def _mlp(x, g, w1, w2, tm, tf):
    s, d = x.shape
    ff = w1.shape[1]
    return pl.pallas_call(
        _mlp_kernel,
        grid=(s // tm, ff // tf),
        in_specs=[pl.BlockSpec((tm, d), lambda i, j: (i, 0)),
                  pl.BlockSpec((1, d), lambda i, j: (0, 0)),
                  pl.BlockSpec((d, tf), lambda i, j: (0, j)),
                  pl.BlockSpec((tf, d), lambda i, j: (j, 0))],
        out_specs=pl.BlockSpec((tm, d), lambda i, j: (i, 0)),
        out_shape=jax.ShapeDtypeStruct((s, d), F32),
        scratch_shapes=[pltpu.VMEM((tm, d), BF16), pltpu.VMEM((tm, d), F32)],
        compiler_params=_cparams(("parallel", "arbitrary")),
        name="mlp",
    )(x, g.reshape(1, d), w1, w2)


def _permute_w_in(w):
    d = w.shape[0]
    pieces = [w[:, _O_DU:_O_GATE], w[:, _O_B:_O_C], w[:, _O_AQ:_O_AK], w[:, _O_C:_O_DU],
              w[:, _O_AV:_O_AR], w[:, _O_AR:_O_ALR], w[:, _O_AK:_O_AV], w[:, _O_ALR:_O_B],
              jnp.zeros((d, Z_GATE - Z_ALR - 2 * GLA_RANK), w.dtype), w[:, _O_GATE:_O_END]]
    return jnp.concatenate(pieces, axis=1).astype(BF16)


def _dil_bias_idx(tq, dil):
    rel = _skewed_rel(tq, tq + 2 * DIL_HALF, -1, -DIL_HALF)
    return _t5_bucket(rel * dil)[None].astype(np.int32)


def _diff_bias_idx(tq, tk):
    far_lo, far_hi = _diff_far_offsets(tq, tk)
    return np.stack([_t5_bucket(_skewed_rel(tk, tq, 1, o * tk)) for o in range(far_lo, far_hi + 1)]).astype(np.int32)


def _forward(x, p, *, t_diff, tk_diff, qc_diff, tq_dil, tb_gla, tm_proj, tn_proj, tm_row, tm_mlp, tf_mlp, cb_dft, kb_dft):
    seq = x.shape[0]
    depth = p["w_in"].shape[0]
    n_dil_bias = DIL_GROUPS * DIL_HEADS
    tq_dil = [min(tq, seq // dil) for tq, (_, dil) in zip(tq_dil, DIL_PATTERNS)]
    dil_bias = [_bias_tiles(p["t5_bias"], _dil_bias_idx(tq, dil), tq, tq + 2 * DIL_HALF, DIL_HEADS, gi * DIL_HEADS)
                for gi, (tq, (_, dil)) in enumerate(zip(tq_dil, DIL_PATTERNS))]
    t_diff, tk_diff = min(t_diff, seq), min(tk_diff, seq)
    diff_bias = _bias_tiles(p["t5_bias"], _diff_bias_idx(t_diff, tk_diff), tk_diff, t_diff, DIFF_HEADS,
                            n_dil_bias, LOG2E)
    tabs = _dft_tables(2 * seq // DFT_N2)
    emb = _hy_positions(seq)
    rep = lambda g, n: jnp.tile(g, n)
    for i in range(depth):
        z = _norm_matmul(x, p["norm1_g"][i], _permute_w_in(p["w_in"][i]), tm_proj, tn_proj)
        wg = [jnp.zeros((LANES, GLA_HEADS * GLA_DK), F32).at[j * GLA_RANK:(j + 1) * GLA_RANK].set(
            p["gla_gate_w"][i, j]).astype(BF16) for j in range(2)]
        gb = p["gla_gate_b"][i]
        o_fwd = _gla_scan(z, wg[0], gb[0:1], tb_gla, False)
        ya = _gla_scan(z, wg[1], gb[1:2], tb_gla, True, fin=(o_fwd, p["gla_norm_g"][i]))
        bq, bk, bv, cq, ck, cv = _prep(
            z, jnp.repeat(p["dil_qnorm_g"][i], DIL_HEADS, axis=0).reshape(-1),
            jnp.repeat(p["dil_knorm_g"][i], DIL_HEADS, axis=0).reshape(-1),
            rep(p["diff_qnorm_g"][i], 2 * DIFF_HEADS), rep(p["diff_knorm_g"][i], 2 * DIFF_HEADS), tm_row)
        dil_outs = [_dil_attn(bq, bk, bv, dil_bias[gi], gi, dil, tq_dil[gi])
                    for gi, (_, dil) in enumerate(DIL_PATTERNS)]
        lam_init = 0.8 - 0.6 * math.exp(-0.3 * i)
        yc = _diff_attn(cq, ck, cv, diff_bias, p["diff_lambda"][i], p["diff_subln_g"][i], lam_init,
                        t_diff, tk_diff, qc_diff)
        yd = _hyena(z, tabs, emb, p, i, cb_dft, kb_dft)
        x = _combine(x, z, ya, dil_outs, yc, yd, p["proj_a"][i].astype(BF16), p["proj_b"][i].astype(BF16),
                     p["proj_c"][i].astype(BF16), p["proj_d"][i].astype(BF16), p["w_out"][i].astype(BF16),
                     tm_row)
        x = _mlp(x, p["norm2_g"][i], p["mlp_w1"][i].astype(BF16), p["mlp_w2"][i].astype(BF16), tm_mlp, tf_mlp)
    return x


def kernel(x, t5_bias, norm1_g, w_in, gla_gate_w, gla_gate_b, gla_norm_g, dil_qnorm_g, dil_knorm_g,
           diff_qnorm_g, diff_knorm_g, diff_lambda, diff_subln_g, hy_conv_w, hy_conv_b, hy_w1, hy_b1,
           hy_freq1, hy_w2, hy_b2, hy_freq2, hy_w3, hy_skip, proj_a, proj_b, proj_c, proj_d, w_out,
           norm2_g, mlp_w1, mlp_w2):
    p = dict(t5_bias=t5_bias, norm1_g=norm1_g, w_in=w_in, gla_gate_w=gla_gate_w, gla_gate_b=gla_gate_b,
             gla_norm_g=gla_norm_g, dil_qnorm_g=dil_qnorm_g, dil_knorm_g=dil_knorm_g,
             diff_qnorm_g=diff_qnorm_g, diff_knorm_g=diff_knorm_g, diff_lambda=diff_lambda,
             diff_subln_g=diff_subln_g, hy_conv_w=hy_conv_w, hy_conv_b=hy_conv_b, hy_w1=hy_w1, hy_b1=hy_b1,
             hy_freq1=hy_freq1, hy_w2=hy_w2, hy_b2=hy_b2, hy_freq2=hy_freq2, hy_w3=hy_w3, hy_skip=hy_skip,
             proj_a=proj_a, proj_b=proj_b, proj_c=proj_c, proj_d=proj_d, w_out=w_out, norm2_g=norm2_g,
             mlp_w1=mlp_w1, mlp_w2=mlp_w2)
    b, s, d = x.shape
    outs = [_forward(x[bi], p, t_diff=2048, tk_diff=1024, qc_diff=(256, 768, 768, 256), tq_dil=(256, 256, 128), tb_gla=512, tm_proj=2048, tn_proj=512, tm_row=512,
                     tm_mlp=1024, tf_mlp=1024, cb_dft=8192, kb_dft=16) for bi in range(b)]
    return jnp.stack(outs)
```

```python
import functools
import math

import jax
import jax.numpy as jnp
import numpy as np
from jax import lax
from jax.experimental import pallas as pl
from jax.experimental.pallas import tpu as pltpu

F32 = jnp.float32
BF16 = jnp.bfloat16

D_MODEL = 1024
HEAD_DIM = 64
GLA_HEADS = 4
GLA_DK = 64
GLA_DV = 128
GLA_RANK = 16
GLA_TAU = 16.0
GLA_CHUNK = 64
DIL_PATTERNS = ((128, 1), (512, 4), (2048, 16))
DIL_GROUPS = 3
DIL_HEADS = 4
DIL_HALF = 64
DIFF_HEADS = 4
DIFF_DV = 128
DIFF_VT = DIFF_DV + 16
HY_WIDTH = 512
HY_EMB = 33
HY_FFN = 64
HY_DECAY_TARGET = 1e-2
HY_FAST_DECAY = 0.3
HY_SLOW_DECAY = 1.5
T5_BUCKETS = 32
T5_MAX_DIST = 1024
N_BIAS_HEADS = 16
D_FF = 4096
RMS_EPS = 1e-6
LOG2E = math.log2(math.e)

LANES = 128
HALO_ROWS = 16
VMEM_LIMIT = 48 * 1024 * 1024

Z_DU = 0
Z_BQ, Z_BK, Z_BV = 1536, 2304, 3072
Z_AQ = 3840
Z_CQ, Z_CK, Z_CV = 4096, 4608, 5120
Z_AV, Z_AR = 5632, 6144
Z_AK = 6656
Z_ALR = 6912
Z_GATE = 7168
Z_COLS = 11264

_O_AQ, _O_AK, _O_AV, _O_AR, _O_ALR, _O_B, _O_C, _O_DU, _O_GATE, _O_END = (
    0, 256, 512, 1024, 1536, 1568, 3872, 5408, 6944, 11040)

DFT_N2 = 128


def _cparams(sem):
    return pltpu.CompilerParams(dimension_semantics=sem, vmem_limit_bytes=VMEM_LIMIT)


def _dot(a, b):
    return jnp.dot(a, b, preferred_element_type=F32)


def _dot_nt(a, b):
    return lax.dot_general(a, b, (((1,), (1,)), ((), ())), preferred_element_type=F32)


def _dot_tn(a, b):
    return lax.dot_general(a, b, (((0,), (0,)), ((), ())), preferred_element_type=F32)


def _split(x):
    hi = x.astype(BF16)
    lo = (x - hi.astype(F32)).astype(BF16)
    return hi, lo


def _dot3(a, b):
    ah, al = _split(a)
    bh, bl = _split(b)
    return _dot(ah, bh) + _dot(ah, bl) + _dot(al, bh)


def _rms(x):
    return x * lax.rsqrt(jnp.mean(x * x, axis=-1, keepdims=True) + RMS_EPS)


def _sigmoid(x):
    return 0.5 * jnp.tanh(0.5 * x) + 0.5


def _norm_matmul_kernel(x_ref, g_ref, w_ref, o_ref, h_ref):
    @pl.when(pl.program_id(1) == 0)
    def _():
        h_ref[...] = (_rms(x_ref[...]) * g_ref[...]).astype(BF16)

    o_ref[...] = _dot(h_ref[...], w_ref[...]).astype(o_ref.dtype)


def _norm_matmul(x, g, w, tm, tn):
    s, d = x.shape
    n = w.shape[1]
    return pl.pallas_call(
        _norm_matmul_kernel,
        grid=(s // tm, n // tn),
        in_specs=[pl.BlockSpec((tm, d), lambda i, j: (i, 0)),
                  pl.BlockSpec((1, d), lambda i, j: (0, 0)),
                  pl.BlockSpec((d, tn), lambda i, j: (0, j))],
        out_specs=pl.BlockSpec((tm, tn), lambda i, j: (i, j)),
        out_shape=jax.ShapeDtypeStruct((s, n), BF16),
        scratch_shapes=[pltpu.VMEM((tm, d), BF16)],
        compiler_params=_cparams(("parallel", "arbitrary")),
        name="in_proj",
    )(x, g.reshape(1, d), w)


def _group_norm(x, e, gain):
    hi, lo = _split(x * x)
    ew = e.shape[0]
    ms = jnp.concatenate([_dot(hi[:, c:c + ew], e) + _dot(lo[:, c:c + ew], e)
                          for c in range(0, x.shape[1], ew)], axis=1) * (1.0 / HEAD_DIM)
    return x * lax.rsqrt(ms + RMS_EPS) * gain


def _prep_kernel(bq_ref, bk_ref, bv_ref, cq_ref, ck_ref, cv_ref, e_ref,
                 gbq_ref, gbk_ref, gcq_ref, gck_ref,
                 obq_ref, obk_ref, obv_ref, ocq_ref, ock_ref, ocv_ref):
    scale = HEAD_DIM ** -0.5
    eb = ec = e_ref[...]
    f32 = lambda ref: ref[...].astype(F32)
    obq_ref[...] = (_group_norm(f32(bq_ref), eb, gbq_ref[...]) * scale).astype(BF16)
    obk_ref[...] = _group_norm(f32(bk_ref), eb, gbk_ref[...]).astype(BF16)
    obv_ref[...] = bv_ref[...]
    ocq_ref[...] = (_group_norm(f32(cq_ref), ec, gcq_ref[...]) * (scale * LOG2E)).T.astype(BF16)
    ock_ref[...] = _group_norm(f32(ck_ref), ec, gck_ref[...]).astype(BF16)
    cvt = f32(cv_ref).T
    ones = jnp.ones((DIFF_VT - DIFF_DV, cvt.shape[1]), F32)
    ocv_ref[...] = jnp.concatenate(
        [t for h in range(DIFF_HEADS) for t in (cvt[h * DIFF_DV:(h + 1) * DIFF_DV], ones)], axis=0).astype(BF16)


def _block_diag_ones(width):
    idx = np.arange(width) // HEAD_DIM
    return jnp.asarray(idx[:, None] == idx[None, :], dtype=BF16)


def _prep(z, gbq, gbk, gcq, gck, tm):
    s = z.shape[0]
    wb, wc = DIL_GROUPS * DIL_HEADS * HEAD_DIM, DIFF_HEADS * 2 * HEAD_DIM
    zspec = lambda w, off: pl.BlockSpec((tm, w), lambda i: (i, off // w))
    cspec = lambda r, c: pl.BlockSpec((r, c), lambda i: (0, 0))
    ospec = lambda w: pl.BlockSpec((tm, w), lambda i: (i, 0))
    tspec = lambda w: pl.BlockSpec((w, tm), lambda i: (0, i))
    return pl.pallas_call(
        _prep_kernel,
        grid=(s // tm,),
        in_specs=[zspec(wb, Z_BQ), zspec(wb, Z_BK), zspec(wb, Z_BV),
                  zspec(wc, Z_CQ), zspec(wc, Z_CK), zspec(wc, Z_CV),
                  cspec(2 * LANES, 2 * LANES),
                  cspec(1, wb), cspec(1, wb), cspec(1, wc), cspec(1, wc)],
        out_specs=[ospec(wb), ospec(wb), ospec(wb), tspec(wc), ospec(wc), tspec(DIFF_HEADS * DIFF_VT)],
        out_shape=[jax.ShapeDtypeStruct((s, wb), BF16)] * 3
        + [jax.ShapeDtypeStruct((wc, s), BF16), jax.ShapeDtypeStruct((s, wc), BF16),
           jax.ShapeDtypeStruct((DIFF_HEADS * DIFF_VT, s), BF16)],
        compiler_params=_cparams(("parallel",)),
        name="qk_prep",
    )(z, z, z, z, z, z, _block_diag_ones(2 * LANES),
      gbq.reshape(1, wb), gbk.reshape(1, wb), gcq.reshape(1, wc), gck.reshape(1, wc))


def _t5_bucket(rel):
    half = T5_BUCKETS // 2
    max_exact = half // 2
    ret = np.where(rel > 0, half, 0)
    n = np.abs(rel)
    nf = np.maximum(n, 1).astype(np.float64)
    large = max_exact + (np.log(nf / max_exact) / math.log(T5_MAX_DIST / max_exact)
                         * (half - max_exact)).astype(np.int64)
    large = np.minimum(large, half - 1)
    return ret + np.where(n < max_exact, n, large)


def _bias_kernel(tab_ref, rng_ref, idx_ref, o_ref, p_ref, *, head_base, out_scale, chunk):
    t = pl.program_id(0)
    col = head_base + pl.program_id(1)
    rows, width = o_ref.shape[2:]
    slabs = width // LANES
    lo, hi = rng_ref[t, 0], rng_ref[t, 1]

    def one_chunk(ci, carry):
        rs = pl.ds(pl.multiple_of(ci * chunk, chunk), chunk)
        idx = idx_ref[0, rs, :]
        p_ref[rs, :] = lax.fori_loop(
            lo, hi + 1, lambda b, acc: jnp.where(idx == b, tab_ref[b, col] * out_scale, acc),
            jnp.zeros((chunk, LANES), F32))
        return carry

    lax.fori_loop(0, p_ref.shape[0] // chunk, one_chunk, 0)
    for c in range(slabs):
        off = LANES * (slabs - 1 - c)
        o_ref[0, 0, :, c * LANES:(c + 1) * LANES] = p_ref[off:off + rows, :]


def _skewed_rel(rows, width, sign, base):
    slabs = width // LANES
    rho = np.arange(rows + LANES * (slabs - 1))[:, None]
    lane = np.arange(LANES)[None, :]
    return sign * (rho - LANES * (slabs - 1) - lane) + base


def _bias_tiles(t5_bias, idx, rows, width, n_heads, head_base, out_scale=1.0, chunk=64):
    nt, rp, _ = idx.shape
    ranges = np.stack([idx.reshape(nt, -1).min(axis=1), idx.reshape(nt, -1).max(axis=1)], axis=1).astype(np.int32)
    return pl.pallas_call(
        functools.partial(_bias_kernel, head_base=head_base, out_scale=out_scale, chunk=chunk),
        grid=(nt, n_heads),
        in_specs=[pl.BlockSpec(memory_space=pltpu.SMEM), pl.BlockSpec(memory_space=pltpu.SMEM),
                  pl.BlockSpec((1, rp, LANES), lambda t, h: (t, 0, 0))],
        out_specs=pl.BlockSpec((1, 1, rows, width), lambda t, h: (t, h, 0, 0)),
        out_shape=jax.ShapeDtypeStruct((nt, n_heads, rows, width), F32),
        scratch_shapes=[pltpu.VMEM((rp, LANES), F32)],
        compiler_params=_cparams(("parallel", "parallel")),
        name="t5_bias_tiles",
    )(t5_bias, jnp.asarray(ranges), jnp.asarray(idx))


def _diff_attn_kernel(qt_ref, k_ref, vt_ref, bias_ref, lam_ref, g_ref, o_ref,
                      qa_ref, qb_ref, m_ref, acc_ref, s_ref, *, lam_init, qc, far_lo, far_hi):
    j = pl.program_id(2)
    t = qt_ref.shape[1]

    @pl.when(j == 0)
    def _():
        qt = qt_ref[...]
        row = lax.broadcasted_iota(jnp.int32, qt.shape, 0)
        qa_ref[...] = jnp.where(row < HEAD_DIM, qt, jnp.zeros_like(qt))
        qb_ref[...] = jnp.where(row >= HEAD_DIM, qt, jnp.zeros_like(qt))
        m_ref[...] = jnp.full(m_ref.shape, -jnp.inf, F32)
        acc_ref[...] = jnp.zeros(acc_ref.shape, F32)

    starts = [sum(qc[:u]) for u in range(len(qc))]
    cuts = [slice(a, a + w) for a, w in zip(starts, qc)]
    chains = [(0, cs) for cs in cuts] + [(1, cs) for cs in reversed(cuts)]
    q_refs = (qa_ref, qb_ref)

    def step(far):
        k = k_ref[...]
        vt = vt_ref[...]
        if far:
            const = bias_ref[0, 0, 0:1, 0:1]
            scores = lambda c, cols: _dot(k, q_refs[c][:, cols])
        else:
            const = 0.0
            scores = lambda c, cols: _dot(k, q_refs[c][:, cols]) + bias_ref[0, 0, :, cols]
        width = lambda n: chains[n][1].stop - chains[n][1].start
        s_ref[0, :, :width(0)] = scores(*chains[0])
        for n, (c, cols) in enumerate(chains):
            if n + 1 < len(chains):
                s_ref[(n + 1) % 2, :, :width(n + 1)] = scores(*chains[n + 1])
            s = s_ref[n % 2, :, :width(n)]
            m_old = m_ref[c, :, cols]
            m_new = jnp.maximum(m_old, jnp.max(s, axis=0, keepdims=True) + const)
            alpha = jnp.exp2(m_old - m_new)
            p = jnp.exp2(s - (m_new - const)).astype(BF16)
            acc_ref[c, :, cols] = alpha * acc_ref[c, :, cols] + _dot(vt, p)
            m_ref[c, :, cols] = m_new

    off = j - pl.program_id(1) * (t // k_ref.shape[0])
    is_far = jnp.logical_or(off <= far_lo, off >= far_hi)
    pl.when(is_far)(lambda: step(True))
    pl.when(jnp.logical_not(is_far))(lambda: step(False))

    @pl.when(j == pl.num_programs(2) - 1)
    def _():
        lp = lam_ref[...]
        lam = (jnp.exp(jnp.sum(lp[0:1] * lp[1:2], axis=-1, keepdims=True))
               - jnp.exp(jnp.sum(lp[2:3] * lp[3:4], axis=-1, keepdims=True)) + lam_init)
        a0 = acc_ref[0]
        a1 = acc_ref[1]
        o0 = a0[:DIFF_DV] / a0[DIFF_DV:DIFF_DV + 1]
        o1 = a1[:DIFF_DV] / a1[DIFF_DV:DIFF_DV + 1]
        att = o0 - lam * o1
        y = att * lax.rsqrt(jnp.mean(att * att, axis=0, keepdims=True) + RMS_EPS)
        o_ref[...] = y.T * g_ref[...] * (1.0 - lam_init)


def _diff_far_offsets(tq, tk):
    far_lo = (-T5_MAX_DIST - tk + 1) // tk
    far_hi = -(-(T5_MAX_DIST + tq - 1) // tk)
    return far_lo, far_hi


def _diff_attn(cqt, ck, cvt, bias_tiles, lam_p, subln_g, lam_init, t, tk, qc):
    s = ck.shape[0]
    far_lo, far_hi = _diff_far_offsets(t, tk)
    ratio = t // tk
    w = 2 * HEAD_DIM
    return pl.pallas_call(
        functools.partial(_diff_attn_kernel, lam_init=lam_init, qc=qc, far_lo=far_lo, far_hi=far_hi),
        grid=(DIFF_HEADS, s // t, s // tk),
        in_specs=[pl.BlockSpec((w, t), lambda h, i, j: (h, i)),
                  pl.BlockSpec((tk, w), lambda h, i, j: (j, h)),
                  pl.BlockSpec((DIFF_VT, tk), lambda h, i, j: (h, j)),
                  pl.BlockSpec((1, 1, tk, t),
                               lambda h, i, j: (jnp.clip(j - i * ratio, far_lo, far_hi) - far_lo, h, 0, 0)),
                  pl.BlockSpec((4, HEAD_DIM), lambda h, i, j: (0, 0)),
                  pl.BlockSpec((1, DIFF_DV), lambda h, i, j: (0, 0))],
        out_specs=pl.BlockSpec((t, DIFF_DV), lambda h, i, j: (i, h)),
        out_shape=jax.ShapeDtypeStruct((s, DIFF_HEADS * DIFF_DV), F32),
        scratch_shapes=[pltpu.VMEM((w, t), BF16), pltpu.VMEM((w, t), BF16),
                        pltpu.VMEM((2, 1, t), F32), pltpu.VMEM((2, DIFF_VT, t), F32),
                        pltpu.VMEM((2, tk, max(qc)), F32)],
        compiler_params=_cparams(("parallel", "parallel", "arbitrary")),
        name="diff_attn",
    )(cqt, ck, cvt, bias_tiles, lam_p, subln_g.reshape(1, DIFF_DV))


def _dil_kernel(q_ref, kp_ref, kc_ref, kn_ref, vp_ref, vc_ref, vn_ref, bias_ref, o_ref, lse_ref,
                qs_ref, ks_ref, vs_ref, os_ref, ls_ref, *, tq, dil, nt, m_len):
    n = pl.program_id(0)
    hd = DIL_HALF * dil
    body_rows = tq * dil * nt
    halves = DIL_HEADS * HEAD_DIM // LANES

    def put(dst, rows, src_ref):
        x = src_ref[...].astype(F32)
        for t in range(halves):
            dst[t, rows, :] = x[:, t * LANES:(t + 1) * LANES]

    def strided(src, r, count):
        return jnp.concatenate([src[t, pl.ds(r, count, stride=dil), :] for t in range(halves)], axis=1)

    put(qs_ref, slice(0, body_rows), q_ref)
    for dst, (p_ref, c_ref, n_ref) in ((ks_ref, (kp_ref, kc_ref, kn_ref)), (vs_ref, (vp_ref, vc_ref, vn_ref))):
        put(dst, slice(0, hd), p_ref)
        put(dst, slice(hd, hd + body_rows), c_ref)
        put(dst, slice(hd + body_rows, 2 * hd + body_rows), n_ref)
    tk = tq + 2 * DIL_HALF
    a = lax.broadcasted_iota(jnp.int32, (tq, tk), 0)
    c = lax.broadcasted_iota(jnp.int32, (tq, tk), 1)
    delta = c - DIL_HALF - a
    in_band = jnp.concatenate([jnp.abs(delta) <= DIL_HALF] * DIL_HEADS, axis=0)
    col = lax.broadcasted_iota(jnp.int32, (1, tk), 1)
    lane = lax.broadcasted_iota(jnp.int32, (tq, DIL_HEADS * HEAD_DIM), 1)
    head_masks = [(lane // HEAD_DIM) == h for h in range(DIL_HEADS)]

    def one_tile(it, carry):
        u = it // dil
        r = it % dil + u * (tq * dil)
        kpos = (n * nt + u) * tq - DIL_HALF + col
        valid = jnp.logical_and(in_band, jnp.logical_and(kpos >= 0, kpos < m_len))
        q = strided(qs_ref, r, tq).astype(BF16)
        k = strided(ks_ref, r, tk).astype(BF16)
        v = strided(vs_ref, r, tk).astype(BF16)
        qs = jnp.concatenate([jnp.where(hm, q, jnp.zeros_like(q)) for hm in head_masks], axis=0)
        s = _dot_nt(qs, k) + bias_ref[0].reshape(DIL_HEADS * tq, tk)
        s = jnp.where(valid, s, -1e30)
        m = jnp.max(s, axis=-1, keepdims=True)
        e = jnp.exp(s - m)
        l = jnp.sum(e, axis=-1, keepdims=True)
        oh = _dot((e / l).astype(BF16), v)
        lse = m + jnp.log(l)
        o = jnp.zeros(q.shape, F32)
        lse_o = jnp.zeros(q.shape, F32)
        for h, hm in enumerate(head_masks):
            rows = slice(h * tq, (h + 1) * tq)
            o = jnp.where(hm, oh[rows], o)
            lse_o = jnp.where(hm, lse[rows], lse_o)
        for t in range(halves):
            os_ref[t, pl.ds(r, tq, stride=dil), :] = o[:, t * LANES:(t + 1) * LANES]
            ls_ref[t, pl.ds(r, tq, stride=dil), :] = lse_o[:, t * LANES:(t + 1) * LANES]
        return carry

    lax.fori_loop(0, dil * nt, one_tile, 0, unroll=min(dil * nt, 4))
    o_ref[...] = jnp.concatenate([os_ref[t] for t in range(halves)], axis=1)
    lse_ref[...] = jnp.concatenate([ls_ref[t] for t in range(halves)], axis=1)


def _dil_attn(bq, bk, bv, bias, gi, dil, tq, nt):
    s = bq.shape[0]
    m_len = s // dil
    gw = DIL_HEADS * HEAD_DIM
    rows = tq * dil * nt
    hd = DIL_HALF * dil
    hb = rows // hd
    last = s // hd - 1
    qspec = pl.BlockSpec((rows, gw), lambda n: (n, gi))
    pspec = pl.BlockSpec((hd, gw), lambda n: (jnp.maximum(n * hb - 1, 0), gi))
    nspec = pl.BlockSpec((hd, gw), lambda n: (jnp.minimum((n + 1) * hb, last), gi))
    ospec = pl.BlockSpec((rows, gw), lambda n: (n, 0))
    return pl.pallas_call(
        functools.partial(_dil_kernel, tq=tq, dil=dil, nt=nt, m_len=m_len),
        grid=(s // rows,),
        in_specs=[qspec, pspec, qspec, nspec, pspec, qspec, nspec,
                  pl.BlockSpec((1, DIL_HEADS, tq, tq + 2 * DIL_HALF), lambda n: (0, 0, 0, 0))],
        out_specs=[ospec, ospec],
        out_shape=[jax.ShapeDtypeStruct((s, gw), F32)] * 2,
        scratch_shapes=[pltpu.VMEM((gw // LANES, rows, LANES), F32),
                        pltpu.VMEM((gw // LANES, rows + 2 * hd, LANES), F32),
                        pltpu.VMEM((gw // LANES, rows + 2 * hd, LANES), F32),
                        pltpu.VMEM((gw // LANES, rows, LANES), F32),
                        pltpu.VMEM((gw // LANES, rows, LANES), F32)],
        compiler_params=_cparams(("parallel",)),
        name=f"dil_attn_g{gi}",
    )(bq, bk, bk, bk, bv, bv, bv, bias)


def _gla_kernel(*refs, reverse, tb, finalize):
    if finalize:
        (q_ref, k_ref, v_ref, lr_ref, wg_ref, gb_ref, tri_ref, ofwd_ref, r_ref, ng_ref,
         o_ref, s_ref, oacc_ref) = refs
    else:
        q_ref, k_ref, v_ref, lr_ref, wg_ref, gb_ref, tri_ref, o_ref, s_ref = refs
        oacc_ref = o_ref
    cw = GLA_CHUNK
    qk = GLA_HEADS * GLA_DK
    vw = GLA_HEADS * GLA_DV

    @pl.when(pl.program_id(0) == 0)
    def _():
        s_ref[...] = jnp.zeros(s_ref.shape, F32)

    logits = _dot(lr_ref[...], wg_ref[...]) + gb_ref[...]
    g = (jnp.minimum(logits, 0.0) - jnp.log(1.0 + jnp.exp(-jnp.abs(logits)))) * (1.0 / GLA_TAU)
    ghi, glo = _split(g)
    tri = tri_ref[...]
    b = _dot(tri, ghi) + _dot(tri, glo)
    qg = (q_ref[...].astype(F32) * (GLA_DK ** -0.5) * jnp.exp(b)).astype(BF16)
    k = k_ref[...].astype(F32)
    kg = (k * jnp.exp(-b)).astype(BF16)
    v = v_ref[...].astype(BF16)

    lane_q = lax.broadcasted_iota(jnp.int32, (cw, qk), 1)
    rr = lax.broadcasted_iota(jnp.int32, (GLA_HEADS * cw, cw), 0)
    cc = lax.broadcasted_iota(jnp.int32, (GLA_HEADS * cw, cw), 1)
    tt = rr % cw
    amask = (cc > tt) if reverse else (cc <= tt)
    srow = lax.broadcasted_iota(jnp.int32, (vw, qk), 0) // GLA_DV
    scol = lax.broadcasted_iota(jnp.int32, (vw, qk), 1) // GLA_DK
    bdmask = srow == scol

    n_chunks = tb // cw
    order = range(n_chunks - 1, -1, -1) if reverse else range(n_chunks)
    for ci in order:
        rows = slice(ci * cw, (ci + 1) * cw)
        bc = b[rows]
        b_end = bc[0:1] if reverse else bc[cw - 1:cw]
        kdec = (k[rows] * jnp.exp(b_end - bc)).astype(BF16)
        qg_c = qg[rows]
        qs = jnp.concatenate(
            [jnp.where((lane_q // GLA_DK) == h, qg_c, jnp.zeros_like(qg_c)) for h in range(GLA_HEADS)], axis=0)
        a = jnp.where(amask, _dot_nt(qs, kg[rows]), 0.0)
        obig = _dot(a.astype(BF16), v[rows])
        o_intra = jnp.concatenate(
            [obig[h * cw:(h + 1) * cw, h * GLA_DV:(h + 1) * GLA_DV] for h in range(GLA_HEADS)], axis=1)
        state = s_ref[...]
        oacc_ref[rows, :] = o_intra + _dot_nt(qg_c, state.astype(BF16))
        ds = _dot_tn(v[rows], kdec)
        s_ref[...] = jnp.where(bdmask, jnp.exp(b_end) * state + ds, 0.0)

    if finalize:
        o = ofwd_ref[...] + oacc_ref[...]
        r = r_ref[...].astype(F32)
        outs = []
        for h in range(GLA_HEADS):
            sl = slice(h * GLA_DV, (h + 1) * GLA_DV)
            outs.append(_rms(o[:, sl]) * ng_ref[...] * (r[:, sl] * _sigmoid(r[:, sl])))
        o_ref[...] = jnp.concatenate(outs, axis=1)


def _chunk_tri(tb, reverse):
    i = np.arange(tb)
    same = (i[:, None] // GLA_CHUNK) == (i[None, :] // GLA_CHUNK)
    tri = (i[None, :] >= i[:, None]) if reverse else (i[None, :] <= i[:, None])
    return jnp.asarray(same & tri, dtype=BF16)


def _gla_scan(z, wg, gb, tb, reverse, fin=None):
    s = z.shape[0]
    nb = s // tb
    qk = GLA_HEADS * GLA_DK
    vw = GLA_HEADS * GLA_DV
    blk = (lambda i: nb - 1 - i) if reverse else (lambda i: i)
    zspec = lambda w, off: pl.BlockSpec((tb, w), lambda i: (blk(i), off // w))
    cspec = lambda r, c: pl.BlockSpec((r, c), lambda i: (0, 0))
    in_specs = [zspec(qk, Z_AQ), zspec(qk, Z_AK), zspec(vw, Z_AV), zspec(LANES, Z_ALR),
                cspec(LANES, qk), cspec(1, qk), cspec(tb, tb)]
    args = [z, z, z, z, wg, gb, _chunk_tri(tb, reverse)]
    scratch = [pltpu.VMEM((vw, qk), F32)]
    if fin is not None:
        o_fwd, norm_g = fin
        in_specs += [pl.BlockSpec((tb, vw), lambda i: (blk(i), 0)), zspec(vw, Z_AR), cspec(1, GLA_DV)]
        args += [o_fwd, z, norm_g.reshape(1, GLA_DV)]
        scratch.append(pltpu.VMEM((tb, vw), F32))
    return pl.pallas_call(
        functools.partial(_gla_kernel, reverse=reverse, tb=tb, finalize=fin is not None),
        grid=(nb,),
        in_specs=in_specs,
        out_specs=pl.BlockSpec((tb, vw), lambda i: (blk(i), 0)),
        out_shape=jax.ShapeDtypeStruct((s, vw), F32),
        scratch_shapes=scratch,
        compiler_params=_cparams(("arbitrary",)),
        name="gla_bwd" if reverse else "gla_fwd",
    )(*args)


def _hy_pre_kernel(u_ref, up_ref, un_ref, w_ref, b_ref, x0_ref, z_ref, zb_ref, *, tm):
    i = pl.program_id(0)
    u = u_ref[...].astype(F32)
    row = lax.broadcasted_iota(jnp.int32, u.shape, 0)
    prev_row = jnp.where(i == 0, 0.0, up_ref[...].astype(F32)[HALO_ROWS - 1:HALO_ROWS, :])
    next_row = jnp.where(i == pl.num_programs(0) - 1, 0.0, un_ref[...].astype(F32)[0:1, :])
    u_prev = jnp.where(row == 0, prev_row, pltpu.roll(u, 1, axis=0))
    u_next = jnp.where(row == tm - 1, next_row, pltpu.roll(u, tm - 1, axis=0))
    y = b_ref[...] + u_prev * w_ref[0:1] + u * w_ref[1:2] + u_next * w_ref[2:3]
    x0_ref[...] = y[:, :HY_WIDTH]
    z = y[:, HY_WIDTH:2 * HY_WIDTH] * y[:, 2 * HY_WIDTH:]
    z_ref[...] = z
    zb_ref[...] = z.astype(BF16)


def _hy_pre(z, conv_w, conv_b, tm):
    s = z.shape[0]
    w = 3 * HY_WIDTH
    nr = s // HALO_ROWS
    ospec = pl.BlockSpec((tm, HY_WIDTH), lambda i: (i, 0))
    return pl.pallas_call(
        functools.partial(_hy_pre_kernel, tm=tm),
        grid=(s // tm,),
        in_specs=[pl.BlockSpec((tm, w), lambda i: (i, 0)),
                  pl.BlockSpec((HALO_ROWS, w), lambda i: (jnp.maximum(i * (tm // HALO_ROWS) - 1, 0), 0)),
                  pl.BlockSpec((HALO_ROWS, w), lambda i: (jnp.minimum((i + 1) * (tm // HALO_ROWS), nr - 1), 0)),
                  pl.BlockSpec((3, w), lambda i: (0, 0)),
                  pl.BlockSpec((1, w), lambda i: (0, 0))],
        out_specs=[ospec, ospec, ospec],
        out_shape=[jax.ShapeDtypeStruct((s, HY_WIDTH), F32)] * 2 + [jax.ShapeDtypeStruct((s, HY_WIDTH), BF16)],
        compiler_params=_cparams(("parallel",)),
        name="hyena_pre",
    )(z, z, z, conv_w, conv_b.reshape(1, w))


def _hy_filter_kernel(emb_ref, w1_ref, b1_ref, f1_ref, w2_ref, b2_ref, f2_ref, w3_ref, dl_ref,
                      kb_ref, norm_ref, *, tl, seq):
    i = pl.program_id(0)
    half = tl // 2
    emb = emb_ref[...]
    x = jnp.concatenate([emb[:half], emb[half:]], axis=1)
    h = jnp.sin(f1_ref[...] * (_dot3(x, w1_ref[...]) + b1_ref[...]))
    h = jnp.sin(f2_ref[...] * (_dot3(h, w2_ref[...]) + b2_ref[...]))
    h = _dot3(h, w3_ref[...])
    back = i * tl >= seq
    pick = lambda y: jnp.where(back, y[:, HY_WIDTH:], y[:, :HY_WIDTH])
    h = jnp.concatenate([pick(h[:, :2 * HY_WIDTH]), pick(h[:, 2 * HY_WIDTH:])], axis=0)
    h = h * jnp.exp(-emb[:, 0:1] * dl_ref[...])
    row = i * tl + lax.broadcasted_iota(jnp.int32, h.shape, 0)
    h = jnp.where(row == seq, 0.0, h)
    kb_ref[...] = h.astype(BF16)

    @pl.when(i == 0)
    def _():
        norm_ref[...] = jnp.zeros(norm_ref.shape, F32)

    norm_ref[...] += jnp.sum(jnp.abs(h), axis=0, keepdims=True)


def _hy_positions(seq):
    t = np.linspace(0.0, 1.0, seq, dtype=np.float32)[:, None]
    bands = (HY_EMB - 1) // 2
    freqs = np.linspace(1e-4, bands - 1, bands, dtype=np.float32)[None]
    w = (np.float32(2.0 * math.pi) * np.arange(seq, dtype=np.float32)[:, None] / np.float32(seq))
    zf = np.concatenate([t, np.cos(freqs * w), -np.sin(freqs * w)], axis=-1).astype(np.float32)
    pos = np.concatenate([np.arange(seq), [0], np.arange(seq - 1, 0, -1)])
    return jnp.asarray(np.pad(zf[pos], ((0, 0), (0, LANES - HY_EMB))))


def _block_diag2(w):
    z = jnp.zeros_like(w)
    return jnp.concatenate([jnp.concatenate([w, z], axis=1), jnp.concatenate([z, w], axis=1)], axis=0)


def _hy_filter(emb, w1, b1, fr1, w2, b2, fr2, w3, tl):
    n = emb.shape[0]
    seq = n // 2
    w1p = jnp.pad(w1, ((0, LANES - HY_EMB), (0, 0)))
    twice = lambda v: jnp.concatenate([v, v])[None]
    min_decay = math.log(HY_DECAY_TARGET) / HY_SLOW_DECAY
    max_decay = math.log(HY_DECAY_TARGET) / HY_FAST_DECAY
    deltas = jnp.asarray(np.abs(np.linspace(min_decay, max_decay, HY_WIDTH, dtype=np.float32))[None])
    cs = lambda r, c: pl.BlockSpec((r, c), lambda i: (0, 0))
    return pl.pallas_call(
        functools.partial(_hy_filter_kernel, tl=tl, seq=seq),
        grid=(n // tl,),
        in_specs=[pl.BlockSpec((tl, LANES), lambda i: (i, 0)),
                  cs(2 * LANES, LANES), cs(1, LANES), cs(1, LANES),
                  cs(LANES, LANES), cs(1, LANES), cs(1, LANES),
                  cs(LANES, 4 * HY_WIDTH), cs(1, HY_WIDTH)],
        out_specs=[pl.BlockSpec((tl, HY_WIDTH), lambda i: (i, 0)), pl.BlockSpec((1, HY_WIDTH), lambda i: (0, 0))],
        out_shape=[jax.ShapeDtypeStruct((n, HY_WIDTH), BF16), jax.ShapeDtypeStruct((1, HY_WIDTH), F32)],
        compiler_params=_cparams(("arbitrary",)),
        name="hyena_filter",
    )(emb, _block_diag2(w1p), twice(b1), twice(fr1), _block_diag2(w2), twice(b2), twice(fr2),
      _block_diag2(w3), deltas)


def _dft_tables(n1):
    n = n1 * DFT_N2
    a = np.arange(n1, dtype=np.int64)
    ang1 = (2.0 * math.pi / n1) * ((a[:, None] * a[None, :]) % n1)
    c1, s1 = np.cos(ang1), np.sin(ang1)
    k1 = np.arange(n1, dtype=np.int64)[:, None, None]
    k2 = np.arange(DFT_N2, dtype=np.int64)[None, :, None]
    n2 = np.arange(DFT_N2, dtype=np.int64)[None, None, :]
    ang = (2.0 * math.pi / n) * ((n2 * (k1 + n1 * k2)) % n)
    gc, gs = np.cos(ang), np.sin(ang)
    tb = lambda x: jnp.asarray(np.ascontiguousarray(x).astype(BF16))
    gc, gs = tb(gc), tb(gs)
    gct, gst = jnp.swapaxes(gc, 1, 2), jnp.swapaxes(gs, 1, 2)
    blk = lambda a, b, c, d: jnp.concatenate([jnp.concatenate([a, b], axis=2), jnp.concatenate([c, d], axis=2)], axis=1)
    return dict(c1=tb(c1), s1=tb(s1), g2=blk(gc, gs, -gs, gc), h2=blk(gct, -gst, gst, gct))


def _dft1_kernel(c_ref, s_ref, x_ref, re_ref, im_ref):
    x = x_ref[...]
    re_ref[...] = _dot(c_ref[...], x).astype(BF16)
    im_ref[...] = (-_dot(s_ref[...], x)).astype(BF16)


def _dft1(x, c1, s1, cb):
    k1, w = x.shape
    n1 = c1.shape[0]
    ospec = pl.BlockSpec((n1, cb), lambda j: (0, j))
    return pl.pallas_call(
        _dft1_kernel,
        grid=(w // cb,),
        in_specs=[pl.BlockSpec((n1, k1), lambda j: (0, 0)), pl.BlockSpec((n1, k1), lambda j: (0, 0)),
                  pl.BlockSpec((k1, cb), lambda j: (0, j))],
        out_specs=[ospec, ospec],
        out_shape=[jax.ShapeDtypeStruct((n1, w), BF16)] * 2,
        compiler_params=_cparams(("parallel",)),
        name="dft_stage1",
    )(c1, s1, x)


def _stack(re, im):
    return jnp.concatenate([re, im], axis=0)


def _dft2_filter_kernel(g2_ref, are_ref, aim_ref, norm_ref, kre_ref, kim_ref, *, kb):
    inv = 1.0 / norm_ref[...]
    for t in range(kb):
        x = _dot(g2_ref[t], _stack(are_ref[t], aim_ref[t])) * inv
        kre_ref[t] = x[:DFT_N2].astype(BF16)
        kim_ref[t] = x[DFT_N2:].astype(BF16)


def _dft2_filter(tabs, are, aim, norm, kb):
    n1, _, c = are.shape
    gspec = pl.BlockSpec((kb, 2 * DFT_N2, 2 * DFT_N2), lambda i: (i, 0, 0))
    aspec = pl.BlockSpec((kb, DFT_N2, c), lambda i: (i, 0, 0))
    return pl.pallas_call(
        functools.partial(_dft2_filter_kernel, kb=kb),
        grid=(n1 // kb,),
        in_specs=[gspec, aspec, aspec, pl.BlockSpec((1, c), lambda i: (0, 0))],
        out_specs=[aspec, aspec],
        out_shape=[jax.ShapeDtypeStruct((n1, DFT_N2, c), BF16)] * 2,
        compiler_params=_cparams(("parallel",)),
        name="dft_stage2_filter",
    )(tabs["g2"], are, aim, norm)


def _conv_mid_kernel(g2_ref, h2_ref, are_ref, aim_ref, kre_ref, kim_ref, bre_ref, bim_ref, *, kb):
    for t in range(kb):
        x = _dot(g2_ref[t], _stack(are_ref[t], aim_ref[t]))
        xre, xim = x[:DFT_N2], x[DFT_N2:]
        kre, kim = kre_ref[t].astype(F32), kim_ref[t].astype(F32)
        yre = (xre * kre - xim * kim).astype(BF16)
        yim = (xre * kim + xim * kre).astype(BF16)
        b = _dot(h2_ref[t], _stack(yre, yim))
        bre_ref[t] = b[:DFT_N2].astype(BF16)
        bim_ref[t] = b[DFT_N2:].astype(BF16)


def _conv_mid(tabs, are, aim, kre, kim, kb):
    n1, _, c = are.shape
    gspec = pl.BlockSpec((kb, 2 * DFT_N2, 2 * DFT_N2), lambda i: (i, 0, 0))
    aspec = pl.BlockSpec((kb, DFT_N2, c), lambda i: (i, 0, 0))
    return pl.pallas_call(
        functools.partial(_conv_mid_kernel, kb=kb),
        grid=(n1 // kb,),
        in_specs=[gspec, gspec, aspec, aspec, aspec, aspec],
        out_specs=[aspec, aspec],
        out_shape=[jax.ShapeDtypeStruct((n1, DFT_N2, c), BF16)] * 2,
        compiler_params=_cparams(("parallel",)),
        name="conv_spectral",
    )(tabs["g2"], tabs["h2"], are, aim, kre, kim)


def _idft1_kernel(c_ref, s_ref, bre_ref, bim_ref, z_ref, x0_ref, skip_ref, o_ref, *, inv_n):
    y = (_dot(c_ref[...], bre_ref[...]) - _dot(s_ref[...], bim_ref[...])) * inv_n
    o_ref[...] = x0_ref[...] * (y + z_ref[...] * skip_ref[...])


def _idft1(c1h, s1h, bre, bim, z, x0, skip_t, cb):
    ko, n1 = c1h.shape
    w = bre.shape[1]
    n = n1 * DFT_N2
    ospec = pl.BlockSpec((ko, cb), lambda j: (0, j))
    return pl.pallas_call(
        functools.partial(_idft1_kernel, inv_n=1.0 / n),
        grid=(w // cb,),
        in_specs=[pl.BlockSpec((ko, n1), lambda j: (0, 0)), pl.BlockSpec((ko, n1), lambda j: (0, 0)),
                  pl.BlockSpec((n1, cb), lambda j: (0, j)), pl.BlockSpec((n1, cb), lambda j: (0, j)),
                  ospec, ospec, pl.BlockSpec((1, cb), lambda j: (0, 0))],
        out_specs=ospec,
        out_shape=jax.ShapeDtypeStruct((ko, w), F32),
        compiler_params=_cparams(("parallel",)),
        name="idft_stage1",
    )(c1h, s1h, bre, bim, z, x0, skip_t)


def _hyena(z_all, tabs, emb, p, i, cb, kb):
    seq = z_all.shape[0]
    n1 = 2 * seq // DFT_N2
    c = HY_WIDTH
    x0, zf, zb = _hy_pre(z_all, p["hy_conv_w"][i], p["hy_conv_b"][i], min(512, seq))
    kern_b, norm = _hy_filter(emb, p["hy_w1"][i], p["hy_b1"][i], p["hy_freq1"][i], p["hy_w2"][i],
                                 p["hy_b2"][i], p["hy_freq2"][i], p["hy_w3"][i], min(1024, seq))
    wide = DFT_N2 * c
    kre, kim = _dft1(kern_b.reshape(n1, wide), tabs["c1"], tabs["s1"], cb)
    kre, kim = _dft2_filter(tabs, kre.reshape(n1, DFT_N2, c), kim.reshape(n1, DFT_N2, c), norm, kb)
    are, aim = _dft1(zb.reshape(n1 // 2, wide), tabs["c1"][:, :n1 // 2], tabs["s1"][:, :n1 // 2], cb)
    bre, bim = _conv_mid(tabs, are.reshape(n1, DFT_N2, c), aim.reshape(n1, DFT_N2, c), kre, kim, kb)
    cbi = cb // 2
    skip_t = jnp.tile(p["hy_skip"][i][None], (1, cbi // c))
    yd = _idft1(tabs["c1"][:n1 // 2], tabs["s1"][:n1 // 2], bre.reshape(n1, wide), bim.reshape(n1, wide),
                zf.reshape(n1 // 2, wide), x0.reshape(n1 // 2, wide), skip_t, cbi)
    return yd.reshape(seq, c)


def _combine_kernel(x_ref, ya_ref, o0_ref, o1_ref, o2_ref, l0_ref, l1_ref, l2_ref, yc_ref, yd_ref,
                    g0_ref, g1_ref, g2_ref, g3_ref, pa_ref, pb_ref, pc_ref, pd_ref, wo_ref, o_ref):
    l0, l1, l2 = l0_ref[...], l1_ref[...], l2_ref[...]
    mx = jnp.maximum(jnp.maximum(l0, l1), l2)
    e0, e1, e2 = jnp.exp(l0 - mx), jnp.exp(l1 - mx), jnp.exp(l2 - mx)
    yb = (e0 * o0_ref[...] + e1 * o1_ref[...] + e2 * o2_ref[...]) / (e0 + e1 + e2)
    gate = lambda ref: _sigmoid(ref[...].astype(F32))
    m = (gate(g0_ref) * _dot(ya_ref[...].astype(BF16), pa_ref[...])
         + gate(g1_ref) * _dot(yb.astype(BF16), pb_ref[...])
         + gate(g2_ref) * _dot(yc_ref[...].astype(BF16), pc_ref[...])
         + gate(g3_ref) * _dot(yd_ref[...].astype(BF16), pd_ref[...]))
    o_ref[...] = x_ref[...] + _dot(m.astype(BF16), wo_ref[...])


def _combine(x, z, ya, dil_outs, yc, yd, pa, pb, pc, pd, wo, tm):
    s, d = x.shape
    rs = lambda w: pl.BlockSpec((tm, w), lambda i: (i, 0))
    gs = lambda b: pl.BlockSpec((tm, d), lambda i: (i, Z_GATE // d + b))
    ws = lambda a: pl.BlockSpec(a.shape, lambda i: (0, 0))
    (o0, l0), (o1, l1), (o2, l2) = dil_outs
    gw = DIL_HEADS * HEAD_DIM
    return pl.pallas_call(
        _combine_kernel,
        grid=(s // tm,),
        in_specs=[rs(d), rs(ya.shape[1]), rs(gw), rs(gw), rs(gw), rs(gw), rs(gw), rs(gw),
                  rs(yc.shape[1]), rs(yd.shape[1]), gs(0), gs(1), gs(2), gs(3),
                  ws(pa), ws(pb), ws(pc), ws(pd), ws(wo)],
        out_specs=rs(d),
        out_shape=jax.ShapeDtypeStruct((s, d), F32),
        compiler_params=_cparams(("parallel",)),
        name="combine",
    )(x, ya, o0, o1, o2, l0, l1, l2, yc, yd, z, z, z, z, pa, pb, pc, pd, wo)


def _mlp_kernel(x_ref, g_ref, w1_ref, w2_ref, o_ref, h_ref, acc_ref):
    j = pl.program_id(1)

    @pl.when(j == 0)
    def _():
        h_ref[...] = (_rms(x_ref[...]) * g_ref[...]).astype(BF16)
        acc_ref[...] = jnp.zeros(acc_ref.shape, F32)

    a = jnp.maximum(_dot(h_ref[...], w1_ref[...]), 0.0)
    acc_ref[...] += _dot((a * a).astype(BF16), w2_ref[...])

    @pl.when(j == pl.num_programs(1) - 1)
    def _():
        o_ref[...] = x_ref[...] + acc_ref[...]


def _mlp(x, g, w1, w2, tm, tf):
    s, d = x.shape
    ff = w1.shape[1]
    return pl.pallas_call(
        _mlp_kernel,
        grid=(s // tm, ff // tf),
        in_specs=[pl.BlockSpec((tm, d), lambda i, j: (i, 0)),
                  pl.BlockSpec((1, d), lambda i, j: (0, 0)),
                  pl.BlockSpec((d, tf), lambda i, j: (0, j)),
                  pl.BlockSpec((tf, d), lambda i, j: (j, 0))],
        out_specs=pl.BlockSpec((tm, d), lambda i, j: (i, 0)),
        out_shape=jax.ShapeDtypeStruct((s, d), F32),
        scratch_shapes=[pltpu.VMEM((tm, d), BF16), pltpu.VMEM((tm, d), F32)],
        compiler_params=_cparams(("parallel", "arbitrary")),
        name="mlp",
    )(x, g.reshape(1, d), w1, w2)


def _permute_w_in(w):
    d = w.shape[0]
    pieces = [w[:, _O_DU:_O_GATE], w[:, _O_B:_O_C], w[:, _O_AQ:_O_AK], w[:, _O_C:_O_DU],
              w[:, _O_AV:_O_AR], w[:, _O_AR:_O_ALR], w[:, _O_AK:_O_AV], w[:, _O_ALR:_O_B],
              jnp.zeros((d, Z_GATE - Z_ALR - 2 * GLA_RANK), w.dtype), w[:, _O_GATE:_O_END]]
    return jnp.concatenate(pieces, axis=1).astype(BF16)


def _dil_bias_idx(tq, dil):
    rel = _skewed_rel(tq, tq + 2 * DIL_HALF, -1, -DIL_HALF)
    return _t5_bucket(rel * dil)[None].astype(np.int32)


def _diff_bias_idx(tq, tk):
    far_lo, far_hi = _diff_far_offsets(tq, tk)
    return np.stack([_t5_bucket(_skewed_rel(tk, tq, 1, o * tk)) for o in range(far_lo, far_hi + 1)]).astype(np.int32)


def _forward(x, p, *, t_diff, tk_diff, qc_diff, tq_dil, nt_dil, tb_gla, tm_proj, tn_proj, tm_row, tm_mlp, tf_mlp, cb_dft, kb_dft):
    seq = x.shape[0]
    depth = p["w_in"].shape[0]
    n_dil_bias = DIL_GROUPS * DIL_HEADS
    tq_dil = [min(tq, seq // dil) for tq, (_, dil) in zip(tq_dil, DIL_PATTERNS)]
    dil_bias = [_bias_tiles(p["t5_bias"], _dil_bias_idx(tq, dil), tq, tq + 2 * DIL_HALF, DIL_HEADS, gi * DIL_HEADS)
                for gi, (tq, (_, dil)) in enumerate(zip(tq_dil, DIL_PATTERNS))]
    t_diff, tk_diff = min(t_diff, seq), min(tk_diff, seq)
    diff_bias = _bias_tiles(p["t5_bias"], _diff_bias_idx(t_diff, tk_diff), tk_diff, t_diff, DIFF_HEADS,
                            n_dil_bias, LOG2E)
    tabs = _dft_tables(2 * seq // DFT_N2)
    emb = _hy_positions(seq)
    rep = lambda g, n: jnp.tile(g, n)
    for i in range(depth):
        z = _norm_matmul(x, p["norm1_g"][i], _permute_w_in(p["w_in"][i]), tm_proj, tn_proj)
        wg = [jnp.zeros((LANES, GLA_HEADS * GLA_DK), F32).at[j * GLA_RANK:(j + 1) * GLA_RANK].set(
            p["gla_gate_w"][i, j]).astype(BF16) for j in range(2)]
        gb = p["gla_gate_b"][i]
        o_fwd = _gla_scan(z, wg[0], gb[0:1], tb_gla, False)
        ya = _gla_scan(z, wg[1], gb[1:2], tb_gla, True, fin=(o_fwd, p["gla_norm_g"][i]))
        bq, bk, bv, cq, ck, cv = _prep(
            z, jnp.repeat(p["dil_qnorm_g"][i], DIL_HEADS, axis=0).reshape(-1),
            jnp.repeat(p["dil_knorm_g"][i], DIL_HEADS, axis=0).reshape(-1),
            rep(p["diff_qnorm_g"][i], 2 * DIFF_HEADS), rep(p["diff_knorm_g"][i], 2 * DIFF_HEADS), tm_row)
        dil_outs = [_dil_attn(bq, bk, bv, dil_bias[gi], gi, dil, tq_dil[gi], max(1, min(nt_dil[gi], seq // (tq_dil[gi] * dil))))
                    for gi, (_, dil) in enumerate(DIL_PATTERNS)]
        lam_init = 0.8 - 0.6 * math.exp(-0.3 * i)
        yc = _diff_attn(cq, ck, cv, diff_bias, p["diff_lambda"][i], p["diff_subln_g"][i], lam_init,
                        t_diff, tk_diff, qc_diff)
        yd = _hyena(z, tabs, emb, p, i, cb_dft, kb_dft)
        x = _combine(x, z, ya, dil_outs, yc, yd, p["proj_a"][i].astype(BF16), p["proj_b"][i].astype(BF16),
                     p["proj_c"][i].astype(BF16), p["proj_d"][i].astype(BF16), p["w_out"][i].astype(BF16),
                     tm_row)
        x = _mlp(x, p["norm2_g"][i], p["mlp_w1"][i].astype(BF16), p["mlp_w2"][i].astype(BF16), tm_mlp, tf_mlp)
    return x


def kernel(x, t5_bias, norm1_g, w_in, gla_gate_w, gla_gate_b, gla_norm_g, dil_qnorm_g, dil_knorm_g,
           diff_qnorm_g, diff_knorm_g, diff_lambda, diff_subln_g, hy_conv_w, hy_conv_b, hy_w1, hy_b1,
           hy_freq1, hy_w2, hy_b2, hy_freq2, hy_w3, hy_skip, proj_a, proj_b, proj_c, proj_d, w_out,
           norm2_g, mlp_w1, mlp_w2):
    p = dict(t5_bias=t5_bias, norm1_g=norm1_g, w_in=w_in, gla_gate_w=gla_gate_w, gla_gate_b=gla_gate_b,
             gla_norm_g=gla_norm_g, dil_qnorm_g=dil_qnorm_g, dil_knorm_g=dil_knorm_g,
             diff_qnorm_g=diff_qnorm_g, diff_knorm_g=diff_knorm_g, diff_lambda=diff_lambda,
             diff_subln_g=diff_subln_g, hy_conv_w=hy_conv_w, hy_conv_b=hy_conv_b, hy_w1=hy_w1, hy_b1=hy_b1,
             hy_freq1=hy_freq1, hy_w2=hy_w2, hy_b2=hy_b2, hy_freq2=hy_freq2, hy_w3=hy_w3, hy_skip=hy_skip,
             proj_a=proj_a, proj_b=proj_b, proj_c=proj_c, proj_d=proj_d, w_out=w_out, norm2_g=norm2_g,
             mlp_w1=mlp_w1, mlp_w2=mlp_w2)
    b, s, d = x.shape
    outs = [_forward(x[bi], p, t_diff=2048, tk_diff=1024, qc_diff=(256, 768, 768, 256), tq_dil=(256, 256, 128), nt_dil=(4, 1, 1), tb_gla=512, tm_proj=2048, tn_proj=512, tm_row=512,
                     tm_mlp=1024, tf_mlp=1024, cb_dft=8192, kb_dft=16) for bi in range(b)]
    return jnp.stack(outs)
```

```python
import functools
import math

import jax
import jax.numpy as jnp
import numpy as np
from jax import lax
from jax.experimental import pallas as pl
from jax.experimental.pallas import tpu as pltpu

F32 = jnp.float32
BF16 = jnp.bfloat16

D_MODEL = 1024
HEAD_DIM = 64
GLA_HEADS = 4
GLA_DK = 64
GLA_DV = 128
GLA_RANK = 16
GLA_TAU = 16.0
GLA_CHUNK = 64
DIL_PATTERNS = ((128, 1), (512, 4), (2048, 16))
DIL_GROUPS = 3
DIL_HEADS = 4
DIL_HALF = 64
DIFF_HEADS = 4
DIFF_DV = 128
DIFF_VT = DIFF_DV + 16
HY_WIDTH = 512
HY_EMB = 33
HY_FFN = 64
HY_DECAY_TARGET = 1e-2
HY_FAST_DECAY = 0.3
HY_SLOW_DECAY = 1.5
T5_BUCKETS = 32
T5_MAX_DIST = 1024
N_BIAS_HEADS = 16
D_FF = 4096
RMS_EPS = 1e-6
LOG2E = math.log2(math.e)

LANES = 128
HALO_ROWS = 16
VMEM_LIMIT = 48 * 1024 * 1024

Z_DU = 0
Z_BQ, Z_BK, Z_BV = 1536, 2304, 3072
Z_AQ = 3840
Z_CQ, Z_CK, Z_CV = 4096, 4608, 5120
Z_AV, Z_AR = 5632, 6144
Z_AK = 6656
Z_ALR = 6912
Z_GATE = 7168
Z_COLS = 11264

_O_AQ, _O_AK, _O_AV, _O_AR, _O_ALR, _O_B, _O_C, _O_DU, _O_GATE, _O_END = (
    0, 256, 512, 1024, 1536, 1568, 3872, 5408, 6944, 11040)

DFT_N2 = 128


def _cparams(sem):
    return pltpu.CompilerParams(dimension_semantics=sem, vmem_limit_bytes=VMEM_LIMIT)


def _dot(a, b):
    return jnp.dot(a, b, preferred_element_type=F32)


def _dot_nt(a, b):
    return lax.dot_general(a, b, (((1,), (1,)), ((), ())), preferred_element_type=F32)


def _dot_tn(a, b):
    return lax.dot_general(a, b, (((0,), (0,)), ((), ())), preferred_element_type=F32)


def _split(x):
    hi = x.astype(BF16)
    lo = (x - hi.astype(F32)).astype(BF16)
    return hi, lo


def _dot3(a, b):
    ah, al = _split(a)
    bh, bl = _split(b)
    return _dot(ah, bh) + _dot(ah, bl) + _dot(al, bh)


def _rms(x):
    return x * lax.rsqrt(jnp.mean(x * x, axis=-1, keepdims=True) + RMS_EPS)


def _sigmoid(x):
    return 0.5 * jnp.tanh(0.5 * x) + 0.5


def _norm_matmul_kernel(x_ref, g_ref, w_ref, o_ref, h_ref):
    @pl.when(pl.program_id(1) == 0)
    def _():
        h_ref[...] = (_rms(x_ref[...]) * g_ref[...]).astype(BF16)

    o_ref[...] = _dot(h_ref[...], w_ref[...]).astype(o_ref.dtype)


def _norm_matmul(x, g, w, tm, tn):
    s, d = x.shape
    n = w.shape[1]
    return pl.pallas_call(
        _norm_matmul_kernel,
        grid=(s // tm, n // tn),
        in_specs=[pl.BlockSpec((tm, d), lambda i, j: (i, 0)),
                  pl.BlockSpec((1, d), lambda i, j: (0, 0)),
                  pl.BlockSpec((d, tn), lambda i, j: (0, j))],
        out_specs=pl.BlockSpec((tm, tn), lambda i, j: (i, j)),
        out_shape=jax.ShapeDtypeStruct((s, n), BF16),
        scratch_shapes=[pltpu.VMEM((tm, d), BF16)],
        compiler_params=_cparams(("parallel", "arbitrary")),
        name="in_proj",
    )(x, g.reshape(1, d), w)


def _group_norm(x, e, gain):
    hi, lo = _split(x * x)
    ew = e.shape[0]
    ms = jnp.concatenate([_dot(hi[:, c:c + ew], e) + _dot(lo[:, c:c + ew], e)
                          for c in range(0, x.shape[1], ew)], axis=1) * (1.0 / HEAD_DIM)
    return x * lax.rsqrt(ms + RMS_EPS) * gain


def _prep_kernel(bq_ref, bk_ref, bv_ref, cq_ref, ck_ref, cv_ref, e_ref,
                 gbq_ref, gbk_ref, gcq_ref, gck_ref,
                 obq_ref, obk_ref, obv_ref, ocq_ref, ock_ref, ocv_ref):
    scale = HEAD_DIM ** -0.5
    eb = ec = e_ref[...]
    f32 = lambda ref: ref[...].astype(F32)
    obq_ref[...] = (_group_norm(f32(bq_ref), eb, gbq_ref[...]) * scale).astype(BF16)
    obk_ref[...] = _group_norm(f32(bk_ref), eb, gbk_ref[...]).astype(BF16)
    obv_ref[...] = bv_ref[...]
    ocq_ref[...] = (_group_norm(f32(cq_ref), ec, gcq_ref[...]) * (scale * LOG2E)).T.astype(BF16)
    ock_ref[...] = _group_norm(f32(ck_ref), ec, gck_ref[...]).astype(BF16)
    cvt = f32(cv_ref).T
    ones = jnp.ones((DIFF_VT - DIFF_DV, cvt.shape[1]), F32)
    ocv_ref[...] = jnp.concatenate(
        [t for h in range(DIFF_HEADS) for t in (cvt[h * DIFF_DV:(h + 1) * DIFF_DV], ones)], axis=0).astype(BF16)


def _block_diag_ones(width):
    idx = np.arange(width) // HEAD_DIM
    return jnp.asarray(idx[:, None] == idx[None, :], dtype=BF16)


def _prep(z, gbq, gbk, gcq, gck, tm):
    s = z.shape[0]
    wb, wc = DIL_GROUPS * DIL_HEADS * HEAD_DIM, DIFF_HEADS * 2 * HEAD_DIM
    zspec = lambda w, off: pl.BlockSpec((tm, w), lambda i: (i, off // w))
    cspec = lambda r, c: pl.BlockSpec((r, c), lambda i: (0, 0))
    ospec = lambda w: pl.BlockSpec((tm, w), lambda i: (i, 0))
    tspec = lambda w: pl.BlockSpec((w, tm), lambda i: (0, i))
    return pl.pallas_call(
        _prep_kernel,
        grid=(s // tm,),
        in_specs=[zspec(wb, Z_BQ), zspec(wb, Z_BK), zspec(wb, Z_BV),
                  zspec(wc, Z_CQ), zspec(wc, Z_CK), zspec(wc, Z_CV),
                  cspec(2 * LANES, 2 * LANES),
                  cspec(1, wb), cspec(1, wb), cspec(1, wc), cspec(1, wc)],
        out_specs=[ospec(wb), ospec(wb), ospec(wb), tspec(wc), ospec(wc), tspec(DIFF_HEADS * DIFF_VT)],
        out_shape=[jax.ShapeDtypeStruct((s, wb), BF16)] * 3
        + [jax.ShapeDtypeStruct((wc, s), BF16), jax.ShapeDtypeStruct((s, wc), BF16),
           jax.ShapeDtypeStruct((DIFF_HEADS * DIFF_VT, s), BF16)],
        compiler_params=_cparams(("parallel",)),
        name="qk_prep",
    )(z, z, z, z, z, z, _block_diag_ones(2 * LANES),
      gbq.reshape(1, wb), gbk.reshape(1, wb), gcq.reshape(1, wc), gck.reshape(1, wc))


def _t5_bucket(rel):
    half = T5_BUCKETS // 2
    max_exact = half // 2
    ret = np.where(rel > 0, half, 0)
    n = np.abs(rel)
    nf = np.maximum(n, 1).astype(np.float64)
    large = max_exact + (np.log(nf / max_exact) / math.log(T5_MAX_DIST / max_exact)
                         * (half - max_exact)).astype(np.int64)
    large = np.minimum(large, half - 1)
    return ret + np.where(n < max_exact, n, large)


def _bias_kernel(tab_ref, rng_ref, idx_ref, o_ref, p_ref, *, head_base, out_scale, chunk):
    t = pl.program_id(0)
    col = head_base + pl.program_id(1)
    rows, width = o_ref.shape[2:]
    slabs = width // LANES
    lo, hi = rng_ref[t, 0], rng_ref[t, 1]

    def one_chunk(ci, carry):
        rs = pl.ds(pl.multiple_of(ci * chunk, chunk), chunk)
        idx = idx_ref[0, rs, :]
        p_ref[rs, :] = lax.fori_loop(
            lo, hi + 1, lambda b, acc: jnp.where(idx == b, tab_ref[b, col] * out_scale, acc),
            jnp.zeros((chunk, LANES), F32))
        return carry

    lax.fori_loop(0, p_ref.shape[0] // chunk, one_chunk, 0)
    for c in range(slabs):
        off = LANES * (slabs - 1 - c)
        o_ref[0, 0, :, c * LANES:(c + 1) * LANES] = p_ref[off:off + rows, :]


def _skewed_rel(rows, width, sign, base):
    slabs = width // LANES
    rho = np.arange(rows + LANES * (slabs - 1))[:, None]
    lane = np.arange(LANES)[None, :]
    return sign * (rho - LANES * (slabs - 1) - lane) + base


def _bias_tiles(t5_bias, idx, rows, width, n_heads, head_base, out_scale=1.0, chunk=64):
    nt, rp, _ = idx.shape
    ranges = np.stack([idx.reshape(nt, -1).min(axis=1), idx.reshape(nt, -1).max(axis=1)], axis=1).astype(np.int32)
    return pl.pallas_call(
        functools.partial(_bias_kernel, head_base=head_base, out_scale=out_scale, chunk=chunk),
        grid=(nt, n_heads),
        in_specs=[pl.BlockSpec(memory_space=pltpu.SMEM), pl.BlockSpec(memory_space=pltpu.SMEM),
                  pl.BlockSpec((1, rp, LANES), lambda t, h: (t, 0, 0))],
        out_specs=pl.BlockSpec((1, 1, rows, width), lambda t, h: (t, h, 0, 0)),
        out_shape=jax.ShapeDtypeStruct((nt, n_heads, rows, width), F32),
        scratch_shapes=[pltpu.VMEM((rp, LANES), F32)],
        compiler_params=_cparams(("parallel", "parallel")),
        name="t5_bias_tiles",
    )(t5_bias, jnp.asarray(ranges), jnp.asarray(idx))


def _diff_attn_kernel(qt_ref, k_ref, vt_ref, bias_ref, lam_ref, g_ref, o_ref,
                      qa_ref, qb_ref, m_ref, acc_ref, s_ref, *, lam_init, qc, far_lo, far_hi):
    j = pl.program_id(2)
    t = qt_ref.shape[1]

    @pl.when(j == 0)
    def _():
        qt = qt_ref[...]
        row = lax.broadcasted_iota(jnp.int32, qt.shape, 0)
        qa_ref[...] = jnp.where(row < HEAD_DIM, qt, jnp.zeros_like(qt))
        qb_ref[...] = jnp.where(row >= HEAD_DIM, qt, jnp.zeros_like(qt))
        m_ref[...] = jnp.full(m_ref.shape, -jnp.inf, F32)
        acc_ref[...] = jnp.zeros(acc_ref.shape, F32)

    starts = [sum(qc[:u]) for u in range(len(qc))]
    cuts = [slice(a, a + w) for a, w in zip(starts, qc)]
    chains = [(0, cs) for cs in cuts] + [(1, cs) for cs in reversed(cuts)]
    q_refs = (qa_ref, qb_ref)

    def step(far):
        k = k_ref[...]
        vt = vt_ref[...]
        if far:
            const = bias_ref[0, 0, 0:1, 0:1]
            scores = lambda c, cols: _dot(k, q_refs[c][:, cols])
        else:
            const = 0.0
            scores = lambda c, cols: _dot(k, q_refs[c][:, cols]) + bias_ref[0, 0, :, cols]
        width = lambda n: chains[n][1].stop - chains[n][1].start
        s_ref[0, :, :width(0)] = scores(*chains[0])
        for n, (c, cols) in enumerate(chains):
            if n + 1 < len(chains):
                s_ref[(n + 1) % 2, :, :width(n + 1)] = scores(*chains[n + 1])
            s = s_ref[n % 2, :, :width(n)]
            m_old = m_ref[c, :, cols]
            m_new = jnp.maximum(m_old, jnp.max(s, axis=0, keepdims=True) + const)
            alpha = jnp.exp2(m_old - m_new)
            p = jnp.exp2(s - (m_new - const)).astype(BF16)
            acc_ref[c, :, cols] = alpha * acc_ref[c, :, cols] + _dot(vt, p)
            m_ref[c, :, cols] = m_new

    off = j - pl.program_id(1) * (t // k_ref.shape[0])
    is_far = jnp.logical_or(off <= far_lo, off >= far_hi)
    pl.when(is_far)(lambda: step(True))
    pl.when(jnp.logical_not(is_far))(lambda: step(False))

    @pl.when(j == pl.num_programs(2) - 1)
    def _():
        lp = lam_ref[...]
        lam = (jnp.exp(jnp.sum(lp[0:1] * lp[1:2], axis=-1, keepdims=True))
               - jnp.exp(jnp.sum(lp[2:3] * lp[3:4], axis=-1, keepdims=True)) + lam_init)
        a0 = acc_ref[0]
        a1 = acc_ref[1]
        o0 = a0[:DIFF_DV] / a0[DIFF_DV:DIFF_DV + 1]
        o1 = a1[:DIFF_DV] / a1[DIFF_DV:DIFF_DV + 1]
        att = o0 - lam * o1
        y = att * lax.rsqrt(jnp.mean(att * att, axis=0, keepdims=True) + RMS_EPS)
        o_ref[...] = y.T * g_ref[...] * (1.0 - lam_init)


def _diff_far_offsets(tq, tk):
    far_lo = (-T5_MAX_DIST - tk + 1) // tk
    far_hi = -(-(T5_MAX_DIST + tq - 1) // tk)
    return far_lo, far_hi


def _diff_attn(cqt, ck, cvt, bias_tiles, lam_p, subln_g, lam_init, t, tk, qc):
    s = ck.shape[0]
    far_lo, far_hi = _diff_far_offsets(t, tk)
    ratio = t // tk
    w = 2 * HEAD_DIM
    return pl.pallas_call(
        functools.partial(_diff_attn_kernel, lam_init=lam_init, qc=qc, far_lo=far_lo, far_hi=far_hi),
        grid=(DIFF_HEADS, s // t, s // tk),
        in_specs=[pl.BlockSpec((w, t), lambda h, i, j: (h, i)),
                  pl.BlockSpec((tk, w), lambda h, i, j: (j, h)),
                  pl.BlockSpec((DIFF_VT, tk), lambda h, i, j: (h, j)),
                  pl.BlockSpec((1, 1, tk, t),
                               lambda h, i, j: (jnp.clip(j - i * ratio, far_lo, far_hi) - far_lo, h, 0, 0)),
                  pl.BlockSpec((4, HEAD_DIM), lambda h, i, j: (0, 0)),
                  pl.BlockSpec((1, DIFF_DV), lambda h, i, j: (0, 0))],
        out_specs=pl.BlockSpec((t, DIFF_DV), lambda h, i, j: (i, h)),
        out_shape=jax.ShapeDtypeStruct((s, DIFF_HEADS * DIFF_DV), F32),
        scratch_shapes=[pltpu.VMEM((w, t), BF16), pltpu.VMEM((w, t), BF16),
                        pltpu.VMEM((2, 1, t), F32), pltpu.VMEM((2, DIFF_VT, t), F32),
                        pltpu.VMEM((2, tk, max(qc)), F32)],
        compiler_params=_cparams(("parallel", "parallel", "arbitrary")),
        name="diff_attn",
    )(cqt, ck, cvt, bias_tiles, lam_p, subln_g.reshape(1, DIFF_DV))


def _dil_kernel(q_ref, kp_ref, kc_ref, kn_ref, vp_ref, vc_ref, vn_ref, bias_ref, o_ref, lse_ref,
                qs_ref, ks_ref, vs_ref, os_ref, ls_ref, *, tq, dil, nt, m_len):
    n = pl.program_id(0)
    hd = DIL_HALF * dil
    body_rows = tq * dil * nt
    halves = DIL_HEADS * HEAD_DIM // LANES

    def put(dst, rows, src_ref):
        x = src_ref[...].astype(F32)
        for t in range(halves):
            dst[t, rows, :] = x[:, t * LANES:(t + 1) * LANES]

    def strided(src, r, count):
        return jnp.concatenate([src[t, pl.ds(r, count, stride=dil), :] for t in range(halves)], axis=1)

    put(qs_ref, slice(0, body_rows), q_ref)
    for dst, (p_ref, c_ref, n_ref) in ((ks_ref, (kp_ref, kc_ref, kn_ref)), (vs_ref, (vp_ref, vc_ref, vn_ref))):
        put(dst, slice(0, hd), p_ref)
        put(dst, slice(hd, hd + body_rows), c_ref)
        put(dst, slice(hd + body_rows, 2 * hd + body_rows), n_ref)
    tk = tq + 2 * DIL_HALF
    a = lax.broadcasted_iota(jnp.int32, (tq, tk), 0)
    c = lax.broadcasted_iota(jnp.int32, (tq, tk), 1)
    delta = c - DIL_HALF - a
    in_band = jnp.concatenate([jnp.abs(delta) <= DIL_HALF] * DIL_HEADS, axis=0)
    col = lax.broadcasted_iota(jnp.int32, (1, tk), 1)
    lane = lax.broadcasted_iota(jnp.int32, (tq, DIL_HEADS * HEAD_DIM), 1)
    head_masks = [(lane // HEAD_DIM) == h for h in range(DIL_HEADS)]

    def one_tile(it, carry):
        u = it // dil
        r = it % dil + u * (tq * dil)
        kpos = (n * nt + u) * tq - DIL_HALF + col
        valid = jnp.logical_and(in_band, jnp.logical_and(kpos >= 0, kpos < m_len))
        q = strided(qs_ref, r, tq).astype(BF16)
        k = strided(ks_ref, r, tk).astype(BF16)
        v = strided(vs_ref, r, tk).astype(BF16)
        qs = jnp.concatenate([jnp.where(hm, q, jnp.zeros_like(q)) for hm in head_masks], axis=0)
        s = _dot_nt(qs, k) + bias_ref[0].reshape(DIL_HEADS * tq, tk)
        s = jnp.where(valid, s, -1e30)
        m = jnp.max(s, axis=-1, keepdims=True)
        e = jnp.exp(s - m)
        l = jnp.sum(e, axis=-1, keepdims=True)
        oh = _dot((e / l).astype(BF16), v)
        lse = m + jnp.log(l)
        o = jnp.zeros(q.shape, F32)
        lse_o = jnp.zeros(q.shape, F32)
        for h, hm in enumerate(head_masks):
            rows = slice(h * tq, (h + 1) * tq)
            o = jnp.where(hm, oh[rows], o)
            lse_o = jnp.where(hm, lse[rows], lse_o)
        for t in range(halves):
            os_ref[t, pl.ds(r, tq, stride=dil), :] = o[:, t * LANES:(t + 1) * LANES]
            ls_ref[t, pl.ds(r, tq, stride=dil), :] = lse_o[:, t * LANES:(t + 1) * LANES]
        return carry

    lax.fori_loop(0, dil * nt, one_tile, 0, unroll=min(dil * nt, 4))
    o_ref[...] = jnp.concatenate([os_ref[t] for t in range(halves)], axis=1)
    lse_ref[...] = jnp.concatenate([ls_ref[t] for t in range(halves)], axis=1)


def _dil_attn(bq, bk, bv, bias, gi, dil, tq, nt):
    s = bq.shape[0]
    m_len = s // dil
    gw = DIL_HEADS * HEAD_DIM
    rows = tq * dil * nt
    hd = DIL_HALF * dil
    hb = rows // hd
    last = s // hd - 1
    qspec = pl.BlockSpec((rows, gw), lambda n: (n, gi))
    pspec = pl.BlockSpec((hd, gw), lambda n: (jnp.maximum(n * hb - 1, 0), gi))
    nspec = pl.BlockSpec((hd, gw), lambda n: (jnp.minimum((n + 1) * hb, last), gi))
    ospec = pl.BlockSpec((rows, gw), lambda n: (n, 0))
    return pl.pallas_call(
        functools.partial(_dil_kernel, tq=tq, dil=dil, nt=nt, m_len=m_len),
        grid=(s // rows,),
        in_specs=[qspec, pspec, qspec, nspec, pspec, qspec, nspec,
                  pl.BlockSpec((1, DIL_HEADS, tq, tq + 2 * DIL_HALF), lambda n: (0, 0, 0, 0))],
        out_specs=[ospec, ospec],
        out_shape=[jax.ShapeDtypeStruct((s, gw), F32)] * 2,
        scratch_shapes=[pltpu.VMEM((gw // LANES, rows, LANES), F32),
                        pltpu.VMEM((gw // LANES, rows + 2 * hd, LANES), F32),
                        pltpu.VMEM((gw // LANES, rows + 2 * hd, LANES), F32),
                        pltpu.VMEM((gw // LANES, rows, LANES), F32),
                        pltpu.VMEM((gw // LANES, rows, LANES), F32)],
        compiler_params=_cparams(("parallel",)),
        name=f"dil_attn_g{gi}",
    )(bq, bk, bk, bk, bv, bv, bv, bias)


def _gla_kernel(*refs, reverse, tb, finalize):
    if finalize:
        (q_ref, k_ref, v_ref, lr_ref, wg_ref, gb_ref, tri_ref, ofwd_ref, r_ref, ng_ref,
         o_ref, s_ref, oacc_ref) = refs
    else:
        q_ref, k_ref, v_ref, lr_ref, wg_ref, gb_ref, tri_ref, o_ref, s_ref = refs
        oacc_ref = o_ref
    cw = GLA_CHUNK
    qk = GLA_HEADS * GLA_DK
    vw = GLA_HEADS * GLA_DV

    @pl.when(pl.program_id(0) == 0)
    def _():
        s_ref[...] = jnp.zeros(s_ref.shape, F32)

    logits = _dot(lr_ref[...], wg_ref[...]) + gb_ref[...]
    g = (jnp.minimum(logits, 0.0) - jnp.log(1.0 + jnp.exp(-jnp.abs(logits)))) * (1.0 / GLA_TAU)
    ghi, glo = _split(g)
    tri = tri_ref[...]
    b = _dot(tri, ghi) + _dot(tri, glo)
    qg = (q_ref[...].astype(F32) * (GLA_DK ** -0.5) * jnp.exp(b)).astype(BF16)
    k = k_ref[...].astype(F32)
    kg = (k * jnp.exp(-b)).astype(BF16)
    v = v_ref[...].astype(BF16)

    lane_q = lax.broadcasted_iota(jnp.int32, (cw, qk), 1)
    rr = lax.broadcasted_iota(jnp.int32, (GLA_HEADS * cw, cw), 0)
    cc = lax.broadcasted_iota(jnp.int32, (GLA_HEADS * cw, cw), 1)
    tt = rr % cw
    amask = (cc > tt) if reverse else (cc <= tt)
    srow = lax.broadcasted_iota(jnp.int32, (vw, qk), 0) // GLA_DV
    scol = lax.broadcasted_iota(jnp.int32, (vw, qk), 1) // GLA_DK
    bdmask = srow == scol

    n_chunks = tb // cw
    order = range(n_chunks - 1, -1, -1) if reverse else range(n_chunks)
    for ci in order:
        rows = slice(ci * cw, (ci + 1) * cw)
        bc = b[rows]
        b_end = bc[0:1] if reverse else bc[cw - 1:cw]
        kdec = (k[rows] * jnp.exp(b_end - bc)).astype(BF16)
        qg_c = qg[rows]
        qs = jnp.concatenate(
            [jnp.where((lane_q // GLA_DK) == h, qg_c, jnp.zeros_like(qg_c)) for h in range(GLA_HEADS)], axis=0)
        a = jnp.where(amask, _dot_nt(qs, kg[rows]), 0.0)
        obig = _dot(a.astype(BF16), v[rows])
        o_intra = jnp.concatenate(
            [obig[h * cw:(h + 1) * cw, h * GLA_DV:(h + 1) * GLA_DV] for h in range(GLA_HEADS)], axis=1)
        state = s_ref[...]
        oacc_ref[rows, :] = o_intra + _dot_nt(qg_c, state.astype(BF16))
        ds = _dot_tn(v[rows], kdec)
        s_ref[...] = jnp.where(bdmask, jnp.exp(b_end) * state + ds, 0.0)

    if finalize:
        o = ofwd_ref[...] + oacc_ref[...]
        r = r_ref[...].astype(F32)
        outs = []
        for h in range(GLA_HEADS):
            sl = slice(h * GLA_DV, (h + 1) * GLA_DV)
            outs.append(_rms(o[:, sl]) * ng_ref[...] * (r[:, sl] * _sigmoid(r[:, sl])))
        o_ref[...] = jnp.concatenate(outs, axis=1)


def _chunk_tri(tb, reverse):
    i = np.arange(tb)
    same = (i[:, None] // GLA_CHUNK) == (i[None, :] // GLA_CHUNK)
    tri = (i[None, :] >= i[:, None]) if reverse else (i[None, :] <= i[:, None])
    return jnp.asarray(same & tri, dtype=BF16)


def _gla_scan(z, wg, gb, tb, reverse, fin=None):
    s = z.shape[0]
    nb = s // tb
    qk = GLA_HEADS * GLA_DK
    vw = GLA_HEADS * GLA_DV
    blk = (lambda i: nb - 1 - i) if reverse else (lambda i: i)
    zspec = lambda w, off: pl.BlockSpec((tb, w), lambda i: (blk(i), off // w))
    cspec = lambda r, c: pl.BlockSpec((r, c), lambda i: (0, 0))
    in_specs = [zspec(qk, Z_AQ), zspec(qk, Z_AK), zspec(vw, Z_AV), zspec(LANES, Z_ALR),
                cspec(LANES, qk), cspec(1, qk), cspec(tb, tb)]
    args = [z, z, z, z, wg, gb, _chunk_tri(tb, reverse)]
    scratch = [pltpu.VMEM((vw, qk), F32)]
    if fin is not None:
        o_fwd, norm_g = fin
        in_specs += [pl.BlockSpec((tb, vw), lambda i: (blk(i), 0)), zspec(vw, Z_AR), cspec(1, GLA_DV)]
        args += [o_fwd, z, norm_g.reshape(1, GLA_DV)]
        scratch.append(pltpu.VMEM((tb, vw), F32))
    return pl.pallas_call(
        functools.partial(_gla_kernel, reverse=reverse, tb=tb, finalize=fin is not None),
        grid=(nb,),
        in_specs=in_specs,
        out_specs=pl.BlockSpec((tb, vw), lambda i: (blk(i), 0)),
        out_shape=jax.ShapeDtypeStruct((s, vw), F32),
        scratch_shapes=scratch,
        compiler_params=_cparams(("arbitrary",)),
        name="gla_bwd" if reverse else "gla_fwd",
    )(*args)


def _hy_pre_kernel(u_ref, up_ref, un_ref, w_ref, b_ref, x0_ref, zb_ref, *, tm):
    i = pl.program_id(0)
    u = u_ref[...].astype(F32)
    row = lax.broadcasted_iota(jnp.int32, u.shape, 0)
    prev_row = jnp.where(i == 0, 0.0, up_ref[...].astype(F32)[HALO_ROWS - 1:HALO_ROWS, :])
    next_row = jnp.where(i == pl.num_programs(0) - 1, 0.0, un_ref[...].astype(F32)[0:1, :])
    u_prev = jnp.where(row == 0, prev_row, pltpu.roll(u, 1, axis=0))
    u_next = jnp.where(row == tm - 1, next_row, pltpu.roll(u, tm - 1, axis=0))
    y = b_ref[...] + u_prev * w_ref[0:1] + u * w_ref[1:2] + u_next * w_ref[2:3]
    x0_ref[...] = y[:, :HY_WIDTH].astype(BF16)
    zb_ref[...] = (y[:, HY_WIDTH:2 * HY_WIDTH] * y[:, 2 * HY_WIDTH:]).astype(BF16)


def _hy_pre(z, conv_w, conv_b, tm):
    s = z.shape[0]
    w = 3 * HY_WIDTH
    nr = s // HALO_ROWS
    ospec = pl.BlockSpec((tm, HY_WIDTH), lambda i: (i, 0))
    return pl.pallas_call(
        functools.partial(_hy_pre_kernel, tm=tm),
        grid=(s // tm,),
        in_specs=[pl.BlockSpec((tm, w), lambda i: (i, 0)),
                  pl.BlockSpec((HALO_ROWS, w), lambda i: (jnp.maximum(i * (tm // HALO_ROWS) - 1, 0), 0)),
                  pl.BlockSpec((HALO_ROWS, w), lambda i: (jnp.minimum((i + 1) * (tm // HALO_ROWS), nr - 1), 0)),
                  pl.BlockSpec((3, w), lambda i: (0, 0)),
                  pl.BlockSpec((1, w), lambda i: (0, 0))],
        out_specs=[ospec, ospec],
        out_shape=[jax.ShapeDtypeStruct((s, HY_WIDTH), BF16)] * 2,
        compiler_params=_cparams(("parallel",)),
        name="hyena_pre",
    )(z, z, z, conv_w, conv_b.reshape(1, w))


def _hy_filter_kernel(emb_ref, w1_ref, b1_ref, f1_ref, w2_ref, b2_ref, f2_ref, w3_ref, dl_ref,
                      kb_ref, norm_ref, *, tl, seq):
    i = pl.program_id(0)
    half = tl // 2
    emb = emb_ref[...]
    x = jnp.concatenate([emb[:half], emb[half:]], axis=1)
    h = jnp.sin(f1_ref[...] * (_dot3(x, w1_ref[...]) + b1_ref[...]))
    h = jnp.sin(f2_ref[...] * (_dot3(h, w2_ref[...]) + b2_ref[...]))
    h = _dot3(h, w3_ref[...])
    back = i * tl >= seq
    pick = lambda y: jnp.where(back, y[:, HY_WIDTH:], y[:, :HY_WIDTH])
    h = jnp.concatenate([pick(h[:, :2 * HY_WIDTH]), pick(h[:, 2 * HY_WIDTH:])], axis=0)
    h = h * jnp.exp(-emb[:, 0:1] * dl_ref[...])
    row = i * tl + lax.broadcasted_iota(jnp.int32, h.shape, 0)
    h = jnp.where(row == seq, 0.0, h)
    kb_ref[...] = h.astype(BF16)

    @pl.when(i == 0)
    def _():
        norm_ref[...] = jnp.zeros(norm_ref.shape, F32)

    norm_ref[...] += jnp.sum(jnp.abs(h), axis=0, keepdims=True)


def _hy_positions(seq):
    t = np.linspace(0.0, 1.0, seq, dtype=np.float32)[:, None]
    bands = (HY_EMB - 1) // 2
    freqs = np.linspace(1e-4, bands - 1, bands, dtype=np.float32)[None]
    w = (np.float32(2.0 * math.pi) * np.arange(seq, dtype=np.float32)[:, None] / np.float32(seq))
    zf = np.concatenate([t, np.cos(freqs * w), -np.sin(freqs * w)], axis=-1).astype(np.float32)
    pos = np.concatenate([np.arange(seq), [0], np.arange(seq - 1, 0, -1)])
    return jnp.asarray(np.pad(zf[pos], ((0, 0), (0, LANES - HY_EMB))))


def _block_diag2(w):
    z = jnp.zeros_like(w)
    return jnp.concatenate([jnp.concatenate([w, z], axis=1), jnp.concatenate([z, w], axis=1)], axis=0)


def _hy_filter(emb, w1, b1, fr1, w2, b2, fr2, w3, tl):
    n = emb.shape[0]
    seq = n // 2
    w1p = jnp.pad(w1, ((0, LANES - HY_EMB), (0, 0)))
    twice = lambda v: jnp.concatenate([v, v])[None]
    min_decay = math.log(HY_DECAY_TARGET) / HY_SLOW_DECAY
    max_decay = math.log(HY_DECAY_TARGET) / HY_FAST_DECAY
    deltas = jnp.asarray(np.abs(np.linspace(min_decay, max_decay, HY_WIDTH, dtype=np.float32))[None])
    cs = lambda r, c: pl.BlockSpec((r, c), lambda i: (0, 0))
    return pl.pallas_call(
        functools.partial(_hy_filter_kernel, tl=tl, seq=seq),
        grid=(n // tl,),
        in_specs=[pl.BlockSpec((tl, LANES), lambda i: (i, 0)),
                  cs(2 * LANES, LANES), cs(1, LANES), cs(1, LANES),
                  cs(LANES, LANES), cs(1, LANES), cs(1, LANES),
                  cs(LANES, 4 * HY_WIDTH), cs(1, HY_WIDTH)],
        out_specs=[pl.BlockSpec((tl, HY_WIDTH), lambda i: (i, 0)), pl.BlockSpec((1, HY_WIDTH), lambda i: (0, 0))],
        out_shape=[jax.ShapeDtypeStruct((n, HY_WIDTH), BF16), jax.ShapeDtypeStruct((1, HY_WIDTH), F32)],
        compiler_params=_cparams(("arbitrary",)),
        name="hyena_filter",
    )(emb, _block_diag2(w1p), twice(b1), twice(fr1), _block_diag2(w2), twice(b2), twice(fr2),
      _block_diag2(w3), deltas)


def _dft_tables(n1):
    n = n1 * DFT_N2
    a = np.arange(n1, dtype=np.int64)
    ang1 = (2.0 * math.pi / n1) * ((a[:, None] * a[None, :]) % n1)
    c1, s1 = np.cos(ang1), np.sin(ang1)
    k1 = np.arange(n1, dtype=np.int64)[:, None, None]
    k2 = np.arange(DFT_N2, dtype=np.int64)[None, :, None]
    n2 = np.arange(DFT_N2, dtype=np.int64)[None, None, :]
    ang = (2.0 * math.pi / n) * ((n2 * (k1 + n1 * k2)) % n)
    gc, gs = np.cos(ang), np.sin(ang)
    tb = lambda x: jnp.asarray(np.ascontiguousarray(x).astype(BF16))
    gc, gs = tb(gc), tb(gs)
    gct, gst = jnp.swapaxes(gc, 1, 2), jnp.swapaxes(gs, 1, 2)
    blk = lambda a, b, c, d: jnp.concatenate([jnp.concatenate([a, b], axis=2), jnp.concatenate([c, d], axis=2)], axis=1)
    return dict(c1=tb(c1), s1=tb(s1), g2=blk(gc, gs, -gs, gc), h2=blk(gct, -gst, gst, gct))


def _dft1_kernel(c_ref, s_ref, x_ref, re_ref, im_ref):
    x = x_ref[...]
    re_ref[...] = _dot(c_ref[...], x).astype(BF16)
    im_ref[...] = (-_dot(s_ref[...], x)).astype(BF16)


def _dft1(x, c1, s1, cb):
    k1, w = x.shape
    n1 = c1.shape[0]
    ospec = pl.BlockSpec((n1, cb), lambda j: (0, j))
    return pl.pallas_call(
        _dft1_kernel,
        grid=(w // cb,),
        in_specs=[pl.BlockSpec((n1, k1), lambda j: (0, 0)), pl.BlockSpec((n1, k1), lambda j: (0, 0)),
                  pl.BlockSpec((k1, cb), lambda j: (0, j))],
        out_specs=[ospec, ospec],
        out_shape=[jax.ShapeDtypeStruct((n1, w), BF16)] * 2,
        compiler_params=_cparams(("parallel",)),
        name="dft_stage1",
    )(c1, s1, x)


def _stack(re, im):
    return jnp.concatenate([re, im], axis=0)


def _dft2_filter_kernel(g2_ref, are_ref, aim_ref, norm_ref, kre_ref, kim_ref, *, kb):
    inv = 1.0 / norm_ref[...]
    for t in range(kb):
        x = _dot(g2_ref[t], _stack(are_ref[t], aim_ref[t])) * inv
        kre_ref[t] = x[:DFT_N2].astype(BF16)
        kim_ref[t] = x[DFT_N2:].astype(BF16)


def _dft2_filter(tabs, are, aim, norm, kb):
    n1, _, c = are.shape
    gspec = pl.BlockSpec((kb, 2 * DFT_N2, 2 * DFT_N2), lambda i: (i, 0, 0))
    aspec = pl.BlockSpec((kb, DFT_N2, c), lambda i: (i, 0, 0))
    return pl.pallas_call(
        functools.partial(_dft2_filter_kernel, kb=kb),
        grid=(n1 // kb,),
        in_specs=[gspec, aspec, aspec, pl.BlockSpec((1, c), lambda i: (0, 0))],
        out_specs=[aspec, aspec],
        out_shape=[jax.ShapeDtypeStruct((n1, DFT_N2, c), BF16)] * 2,
        compiler_params=_cparams(("parallel",)),
        name="dft_stage2_filter",
    )(tabs["g2"], are, aim, norm)


def _conv_mid_kernel(g2_ref, h2_ref, are_ref, aim_ref, kre_ref, kim_ref, bre_ref, bim_ref, *, kb):
    for t in range(kb):
        x = _dot(g2_ref[t], _stack(are_ref[t], aim_ref[t]))
        xre, xim = x[:DFT_N2], x[DFT_N2:]
        kre, kim = kre_ref[t].astype(F32), kim_ref[t].astype(F32)
        yre = (xre * kre - xim * kim).astype(BF16)
        yim = (xre * kim + xim * kre).astype(BF16)
        b = _dot(h2_ref[t], _stack(yre, yim))
        bre_ref[t] = b[:DFT_N2].astype(BF16)
        bim_ref[t] = b[DFT_N2:].astype(BF16)


def _conv_mid(tabs, are, aim, kre, kim, kb):
    n1, _, c = are.shape
    gspec = pl.BlockSpec((kb, 2 * DFT_N2, 2 * DFT_N2), lambda i: (i, 0, 0))
    aspec = pl.BlockSpec((kb, DFT_N2, c), lambda i: (i, 0, 0))
    return pl.pallas_call(
        functools.partial(_conv_mid_kernel, kb=kb),
        grid=(n1 // kb,),
        in_specs=[gspec, gspec, aspec, aspec, aspec, aspec],
        out_specs=[aspec, aspec],
        out_shape=[jax.ShapeDtypeStruct((n1, DFT_N2, c), BF16)] * 2,
        compiler_params=_cparams(("parallel",)),
        name="conv_spectral",
    )(tabs["g2"], tabs["h2"], are, aim, kre, kim)


def _idft1_kernel(c_ref, s_ref, bre_ref, bim_ref, o_ref, *, inv_n):
    y = (_dot(c_ref[...], bre_ref[...]) - _dot(s_ref[...], bim_ref[...])) * inv_n
    o_ref[...] = y.astype(o_ref.dtype)


def _idft1(c1h, s1h, bre, bim, cb):
    ko, n1 = c1h.shape
    w = bre.shape[1]
    n = n1 * DFT_N2
    ospec = pl.BlockSpec((ko, cb), lambda j: (0, j))
    return pl.pallas_call(
        functools.partial(_idft1_kernel, inv_n=1.0 / n),
        grid=(w // cb,),
        in_specs=[pl.BlockSpec((ko, n1), lambda j: (0, 0)), pl.BlockSpec((ko, n1), lambda j: (0, 0)),
                  pl.BlockSpec((n1, cb), lambda j: (0, j)), pl.BlockSpec((n1, cb), lambda j: (0, j))],
        out_specs=ospec,
        out_shape=jax.ShapeDtypeStruct((ko, w), BF16),
        compiler_params=_cparams(("parallel",)),
        name="idft_stage1",
    )(c1h, s1h, bre, bim)


def _hyena(z_all, tabs, emb, p, i, cb, kb):
    seq = z_all.shape[0]
    n1 = 2 * seq // DFT_N2
    c = HY_WIDTH
    x0, zb = _hy_pre(z_all, p["hy_conv_w"][i], p["hy_conv_b"][i], min(512, seq))
    kern_b, norm = _hy_filter(emb, p["hy_w1"][i], p["hy_b1"][i], p["hy_freq1"][i], p["hy_w2"][i],
                                 p["hy_b2"][i], p["hy_freq2"][i], p["hy_w3"][i], min(1024, seq))
    wide = DFT_N2 * c
    kre, kim = _dft1(kern_b.reshape(n1, wide), tabs["c1"], tabs["s1"], cb)
    kre, kim = _dft2_filter(tabs, kre.reshape(n1, DFT_N2, c), kim.reshape(n1, DFT_N2, c), norm, kb)
    are, aim = _dft1(zb.reshape(n1 // 2, wide), tabs["c1"][:, :n1 // 2], tabs["s1"][:, :n1 // 2], cb)
    bre, bim = _conv_mid(tabs, are.reshape(n1, DFT_N2, c), aim.reshape(n1, DFT_N2, c), kre, kim, kb)
    y = _idft1(tabs["c1"][:n1 // 2], tabs["s1"][:n1 // 2], bre.reshape(n1, wide), bim.reshape(n1, wide), cb)
    return y.reshape(seq, c), zb, x0


def _combine_kernel(x_ref, ya_ref, o0_ref, o1_ref, o2_ref, l0_ref, l1_ref, l2_ref, yc_ref,
                    hy_ref, hz_ref, hx_ref, skip_ref,
                    g0_ref, g1_ref, g2_ref, g3_ref, pa_ref, pb_ref, pc_ref, pd_ref, wo_ref, o_ref):
    l0, l1, l2 = l0_ref[...], l1_ref[...], l2_ref[...]
    mx = jnp.maximum(jnp.maximum(l0, l1), l2)
    e0, e1, e2 = jnp.exp(l0 - mx), jnp.exp(l1 - mx), jnp.exp(l2 - mx)
    yb = (e0 * o0_ref[...] + e1 * o1_ref[...] + e2 * o2_ref[...]) / (e0 + e1 + e2)
    f32 = lambda ref: ref[...].astype(F32)
    yd = f32(hx_ref) * (f32(hy_ref) + f32(hz_ref) * skip_ref[...])
    gate = lambda ref: _sigmoid(f32(ref))
    m = (gate(g0_ref) * _dot(ya_ref[...].astype(BF16), pa_ref[...])
         + gate(g1_ref) * _dot(yb.astype(BF16), pb_ref[...])
         + gate(g2_ref) * _dot(yc_ref[...].astype(BF16), pc_ref[...])
         + gate(g3_ref) * _dot(yd.astype(BF16), pd_ref[...]))
    o_ref[...] = x_ref[...] + _dot(m.astype(BF16), wo_ref[...])


def _combine(x, z, ya, dil_outs, yc, hyena, skip, pa, pb, pc, pd, wo, tm):
    s, d = x.shape
    rs = lambda w: pl.BlockSpec((tm, w), lambda i: (i, 0))
    gs = lambda b: pl.BlockSpec((tm, d), lambda i: (i, Z_GATE // d + b))
    ws = lambda a: pl.BlockSpec(a.shape, lambda i: (0, 0))
    (o0, l0), (o1, l1), (o2, l2) = dil_outs
    gw = DIL_HEADS * HEAD_DIM
    hy, hz, hx = hyena
    return pl.pallas_call(
        _combine_kernel,
        grid=(s // tm,),
        in_specs=[rs(d), rs(ya.shape[1]), rs(gw), rs(gw), rs(gw), rs(gw), rs(gw), rs(gw),
                  rs(yc.shape[1]), rs(HY_WIDTH), rs(HY_WIDTH), rs(HY_WIDTH), ws(skip), gs(0), gs(1), gs(2), gs(3),
                  ws(pa), ws(pb), ws(pc), ws(pd), ws(wo)],
        out_specs=rs(d),
        out_shape=jax.ShapeDtypeStruct((s, d), F32),
        compiler_params=_cparams(("parallel",)),
        name="combine",
    )(x, ya, o0, o1, o2, l0, l1, l2, yc, hy, hz, hx, skip, z, z, z, z, pa, pb, pc, pd, wo)


def _mlp_kernel(x_ref, g_ref, w1_ref, w2_ref, o_ref, h_ref, acc_ref):
    j = pl.program_id(1)

    @pl.when(j == 0)
    def _():
        h_ref[...] = (_rms(x_ref[...]) * g_ref[...]).astype(BF16)
        acc_ref[...] = jnp.zeros(acc_ref.shape, F32)

    a = jnp.maximum(_dot(h_ref[...], w1_ref[...]), 0.0)
    acc_ref[...] += _dot((a * a).astype(BF16), w2_ref[...])

    @pl.when(j == pl.num_programs(1) - 1)
    def _():
        o_ref[...] = x_ref[...] + acc_ref[...]


def _mlp(x, g, w1, w2, tm, tf):
    s, d = x.shape
    ff = w1.shape[1]
    return pl.pallas_call(
        _mlp_kernel,
        grid=(s // tm, ff // tf),
        in_specs=[pl.BlockSpec((tm, d), lambda i, j: (i, 0)),
                  pl.BlockSpec((1, d), lambda i, j: (0, 0)),
                  pl.BlockSpec((d, tf), lambda i, j: (0, j)),
                  pl.BlockSpec((tf, d), lambda i, j: (j, 0))],
        out_specs=pl.BlockSpec((tm, d), lambda i, j: (i, 0)),
        out_shape=jax.ShapeDtypeStruct((s, d), F32),
        scratch_shapes=[pltpu.VMEM((tm, d), BF16), pltpu.VMEM((tm, d), F32)],
        compiler_params=_cparams(("parallel", "arbitrary")),
        name="mlp",
    )(x, g.reshape(1, d), w1, w2)


def _permute_w_in(w):
    d = w.shape[0]
    pieces = [w[:, _O_DU:_O_GATE], w[:, _O_B:_O_C], w[:, _O_AQ:_O_AK], w[:, _O_C:_O_DU],
              w[:, _O_AV:_O_AR], w[:, _O_AR:_O_ALR], w[:, _O_AK:_O_AV], w[:, _O_ALR:_O_B],
              jnp.zeros((d, Z_GATE - Z_ALR - 2 * GLA_RANK), w.dtype), w[:, _O_GATE:_O_END]]
    return jnp.concatenate(pieces, axis=1).astype(BF16)


def _dil_bias_idx(tq, dil):
    rel = _skewed_rel(tq, tq + 2 * DIL_HALF, -1, -DIL_HALF)
    return _t5_bucket(rel * dil)[None].astype(np.int32)


def _diff_bias_idx(tq, tk):
    far_lo, far_hi = _diff_far_offsets(tq, tk)
    return np.stack([_t5_bucket(_skewed_rel(tk, tq, 1, o * tk)) for o in range(far_lo, far_hi + 1)]).astype(np.int32)


def _forward(x, p, *, t_diff, tk_diff, qc_diff, tq_dil, nt_dil, tb_gla, tm_proj, tn_proj, tm_row, tm_mlp, tf_mlp, cb_dft, kb_dft):
    seq = x.shape[0]
    depth = p["w_in"].shape[0]
    n_dil_bias = DIL_GROUPS * DIL_HEADS
    tq_dil = [min(tq, seq // dil) for tq, (_, dil) in zip(tq_dil, DIL_PATTERNS)]
    dil_bias = [_bias_tiles(p["t5_bias"], _dil_bias_idx(tq, dil), tq, tq + 2 * DIL_HALF, DIL_HEADS, gi * DIL_HEADS)
                for gi, (tq, (_, dil)) in enumerate(zip(tq_dil, DIL_PATTERNS))]
    t_diff, tk_diff = min(t_diff, seq), min(tk_diff, seq)
    diff_bias = _bias_tiles(p["t5_bias"], _diff_bias_idx(t_diff, tk_diff), tk_diff, t_diff, DIFF_HEADS,
                            n_dil_bias, LOG2E)
    tabs = _dft_tables(2 * seq // DFT_N2)
    emb = _hy_positions(seq)
    rep = lambda g, n: jnp.tile(g, n)
    for i in range(depth):
        z = _norm_matmul(x, p["norm1_g"][i], _permute_w_in(p["w_in"][i]), tm_proj, tn_proj)
        wg = [jnp.zeros((LANES, GLA_HEADS * GLA_DK), F32).at[j * GLA_RANK:(j + 1) * GLA_RANK].set(
            p["gla_gate_w"][i, j]).astype(BF16) for j in range(2)]
        gb = p["gla_gate_b"][i]
        o_fwd = _gla_scan(z, wg[0], gb[0:1], tb_gla, False)
        ya = _gla_scan(z, wg[1], gb[1:2], tb_gla, True, fin=(o_fwd, p["gla_norm_g"][i]))
        bq, bk, bv, cq, ck, cv = _prep(
            z, jnp.repeat(p["dil_qnorm_g"][i], DIL_HEADS, axis=0).reshape(-1),
            jnp.repeat(p["dil_knorm_g"][i], DIL_HEADS, axis=0).reshape(-1),
            rep(p["diff_qnorm_g"][i], 2 * DIFF_HEADS), rep(p["diff_knorm_g"][i], 2 * DIFF_HEADS), tm_row)
        dil_outs = [_dil_attn(bq, bk, bv, dil_bias[gi], gi, dil, tq_dil[gi], max(1, min(nt_dil[gi], seq // (tq_dil[gi] * dil))))
                    for gi, (_, dil) in enumerate(DIL_PATTERNS)]
        lam_init = 0.8 - 0.6 * math.exp(-0.3 * i)
        yc = _diff_attn(cq, ck, cv, diff_bias, p["diff_lambda"][i], p["diff_subln_g"][i], lam_init,
                        t_diff, tk_diff, qc_diff)
        yd = _hyena(z, tabs, emb, p, i, cb_dft, kb_dft)
        x = _combine(x, z, ya, dil_outs, yc, yd, p["hy_skip"][i][None],
                     p["proj_a"][i].astype(BF16), p["proj_b"][i].astype(BF16),
                     p["proj_c"][i].astype(BF16), p["proj_d"][i].astype(BF16), p["w_out"][i].astype(BF16),
                     tm_row)
        x = _mlp(x, p["norm2_g"][i], p["mlp_w1"][i].astype(BF16), p["mlp_w2"][i].astype(BF16), tm_mlp, tf_mlp)
    return x


def kernel(x, t5_bias, norm1_g, w_in, gla_gate_w, gla_gate_b, gla_norm_g, dil_qnorm_g, dil_knorm_g,
           diff_qnorm_g, diff_knorm_g, diff_lambda, diff_subln_g, hy_conv_w, hy_conv_b, hy_w1, hy_b1,
           hy_freq1, hy_w2, hy_b2, hy_freq2, hy_w3, hy_skip, proj_a, proj_b, proj_c, proj_d, w_out,
           norm2_g, mlp_w1, mlp_w2):
    p = dict(t5_bias=t5_bias, norm1_g=norm1_g, w_in=w_in, gla_gate_w=gla_gate_w, gla_gate_b=gla_gate_b,
             gla_norm_g=gla_norm_g, dil_qnorm_g=dil_qnorm_g, dil_knorm_g=dil_knorm_g,
             diff_qnorm_g=diff_qnorm_g, diff_knorm_g=diff_knorm_g, diff_lambda=diff_lambda,
             diff_subln_g=diff_subln_g, hy_conv_w=hy_conv_w, hy_conv_b=hy_conv_b, hy_w1=hy_w1, hy_b1=hy_b1,
             hy_freq1=hy_freq1, hy_w2=hy_w2, hy_b2=hy_b2, hy_freq2=hy_freq2, hy_w3=hy_w3, hy_skip=hy_skip,
             proj_a=proj_a, proj_b=proj_b, proj_c=proj_c, proj_d=proj_d, w_out=w_out, norm2_g=norm2_g,
             mlp_w1=mlp_w1, mlp_w2=mlp_w2)
    b, s, d = x.shape
    outs = [_forward(x[bi], p, t_diff=2048, tk_diff=1024, qc_diff=(256, 768, 768, 256), tq_dil=(256, 256, 128), nt_dil=(4, 1, 1), tb_gla=512, tm_proj=2048, tn_proj=512, tm_row=512,
                     tm_mlp=1024, tf_mlp=1024, cb_dft=8192, kb_dft=16) for bi in range(b)]
    return jnp.stack(outs)
```

```python
import functools
import math

import jax
import jax.numpy as jnp
import numpy as np
from jax import lax
from jax.experimental import pallas as pl
from jax.experimental.pallas import tpu as pltpu

F32 = jnp.float32
BF16 = jnp.bfloat16

D_MODEL = 1024
HEAD_DIM = 64
GLA_HEADS = 4
GLA_DK = 64
GLA_DV = 128
GLA_RANK = 16
GLA_TAU = 16.0
GLA_CHUNK = 64
DIL_PATTERNS = ((128, 1), (512, 4), (2048, 16))
DIL_GROUPS = 3
DIL_HEADS = 4
DIL_HALF = 64
DIFF_HEADS = 4
DIFF_DV = 128
DIFF_VT = DIFF_DV + 16
HY_WIDTH = 512
HY_EMB = 33
HY_FFN = 64
HY_DECAY_TARGET = 1e-2
HY_FAST_DECAY = 0.3
HY_SLOW_DECAY = 1.5
T5_BUCKETS = 32
T5_MAX_DIST = 1024
N_BIAS_HEADS = 16
D_FF = 4096
RMS_EPS = 1e-6
LOG2E = math.log2(math.e)

LANES = 128
HALO_ROWS = 16
VMEM_LIMIT = 48 * 1024 * 1024

Z_DU = 0
Z_BQ, Z_BK, Z_BV = 1536, 2304, 3072
Z_AQ = 3840
Z_CQ, Z_CK, Z_CV = 4096, 4608, 5120
Z_AV, Z_AR = 5632, 6144
Z_AK = 6656
Z_ALR = 6912
Z_GATE = 7168
Z_COLS = 11264

_O_AQ, _O_AK, _O_AV, _O_AR, _O_ALR, _O_B, _O_C, _O_DU, _O_GATE, _O_END = (
    0, 256, 512, 1024, 1536, 1568, 3872, 5408, 6944, 11040)

DFT_N2 = 128


def _cparams(sem):
    return pltpu.CompilerParams(dimension_semantics=sem, vmem_limit_bytes=VMEM_LIMIT)


def _dot(a, b):
    return jnp.dot(a, b, preferred_element_type=F32)


def _dot_nt(a, b):
    return lax.dot_general(a, b, (((1,), (1,)), ((), ())), preferred_element_type=F32)


def _dot_tn(a, b):
    return lax.dot_general(a, b, (((0,), (0,)), ((), ())), preferred_element_type=F32)


def _split(x):
    hi = x.astype(BF16)
    lo = (x - hi.astype(F32)).astype(BF16)
    return hi, lo


def _dot3(a, b):
    ah, al = _split(a)
    bh, bl = _split(b)
    return _dot(ah, bh) + _dot(ah, bl) + _dot(al, bh)


def _rms(x):
    return x * lax.rsqrt(jnp.mean(x * x, axis=-1, keepdims=True) + RMS_EPS)


def _sigmoid(x):
    return 0.5 * jnp.tanh(0.5 * x) + 0.5


def _norm_matmul_kernel(x_ref, g_ref, w_ref, o_ref, h_ref):
    @pl.when(pl.program_id(1) == 0)
    def _():
        h_ref[...] = (_rms(x_ref[...]) * g_ref[...]).astype(BF16)

    o_ref[...] = _dot(h_ref[...], w_ref[...]).astype(o_ref.dtype)


def _norm_matmul(x, g, w, tm, tn):
    s, d = x.shape
    n = w.shape[1]
    return pl.pallas_call(
        _norm_matmul_kernel,
        grid=(s // tm, n // tn),
        in_specs=[pl.BlockSpec((tm, d), lambda i, j: (i, 0)),
                  pl.BlockSpec((1, d), lambda i, j: (0, 0)),
                  pl.BlockSpec((d, tn), lambda i, j: (0, j))],
        out_specs=pl.BlockSpec((tm, tn), lambda i, j: (i, j)),
        out_shape=jax.ShapeDtypeStruct((s, n), BF16),
        scratch_shapes=[pltpu.VMEM((tm, d), BF16)],
        compiler_params=_cparams(("parallel", "arbitrary")),
        name="in_proj",
    )(x, g.reshape(1, d), w)


def _group_norm(x, e, gain):
    hi, lo = _split(x * x)
    ew = e.shape[0]
    ms = jnp.concatenate([_dot(hi[:, c:c + ew], e) + _dot(lo[:, c:c + ew], e)
                          for c in range(0, x.shape[1], ew)], axis=1) * (1.0 / HEAD_DIM)
    return x * lax.rsqrt(ms + RMS_EPS) * gain


def _prep_kernel(bq_ref, bk_ref, bv_ref, cq_ref, ck_ref, cv_ref, e_ref,
                 gbq_ref, gbk_ref, gcq_ref, gck_ref,
                 obq_ref, obk_ref, obv_ref, ocq_ref, ock_ref, ocv_ref):
    scale = HEAD_DIM ** -0.5
    eb = ec = e_ref[...]
    f32 = lambda ref: ref[...].astype(F32)
    obq_ref[...] = (_group_norm(f32(bq_ref), eb, gbq_ref[...]) * scale).astype(BF16)
    obk_ref[...] = _group_norm(f32(bk_ref), eb, gbk_ref[...]).astype(BF16)
    obv_ref[...] = bv_ref[...]
    ocq_ref[...] = (_group_norm(f32(cq_ref), ec, gcq_ref[...]) * (scale * LOG2E)).T.astype(BF16)
    ock_ref[...] = _group_norm(f32(ck_ref), ec, gck_ref[...]).astype(BF16)
    cvt = f32(cv_ref).T
    ones = jnp.ones((DIFF_VT - DIFF_DV, cvt.shape[1]), F32)
    ocv_ref[...] = jnp.concatenate(
        [t for h in range(DIFF_HEADS) for t in (cvt[h * DIFF_DV:(h + 1) * DIFF_DV], ones)], axis=0).astype(BF16)


def _block_diag_ones(width):
    idx = np.arange(width) // HEAD_DIM
    return jnp.asarray(idx[:, None] == idx[None, :], dtype=BF16)


def _prep(z, gbq, gbk, gcq, gck, tm):
    s = z.shape[0]
    wb, wc = DIL_GROUPS * DIL_HEADS * HEAD_DIM, DIFF_HEADS * 2 * HEAD_DIM
    zspec = lambda w, off: pl.BlockSpec((tm, w), lambda i: (i, off // w))
    cspec = lambda r, c: pl.BlockSpec((r, c), lambda i: (0, 0))
    ospec = lambda w: pl.BlockSpec((tm, w), lambda i: (i, 0))
    tspec = lambda w: pl.BlockSpec((w, tm), lambda i: (0, i))
    return pl.pallas_call(
        _prep_kernel,
        grid=(s // tm,),
        in_specs=[zspec(wb, Z_BQ), zspec(wb, Z_BK), zspec(wb, Z_BV),
                  zspec(wc, Z_CQ), zspec(wc, Z_CK), zspec(wc, Z_CV),
                  cspec(2 * LANES, 2 * LANES),
                  cspec(1, wb), cspec(1, wb), cspec(1, wc), cspec(1, wc)],
        out_specs=[ospec(wb), ospec(wb), ospec(wb), tspec(wc), ospec(wc), tspec(DIFF_HEADS * DIFF_VT)],
        out_shape=[jax.ShapeDtypeStruct((s, wb), BF16)] * 3
        + [jax.ShapeDtypeStruct((wc, s), BF16), jax.ShapeDtypeStruct((s, wc), BF16),
           jax.ShapeDtypeStruct((DIFF_HEADS * DIFF_VT, s), BF16)],
        compiler_params=_cparams(("parallel",)),
        name="qk_prep",
    )(z, z, z, z, z, z, _block_diag_ones(2 * LANES),
      gbq.reshape(1, wb), gbk.reshape(1, wb), gcq.reshape(1, wc), gck.reshape(1, wc))


def _t5_bucket(rel):
    half = T5_BUCKETS // 2
    max_exact = half // 2
    ret = np.where(rel > 0, half, 0)
    n = np.abs(rel)
    nf = np.maximum(n, 1).astype(np.float64)
    large = max_exact + (np.log(nf / max_exact) / math.log(T5_MAX_DIST / max_exact)
                         * (half - max_exact)).astype(np.int64)
    large = np.minimum(large, half - 1)
    return ret + np.where(n < max_exact, n, large)


def _bias_kernel(tab_ref, rng_ref, idx_ref, o_ref, p_ref, *, head_base, out_scale, chunk):
    t = pl.program_id(0)
    col = head_base + pl.program_id(1)
    rows, width = o_ref.shape[2:]
    slabs = width // LANES
    lo, hi = rng_ref[t, 0], rng_ref[t, 1]

    def one_chunk(ci, carry):
        rs = pl.ds(pl.multiple_of(ci * chunk, chunk), chunk)
        idx = idx_ref[0, rs, :]
        p_ref[rs, :] = lax.fori_loop(
            lo, hi + 1, lambda b, acc: jnp.where(idx == b, tab_ref[b, col] * out_scale, acc),
            jnp.zeros((chunk, LANES), F32))
        return carry

    lax.fori_loop(0, p_ref.shape[0] // chunk, one_chunk, 0)
    for c in range(slabs):
        off = LANES * (slabs - 1 - c)
        o_ref[0, 0, :, c * LANES:(c + 1) * LANES] = p_ref[off:off + rows, :]


def _skewed_rel(rows, width, sign, base):
    slabs = width // LANES
    rho = np.arange(rows + LANES * (slabs - 1))[:, None]
    lane = np.arange(LANES)[None, :]
    return sign * (rho - LANES * (slabs - 1) - lane) + base


def _bias_tiles(t5_bias, idx, rows, width, n_heads, head_base, out_scale=1.0, chunk=64):
    nt, rp, _ = idx.shape
    ranges = np.stack([idx.reshape(nt, -1).min(axis=1), idx.reshape(nt, -1).max(axis=1)], axis=1).astype(np.int32)
    return pl.pallas_call(
        functools.partial(_bias_kernel, head_base=head_base, out_scale=out_scale, chunk=chunk),
        grid=(nt, n_heads),
        in_specs=[pl.BlockSpec(memory_space=pltpu.SMEM), pl.BlockSpec(memory_space=pltpu.SMEM),
                  pl.BlockSpec((1, rp, LANES), lambda t, h: (t, 0, 0))],
        out_specs=pl.BlockSpec((1, 1, rows, width), lambda t, h: (t, h, 0, 0)),
        out_shape=jax.ShapeDtypeStruct((nt, n_heads, rows, width), F32),
        scratch_shapes=[pltpu.VMEM((rp, LANES), F32)],
        compiler_params=_cparams(("parallel", "parallel")),
        name="t5_bias_tiles",
    )(t5_bias, jnp.asarray(ranges), jnp.asarray(idx))


def _diff_attn_kernel(qt_ref, k_ref, vt_ref, bias_ref, lam_ref, g_ref, o_ref,
                      qa_ref, qb_ref, m_ref, acc_ref, s_ref, *, lam_init, qc, far_lo, far_hi):
    j = pl.program_id(2)
    t = qt_ref.shape[1]

    @pl.when(j == 0)
    def _():
        qt = qt_ref[...]
        row = lax.broadcasted_iota(jnp.int32, qt.shape, 0)
        qa_ref[...] = jnp.where(row < HEAD_DIM, qt, jnp.zeros_like(qt))
        qb_ref[...] = jnp.where(row >= HEAD_DIM, qt, jnp.zeros_like(qt))
        m_ref[...] = jnp.full(m_ref.shape, -jnp.inf, F32)
        acc_ref[...] = jnp.zeros(acc_ref.shape, F32)

    starts = [sum(qc[:u]) for u in range(len(qc))]
    cuts = [slice(a, a + w) for a, w in zip(starts, qc)]
    chains = [(0, cs) for cs in cuts] + [(1, cs) for cs in reversed(cuts)]
    q_refs = (qa_ref, qb_ref)

    def step(far):
        k = k_ref[...]
        vt = vt_ref[...]
        if far:
            const = bias_ref[0, 0, 0:1, 0:1]
            scores = lambda c, cols: _dot(k, q_refs[c][:, cols])
        else:
            const = 0.0
            scores = lambda c, cols: _dot(k, q_refs[c][:, cols]) + bias_ref[0, 0, :, cols]
        width = lambda n: chains[n][1].stop - chains[n][1].start
        s_ref[0, :, :width(0)] = scores(*chains[0])
        for n, (c, cols) in enumerate(chains):
            if n + 1 < len(chains):
                s_ref[(n + 1) % 2, :, :width(n + 1)] = scores(*chains[n + 1])
            s = s_ref[n % 2, :, :width(n)]
            m_old = m_ref[c, :, cols]
            m_new = jnp.maximum(m_old, jnp.max(s, axis=0, keepdims=True) + const)
            alpha = jnp.exp2(m_old - m_new)
            p = jnp.exp2(s - (m_new - const)).astype(BF16)
            acc_ref[c, :, cols] = alpha * acc_ref[c, :, cols] + _dot(vt, p)
            m_ref[c, :, cols] = m_new

    off = j - pl.program_id(1) * (t // k_ref.shape[0])
    is_far = jnp.logical_or(off <= far_lo, off >= far_hi)
    pl.when(is_far)(lambda: step(True))
    pl.when(jnp.logical_not(is_far))(lambda: step(False))

    @pl.when(j == pl.num_programs(2) - 1)
    def _():
        lp = lam_ref[...]
        lam = (jnp.exp(jnp.sum(lp[0:1] * lp[1:2], axis=-1, keepdims=True))
               - jnp.exp(jnp.sum(lp[2:3] * lp[3:4], axis=-1, keepdims=True)) + lam_init)
        a0 = acc_ref[0]
        a1 = acc_ref[1]
        o0 = a0[:DIFF_DV] / a0[DIFF_DV:DIFF_DV + 1]
        o1 = a1[:DIFF_DV] / a1[DIFF_DV:DIFF_DV + 1]
        att = o0 - lam * o1
        y = att * lax.rsqrt(jnp.mean(att * att, axis=0, keepdims=True) + RMS_EPS)
        o_ref[...] = y.T * g_ref[...] * (1.0 - lam_init)


def _diff_far_offsets(tq, tk):
    far_lo = (-T5_MAX_DIST - tk + 1) // tk
    far_hi = -(-(T5_MAX_DIST + tq - 1) // tk)
    return far_lo, far_hi


def _diff_attn(cqt, ck, cvt, bias_tiles, lam_p, subln_g, lam_init, t, tk, qc):
    s = ck.shape[0]
    far_lo, far_hi = _diff_far_offsets(t, tk)
    ratio = t // tk
    w = 2 * HEAD_DIM
    return pl.pallas_call(
        functools.partial(_diff_attn_kernel, lam_init=lam_init, qc=qc, far_lo=far_lo, far_hi=far_hi),
        grid=(DIFF_HEADS, s // t, s // tk),
        in_specs=[pl.BlockSpec((w, t), lambda h, i, j: (h, i)),
                  pl.BlockSpec((tk, w), lambda h, i, j: (j, h)),
                  pl.BlockSpec((DIFF_VT, tk), lambda h, i, j: (h, j)),
                  pl.BlockSpec((1, 1, tk, t),
                               lambda h, i, j: (jnp.clip(j - i * ratio, far_lo, far_hi) - far_lo, h, 0, 0)),
                  pl.BlockSpec((4, HEAD_DIM), lambda h, i, j: (0, 0)),
                  pl.BlockSpec((1, DIFF_DV), lambda h, i, j: (0, 0))],
        out_specs=pl.BlockSpec((t, DIFF_DV), lambda h, i, j: (i, h)),
        out_shape=jax.ShapeDtypeStruct((s, DIFF_HEADS * DIFF_DV), F32),
        scratch_shapes=[pltpu.VMEM((w, t), BF16), pltpu.VMEM((w, t), BF16),
                        pltpu.VMEM((2, 1, t), F32), pltpu.VMEM((2, DIFF_VT, t), F32),
                        pltpu.VMEM((2, tk, max(qc)), F32)],
        compiler_params=_cparams(("parallel", "parallel", "arbitrary")),
        name="diff_attn",
    )(cqt, ck, cvt, bias_tiles, lam_p, subln_g.reshape(1, DIFF_DV))


def _dil_kernel(q_ref, kp_ref, kc_ref, kn_ref, vp_ref, vc_ref, vn_ref, bias_ref, o_ref, lse_ref,
                qs_ref, ks_ref, vs_ref, os_ref, ls_ref, *, tq, dil, nt, m_len):
    n = pl.program_id(0)
    hd = DIL_HALF * dil
    body_rows = tq * dil * nt
    halves = DIL_HEADS * HEAD_DIM // LANES

    def put(dst, rows, src_ref):
        x = src_ref[...].astype(F32)
        for t in range(halves):
            dst[t, rows, :] = x[:, t * LANES:(t + 1) * LANES]

    def strided(src, r, count):
        return jnp.concatenate([src[t, pl.ds(r, count, stride=dil), :] for t in range(halves)], axis=1)

    put(qs_ref, slice(0, body_rows), q_ref)
    for dst, (p_ref, c_ref, n_ref) in ((ks_ref, (kp_ref, kc_ref, kn_ref)), (vs_ref, (vp_ref, vc_ref, vn_ref))):
        put(dst, slice(0, hd), p_ref)
        put(dst, slice(hd, hd + body_rows), c_ref)
        put(dst, slice(hd + body_rows, 2 * hd + body_rows), n_ref)
    tk = tq + 2 * DIL_HALF
    a = lax.broadcasted_iota(jnp.int32, (tq, tk), 0)
    c = lax.broadcasted_iota(jnp.int32, (tq, tk), 1)
    delta = c - DIL_HALF - a
    in_band = jnp.concatenate([jnp.abs(delta) <= DIL_HALF] * DIL_HEADS, axis=0)
    col = lax.broadcasted_iota(jnp.int32, (1, tk), 1)
    lane = lax.broadcasted_iota(jnp.int32, (tq, DIL_HEADS * HEAD_DIM), 1)
    head_masks = [(lane // HEAD_DIM) == h for h in range(DIL_HEADS)]

    def one_tile(it, carry):
        u = it // dil
        r = it % dil + u * (tq * dil)
        kpos = (n * nt + u) * tq - DIL_HALF + col
        valid = jnp.logical_and(in_band, jnp.logical_and(kpos >= 0, kpos < m_len))
        q = strided(qs_ref, r, tq).astype(BF16)
        k = strided(ks_ref, r, tk).astype(BF16)
        v = strided(vs_ref, r, tk).astype(BF16)
        qs = jnp.concatenate([jnp.where(hm, q, jnp.zeros_like(q)) for hm in head_masks], axis=0)
        s = _dot_nt(qs, k) + bias_ref[0].reshape(DIL_HEADS * tq, tk)
        s = jnp.where(valid, s, -1e30)
        m = jnp.max(s, axis=-1, keepdims=True)
        e = jnp.exp(s - m)
        l = jnp.sum(e, axis=-1, keepdims=True)
        oh = _dot((e / l).astype(BF16), v)
        lse = m + jnp.log(l)
        o = jnp.zeros(q.shape, F32)
        lse_o = jnp.zeros(q.shape, F32)
        for h, hm in enumerate(head_masks):
            rows = slice(h * tq, (h + 1) * tq)
            o = jnp.where(hm, oh[rows], o)
            lse_o = jnp.where(hm, lse[rows], lse_o)
        for t in range(halves):
            os_ref[t, pl.ds(r, tq, stride=dil), :] = o[:, t * LANES:(t + 1) * LANES]
            ls_ref[t, pl.ds(r, tq, stride=dil), :] = lse_o[:, t * LANES:(t + 1) * LANES]
        return carry

    lax.fori_loop(0, dil * nt, one_tile, 0, unroll=min(dil * nt, 4))
    o_ref[...] = jnp.concatenate([os_ref[t] for t in range(halves)], axis=1)
    lse_ref[...] = jnp.concatenate([ls_ref[t] for t in range(halves)], axis=1)


def _dil_attn(bq, bk, bv, bias, gi, dil, tq, nt):
    s = bq.shape[0]
    m_len = s // dil
    gw = DIL_HEADS * HEAD_DIM
    rows = tq * dil * nt
    hd = DIL_HALF * dil
    hb = rows // hd
    last = s // hd - 1
    qspec = pl.BlockSpec((rows, gw), lambda n: (n, gi))
    pspec = pl.BlockSpec((hd, gw), lambda n: (jnp.maximum(n * hb - 1, 0), gi))
    nspec = pl.BlockSpec((hd, gw), lambda n: (jnp.minimum((n + 1) * hb, last), gi))
    ospec = pl.BlockSpec((rows, gw), lambda n: (n, 0))
    return pl.pallas_call(
        functools.partial(_dil_kernel, tq=tq, dil=dil, nt=nt, m_len=m_len),
        grid=(s // rows,),
        in_specs=[qspec, pspec, qspec, nspec, pspec, qspec, nspec,
                  pl.BlockSpec((1, DIL_HEADS, tq, tq + 2 * DIL_HALF), lambda n: (0, 0, 0, 0))],
        out_specs=[ospec, ospec],
        out_shape=[jax.ShapeDtypeStruct((s, gw), F32)] * 2,
        scratch_shapes=[pltpu.VMEM((gw // LANES, rows, LANES), F32),
                        pltpu.VMEM((gw // LANES, rows + 2 * hd, LANES), F32),
                        pltpu.VMEM((gw // LANES, rows + 2 * hd, LANES), F32),
                        pltpu.VMEM((gw // LANES, rows, LANES), F32),
                        pltpu.VMEM((gw // LANES, rows, LANES), F32)],
        compiler_params=_cparams(("parallel",)),
        name=f"dil_attn_g{gi}",
    )(bq, bk, bk, bk, bv, bv, bv, bias)


def _gla_kernel(*refs, reverse, tb, finalize):
    if finalize:
        (q_ref, k_ref, v_ref, lr_ref, wg_ref, gb_ref, tri_ref, ofwd_ref, r_ref, ng_ref,
         o_ref, s_ref, oacc_ref) = refs
    else:
        q_ref, k_ref, v_ref, lr_ref, wg_ref, gb_ref, tri_ref, o_ref, s_ref = refs
        oacc_ref = o_ref
    cw = GLA_CHUNK
    qk = GLA_HEADS * GLA_DK
    vw = GLA_HEADS * GLA_DV

    @pl.when(pl.program_id(0) == 0)
    def _():
        s_ref[...] = jnp.zeros(s_ref.shape, F32)

    logits = _dot(lr_ref[...], wg_ref[...]) + gb_ref[...]
    g = (jnp.minimum(logits, 0.0) - jnp.log(1.0 + jnp.exp(-jnp.abs(logits)))) * (1.0 / GLA_TAU)
    ghi, glo = _split(g)
    tri = tri_ref[...]
    b = _dot(tri, ghi) + _dot(tri, glo)
    qg = (q_ref[...].astype(F32) * (GLA_DK ** -0.5) * jnp.exp(b)).astype(BF16)
    k = k_ref[...].astype(F32)
    kg = (k * jnp.exp(-b)).astype(BF16)
    v = v_ref[...].astype(BF16)

    lane_q = lax.broadcasted_iota(jnp.int32, (cw, qk), 1)
    rr = lax.broadcasted_iota(jnp.int32, (GLA_HEADS * cw, cw), 0)
    cc = lax.broadcasted_iota(jnp.int32, (GLA_HEADS * cw, cw), 1)
    tt = rr % cw
    amask = (cc > tt) if reverse else (cc <= tt)
    srow = lax.broadcasted_iota(jnp.int32, (vw, qk), 0) // GLA_DV
    scol = lax.broadcasted_iota(jnp.int32, (vw, qk), 1) // GLA_DK
    bdmask = srow == scol

    n_chunks = tb // cw
    order = range(n_chunks - 1, -1, -1) if reverse else range(n_chunks)
    for ci in order:
        rows = slice(ci * cw, (ci + 1) * cw)
        bc = b[rows]
        b_end = bc[0:1] if reverse else bc[cw - 1:cw]
        kdec = (k[rows] * jnp.exp(b_end - bc)).astype(BF16)
        qg_c = qg[rows]
        qs = jnp.concatenate(
            [jnp.where((lane_q // GLA_DK) == h, qg_c, jnp.zeros_like(qg_c)) for h in range(GLA_HEADS)], axis=0)
        a = jnp.where(amask, _dot_nt(qs, kg[rows]), 0.0)
        obig = _dot(a.astype(BF16), v[rows])
        o_intra = jnp.concatenate(
            [obig[h * cw:(h + 1) * cw, h * GLA_DV:(h + 1) * GLA_DV] for h in range(GLA_HEADS)], axis=1)
        state = s_ref[...]
        oacc_ref[rows, :] = o_intra + _dot_nt(qg_c, state.astype(BF16))
        ds = _dot_tn(v[rows], kdec)
        s_ref[...] = jnp.where(bdmask, jnp.exp(b_end) * state + ds, 0.0)

    if finalize:
        o = ofwd_ref[...] + oacc_ref[...]
        r = r_ref[...].astype(F32)
        outs = []
        for h in range(GLA_HEADS):
            sl = slice(h * GLA_DV, (h + 1) * GLA_DV)
            outs.append(_rms(o[:, sl]) * ng_ref[...] * (r[:, sl] * _sigmoid(r[:, sl])))
        o_ref[...] = jnp.concatenate(outs, axis=1)


def _chunk_tri(tb, reverse):
    i = np.arange(tb)
    same = (i[:, None] // GLA_CHUNK) == (i[None, :] // GLA_CHUNK)
    tri = (i[None, :] >= i[:, None]) if reverse else (i[None, :] <= i[:, None])
    return jnp.asarray(same & tri, dtype=BF16)


def _gla_scan(z, wg, gb, tb, reverse, fin=None):
    s = z.shape[0]
    nb = s // tb
    qk = GLA_HEADS * GLA_DK
    vw = GLA_HEADS * GLA_DV
    blk = (lambda i: nb - 1 - i) if reverse else (lambda i: i)
    zspec = lambda w, off: pl.BlockSpec((tb, w), lambda i: (blk(i), off // w))
    cspec = lambda r, c: pl.BlockSpec((r, c), lambda i: (0, 0))
    in_specs = [zspec(qk, Z_AQ), zspec(qk, Z_AK), zspec(vw, Z_AV), zspec(LANES, Z_ALR),
                cspec(LANES, qk), cspec(1, qk), cspec(tb, tb)]
    args = [z, z, z, z, wg, gb, _chunk_tri(tb, reverse)]
    scratch = [pltpu.VMEM((vw, qk), F32)]
    if fin is not None:
        o_fwd, norm_g = fin
        in_specs += [pl.BlockSpec((tb, vw), lambda i: (blk(i), 0)), zspec(vw, Z_AR), cspec(1, GLA_DV)]
        args += [o_fwd, z, norm_g.reshape(1, GLA_DV)]
        scratch.append(pltpu.VMEM((tb, vw), F32))
    return pl.pallas_call(
        functools.partial(_gla_kernel, reverse=reverse, tb=tb, finalize=fin is not None),
        grid=(nb,),
        in_specs=in_specs,
        out_specs=pl.BlockSpec((tb, vw), lambda i: (blk(i), 0)),
        out_shape=jax.ShapeDtypeStruct((s, vw), F32),
        scratch_shapes=scratch,
        compiler_params=_cparams(("arbitrary",)),
        name="gla_bwd" if reverse else "gla_fwd",
    )(*args)


def _hy_pre_kernel(u_ref, up_ref, un_ref, w_ref, b_ref, x0_ref, zb_ref, *, tm):
    i = pl.program_id(0)
    u = u_ref[...].astype(F32)
    row = lax.broadcasted_iota(jnp.int32, u.shape, 0)
    prev_row = jnp.where(i == 0, 0.0, up_ref[...].astype(F32)[HALO_ROWS - 1:HALO_ROWS, :])
    next_row = jnp.where(i == pl.num_programs(0) - 1, 0.0, un_ref[...].astype(F32)[0:1, :])
    u_prev = jnp.where(row == 0, prev_row, pltpu.roll(u, 1, axis=0))
    u_next = jnp.where(row == tm - 1, next_row, pltpu.roll(u, tm - 1, axis=0))
    y = b_ref[...] + u_prev * w_ref[0:1] + u * w_ref[1:2] + u_next * w_ref[2:3]
    x0_ref[...] = y[:, :HY_WIDTH].astype(BF16)
    zb_ref[...] = (y[:, HY_WIDTH:2 * HY_WIDTH] * y[:, 2 * HY_WIDTH:]).astype(BF16)


def _hy_pre(z, conv_w, conv_b, tm):
    s = z.shape[0]
    w = 3 * HY_WIDTH
    nr = s // HALO_ROWS
    ospec = pl.BlockSpec((tm, HY_WIDTH), lambda i: (i, 0))
    return pl.pallas_call(
        functools.partial(_hy_pre_kernel, tm=tm),
        grid=(s // tm,),
        in_specs=[pl.BlockSpec((tm, w), lambda i: (i, 0)),
                  pl.BlockSpec((HALO_ROWS, w), lambda i: (jnp.maximum(i * (tm // HALO_ROWS) - 1, 0), 0)),
                  pl.BlockSpec((HALO_ROWS, w), lambda i: (jnp.minimum((i + 1) * (tm // HALO_ROWS), nr - 1), 0)),
                  pl.BlockSpec((3, w), lambda i: (0, 0)),
                  pl.BlockSpec((1, w), lambda i: (0, 0))],
        out_specs=[ospec, ospec],
        out_shape=[jax.ShapeDtypeStruct((s, HY_WIDTH), BF16)] * 2,
        compiler_params=_cparams(("parallel",)),
        name="hyena_pre",
    )(z, z, z, conv_w, conv_b.reshape(1, w))


def _hy_filter_kernel(emb_ref, w1_ref, b1_ref, f1_ref, w2_ref, b2_ref, f2_ref, w3_ref, dl_ref,
                      kb_ref, norm_ref, *, tl, seq):
    i = pl.program_id(0)
    half = tl // 2
    emb = emb_ref[...]
    x = jnp.concatenate([emb[:half], emb[half:]], axis=1)
    h = jnp.sin(f1_ref[...] * (_dot3(x, w1_ref[...]) + b1_ref[...]))
    h = jnp.sin(f2_ref[...] * (_dot3(h, w2_ref[...]) + b2_ref[...]))
    h = _dot3(h, w3_ref[...])
    back = i * tl >= seq
    pick = lambda y: jnp.where(back, y[:, HY_WIDTH:], y[:, :HY_WIDTH])
    h = jnp.concatenate([pick(h[:, :2 * HY_WIDTH]), pick(h[:, 2 * HY_WIDTH:])], axis=0)
    h = h * jnp.exp(-emb[:, 0:1] * dl_ref[...])
    row = i * tl + lax.broadcasted_iota(jnp.int32, h.shape, 0)
    h = jnp.where(row == seq, 0.0, h)
    kb_ref[...] = h.astype(BF16)

    @pl.when(i == 0)
    def _():
        norm_ref[...] = jnp.zeros(norm_ref.shape, F32)

    norm_ref[...] += jnp.sum(jnp.abs(h), axis=0, keepdims=True)


def _hy_positions(seq):
    t = np.linspace(0.0, 1.0, seq, dtype=np.float32)[:, None]
    bands = (HY_EMB - 1) // 2
    freqs = np.linspace(1e-4, bands - 1, bands, dtype=np.float32)[None]
    w = (np.float32(2.0 * math.pi) * np.arange(seq, dtype=np.float32)[:, None] / np.float32(seq))
    zf = np.concatenate([t, np.cos(freqs * w), -np.sin(freqs * w)], axis=-1).astype(np.float32)
    pos = np.concatenate([np.arange(seq), [0], np.arange(seq - 1, 0, -1)])
    return jnp.asarray(np.pad(zf[pos], ((0, 0), (0, LANES - HY_EMB))))


def _block_diag2(w):
    z = jnp.zeros_like(w)
    return jnp.concatenate([jnp.concatenate([w, z], axis=1), jnp.concatenate([z, w], axis=1)], axis=0)


def _hy_filter(emb, w1, b1, fr1, w2, b2, fr2, w3, tl):
    n = emb.shape[0]
    seq = n // 2
    w1p = jnp.pad(w1, ((0, LANES - HY_EMB), (0, 0)))
    twice = lambda v: jnp.concatenate([v, v])[None]
    min_decay = math.log(HY_DECAY_TARGET) / HY_SLOW_DECAY
    max_decay = math.log(HY_DECAY_TARGET) / HY_FAST_DECAY
    deltas = jnp.asarray(np.abs(np.linspace(min_decay, max_decay, HY_WIDTH, dtype=np.float32))[None])
    cs = lambda r, c: pl.BlockSpec((r, c), lambda i: (0, 0))
    return pl.pallas_call(
        functools.partial(_hy_filter_kernel, tl=tl, seq=seq),
        grid=(n // tl,),
        in_specs=[pl.BlockSpec((tl, LANES), lambda i: (i, 0)),
                  cs(2 * LANES, LANES), cs(1, LANES), cs(1, LANES),
                  cs(LANES, LANES), cs(1, LANES), cs(1, LANES),
                  cs(LANES, 4 * HY_WIDTH), cs(1, HY_WIDTH)],
        out_specs=[pl.BlockSpec((tl, HY_WIDTH), lambda i: (i, 0)), pl.BlockSpec((1, HY_WIDTH), lambda i: (0, 0))],
        out_shape=[jax.ShapeDtypeStruct((n, HY_WIDTH), BF16), jax.ShapeDtypeStruct((1, HY_WIDTH), F32)],
        compiler_params=_cparams(("arbitrary",)),
        name="hyena_filter",
    )(emb, _block_diag2(w1p), twice(b1), twice(fr1), _block_diag2(w2), twice(b2), twice(fr2),
      _block_diag2(w3), deltas)


def _dft_tables(n1):
    n = n1 * DFT_N2
    a = np.arange(n1, dtype=np.int64)
    ang1 = (2.0 * math.pi / n1) * ((a[:, None] * a[None, :]) % n1)
    c1, s1 = np.cos(ang1), np.sin(ang1)
    k1 = np.arange(n1, dtype=np.int64)[:, None, None]
    k2 = np.arange(DFT_N2, dtype=np.int64)[None, :, None]
    n2 = np.arange(DFT_N2, dtype=np.int64)[None, None, :]
    ang = (2.0 * math.pi / n) * ((n2 * (k1 + n1 * k2)) % n)
    gc, gs = np.cos(ang), np.sin(ang)
    tb = lambda x: jnp.asarray(np.ascontiguousarray(x).astype(BF16))
    gc, gs = tb(gc), tb(gs)
    gct, gst = jnp.swapaxes(gc, 1, 2), jnp.swapaxes(gs, 1, 2)
    blk = lambda a, b, c, d: jnp.concatenate([jnp.concatenate([a, b], axis=2), jnp.concatenate([c, d], axis=2)], axis=1)
    return dict(c1=tb(c1), s1=tb(s1), g2=blk(gc, gs, -gs, gc), h2=blk(gct, -gst, gst, gct))


def _dft1_kernel(c_ref, s_ref, x_ref, re_ref, im_ref):
    x = x_ref[...]
    re_ref[...] = _dot(c_ref[...], x).astype(BF16)
    im_ref[...] = (-_dot(s_ref[...], x)).astype(BF16)


def _dft1(x, c1, s1, cb):
    k1, w = x.shape
    n1 = c1.shape[0]
    ospec = pl.BlockSpec((n1, cb), lambda j: (0, j))
    return pl.pallas_call(
        _dft1_kernel,
        grid=(w // cb,),
        in_specs=[pl.BlockSpec((n1, k1), lambda j: (0, 0)), pl.BlockSpec((n1, k1), lambda j: (0, 0)),
                  pl.BlockSpec((k1, cb), lambda j: (0, j))],
        out_specs=[ospec, ospec],
        out_shape=[jax.ShapeDtypeStruct((n1, w), BF16)] * 2,
        compiler_params=_cparams(("parallel",)),
        name="dft_stage1",
    )(c1, s1, x)


def _stack(re, im):
    return jnp.concatenate([re, im], axis=0)


def _conv_mid_kernel(g2_ref, h2_ref, are_ref, aim_ref, fre_ref, fim_ref, norm_ref, bre_ref, bim_ref, *, kb):
    inv = 1.0 / norm_ref[...]
    for t in range(kb):
        g2 = g2_ref[t]
        x = _dot(g2, _stack(are_ref[t], aim_ref[t]))
        xre, xim = x[:DFT_N2], x[DFT_N2:]
        f = _dot(g2, _stack(fre_ref[t], fim_ref[t])) * inv
        kre, kim = f[:DFT_N2], f[DFT_N2:]
        yre = (xre * kre - xim * kim).astype(BF16)
        yim = (xre * kim + xim * kre).astype(BF16)
        b = _dot(h2_ref[t], _stack(yre, yim))
        bre_ref[t] = b[:DFT_N2].astype(BF16)
        bim_ref[t] = b[DFT_N2:].astype(BF16)


def _conv_mid(tabs, are, aim, fre, fim, norm, kb):
    n1, _, c = are.shape
    gspec = pl.BlockSpec((kb, 2 * DFT_N2, 2 * DFT_N2), lambda i: (i, 0, 0))
    aspec = pl.BlockSpec((kb, DFT_N2, c), lambda i: (i, 0, 0))
    return pl.pallas_call(
        functools.partial(_conv_mid_kernel, kb=kb),
        grid=(n1 // kb,),
        in_specs=[gspec, gspec, aspec, aspec, aspec, aspec, pl.BlockSpec((1, c), lambda i: (0, 0))],
        out_specs=[aspec, aspec],
        out_shape=[jax.ShapeDtypeStruct((n1, DFT_N2, c), BF16)] * 2,
        compiler_params=_cparams(("parallel",)),
        name="conv_spectral",
    )(tabs["g2"], tabs["h2"], are, aim, fre, fim, norm)


def _idft1_kernel(c_ref, s_ref, bre_ref, bim_ref, o_ref, *, inv_n):
    y = (_dot(c_ref[...], bre_ref[...]) - _dot(s_ref[...], bim_ref[...])) * inv_n
    o_ref[...] = y.astype(o_ref.dtype)


def _idft1(c1h, s1h, bre, bim, cb):
    ko, n1 = c1h.shape
    w = bre.shape[1]
    n = n1 * DFT_N2
    ospec = pl.BlockSpec((ko, cb), lambda j: (0, j))
    return pl.pallas_call(
        functools.partial(_idft1_kernel, inv_n=1.0 / n),
        grid=(w // cb,),
        in_specs=[pl.BlockSpec((ko, n1), lambda j: (0, 0)), pl.BlockSpec((ko, n1), lambda j: (0, 0)),
                  pl.BlockSpec((n1, cb), lambda j: (0, j)), pl.BlockSpec((n1, cb), lambda j: (0, j))],
        out_specs=ospec,
        out_shape=jax.ShapeDtypeStruct((ko, w), BF16),
        compiler_params=_cparams(("parallel",)),
        name="idft_stage1",
    )(c1h, s1h, bre, bim)


def _hyena(z_all, tabs, emb, p, i, cb, kb):
    seq = z_all.shape[0]
    n1 = 2 * seq // DFT_N2
    c = HY_WIDTH
    x0, zb = _hy_pre(z_all, p["hy_conv_w"][i], p["hy_conv_b"][i], min(512, seq))
    kern_b, norm = _hy_filter(emb, p["hy_w1"][i], p["hy_b1"][i], p["hy_freq1"][i], p["hy_w2"][i],
                                 p["hy_b2"][i], p["hy_freq2"][i], p["hy_w3"][i], min(1024, seq))
    wide = DFT_N2 * c
    fre, fim = _dft1(kern_b.reshape(n1, wide), tabs["c1"], tabs["s1"], cb)
    are, aim = _dft1(zb.reshape(n1 // 2, wide), tabs["c1"][:, :n1 // 2], tabs["s1"][:, :n1 // 2], cb)
    bre, bim = _conv_mid(tabs, are.reshape(n1, DFT_N2, c), aim.reshape(n1, DFT_N2, c),
                         fre.reshape(n1, DFT_N2, c), fim.reshape(n1, DFT_N2, c), norm, kb)
    y = _idft1(tabs["c1"][:n1 // 2], tabs["s1"][:n1 // 2], bre.reshape(n1, wide), bim.reshape(n1, wide), cb)
    return y.reshape(seq, c), zb, x0


def _combine_kernel(x_ref, ya_ref, o0_ref, o1_ref, o2_ref, l0_ref, l1_ref, l2_ref, yc_ref,
                    hy_ref, hz_ref, hx_ref, skip_ref,
                    g0_ref, g1_ref, g2_ref, g3_ref, pa_ref, pb_ref, pc_ref, pd_ref, wo_ref, o_ref):
    l0, l1, l2 = l0_ref[...], l1_ref[...], l2_ref[...]
    mx = jnp.maximum(jnp.maximum(l0, l1), l2)
    e0, e1, e2 = jnp.exp(l0 - mx), jnp.exp(l1 - mx), jnp.exp(l2 - mx)
    yb = (e0 * o0_ref[...] + e1 * o1_ref[...] + e2 * o2_ref[...]) / (e0 + e1 + e2)
    f32 = lambda ref: ref[...].astype(F32)
    yd = f32(hx_ref) * (f32(hy_ref) + f32(hz_ref) * skip_ref[...])
    gate = lambda ref: _sigmoid(f32(ref))
    m = (gate(g0_ref) * _dot(ya_ref[...].astype(BF16), pa_ref[...])
         + gate(g1_ref) * _dot(yb.astype(BF16), pb_ref[...])
         + gate(g2_ref) * _dot(yc_ref[...].astype(BF16), pc_ref[...])
         + gate(g3_ref) * _dot(yd.astype(BF16), pd_ref[...]))
    o_ref[...] = x_ref[...] + _dot(m.astype(BF16), wo_ref[...])


def _combine(x, z, ya, dil_outs, yc, hyena, skip, pa, pb, pc, pd, wo, tm):
    s, d = x.shape
    rs = lambda w: pl.BlockSpec((tm, w), lambda i: (i, 0))
    gs = lambda b: pl.BlockSpec((tm, d), lambda i: (i, Z_GATE // d + b))
    ws = lambda a: pl.BlockSpec(a.shape, lambda i: (0, 0))
    (o0, l0), (o1, l1), (o2, l2) = dil_outs
    gw = DIL_HEADS * HEAD_DIM
    hy, hz, hx = hyena
    return pl.pallas_call(
        _combine_kernel,
        grid=(s // tm,),
        in_specs=[rs(d), rs(ya.shape[1]), rs(gw), rs(gw), rs(gw), rs(gw), rs(gw), rs(gw),
                  rs(yc.shape[1]), rs(HY_WIDTH), rs(HY_WIDTH), rs(HY_WIDTH), ws(skip), gs(0), gs(1), gs(2), gs(3),
                  ws(pa), ws(pb), ws(pc), ws(pd), ws(wo)],
        out_specs=rs(d),
        out_shape=jax.ShapeDtypeStruct((s, d), F32),
        compiler_params=_cparams(("parallel",)),
        name="combine",
    )(x, ya, o0, o1, o2, l0, l1, l2, yc, hy, hz, hx, skip, z, z, z, z, pa, pb, pc, pd, wo)


def _mlp_kernel(x_ref, g_ref, w1_ref, w2_ref, o_ref, h_ref, acc_ref):
    j = pl.program_id(1)

    @pl.when(j == 0)
    def _():
        h_ref[...] = (_rms(x_ref[...]) * g_ref[...]).astype(BF16)
        acc_ref[...] = jnp.zeros(acc_ref.shape, F32)

    a = jnp.maximum(_dot(h_ref[...], w1_ref[...]), 0.0)
    acc_ref[...] += _dot((a * a).astype(BF16), w2_ref[...])

    @pl.when(j == pl.num_programs(1) - 1)
    def _():
        o_ref[...] = x_ref[...] + acc_ref[...]


def _mlp(x, g, w1, w2, tm, tf):
    s, d = x.shape
    ff = w1.shape[1]
    return pl.pallas_call(
        _mlp_kernel,
        grid=(s // tm, ff // tf),
        in_specs=[pl.BlockSpec((tm, d), lambda i, j: (i, 0)),
                  pl.BlockSpec((1, d), lambda i, j: (0, 0)),
                  pl.BlockSpec((d, tf), lambda i, j: (0, j)),
                  pl.BlockSpec((tf, d), lambda i, j: (j, 0))],
        out_specs=pl.BlockSpec((tm, d), lambda i, j: (i, 0)),
        out_shape=jax.ShapeDtypeStruct((s, d), F32),
        scratch_shapes=[pltpu.VMEM((tm, d), BF16), pltpu.VMEM((tm, d), F32)],
        compiler_params=_cparams(("parallel", "arbitrary")),
        name="mlp",
    )(x, g.reshape(1, d), w1, w2)


def _permute_w_in(w):
    d = w.shape[0]
    pieces = [w[:, _O_DU:_O_GATE], w[:, _O_B:_O_C], w[:, _O_AQ:_O_AK], w[:, _O_C:_O_DU],
              w[:, _O_AV:_O_AR], w[:, _O_AR:_O_ALR], w[:, _O_AK:_O_AV], w[:, _O_ALR:_O_B],
              jnp.zeros((d, Z_GATE - Z_ALR - 2 * GLA_RANK), w.dtype), w[:, _O_GATE:_O_END]]
    return jnp.concatenate(pieces, axis=1).astype(BF16)


def _dil_bias_idx(tq, dil):
    rel = _skewed_rel(tq, tq + 2 * DIL_HALF, -1, -DIL_HALF)
    return _t5_bucket(rel * dil)[None].astype(np.int32)


def _diff_bias_idx(tq, tk):
    far_lo, far_hi = _diff_far_offsets(tq, tk)
    return np.stack([_t5_bucket(_skewed_rel(tk, tq, 1, o * tk)) for o in range(far_lo, far_hi + 1)]).astype(np.int32)


def _forward(x, p, *, t_diff, tk_diff, qc_diff, tq_dil, nt_dil, tb_gla, tm_proj, tn_proj, tm_row, tm_mlp, tf_mlp, cb_dft, kb_dft):
    seq = x.shape[0]
    depth = p["w_in"].shape[0]
    n_dil_bias = DIL_GROUPS * DIL_HEADS
    tq_dil = [min(tq, seq // dil) for tq, (_, dil) in zip(tq_dil, DIL_PATTERNS)]
    dil_bias = [_bias_tiles(p["t5_bias"], _dil_bias_idx(tq, dil), tq, tq + 2 * DIL_HALF, DIL_HEADS, gi * DIL_HEADS)
                for gi, (tq, (_, dil)) in enumerate(zip(tq_dil, DIL_PATTERNS))]
    t_diff, tk_diff = min(t_diff, seq), min(tk_diff, seq)
    diff_bias = _bias_tiles(p["t5_bias"], _diff_bias_idx(t_diff, tk_diff), tk_diff, t_diff, DIFF_HEADS,
                            n_dil_bias, LOG2E)
    tabs = _dft_tables(2 * seq // DFT_N2)
    emb = _hy_positions(seq)
    rep = lambda g, n: jnp.tile(g, n)
    for i in range(depth):
        z = _norm_matmul(x, p["norm1_g"][i], _permute_w_in(p["w_in"][i]), tm_proj, tn_proj)
        wg = [jnp.zeros((LANES, GLA_HEADS * GLA_DK), F32).at[j * GLA_RANK:(j + 1) * GLA_RANK].set(
            p["gla_gate_w"][i, j]).astype(BF16) for j in range(2)]
        gb = p["gla_gate_b"][i]
        o_fwd = _gla_scan(z, wg[0], gb[0:1], tb_gla, False)
        ya = _gla_scan(z, wg[1], gb[1:2], tb_gla, True, fin=(o_fwd, p["gla_norm_g"][i]))
        bq, bk, bv, cq, ck, cv = _prep(
            z, jnp.repeat(p["dil_qnorm_g"][i], DIL_HEADS, axis=0).reshape(-1),
            jnp.repeat(p["dil_knorm_g"][i], DIL_HEADS, axis=0).reshape(-1),
            rep(p["diff_qnorm_g"][i], 2 * DIFF_HEADS), rep(p["diff_knorm_g"][i], 2 * DIFF_HEADS), tm_row)
        dil_outs = [_dil_attn(bq, bk, bv, dil_bias[gi], gi, dil, tq_dil[gi], max(1, min(nt_dil[gi], seq // (tq_dil[gi] * dil))))
                    for gi, (_, dil) in enumerate(DIL_PATTERNS)]
        lam_init = 0.8 - 0.6 * math.exp(-0.3 * i)
        yc = _diff_attn(cq, ck, cv, diff_bias, p["diff_lambda"][i], p["diff_subln_g"][i], lam_init,
                        t_diff, tk_diff, qc_diff)
        yd = _hyena(z, tabs, emb, p, i, cb_dft, kb_dft)
        x = _combine(x, z, ya, dil_outs, yc, yd, p["hy_skip"][i][None],
                     p["proj_a"][i].astype(BF16), p["proj_b"][i].astype(BF16),
                     p["proj_c"][i].astype(BF16), p["proj_d"][i].astype(BF16), p["w_out"][i].astype(BF16),
                     tm_row)
        x = _mlp(x, p["norm2_g"][i], p["mlp_w1"][i].astype(BF16), p["mlp_w2"][i].astype(BF16), tm_mlp, tf_mlp)
    return x


def kernel(x, t5_bias, norm1_g, w_in, gla_gate_w, gla_gate_b, gla_norm_g, dil_qnorm_g, dil_knorm_g,
           diff_qnorm_g, diff_knorm_g, diff_lambda, diff_subln_g, hy_conv_w, hy_conv_b, hy_w1, hy_b1,
           hy_freq1, hy_w2, hy_b2, hy_freq2, hy_w3, hy_skip, proj_a, proj_b, proj_c, proj_d, w_out,
           norm2_g, mlp_w1, mlp_w2):
    p = dict(t5_bias=t5_bias, norm1_g=norm1_g, w_in=w_in, gla_gate_w=gla_gate_w, gla_gate_b=gla_gate_b,
             gla_norm_g=gla_norm_g, dil_qnorm_g=dil_qnorm_g, dil_knorm_g=dil_knorm_g,
             diff_qnorm_g=diff_qnorm_g, diff_knorm_g=diff_knorm_g, diff_lambda=diff_lambda,
             diff_subln_g=diff_subln_g, hy_conv_w=hy_conv_w, hy_conv_b=hy_conv_b, hy_w1=hy_w1, hy_b1=hy_b1,
             hy_freq1=hy_freq1, hy_w2=hy_w2, hy_b2=hy_b2, hy_freq2=hy_freq2, hy_w3=hy_w3, hy_skip=hy_skip,
             proj_a=proj_a, proj_b=proj_b, proj_c=proj_c, proj_d=proj_d, w_out=w_out, norm2_g=norm2_g,
             mlp_w1=mlp_w1, mlp_w2=mlp_w2)
    b, s, d = x.shape
    outs = [_forward(x[bi], p, t_diff=2048, tk_diff=1024, qc_diff=(256, 768, 768, 256), tq_dil=(256, 256, 128), nt_dil=(4, 1, 1), tb_gla=512, tm_proj=2048, tn_proj=1024, tm_row=512,
                     tm_mlp=1024, tf_mlp=1024, cb_dft=8192, kb_dft=16) for bi in range(b)]
    return jnp.stack(outs)
```

```python
import functools
import math

import jax
import jax.numpy as jnp
import numpy as np
from jax import lax
from jax.experimental import pallas as pl
from jax.experimental.pallas import tpu as pltpu

F32 = jnp.float32
BF16 = jnp.bfloat16

D_MODEL = 1024
HEAD_DIM = 64
GLA_HEADS = 4
GLA_DK = 64
GLA_DV = 128
GLA_RANK = 16
GLA_TAU = 16.0
GLA_CHUNK = 64
DIL_PATTERNS = ((128, 1), (512, 4), (2048, 16))
DIL_GROUPS = 3
DIL_HEADS = 4
DIL_HALF = 64
DIFF_HEADS = 4
DIFF_DV = 128
DIFF_VT = DIFF_DV + 16
HY_WIDTH = 512
HY_EMB = 33
HY_FFN = 64
HY_DECAY_TARGET = 1e-2
HY_FAST_DECAY = 0.3
HY_SLOW_DECAY = 1.5
T5_BUCKETS = 32
T5_MAX_DIST = 1024
N_BIAS_HEADS = 16
D_FF = 4096
RMS_EPS = 1e-6
LOG2E = math.log2(math.e)

LANES = 128
HALO_ROWS = 16
VMEM_LIMIT = 48 * 1024 * 1024

Z_DU = 0
Z_BQ, Z_BK, Z_BV = 1536, 2304, 3072
Z_AQ = 3840
Z_CQ, Z_CK, Z_CV = 4096, 4608, 5120
Z_AV, Z_AR = 5632, 6144
Z_AK = 6656
Z_ALR = 6912
Z_GATE = 7168
Z_COLS = 11264

_O_AQ, _O_AK, _O_AV, _O_AR, _O_ALR, _O_B, _O_C, _O_DU, _O_GATE, _O_END = (
    0, 256, 512, 1024, 1536, 1568, 3872, 5408, 6944, 11040)

DFT_N2 = 128


def _cparams(sem):
    return pltpu.CompilerParams(dimension_semantics=sem, vmem_limit_bytes=VMEM_LIMIT)


def _dot(a, b):
    return jnp.dot(a, b, preferred_element_type=F32)


def _dot_nt(a, b):
    return lax.dot_general(a, b, (((1,), (1,)), ((), ())), preferred_element_type=F32)


def _dot_tn(a, b):
    return lax.dot_general(a, b, (((0,), (0,)), ((), ())), preferred_element_type=F32)


def _split(x):
    hi = x.astype(BF16)
    lo = (x - hi.astype(F32)).astype(BF16)
    return hi, lo


def _dot3(a, b):
    ah, al = _split(a)
    bh, bl = _split(b)
    return _dot(ah, bh) + _dot(ah, bl) + _dot(al, bh)


def _rms(x):
    return x * lax.rsqrt(jnp.mean(x * x, axis=-1, keepdims=True) + RMS_EPS)


def _sigmoid(x):
    return 0.5 * jnp.tanh(0.5 * x) + 0.5


def _norm_matmul_kernel(x_ref, g_ref, w_ref, o_ref, h_ref):
    @pl.when(pl.program_id(1) == 0)
    def _():
        h_ref[...] = (_rms(x_ref[...]) * g_ref[...]).astype(BF16)

    o_ref[...] = _dot(h_ref[...], w_ref[...]).astype(o_ref.dtype)


def _norm_matmul(x, g, w, tm, tn):
    s, d = x.shape
    n = w.shape[1]
    return pl.pallas_call(
        _norm_matmul_kernel,
        grid=(s // tm, n // tn),
        in_specs=[pl.BlockSpec((tm, d), lambda i, j: (i, 0)),
                  pl.BlockSpec((1, d), lambda i, j: (0, 0)),
                  pl.BlockSpec((d, tn), lambda i, j: (0, j))],
        out_specs=pl.BlockSpec((tm, tn), lambda i, j: (i, j)),
        out_shape=jax.ShapeDtypeStruct((s, n), BF16),
        scratch_shapes=[pltpu.VMEM((tm, d), BF16)],
        compiler_params=_cparams(("parallel", "arbitrary")),
        name="in_proj",
    )(x, g.reshape(1, d), w)


def _group_norm(x, e, gain):
    hi, lo = _split(x * x)
    ew = e.shape[0]
    ms = jnp.concatenate([_dot(hi[:, c:c + ew], e) + _dot(lo[:, c:c + ew], e)
                          for c in range(0, x.shape[1], ew)], axis=1) * (1.0 / HEAD_DIM)
    return x * lax.rsqrt(ms + RMS_EPS) * gain


def _prep_kernel(bq_ref, bk_ref, bv_ref, cq_ref, ck_ref, cv_ref, e_ref,
                 gbq_ref, gbk_ref, gcq_ref, gck_ref,
                 obq_ref, obk_ref, obv_ref, ocq_ref, ock_ref, ocv_ref):
    scale = HEAD_DIM ** -0.5
    eb = ec = e_ref[...]
    f32 = lambda ref: ref[...].astype(F32)
    obq_ref[...] = (_group_norm(f32(bq_ref), eb, gbq_ref[...]) * scale).astype(BF16)
    obk_ref[...] = _group_norm(f32(bk_ref), eb, gbk_ref[...]).astype(BF16)
    obv_ref[...] = bv_ref[...]
    ocq_ref[...] = (_group_norm(f32(cq_ref), ec, gcq_ref[...]) * (scale * LOG2E)).T.astype(BF16)
    ock_ref[...] = _group_norm(f32(ck_ref), ec, gck_ref[...]).astype(BF16)
    cvt = f32(cv_ref).T
    ones = jnp.ones((DIFF_VT - DIFF_DV, cvt.shape[1]), F32)
    ocv_ref[...] = jnp.concatenate(
        [t for h in range(DIFF_HEADS) for t in (cvt[h * DIFF_DV:(h + 1) * DIFF_DV], ones)], axis=0).astype(BF16)


def _block_diag_ones(width):
    idx = np.arange(width) // HEAD_DIM
    return jnp.asarray(idx[:, None] == idx[None, :], dtype=BF16)


def _prep(z, gbq, gbk, gcq, gck, tm):
    s = z.shape[0]
    wb, wc = DIL_GROUPS * DIL_HEADS * HEAD_DIM, DIFF_HEADS * 2 * HEAD_DIM
    zspec = lambda w, off: pl.BlockSpec((tm, w), lambda i: (i, off // w))
    cspec = lambda r, c: pl.BlockSpec((r, c), lambda i: (0, 0))
    ospec = lambda w: pl.BlockSpec((tm, w), lambda i: (i, 0))
    tspec = lambda w: pl.BlockSpec((w, tm), lambda i: (0, i))
    return pl.pallas_call(
        _prep_kernel,
        grid=(s // tm,),
        in_specs=[zspec(wb, Z_BQ), zspec(wb, Z_BK), zspec(wb, Z_BV),
                  zspec(wc, Z_CQ), zspec(wc, Z_CK), zspec(wc, Z_CV),
                  cspec(2 * LANES, 2 * LANES),
                  cspec(1, wb), cspec(1, wb), cspec(1, wc), cspec(1, wc)],
        out_specs=[ospec(wb), ospec(wb), ospec(wb), tspec(wc), ospec(wc), tspec(DIFF_HEADS * DIFF_VT)],
        out_shape=[jax.ShapeDtypeStruct((s, wb), BF16)] * 3
        + [jax.ShapeDtypeStruct((wc, s), BF16), jax.ShapeDtypeStruct((s, wc), BF16),
           jax.ShapeDtypeStruct((DIFF_HEADS * DIFF_VT, s), BF16)],
        compiler_params=_cparams(("parallel",)),
        name="qk_prep",
    )(z, z, z, z, z, z, _block_diag_ones(2 * LANES),
      gbq.reshape(1, wb), gbk.reshape(1, wb), gcq.reshape(1, wc), gck.reshape(1, wc))


def _t5_bucket(rel):
    half = T5_BUCKETS // 2
    max_exact = half // 2
    ret = np.where(rel > 0, half, 0)
    n = np.abs(rel)
    nf = np.maximum(n, 1).astype(np.float64)
    large = max_exact + (np.log(nf / max_exact) / math.log(T5_MAX_DIST / max_exact)
                         * (half - max_exact)).astype(np.int64)
    large = np.minimum(large, half - 1)
    return ret + np.where(n < max_exact, n, large)


def _bias_kernel(tab_ref, rng_ref, idx_ref, o_ref, p_ref, *, head_base, out_scale, chunk):
    t = pl.program_id(0)
    col = head_base + pl.program_id(1)
    rows, width = o_ref.shape[2:]
    slabs = width // LANES
    lo, hi = rng_ref[t, 0], rng_ref[t, 1]

    def one_chunk(ci, carry):
        rs = pl.ds(pl.multiple_of(ci * chunk, chunk), chunk)
        idx = idx_ref[0, rs, :]
        p_ref[rs, :] = lax.fori_loop(
            lo, hi + 1, lambda b, acc: jnp.where(idx == b, tab_ref[b, col] * out_scale, acc),
            jnp.zeros((chunk, LANES), F32))
        return carry

    lax.fori_loop(0, p_ref.shape[0] // chunk, one_chunk, 0)
    for c in range(slabs):
        off = LANES * (slabs - 1 - c)
        o_ref[0, 0, :, c * LANES:(c + 1) * LANES] = p_ref[off:off + rows, :]


def _skewed_rel(rows, width, sign, base):
    slabs = width // LANES
    rho = np.arange(rows + LANES * (slabs - 1))[:, None]
    lane = np.arange(LANES)[None, :]
    return sign * (rho - LANES * (slabs - 1) - lane) + base


def _bias_tiles(t5_bias, idx, rows, width, n_heads, head_base, out_scale=1.0, chunk=64):
    nt, rp, _ = idx.shape
    ranges = np.stack([idx.reshape(nt, -1).min(axis=1), idx.reshape(nt, -1).max(axis=1)], axis=1).astype(np.int32)
    return pl.pallas_call(
        functools.partial(_bias_kernel, head_base=head_base, out_scale=out_scale, chunk=chunk),
        grid=(nt, n_heads),
        in_specs=[pl.BlockSpec(memory_space=pltpu.SMEM), pl.BlockSpec(memory_space=pltpu.SMEM),
                  pl.BlockSpec((1, rp, LANES), lambda t, h: (t, 0, 0))],
        out_specs=pl.BlockSpec((1, 1, rows, width), lambda t, h: (t, h, 0, 0)),
        out_shape=jax.ShapeDtypeStruct((nt, n_heads, rows, width), F32),
        scratch_shapes=[pltpu.VMEM((rp, LANES), F32)],
        compiler_params=_cparams(("parallel", "parallel")),
        name="t5_bias_tiles",
    )(t5_bias, jnp.asarray(ranges), jnp.asarray(idx))


def _diff_attn_kernel(qt_ref, k_ref, vt_ref, bias_ref, lam_ref, g_ref, o_ref,
                      qa_ref, qb_ref, m_ref, acc_ref, s_ref, *, lam_init, qc, far_lo, far_hi):
    j = pl.program_id(2)
    t = qt_ref.shape[1]

    @pl.when(j == 0)
    def _():
        qt = qt_ref[...]
        row = lax.broadcasted_iota(jnp.int32, qt.shape, 0)
        qa_ref[...] = jnp.where(row < HEAD_DIM, qt, jnp.zeros_like(qt))
        qb_ref[...] = jnp.where(row >= HEAD_DIM, qt, jnp.zeros_like(qt))
        m_ref[...] = jnp.full(m_ref.shape, -jnp.inf, F32)
        acc_ref[...] = jnp.zeros(acc_ref.shape, F32)

    starts = [sum(qc[:u]) for u in range(len(qc))]
    cuts = [slice(a, a + w) for a, w in zip(starts, qc)]
    chains = [(0, cs) for cs in cuts] + [(1, cs) for cs in reversed(cuts)]
    q_refs = (qa_ref, qb_ref)

    def step(far):
        k = k_ref[...]
        vt = vt_ref[...]
        if far:
            const = bias_ref[0, 0, 0:1, 0:1]
            scores = lambda c, cols: _dot(k, q_refs[c][:, cols])
        else:
            const = 0.0
            scores = lambda c, cols: _dot(k, q_refs[c][:, cols]) + bias_ref[0, 0, :, cols]
        width = lambda n: chains[n][1].stop - chains[n][1].start
        s_ref[0, :, :width(0)] = scores(*chains[0])
        for n, (c, cols) in enumerate(chains):
            if n + 1 < len(chains):
                s_ref[(n + 1) % 2, :, :width(n + 1)] = scores(*chains[n + 1])
            s = s_ref[n % 2, :, :width(n)]
            m_old = m_ref[c, :, cols]
            m_new = jnp.maximum(m_old, jnp.max(s, axis=0, keepdims=True) + const)
            alpha = jnp.exp2(m_old - m_new)
            p = jnp.exp2(s - (m_new - const)).astype(BF16)
            acc_ref[c, :, cols] = alpha * acc_ref[c, :, cols] + _dot(vt, p)
            m_ref[c, :, cols] = m_new

    off = j - pl.program_id(1) * (t // k_ref.shape[0])
    is_far = jnp.logical_or(off <= far_lo, off >= far_hi)
    pl.when(is_far)(lambda: step(True))
    pl.when(jnp.logical_not(is_far))(lambda: step(False))

    @pl.when(j == pl.num_programs(2) - 1)
    def _():
        lp = lam_ref[...]
        lam = (jnp.exp(jnp.sum(lp[0:1] * lp[1:2], axis=-1, keepdims=True))
               - jnp.exp(jnp.sum(lp[2:3] * lp[3:4], axis=-1, keepdims=True)) + lam_init)
        a0 = acc_ref[0]
        a1 = acc_ref[1]
        o0 = a0[:DIFF_DV] / a0[DIFF_DV:DIFF_DV + 1]
        o1 = a1[:DIFF_DV] / a1[DIFF_DV:DIFF_DV + 1]
        att = o0 - lam * o1
        y = att * lax.rsqrt(jnp.mean(att * att, axis=0, keepdims=True) + RMS_EPS)
        o_ref[...] = y.T * g_ref[...] * (1.0 - lam_init)


def _diff_far_offsets(tq, tk):
    far_lo = (-T5_MAX_DIST - tk + 1) // tk
    far_hi = -(-(T5_MAX_DIST + tq - 1) // tk)
    return far_lo, far_hi


def _diff_attn(cqt, ck, cvt, bias_tiles, lam_p, subln_g, lam_init, t, tk, qc):
    s = ck.shape[0]
    far_lo, far_hi = _diff_far_offsets(t, tk)
    ratio = t // tk
    w = 2 * HEAD_DIM
    return pl.pallas_call(
        functools.partial(_diff_attn_kernel, lam_init=lam_init, qc=qc, far_lo=far_lo, far_hi=far_hi),
        grid=(DIFF_HEADS, s // t, s // tk),
        in_specs=[pl.BlockSpec((w, t), lambda h, i, j: (h, i)),
                  pl.BlockSpec((tk, w), lambda h, i, j: (j, h)),
                  pl.BlockSpec((DIFF_VT, tk), lambda h, i, j: (h, j)),
                  pl.BlockSpec((1, 1, tk, t),
                               lambda h, i, j: (jnp.clip(j - i * ratio, far_lo, far_hi) - far_lo, h, 0, 0)),
                  pl.BlockSpec((4, HEAD_DIM), lambda h, i, j: (0, 0)),
                  pl.BlockSpec((1, DIFF_DV), lambda h, i, j: (0, 0))],
        out_specs=pl.BlockSpec((t, DIFF_DV), lambda h, i, j: (i, h)),
        out_shape=jax.ShapeDtypeStruct((s, DIFF_HEADS * DIFF_DV), F32),
        scratch_shapes=[pltpu.VMEM((w, t), BF16), pltpu.VMEM((w, t), BF16),
                        pltpu.VMEM((2, 1, t), F32), pltpu.VMEM((2, DIFF_VT, t), F32),
                        pltpu.VMEM((2, tk, max(qc)), F32)],
        compiler_params=_cparams(("parallel", "parallel", "arbitrary")),
        name="diff_attn",
    )(cqt, ck, cvt, bias_tiles, lam_p, subln_g.reshape(1, DIFF_DV))


def _dil_kernel(q_ref, kp_ref, kc_ref, kn_ref, vp_ref, vc_ref, vn_ref, bias_ref, o_ref, lse_ref,
                qs_ref, ks_ref, vs_ref, os_ref, ls_ref, *, tq, dil, nt, m_len):
    n = pl.program_id(0)
    hd = DIL_HALF * dil
    body_rows = tq * dil * nt
    halves = DIL_HEADS * HEAD_DIM // LANES

    def put(dst, rows, src_ref):
        x = src_ref[...].astype(F32)
        for t in range(halves):
            dst[t, rows, :] = x[:, t * LANES:(t + 1) * LANES]

    def strided(src, r, count):
        return jnp.concatenate([src[t, pl.ds(r, count, stride=dil), :] for t in range(halves)], axis=1)

    put(qs_ref, slice(0, body_rows), q_ref)
    for dst, (p_ref, c_ref, n_ref) in ((ks_ref, (kp_ref, kc_ref, kn_ref)), (vs_ref, (vp_ref, vc_ref, vn_ref))):
        put(dst, slice(0, hd), p_ref)
        put(dst, slice(hd, hd + body_rows), c_ref)
        put(dst, slice(hd + body_rows, 2 * hd + body_rows), n_ref)
    tk = tq + 2 * DIL_HALF
    a = lax.broadcasted_iota(jnp.int32, (tq, tk), 0)
    c = lax.broadcasted_iota(jnp.int32, (tq, tk), 1)
    delta = c - DIL_HALF - a
    in_band = jnp.concatenate([jnp.abs(delta) <= DIL_HALF] * DIL_HEADS, axis=0)
    col = lax.broadcasted_iota(jnp.int32, (1, tk), 1)
    lane = lax.broadcasted_iota(jnp.int32, (tq, DIL_HEADS * HEAD_DIM), 1)
    head_masks = [(lane // HEAD_DIM) == h for h in range(DIL_HEADS)]

    def one_tile(it, carry):
        u = it // dil
        r = it % dil + u * (tq * dil)
        kpos = (n * nt + u) * tq - DIL_HALF + col
        valid = jnp.logical_and(in_band, jnp.logical_and(kpos >= 0, kpos < m_len))
        q = strided(qs_ref, r, tq).astype(BF16)
        k = strided(ks_ref, r, tk).astype(BF16)
        v = strided(vs_ref, r, tk).astype(BF16)
        qs = jnp.concatenate([jnp.where(hm, q, jnp.zeros_like(q)) for hm in head_masks], axis=0)
        s = _dot_nt(qs, k) + bias_ref[0].reshape(DIL_HEADS * tq, tk)
        s = jnp.where(valid, s, -1e30)
        m = jnp.max(s, axis=-1, keepdims=True)
        e = jnp.exp(s - m)
        l = jnp.sum(e, axis=-1, keepdims=True)
        oh = _dot((e / l).astype(BF16), v)
        lse = m + jnp.log(l)
        o = jnp.zeros(q.shape, F32)
        lse_o = jnp.zeros(q.shape, F32)
        for h, hm in enumerate(head_masks):
            rows = slice(h * tq, (h + 1) * tq)
            o = jnp.where(hm, oh[rows], o)
            lse_o = jnp.where(hm, lse[rows], lse_o)
        for t in range(halves):
            os_ref[t, pl.ds(r, tq, stride=dil), :] = o[:, t * LANES:(t + 1) * LANES]
            ls_ref[t, pl.ds(r, tq, stride=dil), :] = lse_o[:, t * LANES:(t + 1) * LANES]
        return carry

    lax.fori_loop(0, dil * nt, one_tile, 0, unroll=min(dil * nt, 4))
    o_ref[...] = jnp.concatenate([os_ref[t] for t in range(halves)], axis=1)
    lse_ref[...] = jnp.concatenate([ls_ref[t] for t in range(halves)], axis=1)


def _dil_attn(bq, bk, bv, bias, gi, dil, tq, nt):
    s = bq.shape[0]
    m_len = s // dil
    gw = DIL_HEADS * HEAD_DIM
    rows = tq * dil * nt
    hd = DIL_HALF * dil
    hb = rows // hd
    last = s // hd - 1
    qspec = pl.BlockSpec((rows, gw), lambda n: (n, gi))
    pspec = pl.BlockSpec((hd, gw), lambda n: (jnp.maximum(n * hb - 1, 0), gi))
    nspec = pl.BlockSpec((hd, gw), lambda n: (jnp.minimum((n + 1) * hb, last), gi))
    ospec = pl.BlockSpec((rows, gw), lambda n: (n, 0))
    return pl.pallas_call(
        functools.partial(_dil_kernel, tq=tq, dil=dil, nt=nt, m_len=m_len),
        grid=(s // rows,),
        in_specs=[qspec, pspec, qspec, nspec, pspec, qspec, nspec,
                  pl.BlockSpec((1, DIL_HEADS, tq, tq + 2 * DIL_HALF), lambda n: (0, 0, 0, 0))],
        out_specs=[ospec, ospec],
        out_shape=[jax.ShapeDtypeStruct((s, gw), F32)] * 2,
        scratch_shapes=[pltpu.VMEM((gw // LANES, rows, LANES), F32),
                        pltpu.VMEM((gw // LANES, rows + 2 * hd, LANES), F32),
                        pltpu.VMEM((gw // LANES, rows + 2 * hd, LANES), F32),
                        pltpu.VMEM((gw // LANES, rows, LANES), F32),
                        pltpu.VMEM((gw // LANES, rows, LANES), F32)],
        compiler_params=_cparams(("parallel",)),
        name=f"dil_attn_g{gi}",
    )(bq, bk, bk, bk, bv, bv, bv, bias)


def _gla_kernel(*refs, reverse, tb, finalize):
    if finalize:
        (q_ref, k_ref, v_ref, lr_ref, wg_ref, gb_ref, tri_ref, ofwd_ref, r_ref, ng_ref,
         o_ref, s_ref, oacc_ref) = refs
    else:
        q_ref, k_ref, v_ref, lr_ref, wg_ref, gb_ref, tri_ref, o_ref, s_ref = refs
        oacc_ref = o_ref
    cw = GLA_CHUNK
    qk = GLA_HEADS * GLA_DK
    vw = GLA_HEADS * GLA_DV

    @pl.when(pl.program_id(0) == 0)
    def _():
        s_ref[...] = jnp.zeros(s_ref.shape, F32)

    logits = _dot(lr_ref[...], wg_ref[...]) + gb_ref[...]
    g = (jnp.minimum(logits, 0.0) - jnp.log(1.0 + jnp.exp(-jnp.abs(logits)))) * (1.0 / GLA_TAU)
    ghi, glo = _split(g)
    tri = tri_ref[...]
    b = _dot(tri, ghi) + _dot(tri, glo)
    qg = (q_ref[...].astype(F32) * (GLA_DK ** -0.5) * jnp.exp(b)).astype(BF16)
    k = k_ref[...].astype(F32)
    kg = (k * jnp.exp(-b)).astype(BF16)
    v = v_ref[...].astype(BF16)

    lane_q = lax.broadcasted_iota(jnp.int32, (cw, qk), 1)
    rr = lax.broadcasted_iota(jnp.int32, (GLA_HEADS * cw, cw), 0)
    cc = lax.broadcasted_iota(jnp.int32, (GLA_HEADS * cw, cw), 1)
    tt = rr % cw
    amask = (cc > tt) if reverse else (cc <= tt)
    srow = lax.broadcasted_iota(jnp.int32, (vw, qk), 0) // GLA_DV
    scol = lax.broadcasted_iota(jnp.int32, (vw, qk), 1) // GLA_DK
    bdmask = srow == scol

    n_chunks = tb // cw
    order = range(n_chunks - 1, -1, -1) if reverse else range(n_chunks)
    for ci in order:
        rows = slice(ci * cw, (ci + 1) * cw)
        bc = b[rows]
        b_end = bc[0:1] if reverse else bc[cw - 1:cw]
        kdec = (k[rows] * jnp.exp(b_end - bc)).astype(BF16)
        qg_c = qg[rows]
        qs = jnp.concatenate(
            [jnp.where((lane_q // GLA_DK) == h, qg_c, jnp.zeros_like(qg_c)) for h in range(GLA_HEADS)], axis=0)
        a = jnp.where(amask, _dot_nt(qs, kg[rows]), 0.0)
        obig = _dot(a.astype(BF16), v[rows])
        o_intra = jnp.concatenate(
            [obig[h * cw:(h + 1) * cw, h * GLA_DV:(h + 1) * GLA_DV] for h in range(GLA_HEADS)], axis=1)
        state = s_ref[...]
        oacc_ref[rows, :] = o_intra + _dot_nt(qg_c, state.astype(BF16))
        ds = _dot_tn(v[rows], kdec)
        s_ref[...] = jnp.where(bdmask, jnp.exp(b_end) * state + ds, 0.0)

    if finalize:
        o = ofwd_ref[...] + oacc_ref[...]
        r = r_ref[...].astype(F32)
        outs = []
        for h in range(GLA_HEADS):
            sl = slice(h * GLA_DV, (h + 1) * GLA_DV)
            outs.append(_rms(o[:, sl]) * ng_ref[...] * (r[:, sl] * _sigmoid(r[:, sl])))
        o_ref[...] = jnp.concatenate(outs, axis=1)


def _chunk_tri(tb, reverse):
    i = np.arange(tb)
    same = (i[:, None] // GLA_CHUNK) == (i[None, :] // GLA_CHUNK)
    tri = (i[None, :] >= i[:, None]) if reverse else (i[None, :] <= i[:, None])
    return jnp.asarray(same & tri, dtype=BF16)


def _gla_scan(z, wg, gb, tb, reverse, fin=None):
    s = z.shape[0]
    nb = s // tb
    qk = GLA_HEADS * GLA_DK
    vw = GLA_HEADS * GLA_DV
    blk = (lambda i: nb - 1 - i) if reverse else (lambda i: i)
    zspec = lambda w, off: pl.BlockSpec((tb, w), lambda i: (blk(i), off // w))
    cspec = lambda r, c: pl.BlockSpec((r, c), lambda i: (0, 0))
    in_specs = [zspec(qk, Z_AQ), zspec(qk, Z_AK), zspec(vw, Z_AV), zspec(LANES, Z_ALR),
                cspec(LANES, qk), cspec(1, qk), cspec(tb, tb)]
    args = [z, z, z, z, wg, gb, _chunk_tri(tb, reverse)]
    scratch = [pltpu.VMEM((vw, qk), F32)]
    if fin is not None:
        o_fwd, norm_g = fin
        in_specs += [pl.BlockSpec((tb, vw), lambda i: (blk(i), 0)), zspec(vw, Z_AR), cspec(1, GLA_DV)]
        args += [o_fwd, z, norm_g.reshape(1, GLA_DV)]
        scratch.append(pltpu.VMEM((tb, vw), F32))
    return pl.pallas_call(
        functools.partial(_gla_kernel, reverse=reverse, tb=tb, finalize=fin is not None),
        grid=(nb,),
        in_specs=in_specs,
        out_specs=pl.BlockSpec((tb, vw), lambda i: (blk(i), 0)),
        out_shape=jax.ShapeDtypeStruct((s, vw), F32),
        scratch_shapes=scratch,
        compiler_params=_cparams(("arbitrary",)),
        name="gla_bwd" if reverse else "gla_fwd",
    )(*args)


def _hy_pre_kernel(u_ref, up_ref, un_ref, w_ref, b_ref, x0_ref, zb_ref, *, tm):
    i = pl.program_id(0)
    u = u_ref[...].astype(F32)
    row = lax.broadcasted_iota(jnp.int32, u.shape, 0)
    prev_row = jnp.where(i == 0, 0.0, up_ref[...].astype(F32)[HALO_ROWS - 1:HALO_ROWS, :])
    next_row = jnp.where(i == pl.num_programs(0) - 1, 0.0, un_ref[...].astype(F32)[0:1, :])
    u_prev = jnp.where(row == 0, prev_row, pltpu.roll(u, 1, axis=0))
    u_next = jnp.where(row == tm - 1, next_row, pltpu.roll(u, tm - 1, axis=0))
    y = b_ref[...] + u_prev * w_ref[0:1] + u * w_ref[1:2] + u_next * w_ref[2:3]
    x0_ref[...] = y[:, :HY_WIDTH].astype(BF16)
    zb_ref[...] = (y[:, HY_WIDTH:2 * HY_WIDTH] * y[:, 2 * HY_WIDTH:]).astype(BF16)


def _hy_pre(z, conv_w, conv_b, tm):
    s = z.shape[0]
    w = 3 * HY_WIDTH
    nr = s // HALO_ROWS
    ospec = pl.BlockSpec((tm, HY_WIDTH), lambda i: (i, 0))
    return pl.pallas_call(
        functools.partial(_hy_pre_kernel, tm=tm),
        grid=(s // tm,),
        in_specs=[pl.BlockSpec((tm, w), lambda i: (i, 0)),
                  pl.BlockSpec((HALO_ROWS, w), lambda i: (jnp.maximum(i * (tm // HALO_ROWS) - 1, 0), 0)),
                  pl.BlockSpec((HALO_ROWS, w), lambda i: (jnp.minimum((i + 1) * (tm // HALO_ROWS), nr - 1), 0)),
                  pl.BlockSpec((3, w), lambda i: (0, 0)),
                  pl.BlockSpec((1, w), lambda i: (0, 0))],
        out_specs=[ospec, ospec],
        out_shape=[jax.ShapeDtypeStruct((s, HY_WIDTH), BF16)] * 2,
        compiler_params=_cparams(("parallel",)),
        name="hyena_pre",
    )(z, z, z, conv_w, conv_b.reshape(1, w))


def _hy_filter_kernel(emb_ref, w1_ref, b1_ref, f1_ref, w2_ref, b2_ref, f2_ref, w3_ref, dl_ref,
                      kb_ref, norm_ref, *, tl, seq):
    i = pl.program_id(0)
    half = tl // 2
    emb = emb_ref[...]
    x = jnp.concatenate([emb[:half], emb[half:]], axis=1)
    h = jnp.sin(f1_ref[...] * (_dot3(x, w1_ref[...]) + b1_ref[...]))
    h = jnp.sin(f2_ref[...] * (_dot3(h, w2_ref[...]) + b2_ref[...]))
    h = _dot3(h, w3_ref[...])
    back = i * tl >= seq
    pick = lambda y: jnp.where(back, y[:, HY_WIDTH:], y[:, :HY_WIDTH])
    h = jnp.concatenate([pick(h[:, :2 * HY_WIDTH]), pick(h[:, 2 * HY_WIDTH:])], axis=0)
    h = h * jnp.exp(-emb[:, 0:1] * dl_ref[...])
    row = i * tl + lax.broadcasted_iota(jnp.int32, h.shape, 0)
    h = jnp.where(row == seq, 0.0, h)
    kb_ref[...] = h.astype(BF16)

    @pl.when(i == 0)
    def _():
        norm_ref[...] = jnp.zeros(norm_ref.shape, F32)

    norm_ref[...] += jnp.sum(jnp.abs(h), axis=0, keepdims=True)


def _hy_positions(seq):
    t = np.linspace(0.0, 1.0, seq, dtype=np.float32)[:, None]
    bands = (HY_EMB - 1) // 2
    freqs = np.linspace(1e-4, bands - 1, bands, dtype=np.float32)[None]
    w = (np.float32(2.0 * math.pi) * np.arange(seq, dtype=np.float32)[:, None] / np.float32(seq))
    zf = np.concatenate([t, np.cos(freqs * w), -np.sin(freqs * w)], axis=-1).astype(np.float32)
    pos = np.concatenate([np.arange(seq), [0], np.arange(seq - 1, 0, -1)])
    return jnp.asarray(np.pad(zf[pos], ((0, 0), (0, LANES - HY_EMB))))


def _block_diag2(w):
    z = jnp.zeros_like(w)
    return jnp.concatenate([jnp.concatenate([w, z], axis=1), jnp.concatenate([z, w], axis=1)], axis=0)


def _hy_filter(emb, w1, b1, fr1, w2, b2, fr2, w3, tl):
    n = emb.shape[0]
    seq = n // 2
    w1p = jnp.pad(w1, ((0, LANES - HY_EMB), (0, 0)))
    twice = lambda v: jnp.concatenate([v, v])[None]
    min_decay = math.log(HY_DECAY_TARGET) / HY_SLOW_DECAY
    max_decay = math.log(HY_DECAY_TARGET) / HY_FAST_DECAY
    deltas = jnp.asarray(np.abs(np.linspace(min_decay, max_decay, HY_WIDTH, dtype=np.float32))[None])
    cs = lambda r, c: pl.BlockSpec((r, c), lambda i: (0, 0))
    return pl.pallas_call(
        functools.partial(_hy_filter_kernel, tl=tl, seq=seq),
        grid=(n // tl,),
        in_specs=[pl.BlockSpec((tl, LANES), lambda i: (i, 0)),
                  cs(2 * LANES, LANES), cs(1, LANES), cs(1, LANES),
                  cs(LANES, LANES), cs(1, LANES), cs(1, LANES),
                  cs(LANES, 4 * HY_WIDTH), cs(1, HY_WIDTH)],
        out_specs=[pl.BlockSpec((tl, HY_WIDTH), lambda i: (i, 0)), pl.BlockSpec((1, HY_WIDTH), lambda i: (0, 0))],
        out_shape=[jax.ShapeDtypeStruct((n, HY_WIDTH), BF16), jax.ShapeDtypeStruct((1, HY_WIDTH), F32)],
        compiler_params=_cparams(("arbitrary",)),
        name="hyena_filter",
    )(emb, _block_diag2(w1p), twice(b1), twice(fr1), _block_diag2(w2), twice(b2), twice(fr2),
      _block_diag2(w3), deltas)


def _dft_tables(n1):
    n = n1 * DFT_N2
    a = np.arange(n1, dtype=np.int64)
    ang1 = (2.0 * math.pi / n1) * ((a[:, None] * a[None, :]) % n1)
    c1, s1 = np.cos(ang1), np.sin(ang1)
    k1 = np.arange(n1, dtype=np.int64)[:, None, None]
    k2 = np.arange(DFT_N2, dtype=np.int64)[None, :, None]
    n2 = np.arange(DFT_N2, dtype=np.int64)[None, None, :]
    ang = (2.0 * math.pi / n) * ((n2 * (k1 + n1 * k2)) % n)
    gc, gs = np.cos(ang), np.sin(ang)
    tb = lambda x: jnp.asarray(np.ascontiguousarray(x).astype(BF16))
    gc, gs = tb(gc), tb(gs)
    gct, gst = jnp.swapaxes(gc, 1, 2), jnp.swapaxes(gs, 1, 2)
    blk = lambda a, b, c, d: jnp.concatenate([jnp.concatenate([a, b], axis=2), jnp.concatenate([c, d], axis=2)], axis=1)
    return dict(c1=tb(c1), s1=tb(s1), g2=blk(gc, gs, -gs, gc), h2=blk(gct, -gst, gst, gct))


def _dft1_kernel(c_ref, s_ref, x_ref, re_ref, im_ref):
    x = x_ref[...]
    re_ref[...] = _dot(c_ref[...], x).astype(BF16)
    im_ref[...] = (-_dot(s_ref[...], x)).astype(BF16)


def _dft1(x, c1, s1, cb):
    k1, w = x.shape
    n1 = c1.shape[0]
    ospec = pl.BlockSpec((n1, cb), lambda j: (0, j))
    return pl.pallas_call(
        _dft1_kernel,
        grid=(w // cb,),
        in_specs=[pl.BlockSpec((n1, k1), lambda j: (0, 0)), pl.BlockSpec((n1, k1), lambda j: (0, 0)),
                  pl.BlockSpec((k1, cb), lambda j: (0, j))],
        out_specs=[ospec, ospec],
        out_shape=[jax.ShapeDtypeStruct((n1, w), BF16)] * 2,
        compiler_params=_cparams(("parallel",)),
        name="dft_stage1",
    )(c1, s1, x)


def _stack(re, im):
    return jnp.concatenate([re, im], axis=0)


def _conv_mid_kernel(g2_ref, h2_ref, are_ref, aim_ref, fre_ref, fim_ref, norm_ref, bre_ref, bim_ref, *, kb):
    inv = 1.0 / norm_ref[...]
    for t in range(kb):
        g2 = g2_ref[t]
        x = _dot(g2, _stack(are_ref[t], aim_ref[t]))
        xre, xim = x[:DFT_N2], x[DFT_N2:]
        f = _dot(g2, _stack(fre_ref[t], fim_ref[t])) * inv
        kre, kim = f[:DFT_N2], f[DFT_N2:]
        yre = (xre * kre - xim * kim).astype(BF16)
        yim = (xre * kim + xim * kre).astype(BF16)
        b = _dot(h2_ref[t], _stack(yre, yim))
        bre_ref[t] = b[:DFT_N2].astype(BF16)
        bim_ref[t] = b[DFT_N2:].astype(BF16)


def _conv_mid(tabs, are, aim, fre, fim, norm, kb):
    n1, _, c = are.shape
    gspec = pl.BlockSpec((kb, 2 * DFT_N2, 2 * DFT_N2), lambda i: (i, 0, 0))
    aspec = pl.BlockSpec((kb, DFT_N2, c), lambda i: (i, 0, 0))
    return pl.pallas_call(
        functools.partial(_conv_mid_kernel, kb=kb),
        grid=(n1 // kb,),
        in_specs=[gspec, gspec, aspec, aspec, aspec, aspec, pl.BlockSpec((1, c), lambda i: (0, 0))],
        out_specs=[aspec, aspec],
        out_shape=[jax.ShapeDtypeStruct((n1, DFT_N2, c), BF16)] * 2,
        compiler_params=_cparams(("parallel",)),
        name="conv_spectral",
    )(tabs["g2"], tabs["h2"], are, aim, fre, fim, norm)


def _idft1_kernel(c_ref, s_ref, bre_ref, bim_ref, o_ref, *, inv_n):
    y = (_dot(c_ref[...], bre_ref[...]) - _dot(s_ref[...], bim_ref[...])) * inv_n
    o_ref[...] = y.astype(o_ref.dtype)


def _idft1(c1h, s1h, bre, bim, cb):
    ko, n1 = c1h.shape
    w = bre.shape[1]
    n = n1 * DFT_N2
    ospec = pl.BlockSpec((ko, cb), lambda j: (0, j))
    return pl.pallas_call(
        functools.partial(_idft1_kernel, inv_n=1.0 / n),
        grid=(w // cb,),
        in_specs=[pl.BlockSpec((ko, n1), lambda j: (0, 0)), pl.BlockSpec((ko, n1), lambda j: (0, 0)),
                  pl.BlockSpec((n1, cb), lambda j: (0, j)), pl.BlockSpec((n1, cb), lambda j: (0, j))],
        out_specs=ospec,
        out_shape=jax.ShapeDtypeStruct((ko, w), BF16),
        compiler_params=_cparams(("parallel",)),
        name="idft_stage1",
    )(c1h, s1h, bre, bim)


def _hyena(z_all, tabs, emb, p, i, tm, cb, kb):
    seq = z_all.shape[0]
    n1 = 2 * seq // DFT_N2
    c = HY_WIDTH
    x0, zb = _hy_pre(z_all, p["hy_conv_w"][i], p["hy_conv_b"][i], min(tm, seq))
    kern_b, norm = _hy_filter(emb, p["hy_w1"][i], p["hy_b1"][i], p["hy_freq1"][i], p["hy_w2"][i],
                                 p["hy_b2"][i], p["hy_freq2"][i], p["hy_w3"][i], min(1024, seq))
    wide = DFT_N2 * c
    fre, fim = _dft1(kern_b.reshape(n1, wide), tabs["c1"], tabs["s1"], cb)
    are, aim = _dft1(zb.reshape(n1 // 2, wide), tabs["c1"][:, :n1 // 2], tabs["s1"][:, :n1 // 2], cb)
    bre, bim = _conv_mid(tabs, are.reshape(n1, DFT_N2, c), aim.reshape(n1, DFT_N2, c),
                         fre.reshape(n1, DFT_N2, c), fim.reshape(n1, DFT_N2, c), norm, kb)
    y = _idft1(tabs["c1"][:n1 // 2], tabs["s1"][:n1 // 2], bre.reshape(n1, wide), bim.reshape(n1, wide), cb)
    return y.reshape(seq, c), zb, x0


def _combine_kernel(x_ref, ya_ref, o0_ref, o1_ref, o2_ref, l0_ref, l1_ref, l2_ref, yc_ref,
                    hy_ref, hz_ref, hx_ref, skip_ref,
                    g0_ref, g1_ref, g2_ref, g3_ref, pa_ref, pb_ref, pc_ref, pd_ref, wo_ref, o_ref):
    l0, l1, l2 = l0_ref[...], l1_ref[...], l2_ref[...]
    mx = jnp.maximum(jnp.maximum(l0, l1), l2)
    e0, e1, e2 = jnp.exp(l0 - mx), jnp.exp(l1 - mx), jnp.exp(l2 - mx)
    yb = (e0 * o0_ref[...] + e1 * o1_ref[...] + e2 * o2_ref[...]) / (e0 + e1 + e2)
    f32 = lambda ref: ref[...].astype(F32)
    yd = f32(hx_ref) * (f32(hy_ref) + f32(hz_ref) * skip_ref[...])
    gate = lambda ref: _sigmoid(f32(ref))
    m = (gate(g0_ref) * _dot(ya_ref[...].astype(BF16), pa_ref[...])
         + gate(g1_ref) * _dot(yb.astype(BF16), pb_ref[...])
         + gate(g2_ref) * _dot(yc_ref[...].astype(BF16), pc_ref[...])
         + gate(g3_ref) * _dot(yd.astype(BF16), pd_ref[...]))
    o_ref[...] = x_ref[...] + _dot(m.astype(BF16), wo_ref[...])


def _combine(x, z, ya, dil_outs, yc, hyena, skip, pa, pb, pc, pd, wo, tm):
    s, d = x.shape
    rs = lambda w: pl.BlockSpec((tm, w), lambda i: (i, 0))
    gs = lambda b: pl.BlockSpec((tm, d), lambda i: (i, Z_GATE // d + b))
    ws = lambda a: pl.BlockSpec(a.shape, lambda i: (0, 0))
    (o0, l0), (o1, l1), (o2, l2) = dil_outs
    gw = DIL_HEADS * HEAD_DIM
    hy, hz, hx = hyena
    return pl.pallas_call(
        _combine_kernel,
        grid=(s // tm,),
        in_specs=[rs(d), rs(ya.shape[1]), rs(gw), rs(gw), rs(gw), rs(gw), rs(gw), rs(gw),
                  rs(yc.shape[1]), rs(HY_WIDTH), rs(HY_WIDTH), rs(HY_WIDTH), ws(skip), gs(0), gs(1), gs(2), gs(3),
                  ws(pa), ws(pb), ws(pc), ws(pd), ws(wo)],
        out_specs=rs(d),
        out_shape=jax.ShapeDtypeStruct((s, d), F32),
        compiler_params=_cparams(("parallel",)),
        name="combine",
    )(x, ya, o0, o1, o2, l0, l1, l2, yc, hy, hz, hx, skip, z, z, z, z, pa, pb, pc, pd, wo)


def _mlp_kernel(x_ref, g_ref, w1_ref, w2_ref, o_ref, h_ref, acc_ref):
    j = pl.program_id(1)

    @pl.when(j == 0)
    def _():
        h_ref[...] = (_rms(x_ref[...]) * g_ref[...]).astype(BF16)
        acc_ref[...] = jnp.zeros(acc_ref.shape, F32)

    a = jnp.maximum(_dot(h_ref[...], w1_ref[...]), 0.0)
    acc_ref[...] += _dot((a * a).astype(BF16), w2_ref[...])

    @pl.when(j == pl.num_programs(1) - 1)
    def _():
        o_ref[...] = x_ref[...] + acc_ref[...]


def _mlp(x, g, w1, w2, tm, tf):
    s, d = x.shape
    ff = w1.shape[1]
    return pl.pallas_call(
        _mlp_kernel,
        grid=(s // tm, ff // tf),
        in_specs=[pl.BlockSpec((tm, d), lambda i, j: (i, 0)),
                  pl.BlockSpec((1, d), lambda i, j: (0, 0)),
                  pl.BlockSpec((d, tf), lambda i, j: (0, j)),
                  pl.BlockSpec((tf, d), lambda i, j: (j, 0))],
        out_specs=pl.BlockSpec((tm, d), lambda i, j: (i, 0)),
        out_shape=jax.ShapeDtypeStruct((s, d), F32),
        scratch_shapes=[pltpu.VMEM((tm, d), BF16), pltpu.VMEM((tm, d), F32)],
        compiler_params=_cparams(("parallel", "arbitrary")),
        name="mlp",
    )(x, g.reshape(1, d), w1, w2)


def _permute_w_in(w):
    d = w.shape[0]
    pieces = [w[:, _O_DU:_O_GATE], w[:, _O_B:_O_C], w[:, _O_AQ:_O_AK], w[:, _O_C:_O_DU],
              w[:, _O_AV:_O_AR], w[:, _O_AR:_O_ALR], w[:, _O_AK:_O_AV], w[:, _O_ALR:_O_B],
              jnp.zeros((d, Z_GATE - Z_ALR - 2 * GLA_RANK), w.dtype), w[:, _O_GATE:_O_END]]
    return jnp.concatenate(pieces, axis=1).astype(BF16)


def _dil_bias_idx(tq, dil):
    rel = _skewed_rel(tq, tq + 2 * DIL_HALF, -1, -DIL_HALF)
    return _t5_bucket(rel * dil)[None].astype(np.int32)


def _diff_bias_idx(tq, tk):
    far_lo, far_hi = _diff_far_offsets(tq, tk)
    return np.stack([_t5_bucket(_skewed_rel(tk, tq, 1, o * tk)) for o in range(far_lo, far_hi + 1)]).astype(np.int32)


def _forward(x, p, *, t_diff, tk_diff, qc_diff, tq_dil, nt_dil, tb_gla, tm_proj, tn_proj, tm_prep, tm_row, tm_mlp, tf_mlp, tm_hy, cb_dft, kb_dft):
    seq = x.shape[0]
    depth = p["w_in"].shape[0]
    n_dil_bias = DIL_GROUPS * DIL_HEADS
    tq_dil = [min(tq, seq // dil) for tq, (_, dil) in zip(tq_dil, DIL_PATTERNS)]
    dil_bias = [_bias_tiles(p["t5_bias"], _dil_bias_idx(tq, dil), tq, tq + 2 * DIL_HALF, DIL_HEADS, gi * DIL_HEADS)
                for gi, (tq, (_, dil)) in enumerate(zip(tq_dil, DIL_PATTERNS))]
    t_diff, tk_diff = min(t_diff, seq), min(tk_diff, seq)
    diff_bias = _bias_tiles(p["t5_bias"], _diff_bias_idx(t_diff, tk_diff), tk_diff, t_diff, DIFF_HEADS,
                            n_dil_bias, LOG2E)
    tabs = _dft_tables(2 * seq // DFT_N2)
    emb = _hy_positions(seq)
    rep = lambda g, n: jnp.tile(g, n)
    for i in range(depth):
        z = _norm_matmul(x, p["norm1_g"][i], _permute_w_in(p["w_in"][i]), tm_proj, tn_proj)
        wg = [jnp.zeros((LANES, GLA_HEADS * GLA_DK), F32).at[j * GLA_RANK:(j + 1) * GLA_RANK].set(
            p["gla_gate_w"][i, j]).astype(BF16) for j in range(2)]
        gb = p["gla_gate_b"][i]
        o_fwd = _gla_scan(z, wg[0], gb[0:1], tb_gla, False)
        ya = _gla_scan(z, wg[1], gb[1:2], tb_gla, True, fin=(o_fwd, p["gla_norm_g"][i]))
        bq, bk, bv, cq, ck, cv = _prep(
            z, jnp.repeat(p["dil_qnorm_g"][i], DIL_HEADS, axis=0).reshape(-1),
            jnp.repeat(p["dil_knorm_g"][i], DIL_HEADS, axis=0).reshape(-1),
            rep(p["diff_qnorm_g"][i], 2 * DIFF_HEADS), rep(p["diff_knorm_g"][i], 2 * DIFF_HEADS), tm_prep)
        dil_outs = [_dil_attn(bq, bk, bv, dil_bias[gi], gi, dil, tq_dil[gi], max(1, min(nt_dil[gi], seq // (tq_dil[gi] * dil))))
                    for gi, (_, dil) in enumerate(DIL_PATTERNS)]
        lam_init = 0.8 - 0.6 * math.exp(-0.3 * i)
        yc = _diff_attn(cq, ck, cv, diff_bias, p["diff_lambda"][i], p["diff_subln_g"][i], lam_init,
                        t_diff, tk_diff, qc_diff)
        yd = _hyena(z, tabs, emb, p, i, tm_hy, cb_dft, kb_dft)
        x = _combine(x, z, ya, dil_outs, yc, yd, p["hy_skip"][i][None],
                     p["proj_a"][i].astype(BF16), p["proj_b"][i].astype(BF16),
                     p["proj_c"][i].astype(BF16), p["proj_d"][i].astype(BF16), p["w_out"][i].astype(BF16),
                     tm_row)
        x = _mlp(x, p["norm2_g"][i], p["mlp_w1"][i].astype(BF16), p["mlp_w2"][i].astype(BF16), tm_mlp, tf_mlp)
    return x


def kernel(x, t5_bias, norm1_g, w_in, gla_gate_w, gla_gate_b, gla_norm_g, dil_qnorm_g, dil_knorm_g,
           diff_qnorm_g, diff_knorm_g, diff_lambda, diff_subln_g, hy_conv_w, hy_conv_b, hy_w1, hy_b1,
           hy_freq1, hy_w2, hy_b2, hy_freq2, hy_w3, hy_skip, proj_a, proj_b, proj_c, proj_d, w_out,
           norm2_g, mlp_w1, mlp_w2):
    p = dict(t5_bias=t5_bias, norm1_g=norm1_g, w_in=w_in, gla_gate_w=gla_gate_w, gla_gate_b=gla_gate_b,
             gla_norm_g=gla_norm_g, dil_qnorm_g=dil_qnorm_g, dil_knorm_g=dil_knorm_g,
             diff_qnorm_g=diff_qnorm_g, diff_knorm_g=diff_knorm_g, diff_lambda=diff_lambda,
             diff_subln_g=diff_subln_g, hy_conv_w=hy_conv_w, hy_conv_b=hy_conv_b, hy_w1=hy_w1, hy_b1=hy_b1,
             hy_freq1=hy_freq1, hy_w2=hy_w2, hy_b2=hy_b2, hy_freq2=hy_freq2, hy_w3=hy_w3, hy_skip=hy_skip,
             proj_a=proj_a, proj_b=proj_b, proj_c=proj_c, proj_d=proj_d, w_out=w_out, norm2_g=norm2_g,
             mlp_w1=mlp_w1, mlp_w2=mlp_w2)
    b, s, d = x.shape
    tiles = dict(
        t_diff=2048, tk_diff=1024, qc_diff=(256, 768, 768, 256),
        tq_dil=(256, 256, 128), nt_dil=(4, 1, 1),
        tb_gla=512, tm_proj=2048, tn_proj=1024, tm_prep=1024, tm_row=512, tm_mlp=1024, tf_mlp=2048,
        tm_hy=1024, cb_dft=8192, kb_dft=16)
    outs = [_forward(x[bi], p, **tiles) for bi in range(b)]
    return jnp.stack(outs)
```

```python
import functools
import math

import jax
import jax.numpy as jnp
import numpy as np
from jax import lax
from jax.experimental import pallas as pl
from jax.experimental.pallas import tpu as pltpu

F32 = jnp.float32
BF16 = jnp.bfloat16

D_MODEL = 1024
HEAD_DIM = 64
GLA_HEADS = 4
GLA_DK = 64
GLA_DV = 128
GLA_RANK = 16
GLA_TAU = 16.0
GLA_CHUNK = 64
DIL_PATTERNS = ((128, 1), (512, 4), (2048, 16))
DIL_GROUPS = 3
DIL_HEADS = 4
DIL_HALF = 64
DIFF_HEADS = 4
DIFF_DV = 128
DIFF_VT = DIFF_DV + 16
HY_WIDTH = 512
HY_EMB = 33
HY_FFN = 64
HY_DECAY_TARGET = 1e-2
HY_FAST_DECAY = 0.3
HY_SLOW_DECAY = 1.5
T5_BUCKETS = 32
T5_MAX_DIST = 1024
BUCKET_GROUP = 4
N_BIAS_HEADS = 16
D_FF = 4096
RMS_EPS = 1e-6
LOG2E = math.log2(math.e)

LANES = 128
HALO_ROWS = 16
VMEM_LIMIT = 48 * 1024 * 1024

Z_DU = 0
Z_BQ, Z_BK, Z_BV = 1536, 2304, 3072
Z_AQ = 3840
Z_CQ, Z_CK, Z_CV = 4096, 4608, 5120
Z_AV, Z_AR = 5632, 6144
Z_AK = 6656
Z_ALR = 6912
Z_GATE = 7168
Z_COLS = 11264

_O_AQ, _O_AK, _O_AV, _O_AR, _O_ALR, _O_B, _O_C, _O_DU, _O_GATE, _O_END = (
    0, 256, 512, 1024, 1536, 1568, 3872, 5408, 6944, 11040)

DFT_N2 = 128


def _cparams(sem):
    return pltpu.CompilerParams(dimension_semantics=sem, vmem_limit_bytes=VMEM_LIMIT)


def _dot(a, b):
    return jnp.dot(a, b, preferred_element_type=F32)


def _dot_nt(a, b):
    return lax.dot_general(a, b, (((1,), (1,)), ((), ())), preferred_element_type=F32)


def _dot_tn(a, b):
    return lax.dot_general(a, b, (((0,), (0,)), ((), ())), preferred_element_type=F32)


def _split(x):
    hi = x.astype(BF16)
    lo = (x - hi.astype(F32)).astype(BF16)
    return hi, lo


def _dot3(a, b):
    ah, al = _split(a)
    bh, bl = _split(b)
    return _dot(ah, bh) + _dot(ah, bl) + _dot(al, bh)


def _rms(x):
    return x * lax.rsqrt(jnp.mean(x * x, axis=-1, keepdims=True) + RMS_EPS)


def _sigmoid(x):
    return 0.5 * jnp.tanh(0.5 * x) + 0.5


def _norm_matmul_kernel(x_ref, g_ref, w_ref, o_ref, h_ref):
    @pl.when(pl.program_id(1) == 0)
    def _():
        h_ref[...] = (_rms(x_ref[...]) * g_ref[...]).astype(BF16)

    o_ref[...] = _dot(h_ref[...], w_ref[...]).astype(o_ref.dtype)


def _norm_matmul(x, g, w, tm, tn):
    s, d = x.shape
    n = w.shape[1]
    return pl.pallas_call(
        _norm_matmul_kernel,
        grid=(s // tm, n // tn),
        in_specs=[pl.BlockSpec((tm, d), lambda i, j: (i, 0)),
                  pl.BlockSpec((1, d), lambda i, j: (0, 0)),
                  pl.BlockSpec((d, tn), lambda i, j: (0, j))],
        out_specs=pl.BlockSpec((tm, tn), lambda i, j: (i, j)),
        out_shape=jax.ShapeDtypeStruct((s, n), BF16),
        scratch_shapes=[pltpu.VMEM((tm, d), BF16)],
        compiler_params=_cparams(("parallel", "arbitrary")),
        name="in_proj",
    )(x, g.reshape(1, d), w)


def _group_norm(x, e, gain):
    hi, lo = _split(x * x)
    ew = e.shape[0]
    ms = jnp.concatenate([_dot(hi[:, c:c + ew], e) + _dot(lo[:, c:c + ew], e)
                          for c in range(0, x.shape[1], ew)], axis=1) * (1.0 / HEAD_DIM)
    return x * lax.rsqrt(ms + RMS_EPS) * gain


def _prep_kernel(bq_ref, bk_ref, bv_ref, cq_ref, ck_ref, cv_ref, e_ref,
                 gbq_ref, gbk_ref, gcq_ref, gck_ref,
                 obq_ref, obk_ref, obv_ref, ocq_ref, ock_ref, ocv_ref):
    scale = HEAD_DIM ** -0.5
    eb = ec = e_ref[...]
    f32 = lambda ref: ref[...].astype(F32)
    obq_ref[...] = (_group_norm(f32(bq_ref), eb, gbq_ref[...]) * scale).astype(BF16)
    obk_ref[...] = _group_norm(f32(bk_ref), eb, gbk_ref[...]).astype(BF16)
    obv_ref[...] = bv_ref[...]
    ocq_ref[...] = (_group_norm(f32(cq_ref), ec, gcq_ref[...]) * (scale * LOG2E)).T.astype(BF16)
    ock_ref[...] = _group_norm(f32(ck_ref), ec, gck_ref[...]).astype(BF16)
    cvt = f32(cv_ref).T
    ones = jnp.ones((DIFF_VT - DIFF_DV, cvt.shape[1]), F32)
    ocv_ref[...] = jnp.concatenate(
        [t for h in range(DIFF_HEADS) for t in (cvt[h * DIFF_DV:(h + 1) * DIFF_DV], ones)], axis=0).astype(BF16)


def _block_diag_ones(width):
    idx = np.arange(width) // HEAD_DIM
    return jnp.asarray(idx[:, None] == idx[None, :], dtype=BF16)


def _prep(z, gbq, gbk, gcq, gck, tm):
    s = z.shape[0]
    wb, wc = DIL_GROUPS * DIL_HEADS * HEAD_DIM, DIFF_HEADS * 2 * HEAD_DIM
    zspec = lambda w, off: pl.BlockSpec((tm, w), lambda i: (i, off // w))
    cspec = lambda r, c: pl.BlockSpec((r, c), lambda i: (0, 0))
    ospec = lambda w: pl.BlockSpec((tm, w), lambda i: (i, 0))
    tspec = lambda w: pl.BlockSpec((w, tm), lambda i: (0, i))
    return pl.pallas_call(
        _prep_kernel,
        grid=(s // tm,),
        in_specs=[zspec(wb, Z_BQ), zspec(wb, Z_BK), zspec(wb, Z_BV),
                  zspec(wc, Z_CQ), zspec(wc, Z_CK), zspec(wc, Z_CV),
                  cspec(2 * LANES, 2 * LANES),
                  cspec(1, wb), cspec(1, wb), cspec(1, wc), cspec(1, wc)],
        out_specs=[ospec(wb), ospec(wb), ospec(wb), tspec(wc), ospec(wc), tspec(DIFF_HEADS * DIFF_VT)],
        out_shape=[jax.ShapeDtypeStruct((s, wb), BF16)] * 3
        + [jax.ShapeDtypeStruct((wc, s), BF16), jax.ShapeDtypeStruct((s, wc), BF16),
           jax.ShapeDtypeStruct((DIFF_HEADS * DIFF_VT, s), BF16)],
        compiler_params=_cparams(("parallel",)),
        name="qk_prep",
    )(z, z, z, z, z, z, _block_diag_ones(2 * LANES),
      gbq.reshape(1, wb), gbk.reshape(1, wb), gcq.reshape(1, wc), gck.reshape(1, wc))


def _t5_bucket(rel):
    half = T5_BUCKETS // 2
    max_exact = half // 2
    ret = np.where(rel > 0, half, 0)
    n = np.abs(rel)
    nf = np.maximum(n, 1).astype(np.float64)
    large = max_exact + (np.log(nf / max_exact) / math.log(T5_MAX_DIST / max_exact)
                         * (half - max_exact)).astype(np.int64)
    large = np.minimum(large, half - 1)
    return ret + np.where(n < max_exact, n, large)


def _bias_kernel(tab_ref, rng_ref, idx_ref, o_ref, p_ref, *, head_base, out_scale, chunk):
    t = pl.program_id(0)
    col = head_base + pl.program_id(1)
    rows, width = o_ref.shape[2:]
    slabs = width // LANES
    lo, hi = rng_ref[t, 0], rng_ref[t, 1]

    def one_chunk(ci, carry):
        rs = pl.ds(pl.multiple_of(ci * chunk, chunk), chunk)
        idx = idx_ref[0, rs, :]
        def bucket_group(g, acc):
            for b in range(BUCKET_GROUP):
                bb = g * BUCKET_GROUP + b
                acc = jnp.where(idx == bb, tab_ref[bb, col] * out_scale, acc)
            return acc

        p_ref[rs, :] = lax.fori_loop(lo // BUCKET_GROUP, hi // BUCKET_GROUP + 1, bucket_group,
                                     jnp.zeros((chunk, LANES), F32))
        return carry

    lax.fori_loop(0, p_ref.shape[0] // chunk, one_chunk, 0)
    for c in range(slabs):
        off = LANES * (slabs - 1 - c)
        o_ref[0, 0, :, c * LANES:(c + 1) * LANES] = p_ref[off:off + rows, :]


def _skewed_rel(rows, width, sign, base):
    slabs = width // LANES
    rho = np.arange(rows + LANES * (slabs - 1))[:, None]
    lane = np.arange(LANES)[None, :]
    return sign * (rho - LANES * (slabs - 1) - lane) + base


def _bias_tiles(t5_bias, idx, rows, width, n_heads, head_base, out_scale=1.0, chunk=128):
    nt, rp, _ = idx.shape
    ranges = np.stack([idx.reshape(nt, -1).min(axis=1), idx.reshape(nt, -1).max(axis=1)], axis=1).astype(np.int32)
    return pl.pallas_call(
        functools.partial(_bias_kernel, head_base=head_base, out_scale=out_scale, chunk=chunk),
        grid=(nt, n_heads),
        in_specs=[pl.BlockSpec(memory_space=pltpu.SMEM), pl.BlockSpec(memory_space=pltpu.SMEM),
                  pl.BlockSpec((1, rp, LANES), lambda t, h: (t, 0, 0))],
        out_specs=pl.BlockSpec((1, 1, rows, width), lambda t, h: (t, h, 0, 0)),
        out_shape=jax.ShapeDtypeStruct((nt, n_heads, rows, width), F32),
        scratch_shapes=[pltpu.VMEM((rp, LANES), F32)],
        compiler_params=_cparams(("parallel", "parallel")),
        name="t5_bias_tiles",
    )(t5_bias, jnp.asarray(ranges), jnp.asarray(idx))


def _diff_attn_kernel(qt_ref, k_ref, vt_ref, bias_ref, lam_ref, g_ref, o_ref,
                      qa_ref, qb_ref, m_ref, acc_ref, s_ref, *, lam_init, qc, far_lo, far_hi):
    j = pl.program_id(2)
    t = qt_ref.shape[1]

    @pl.when(j == 0)
    def _():
        qt = qt_ref[...]
        row = lax.broadcasted_iota(jnp.int32, qt.shape, 0)
        qa_ref[...] = jnp.where(row < HEAD_DIM, qt, jnp.zeros_like(qt))
        qb_ref[...] = jnp.where(row >= HEAD_DIM, qt, jnp.zeros_like(qt))
        m_ref[...] = jnp.full(m_ref.shape, -jnp.inf, F32)
        acc_ref[...] = jnp.zeros(acc_ref.shape, F32)

    starts = [sum(qc[:u]) for u in range(len(qc))]
    cuts = [slice(a, a + w) for a, w in zip(starts, qc)]
    chains = [(0, cs) for cs in cuts] + [(1, cs) for cs in reversed(cuts)]
    q_refs = (qa_ref, qb_ref)

    def step(far):
        k = k_ref[...]
        vt = vt_ref[...]
        if far:
            const = bias_ref[0, 0, 0:1, 0:1]
            scores = lambda c, cols: _dot(k, q_refs[c][:, cols])
        else:
            const = 0.0
            scores = lambda c, cols: _dot(k, q_refs[c][:, cols]) + bias_ref[0, 0, :, cols]
        width = lambda n: chains[n][1].stop - chains[n][1].start
        slots = s_ref.shape[0]
        for n in range(min(slots - 1, len(chains))):
            s_ref[n, :, :width(n)] = scores(*chains[n])
        for n, (c, cols) in enumerate(chains):
            ahead = n + slots - 1
            if ahead < len(chains):
                s_ref[ahead % slots, :, :width(ahead)] = scores(*chains[ahead])
            s = s_ref[n % slots, :, :width(n)]
            m_old = m_ref[c, :, cols]
            m_new = jnp.maximum(m_old, jnp.max(s, axis=0, keepdims=True) + const)
            alpha = jnp.exp2(m_old - m_new)
            p = jnp.exp2(s - (m_new - const)).astype(BF16)
            acc_ref[c, :, cols] = alpha * acc_ref[c, :, cols] + _dot(vt, p)
            m_ref[c, :, cols] = m_new

    off = j - pl.program_id(1) * (t // k_ref.shape[0])
    is_far = jnp.logical_or(off <= far_lo, off >= far_hi)
    pl.when(is_far)(lambda: step(True))
    pl.when(jnp.logical_not(is_far))(lambda: step(False))

    @pl.when(j == pl.num_programs(2) - 1)
    def _():
        lp = lam_ref[...]
        lam = (jnp.exp(jnp.sum(lp[0:1] * lp[1:2], axis=-1, keepdims=True))
               - jnp.exp(jnp.sum(lp[2:3] * lp[3:4], axis=-1, keepdims=True)) + lam_init)
        a0 = acc_ref[0]
        a1 = acc_ref[1]
        o0 = a0[:DIFF_DV] / a0[DIFF_DV:DIFF_DV + 1]
        o1 = a1[:DIFF_DV] / a1[DIFF_DV:DIFF_DV + 1]
        att = o0 - lam * o1
        y = att * lax.rsqrt(jnp.mean(att * att, axis=0, keepdims=True) + RMS_EPS)
        o_ref[...] = y.T * g_ref[...] * (1.0 - lam_init)


def _diff_far_offsets(tq, tk):
    far_lo = (-T5_MAX_DIST - tk + 1) // tk
    far_hi = -(-(T5_MAX_DIST + tq - 1) // tk)
    return far_lo, far_hi


def _diff_attn(cqt, ck, cvt, bias_tiles, lam_p, subln_g, lam_init, t, tk, qc):
    s = ck.shape[0]
    far_lo, far_hi = _diff_far_offsets(t, tk)
    ratio = t // tk
    w = 2 * HEAD_DIM
    return pl.pallas_call(
        functools.partial(_diff_attn_kernel, lam_init=lam_init, qc=qc, far_lo=far_lo, far_hi=far_hi),
        grid=(DIFF_HEADS, s // t, s // tk),
        in_specs=[pl.BlockSpec((w, t), lambda h, i, j: (h, i)),
                  pl.BlockSpec((tk, w), lambda h, i, j: (j, h)),
                  pl.BlockSpec((DIFF_VT, tk), lambda h, i, j: (h, j)),
                  pl.BlockSpec((1, 1, tk, t),
                               lambda h, i, j: (jnp.clip(j - i * ratio, far_lo, far_hi) - far_lo, h, 0, 0)),
                  pl.BlockSpec((4, HEAD_DIM), lambda h, i, j: (0, 0)),
                  pl.BlockSpec((1, DIFF_DV), lambda h, i, j: (0, 0))],
        out_specs=pl.BlockSpec((t, DIFF_DV), lambda h, i, j: (i, h)),
        out_shape=jax.ShapeDtypeStruct((s, DIFF_HEADS * DIFF_DV), F32),
        scratch_shapes=[pltpu.VMEM((w, t), BF16), pltpu.VMEM((w, t), BF16),
                        pltpu.VMEM((2, 1, t), F32), pltpu.VMEM((2, DIFF_VT, t), F32),
                        pltpu.VMEM((2, tk, max(qc)), F32)],
        compiler_params=_cparams(("parallel", "parallel", "arbitrary")),
        name="diff_attn",
    )(cqt, ck, cvt, bias_tiles, lam_p, subln_g.reshape(1, DIFF_DV))


def _dil_kernel(q_ref, kp_ref, kc_ref, kn_ref, vp_ref, vc_ref, vn_ref, bias_ref, o_ref, lse_ref,
                qs_ref, ks_ref, vs_ref, os_ref, ls_ref, *, tq, dil, nt, m_len):
    n = pl.program_id(0)
    hd = DIL_HALF * dil
    body_rows = tq * dil * nt
    halves = DIL_HEADS * HEAD_DIM // LANES

    def put(dst, rows, src_ref):
        x = src_ref[...].astype(F32)
        for t in range(halves):
            dst[t, rows, :] = x[:, t * LANES:(t + 1) * LANES]

    def strided(src, r, count):
        return jnp.concatenate([src[t, pl.ds(r, count, stride=dil), :] for t in range(halves)], axis=1)

    put(qs_ref, slice(0, body_rows), q_ref)
    for dst, (p_ref, c_ref, n_ref) in ((ks_ref, (kp_ref, kc_ref, kn_ref)), (vs_ref, (vp_ref, vc_ref, vn_ref))):
        put(dst, slice(0, hd), p_ref)
        put(dst, slice(hd, hd + body_rows), c_ref)
        put(dst, slice(hd + body_rows, 2 * hd + body_rows), n_ref)
    tk = tq + 2 * DIL_HALF
    a = lax.broadcasted_iota(jnp.int32, (tq, tk), 0)
    c = lax.broadcasted_iota(jnp.int32, (tq, tk), 1)
    delta = c - DIL_HALF - a
    in_band = jnp.concatenate([jnp.abs(delta) <= DIL_HALF] * DIL_HEADS, axis=0)
    col = lax.broadcasted_iota(jnp.int32, (1, tk), 1)
    lane = lax.broadcasted_iota(jnp.int32, (tq, DIL_HEADS * HEAD_DIM), 1)
    head_masks = [(lane // HEAD_DIM) == h for h in range(DIL_HEADS)]

    def one_tile(it, carry):
        u = it // dil
        r = it % dil + u * (tq * dil)
        kpos = (n * nt + u) * tq - DIL_HALF + col
        valid = jnp.logical_and(in_band, jnp.logical_and(kpos >= 0, kpos < m_len))
        q = strided(qs_ref, r, tq).astype(BF16)
        k = strided(ks_ref, r, tk).astype(BF16)
        v = strided(vs_ref, r, tk).astype(BF16)
        qs = jnp.concatenate([jnp.where(hm, q, jnp.zeros_like(q)) for hm in head_masks], axis=0)
        s = _dot_nt(qs, k) + bias_ref[0].reshape(DIL_HEADS * tq, tk)
        s = jnp.where(valid, s, -1e30)
        m = jnp.max(s, axis=-1, keepdims=True)
        e = jnp.exp(s - m)
        l = jnp.sum(e, axis=-1, keepdims=True)
        oh = _dot((e / l).astype(BF16), v)
        lse = m + jnp.log(l)
        o = jnp.zeros(q.shape, F32)
        lse_o = jnp.zeros(q.shape, F32)
        for h, hm in enumerate(head_masks):
            rows = slice(h * tq, (h + 1) * tq)
            o = jnp.where(hm, oh[rows], o)
            lse_o = jnp.where(hm, lse[rows], lse_o)
        for t in range(halves):
            os_ref[t, pl.ds(r, tq, stride=dil), :] = o[:, t * LANES:(t + 1) * LANES]
            ls_ref[t, pl.ds(r, tq, stride=dil), :] = lse_o[:, t * LANES:(t + 1) * LANES]
        return carry

    lax.fori_loop(0, dil * nt, one_tile, 0, unroll=min(dil * nt, 4))
    o_ref[...] = jnp.concatenate([os_ref[t] for t in range(halves)], axis=1)
    lse_ref[...] = jnp.concatenate([ls_ref[t] for t in range(halves)], axis=1)


def _dil_attn(bq, bk, bv, bias, gi, dil, tq, nt):
    s = bq.shape[0]
    m_len = s // dil
    gw = DIL_HEADS * HEAD_DIM
    rows = tq * dil * nt
    hd = DIL_HALF * dil
    hb = rows // hd
    last = s // hd - 1
    qspec = pl.BlockSpec((rows, gw), lambda n: (n, gi))
    pspec = pl.BlockSpec((hd, gw), lambda n: (jnp.maximum(n * hb - 1, 0), gi))
    nspec = pl.BlockSpec((hd, gw), lambda n: (jnp.minimum((n + 1) * hb, last), gi))
    ospec = pl.BlockSpec((rows, gw), lambda n: (n, 0))
    return pl.pallas_call(
        functools.partial(_dil_kernel, tq=tq, dil=dil, nt=nt, m_len=m_len),
        grid=(s // rows,),
        in_specs=[qspec, pspec, qspec, nspec, pspec, qspec, nspec,
                  pl.BlockSpec((1, DIL_HEADS, tq, tq + 2 * DIL_HALF), lambda n: (0, 0, 0, 0))],
        out_specs=[ospec, ospec],
        out_shape=[jax.ShapeDtypeStruct((s, gw), F32)] * 2,
        scratch_shapes=[pltpu.VMEM((gw // LANES, rows, LANES), F32),
                        pltpu.VMEM((gw // LANES, rows + 2 * hd, LANES), F32),
                        pltpu.VMEM((gw // LANES, rows + 2 * hd, LANES), F32),
                        pltpu.VMEM((gw // LANES, rows, LANES), F32),
                        pltpu.VMEM((gw // LANES, rows, LANES), F32)],
        compiler_params=_cparams(("parallel",)),
        name=f"dil_attn_g{gi}",
    )(bq, bk, bk, bk, bv, bv, bv, bias)


def _gla_kernel(*refs, reverse, tb, finalize):
    if finalize:
        (q_ref, k_ref, v_ref, lr_ref, wg_ref, gb_ref, tri_ref, ofwd_ref, r_ref, ng_ref,
         o_ref, s_ref, oacc_ref) = refs
    else:
        q_ref, k_ref, v_ref, lr_ref, wg_ref, gb_ref, tri_ref, o_ref, s_ref = refs
        oacc_ref = o_ref
    cw = GLA_CHUNK
    qk = GLA_HEADS * GLA_DK
    vw = GLA_HEADS * GLA_DV

    @pl.when(pl.program_id(0) == 0)
    def _():
        s_ref[...] = jnp.zeros(s_ref.shape, F32)

    logits = _dot(lr_ref[...], wg_ref[...]) + gb_ref[...]
    g = (jnp.minimum(logits, 0.0) - jnp.log(1.0 + jnp.exp(-jnp.abs(logits)))) * (1.0 / GLA_TAU)
    ghi, glo = _split(g)
    tri = tri_ref[...]
    b = _dot(tri, ghi) + _dot(tri, glo)
    qg = (q_ref[...].astype(F32) * (GLA_DK ** -0.5) * jnp.exp(b)).astype(BF16)
    k = k_ref[...].astype(F32)
    kg = (k * jnp.exp(-b)).astype(BF16)
    v = v_ref[...].astype(BF16)

    lane_q = lax.broadcasted_iota(jnp.int32, (cw, qk), 1)
    rr = lax.broadcasted_iota(jnp.int32, (GLA_HEADS * cw, cw), 0)
    cc = lax.broadcasted_iota(jnp.int32, (GLA_HEADS * cw, cw), 1)
    tt = rr % cw
    amask = (cc > tt) if reverse else (cc <= tt)
    srow = lax.broadcasted_iota(jnp.int32, (vw, qk), 0) // GLA_DV
    scol = lax.broadcasted_iota(jnp.int32, (vw, qk), 1) // GLA_DK
    bdmask = srow == scol

    n_chunks = tb // cw
    order = range(n_chunks - 1, -1, -1) if reverse else range(n_chunks)
    for ci in order:
        rows = slice(ci * cw, (ci + 1) * cw)
        bc = b[rows]
        b_end = bc[0:1] if reverse else bc[cw - 1:cw]
        kdec = (k[rows] * jnp.exp(b_end - bc)).astype(BF16)
        qg_c = qg[rows]
        qs = jnp.concatenate(
            [jnp.where((lane_q // GLA_DK) == h, qg_c, jnp.zeros_like(qg_c)) for h in range(GLA_HEADS)], axis=0)
        a = jnp.where(amask, _dot_nt(qs, kg[rows]), 0.0)
        obig = _dot(a.astype(BF16), v[rows])
        o_intra = jnp.concatenate(
            [obig[h * cw:(h + 1) * cw, h * GLA_DV:(h + 1) * GLA_DV] for h in range(GLA_HEADS)], axis=1)
        state = s_ref[...]
        oacc_ref[rows, :] = o_intra + _dot_nt(qg_c, state.astype(BF16))
        ds = _dot_tn(v[rows], kdec)
        s_ref[...] = jnp.where(bdmask, jnp.exp(b_end) * state + ds, 0.0)

    if finalize:
        o = ofwd_ref[...] + oacc_ref[...]
        r = r_ref[...].astype(F32)
        outs = []
        for h in range(GLA_HEADS):
            sl = slice(h * GLA_DV, (h + 1) * GLA_DV)
            outs.append(_rms(o[:, sl]) * ng_ref[...] * (r[:, sl] * _sigmoid(r[:, sl])))
        o_ref[...] = jnp.concatenate(outs, axis=1)


def _chunk_tri(tb, reverse):
    i = np.arange(tb)
    same = (i[:, None] // GLA_CHUNK) == (i[None, :] // GLA_CHUNK)
    tri = (i[None, :] >= i[:, None]) if reverse else (i[None, :] <= i[:, None])
    return jnp.asarray(same & tri, dtype=BF16)


def _gla_scan(z, wg, gb, tb, reverse, fin=None):
    s = z.shape[0]
    nb = s // tb
    qk = GLA_HEADS * GLA_DK
    vw = GLA_HEADS * GLA_DV
    blk = (lambda i: nb - 1 - i) if reverse else (lambda i: i)
    zspec = lambda w, off: pl.BlockSpec((tb, w), lambda i: (blk(i), off // w))
    cspec = lambda r, c: pl.BlockSpec((r, c), lambda i: (0, 0))
    in_specs = [zspec(qk, Z_AQ), zspec(qk, Z_AK), zspec(vw, Z_AV), zspec(LANES, Z_ALR),
                cspec(LANES, qk), cspec(1, qk), cspec(tb, tb)]
    args = [z, z, z, z, wg, gb, _chunk_tri(tb, reverse)]
    scratch = [pltpu.VMEM((vw, qk), F32)]
    if fin is not None:
        o_fwd, norm_g = fin
        in_specs += [pl.BlockSpec((tb, vw), lambda i: (blk(i), 0)), zspec(vw, Z_AR), cspec(1, GLA_DV)]
        args += [o_fwd, z, norm_g.reshape(1, GLA_DV)]
        scratch.append(pltpu.VMEM((tb, vw), F32))
    return pl.pallas_call(
        functools.partial(_gla_kernel, reverse=reverse, tb=tb, finalize=fin is not None),
        grid=(nb,),
        in_specs=in_specs,
        out_specs=pl.BlockSpec((tb, vw), lambda i: (blk(i), 0)),
        out_shape=jax.ShapeDtypeStruct((s, vw), F32),
        scratch_shapes=scratch,
        compiler_params=_cparams(("arbitrary",)),
        name="gla_bwd" if reverse else "gla_fwd",
    )(*args)


def _hy_pre_kernel(u_ref, up_ref, un_ref, w_ref, b_ref, x0_ref, zb_ref, *, tm):
    i = pl.program_id(0)
    u = u_ref[...].astype(F32)
    row = lax.broadcasted_iota(jnp.int32, u.shape, 0)
    prev_row = jnp.where(i == 0, 0.0, up_ref[...].astype(F32)[HALO_ROWS - 1:HALO_ROWS, :])
    next_row = jnp.where(i == pl.num_programs(0) - 1, 0.0, un_ref[...].astype(F32)[0:1, :])
    u_prev = jnp.where(row == 0, prev_row, pltpu.roll(u, 1, axis=0))
    u_next = jnp.where(row == tm - 1, next_row, pltpu.roll(u, tm - 1, axis=0))
    y = b_ref[...] + u_prev * w_ref[0:1] + u * w_ref[1:2] + u_next * w_ref[2:3]
    x0_ref[...] = y[:, :HY_WIDTH].astype(BF16)
    zb_ref[...] = (y[:, HY_WIDTH:2 * HY_WIDTH] * y[:, 2 * HY_WIDTH:]).astype(BF16)


def _hy_pre(z, conv_w, conv_b, tm):
    s = z.shape[0]
    w = 3 * HY_WIDTH
    nr = s // HALO_ROWS
    ospec = pl.BlockSpec((tm, HY_WIDTH), lambda i: (i, 0))
    return pl.pallas_call(
        functools.partial(_hy_pre_kernel, tm=tm),
        grid=(s // tm,),
        in_specs=[pl.BlockSpec((tm, w), lambda i: (i, 0)),
                  pl.BlockSpec((HALO_ROWS, w), lambda i: (jnp.maximum(i * (tm // HALO_ROWS) - 1, 0), 0)),
                  pl.BlockSpec((HALO_ROWS, w), lambda i: (jnp.minimum((i + 1) * (tm // HALO_ROWS), nr - 1), 0)),
                  pl.BlockSpec((3, w), lambda i: (0, 0)),
                  pl.BlockSpec((1, w), lambda i: (0, 0))],
        out_specs=[ospec, ospec],
        out_shape=[jax.ShapeDtypeStruct((s, HY_WIDTH), BF16)] * 2,
        compiler_params=_cparams(("parallel",)),
        name="hyena_pre",
    )(z, z, z, conv_w, conv_b.reshape(1, w))


def _hy_filter_kernel(emb_ref, w1_ref, b1_ref, f1_ref, w2_ref, b2_ref, f2_ref, w3_ref, dl_ref,
                      kb_ref, norm_ref, *, tl, seq):
    i = pl.program_id(0)
    half = tl // 2
    emb = emb_ref[...]
    x = jnp.concatenate([emb[:half], emb[half:]], axis=1)
    h = jnp.sin(f1_ref[...] * (_dot3(x, w1_ref[...]) + b1_ref[...]))
    h = jnp.sin(f2_ref[...] * (_dot3(h, w2_ref[...]) + b2_ref[...]))
    h = _dot3(h, w3_ref[...])
    back = i * tl >= seq
    pick = lambda y: jnp.where(back, y[:, HY_WIDTH:], y[:, :HY_WIDTH])
    h = jnp.concatenate([pick(h[:, :2 * HY_WIDTH]), pick(h[:, 2 * HY_WIDTH:])], axis=0)
    h = h * jnp.exp(-emb[:, 0:1] * dl_ref[...])
    row = i * tl + lax.broadcasted_iota(jnp.int32, h.shape, 0)
    h = jnp.where(row == seq, 0.0, h)
    kb_ref[...] = h.astype(BF16)

    @pl.when(i == 0)
    def _():
        norm_ref[...] = jnp.zeros(norm_ref.shape, F32)

    norm_ref[...] += jnp.sum(jnp.abs(h), axis=0, keepdims=True)


def _hy_positions(seq):
    t = np.linspace(0.0, 1.0, seq, dtype=np.float32)[:, None]
    bands = (HY_EMB - 1) // 2
    freqs = np.linspace(1e-4, bands - 1, bands, dtype=np.float32)[None]
    w = (np.float32(2.0 * math.pi) * np.arange(seq, dtype=np.float32)[:, None] / np.float32(seq))
    zf = np.concatenate([t, np.cos(freqs * w), -np.sin(freqs * w)], axis=-1).astype(np.float32)
    pos = np.concatenate([np.arange(seq), [0], np.arange(seq - 1, 0, -1)])
    return jnp.asarray(np.pad(zf[pos], ((0, 0), (0, LANES - HY_EMB))))


def _block_diag2(w):
    z = jnp.zeros_like(w)
    return jnp.concatenate([jnp.concatenate([w, z], axis=1), jnp.concatenate([z, w], axis=1)], axis=0)


def _hy_filter(emb, w1, b1, fr1, w2, b2, fr2, w3, tl):
    n = emb.shape[0]
    seq = n // 2
    w1p = jnp.pad(w1, ((0, LANES - HY_EMB), (0, 0)))
    twice = lambda v: jnp.concatenate([v, v])[None]
    min_decay = math.log(HY_DECAY_TARGET) / HY_SLOW_DECAY
    max_decay = math.log(HY_DECAY_TARGET) / HY_FAST_DECAY
    deltas = jnp.asarray(np.abs(np.linspace(min_decay, max_decay, HY_WIDTH, dtype=np.float32))[None])
    cs = lambda r, c: pl.BlockSpec((r, c), lambda i: (0, 0))
    return pl.pallas_call(
        functools.partial(_hy_filter_kernel, tl=tl, seq=seq),
        grid=(n // tl,),
        in_specs=[pl.BlockSpec((tl, LANES), lambda i: (i, 0)),
                  cs(2 * LANES, LANES), cs(1, LANES), cs(1, LANES),
                  cs(LANES, LANES), cs(1, LANES), cs(1, LANES),
                  cs(LANES, 4 * HY_WIDTH), cs(1, HY_WIDTH)],
        out_specs=[pl.BlockSpec((tl, HY_WIDTH), lambda i: (i, 0)), pl.BlockSpec((1, HY_WIDTH), lambda i: (0, 0))],
        out_shape=[jax.ShapeDtypeStruct((n, HY_WIDTH), BF16), jax.ShapeDtypeStruct((1, HY_WIDTH), F32)],
        compiler_params=_cparams(("arbitrary",)),
        name="hyena_filter",
    )(emb, _block_diag2(w1p), twice(b1), twice(fr1), _block_diag2(w2), twice(b2), twice(fr2),
      _block_diag2(w3), deltas)


def _dft_tables(n1):
    n = n1 * DFT_N2
    a = np.arange(n1, dtype=np.int64)
    ang1 = (2.0 * math.pi / n1) * ((a[:, None] * a[None, :]) % n1)
    c1, s1 = np.cos(ang1), np.sin(ang1)
    k1 = np.arange(n1, dtype=np.int64)[:, None, None]
    k2 = np.arange(DFT_N2, dtype=np.int64)[None, :, None]
    n2 = np.arange(DFT_N2, dtype=np.int64)[None, None, :]
    ang = (2.0 * math.pi / n) * ((n2 * (k1 + n1 * k2)) % n)
    gc, gs = np.cos(ang), np.sin(ang)
    tb = lambda x: jnp.asarray(np.ascontiguousarray(x).astype(BF16))
    gc, gs = tb(gc), tb(gs)
    gct, gst = jnp.swapaxes(gc, 1, 2), jnp.swapaxes(gs, 1, 2)
    blk = lambda a, b, c, d: jnp.concatenate([jnp.concatenate([a, b], axis=2), jnp.concatenate([c, d], axis=2)], axis=1)
    return dict(c1=tb(c1), s1=tb(s1), g2=blk(gc, gs, -gs, gc), h2=blk(gct, -gst, gst, gct))


def _dft1_kernel(c_ref, s_ref, x_ref, re_ref, im_ref):
    x = x_ref[...]
    re_ref[...] = _dot(c_ref[...], x).astype(BF16)
    im_ref[...] = (-_dot(s_ref[...], x)).astype(BF16)


def _dft1(x, c1, s1, cb):
    k1, w = x.shape
    n1 = c1.shape[0]
    ospec = pl.BlockSpec((n1, cb), lambda j: (0, j))
    return pl.pallas_call(
        _dft1_kernel,
        grid=(w // cb,),
        in_specs=[pl.BlockSpec((n1, k1), lambda j: (0, 0)), pl.BlockSpec((n1, k1), lambda j: (0, 0)),
                  pl.BlockSpec((k1, cb), lambda j: (0, j))],
        out_specs=[ospec, ospec],
        out_shape=[jax.ShapeDtypeStruct((n1, w), BF16)] * 2,
        compiler_params=_cparams(("parallel",)),
        name="dft_stage1",
    )(c1, s1, x)


def _stack(re, im):
    return jnp.concatenate([re, im], axis=0)


def _conv_mid_kernel(g2_ref, h2_ref, are_ref, aim_ref, fre_ref, fim_ref, norm_ref, bre_ref, bim_ref, *, kb):
    inv = 1.0 / norm_ref[...]
    for t in range(kb):
        g2 = g2_ref[t]
        x = _dot(g2, _stack(are_ref[t], aim_ref[t]))
        xre, xim = x[:DFT_N2], x[DFT_N2:]
        f = _dot(g2, _stack(fre_ref[t], fim_ref[t])) * inv
        kre, kim = f[:DFT_N2], f[DFT_N2:]
        yre = (xre * kre - xim * kim).astype(BF16)
        yim = (xre * kim + xim * kre).astype(BF16)
        b = _dot(h2_ref[t], _stack(yre, yim))
        bre_ref[t] = b[:DFT_N2].astype(BF16)
        bim_ref[t] = b[DFT_N2:].astype(BF16)


def _conv_mid(tabs, are, aim, fre, fim, norm, kb):
    n1, _, c = are.shape
    gspec = pl.BlockSpec((kb, 2 * DFT_N2, 2 * DFT_N2), lambda i: (i, 0, 0))
    aspec = pl.BlockSpec((kb, DFT_N2, c), lambda i: (i, 0, 0))
    return pl.pallas_call(
        functools.partial(_conv_mid_kernel, kb=kb),
        grid=(n1 // kb,),
        in_specs=[gspec, gspec, aspec, aspec, aspec, aspec, pl.BlockSpec((1, c), lambda i: (0, 0))],
        out_specs=[aspec, aspec],
        out_shape=[jax.ShapeDtypeStruct((n1, DFT_N2, c), BF16)] * 2,
        compiler_params=_cparams(("parallel",)),
        name="conv_spectral",
    )(tabs["g2"], tabs["h2"], are, aim, fre, fim, norm)


def _idft1_kernel(c_ref, s_ref, bre_ref, bim_ref, o_ref, *, inv_n):
    y = (_dot(c_ref[...], bre_ref[...]) - _dot(s_ref[...], bim_ref[...])) * inv_n
    o_ref[...] = y.astype(o_ref.dtype)


def _idft1(c1h, s1h, bre, bim, cb):
    ko, n1 = c1h.shape
    w = bre.shape[1]
    n = n1 * DFT_N2
    ospec = pl.BlockSpec((ko, cb), lambda j: (0, j))
    return pl.pallas_call(
        functools.partial(_idft1_kernel, inv_n=1.0 / n),
        grid=(w // cb,),
        in_specs=[pl.BlockSpec((ko, n1), lambda j: (0, 0)), pl.BlockSpec((ko, n1), lambda j: (0, 0)),
                  pl.BlockSpec((n1, cb), lambda j: (0, j)), pl.BlockSpec((n1, cb), lambda j: (0, j))],
        out_specs=ospec,
        out_shape=jax.ShapeDtypeStruct((ko, w), BF16),
        compiler_params=_cparams(("parallel",)),
        name="idft_stage1",
    )(c1h, s1h, bre, bim)


def _hyena(z_all, tabs, emb, p, i, tm, cb, kb):
    seq = z_all.shape[0]
    n1 = 2 * seq // DFT_N2
    c = HY_WIDTH
    x0, zb = _hy_pre(z_all, p["hy_conv_w"][i], p["hy_conv_b"][i], min(tm, seq))
    kern_b, norm = _hy_filter(emb, p["hy_w1"][i], p["hy_b1"][i], p["hy_freq1"][i], p["hy_w2"][i],
                                 p["hy_b2"][i], p["hy_freq2"][i], p["hy_w3"][i], min(1024, seq))
    wide = DFT_N2 * c
    fre, fim = _dft1(kern_b.reshape(n1, wide), tabs["c1"], tabs["s1"], cb)
    are, aim = _dft1(zb.reshape(n1 // 2, wide), tabs["c1"][:, :n1 // 2], tabs["s1"][:, :n1 // 2], cb)
    bre, bim = _conv_mid(tabs, are.reshape(n1, DFT_N2, c), aim.reshape(n1, DFT_N2, c),
                         fre.reshape(n1, DFT_N2, c), fim.reshape(n1, DFT_N2, c), norm, kb)
    y = _idft1(tabs["c1"][:n1 // 2], tabs["s1"][:n1 // 2], bre.reshape(n1, wide), bim.reshape(n1, wide), cb)
    return y.reshape(seq, c), zb, x0


def _combine_kernel(x_ref, ya_ref, o0_ref, o1_ref, o2_ref, l0_ref, l1_ref, l2_ref, yc_ref,
                    hy_ref, hz_ref, hx_ref, skip_ref,
                    g0_ref, g1_ref, g2_ref, g3_ref, pa_ref, pb_ref, pc_ref, pd_ref, wo_ref, o_ref):
    l0, l1, l2 = l0_ref[...], l1_ref[...], l2_ref[...]
    mx = jnp.maximum(jnp.maximum(l0, l1), l2)
    e0, e1, e2 = jnp.exp(l0 - mx), jnp.exp(l1 - mx), jnp.exp(l2 - mx)
    yb = (e0 * o0_ref[...] + e1 * o1_ref[...] + e2 * o2_ref[...]) / (e0 + e1 + e2)
    f32 = lambda ref: ref[...].astype(F32)
    yd = f32(hx_ref) * (f32(hy_ref) + f32(hz_ref) * skip_ref[...])
    gate = lambda ref: _sigmoid(f32(ref))
    m = (gate(g0_ref) * _dot(ya_ref[...].astype(BF16), pa_ref[...])
         + gate(g1_ref) * _dot(yb.astype(BF16), pb_ref[...])
         + gate(g2_ref) * _dot(yc_ref[...].astype(BF16), pc_ref[...])
         + gate(g3_ref) * _dot(yd.astype(BF16), pd_ref[...]))
    o_ref[...] = x_ref[...] + _dot(m.astype(BF16), wo_ref[...])


def _combine(x, z, ya, dil_outs, yc, hyena, skip, pa, pb, pc, pd, wo, tm):
    s, d = x.shape
    rs = lambda w: pl.BlockSpec((tm, w), lambda i: (i, 0))
    gs = lambda b: pl.BlockSpec((tm, d), lambda i: (i, Z_GATE // d + b))
    ws = lambda a: pl.BlockSpec(a.shape, lambda i: (0, 0))
    (o0, l0), (o1, l1), (o2, l2) = dil_outs
    gw = DIL_HEADS * HEAD_DIM
    hy, hz, hx = hyena
    return pl.pallas_call(
        _combine_kernel,
        grid=(s // tm,),
        in_specs=[rs(d), rs(ya.shape[1]), rs(gw), rs(gw), rs(gw), rs(gw), rs(gw), rs(gw),
                  rs(yc.shape[1]), rs(HY_WIDTH), rs(HY_WIDTH), rs(HY_WIDTH), ws(skip), gs(0), gs(1), gs(2), gs(3),
                  ws(pa), ws(pb), ws(pc), ws(pd), ws(wo)],
        out_specs=rs(d),
        out_shape=jax.ShapeDtypeStruct((s, d), F32),
        compiler_params=_cparams(("parallel",)),
        name="combine",
    )(x, ya, o0, o1, o2, l0, l1, l2, yc, hy, hz, hx, skip, z, z, z, z, pa, pb, pc, pd, wo)


def _mlp_kernel(x_ref, g_ref, w1_ref, w2_ref, o_ref, h_ref, acc_ref):
    j = pl.program_id(1)

    @pl.when(j == 0)
    def _():
        h_ref[...] = (_rms(x_ref[...]) * g_ref[...]).astype(BF16)
        acc_ref[...] = jnp.zeros(acc_ref.shape, F32)

    a = jnp.maximum(_dot(h_ref[...], w1_ref[...]), 0.0)
    acc_ref[...] += _dot((a * a).astype(BF16), w2_ref[...])

    @pl.when(j == pl.num_programs(1) - 1)
    def _():
        o_ref[...] = x_ref[...] + acc_ref[...]


def _mlp(x, g, w1, w2, tm, tf):
    s, d = x.shape
    ff = w1.shape[1]
    return pl.pallas_call(
        _mlp_kernel,
        grid=(s // tm, ff // tf),
        in_specs=[pl.BlockSpec((tm, d), lambda i, j: (i, 0)),
                  pl.BlockSpec((1, d), lambda i, j: (0, 0)),
                  pl.BlockSpec((d, tf), lambda i, j: (0, j)),
                  pl.BlockSpec((tf, d), lambda i, j: (j, 0))],
        out_specs=pl.BlockSpec((tm, d), lambda i, j: (i, 0)),
        out_shape=jax.ShapeDtypeStruct((s, d), F32),
        scratch_shapes=[pltpu.VMEM((tm, d), BF16), pltpu.VMEM((tm, d), F32)],
        compiler_params=_cparams(("parallel", "arbitrary")),
        name="mlp",
    )(x, g.reshape(1, d), w1, w2)


def _permute_w_in(w):
    d = w.shape[0]
    pieces = [w[:, _O_DU:_O_GATE], w[:, _O_B:_O_C], w[:, _O_AQ:_O_AK], w[:, _O_C:_O_DU],
              w[:, _O_AV:_O_AR], w[:, _O_AR:_O_ALR], w[:, _O_AK:_O_AV], w[:, _O_ALR:_O_B],
              jnp.zeros((d, Z_GATE - Z_ALR - 2 * GLA_RANK), w.dtype), w[:, _O_GATE:_O_END]]
    return jnp.concatenate(pieces, axis=1).astype(BF16)


def _dil_bias_idx(tq, dil):
    rel = _skewed_rel(tq, tq + 2 * DIL_HALF, -1, -DIL_HALF)
    return _t5_bucket(rel * dil)[None].astype(np.int32)


def _diff_bias_idx(tq, tk):
    far_lo, far_hi = _diff_far_offsets(tq, tk)
    return np.stack([_t5_bucket(_skewed_rel(tk, tq, 1, o * tk)) for o in range(far_lo, far_hi + 1)]).astype(np.int32)


def _forward(x, p, *, t_diff, tk_diff, qc_diff, tq_dil, nt_dil, tb_gla, tm_proj, tn_proj, tm_prep, tm_row, tm_mlp, tf_mlp, tm_hy, cb_dft, kb_dft):
    seq = x.shape[0]
    depth = p["w_in"].shape[0]
    n_dil_bias = DIL_GROUPS * DIL_HEADS
    tq_dil = [min(tq, seq // dil) for tq, (_, dil) in zip(tq_dil, DIL_PATTERNS)]
    dil_bias = [_bias_tiles(p["t5_bias"], _dil_bias_idx(tq, dil), tq, tq + 2 * DIL_HALF, DIL_HEADS, gi * DIL_HEADS)
                for gi, (tq, (_, dil)) in enumerate(zip(tq_dil, DIL_PATTERNS))]
    t_diff, tk_diff = min(t_diff, seq), min(tk_diff, seq)
    diff_bias = _bias_tiles(p["t5_bias"], _diff_bias_idx(t_diff, tk_diff), tk_diff, t_diff, DIFF_HEADS,
                            n_dil_bias, LOG2E)
    tabs = _dft_tables(2 * seq // DFT_N2)
    emb = _hy_positions(seq)
    rep = lambda g, n: jnp.tile(g, n)
    for i in range(depth):
        z = _norm_matmul(x, p["norm1_g"][i], _permute_w_in(p["w_in"][i]), tm_proj, tn_proj)
        wg = [jnp.zeros((LANES, GLA_HEADS * GLA_DK), F32).at[j * GLA_RANK:(j + 1) * GLA_RANK].set(
            p["gla_gate_w"][i, j]).astype(BF16) for j in range(2)]
        gb = p["gla_gate_b"][i]
        o_fwd = _gla_scan(z, wg[0], gb[0:1], tb_gla, False)
        ya = _gla_scan(z, wg[1], gb[1:2], tb_gla, True, fin=(o_fwd, p["gla_norm_g"][i]))
        bq, bk, bv, cq, ck, cv = _prep(
            z, jnp.repeat(p["dil_qnorm_g"][i], DIL_HEADS, axis=0).reshape(-1),
            jnp.repeat(p["dil_knorm_g"][i], DIL_HEADS, axis=0).reshape(-1),
            rep(p["diff_qnorm_g"][i], 2 * DIFF_HEADS), rep(p["diff_knorm_g"][i], 2 * DIFF_HEADS), tm_prep)
        dil_outs = [_dil_attn(bq, bk, bv, dil_bias[gi], gi, dil, tq_dil[gi], max(1, min(nt_dil[gi], seq // (tq_dil[gi] * dil))))
                    for gi, (_, dil) in enumerate(DIL_PATTERNS)]
        lam_init = 0.8 - 0.6 * math.exp(-0.3 * i)
        yc = _diff_attn(cq, ck, cv, diff_bias, p["diff_lambda"][i], p["diff_subln_g"][i], lam_init,
                        t_diff, tk_diff, qc_diff)
        yd = _hyena(z, tabs, emb, p, i, tm_hy, cb_dft, kb_dft)
        x = _combine(x, z, ya, dil_outs, yc, yd, p["hy_skip"][i][None],
                     p["proj_a"][i].astype(BF16), p["proj_b"][i].astype(BF16),
                     p["proj_c"][i].astype(BF16), p["proj_d"][i].astype(BF16), p["w_out"][i].astype(BF16),
                     tm_row)
        x = _mlp(x, p["norm2_g"][i], p["mlp_w1"][i].astype(BF16), p["mlp_w2"][i].astype(BF16), tm_mlp, tf_mlp)
    return x


def kernel(x, t5_bias, norm1_g, w_in, gla_gate_w, gla_gate_b, gla_norm_g, dil_qnorm_g, dil_knorm_g,
           diff_qnorm_g, diff_knorm_g, diff_lambda, diff_subln_g, hy_conv_w, hy_conv_b, hy_w1, hy_b1,
           hy_freq1, hy_w2, hy_b2, hy_freq2, hy_w3, hy_skip, proj_a, proj_b, proj_c, proj_d, w_out,
           norm2_g, mlp_w1, mlp_w2):
    p = dict(t5_bias=t5_bias, norm1_g=norm1_g, w_in=w_in, gla_gate_w=gla_gate_w, gla_gate_b=gla_gate_b,
             gla_norm_g=gla_norm_g, dil_qnorm_g=dil_qnorm_g, dil_knorm_g=dil_knorm_g,
             diff_qnorm_g=diff_qnorm_g, diff_knorm_g=diff_knorm_g, diff_lambda=diff_lambda,
             diff_subln_g=diff_subln_g, hy_conv_w=hy_conv_w, hy_conv_b=hy_conv_b, hy_w1=hy_w1, hy_b1=hy_b1,
             hy_freq1=hy_freq1, hy_w2=hy_w2, hy_b2=hy_b2, hy_freq2=hy_freq2, hy_w3=hy_w3, hy_skip=hy_skip,
             proj_a=proj_a, proj_b=proj_b, proj_c=proj_c, proj_d=proj_d, w_out=w_out, norm2_g=norm2_g,
             mlp_w1=mlp_w1, mlp_w2=mlp_w2)
    b, s, d = x.shape
    tiles = dict(
        t_diff=2048, tk_diff=1024, qc_diff=(256, 768, 768, 256),
        tq_dil=(256, 256, 128), nt_dil=(4, 1, 1),
        tb_gla=512, tm_proj=2048, tn_proj=1024, tm_prep=1024, tm_row=512, tm_mlp=1024, tf_mlp=2048,
        tm_hy=1024, cb_dft=8192, kb_dft=16)
    outs = [_forward(x[bi], p, **tiles) for bi in range(b)]
    return jnp.stack(outs)
```

```python
import functools
import math

import jax
import jax.numpy as jnp
import numpy as np
from jax import lax
from jax.experimental import pallas as pl
from jax.experimental.pallas import tpu as pltpu

F32 = jnp.float32
BF16 = jnp.bfloat16

D_MODEL = 1024
HEAD_DIM = 64
GLA_HEADS = 4
GLA_DK = 64
GLA_DV = 128
GLA_RANK = 16
GLA_TAU = 16.0
GLA_CHUNK = 64
DIL_PATTERNS = ((128, 1), (512, 4), (2048, 16))
DIL_GROUPS = 3
DIL_HEADS = 4
DIL_HALF = 64
DIFF_HEADS = 4
DIFF_DV = 128
DIFF_VT = DIFF_DV + 16
HY_WIDTH = 512
HY_EMB = 33
HY_FFN = 64
HY_DECAY_TARGET = 1e-2
HY_FAST_DECAY = 0.3
HY_SLOW_DECAY = 1.5
T5_BUCKETS = 32
T5_MAX_DIST = 1024
BUCKET_GROUP = 4
N_BIAS_HEADS = 16
D_FF = 4096
RMS_EPS = 1e-6
LOG2E = math.log2(math.e)

LANES = 128
HALO_ROWS = 16
VMEM_LIMIT = 48 * 1024 * 1024

Z_DU = 0
Z_BQ, Z_BK, Z_BV = 1536, 2304, 3072
Z_AQ = 3840
Z_CQ, Z_CK, Z_CV = 4096, 4608, 5120
Z_AV, Z_AR = 5632, 6144
Z_AK = 6656
Z_ALR = 6912
Z_GATE = 7168
Z_COLS = 11264

_O_AQ, _O_AK, _O_AV, _O_AR, _O_ALR, _O_B, _O_C, _O_DU, _O_GATE, _O_END = (
    0, 256, 512, 1024, 1536, 1568, 3872, 5408, 6944, 11040)

DFT_N2 = 128


def _cparams(sem):
    return pltpu.CompilerParams(dimension_semantics=sem, vmem_limit_bytes=VMEM_LIMIT)


def _dot(a, b):
    return jnp.dot(a, b, preferred_element_type=F32)


def _dot_nt(a, b):
    return lax.dot_general(a, b, (((1,), (1,)), ((), ())), preferred_element_type=F32)


def _dot_tn(a, b):
    return lax.dot_general(a, b, (((0,), (0,)), ((), ())), preferred_element_type=F32)


def _split(x):
    hi = x.astype(BF16)
    lo = (x - hi.astype(F32)).astype(BF16)
    return hi, lo


def _dot3(a, b):
    ah, al = _split(a)
    bh, bl = _split(b)
    return _dot(ah, bh) + _dot(ah, bl) + _dot(al, bh)


def _rms(x):
    return x * lax.rsqrt(jnp.mean(x * x, axis=-1, keepdims=True) + RMS_EPS)


def _sigmoid(x):
    return 0.5 * jnp.tanh(0.5 * x) + 0.5


def _norm_matmul_kernel(x_ref, g_ref, w_ref, o_ref, h_ref):
    @pl.when(pl.program_id(1) == 0)
    def _():
        h_ref[...] = (_rms(x_ref[...]) * g_ref[...]).astype(BF16)

    o_ref[...] = _dot(h_ref[...], w_ref[...]).astype(o_ref.dtype)


def _norm_matmul(x, g, w, tm, tn):
    s, d = x.shape
    n = w.shape[1]
    return pl.pallas_call(
        _norm_matmul_kernel,
        grid=(s // tm, n // tn),
        in_specs=[pl.BlockSpec((tm, d), lambda i, j: (i, 0)),
                  pl.BlockSpec((1, d), lambda i, j: (0, 0)),
                  pl.BlockSpec((d, tn), lambda i, j: (0, j))],
        out_specs=pl.BlockSpec((tm, tn), lambda i, j: (i, j)),
        out_shape=jax.ShapeDtypeStruct((s, n), BF16),
        scratch_shapes=[pltpu.VMEM((tm, d), BF16)],
        compiler_params=_cparams(("parallel", "arbitrary")),
        name="in_proj",
    )(x, g.reshape(1, d), w)


def _group_norm(x, e, gain):
    hi, lo = _split(x * x)
    ew = e.shape[0]
    ms = jnp.concatenate([_dot(hi[:, c:c + ew], e) + _dot(lo[:, c:c + ew], e)
                          for c in range(0, x.shape[1], ew)], axis=1) * (1.0 / HEAD_DIM)
    return x * lax.rsqrt(ms + RMS_EPS) * gain


def _prep_kernel(bq_ref, bk_ref, bv_ref, cq_ref, ck_ref, cv_ref, e_ref,
                 gbq_ref, gbk_ref, gcq_ref, gck_ref,
                 obq_ref, obk_ref, obv_ref, ocq_ref, ock_ref, ocv_ref):
    scale = HEAD_DIM ** -0.5
    eb = ec = e_ref[...]
    f32 = lambda ref: ref[...].astype(F32)
    obq_ref[...] = (_group_norm(f32(bq_ref), eb, gbq_ref[...]) * scale).astype(BF16)
    obk_ref[...] = _group_norm(f32(bk_ref), eb, gbk_ref[...]).astype(BF16)
    obv_ref[...] = bv_ref[...]
    ocq_ref[...] = (_group_norm(f32(cq_ref), ec, gcq_ref[...]) * (scale * LOG2E)).T.astype(BF16)
    ock_ref[...] = _group_norm(f32(ck_ref), ec, gck_ref[...]).astype(BF16)
    cvt = f32(cv_ref).T
    ones = jnp.ones((DIFF_VT - DIFF_DV, cvt.shape[1]), F32)
    ocv_ref[...] = jnp.concatenate(
        [t for h in range(DIFF_HEADS) for t in (cvt[h * DIFF_DV:(h + 1) * DIFF_DV], ones)], axis=0).astype(BF16)


def _block_diag_ones(width):
    idx = np.arange(width) // HEAD_DIM
    return jnp.asarray(idx[:, None] == idx[None, :], dtype=BF16)


def _prep(z, gbq, gbk, gcq, gck, tm):
    s = z.shape[0]
    wb, wc = DIL_GROUPS * DIL_HEADS * HEAD_DIM, DIFF_HEADS * 2 * HEAD_DIM
    zspec = lambda w, off: pl.BlockSpec((tm, w), lambda i: (i, off // w))
    cspec = lambda r, c: pl.BlockSpec((r, c), lambda i: (0, 0))
    ospec = lambda w: pl.BlockSpec((tm, w), lambda i: (i, 0))
    tspec = lambda w: pl.BlockSpec((w, tm), lambda i: (0, i))
    return pl.pallas_call(
        _prep_kernel,
        grid=(s // tm,),
        in_specs=[zspec(wb, Z_BQ), zspec(wb, Z_BK), zspec(wb, Z_BV),
                  zspec(wc, Z_CQ), zspec(wc, Z_CK), zspec(wc, Z_CV),
                  cspec(2 * LANES, 2 * LANES),
                  cspec(1, wb), cspec(1, wb), cspec(1, wc), cspec(1, wc)],
        out_specs=[ospec(wb), ospec(wb), ospec(wb), tspec(wc), ospec(wc), tspec(DIFF_HEADS * DIFF_VT)],
        out_shape=[jax.ShapeDtypeStruct((s, wb), BF16)] * 3
        + [jax.ShapeDtypeStruct((wc, s), BF16), jax.ShapeDtypeStruct((s, wc), BF16),
           jax.ShapeDtypeStruct((DIFF_HEADS * DIFF_VT, s), BF16)],
        compiler_params=_cparams(("parallel",)),
        name="qk_prep",
    )(z, z, z, z, z, z, _block_diag_ones(2 * LANES),
      gbq.reshape(1, wb), gbk.reshape(1, wb), gcq.reshape(1, wc), gck.reshape(1, wc))


def _t5_bucket(rel):
    half = T5_BUCKETS // 2
    max_exact = half // 2
    ret = np.where(rel > 0, half, 0)
    n = np.abs(rel)
    nf = np.maximum(n, 1).astype(np.float64)
    large = max_exact + (np.log(nf / max_exact) / math.log(T5_MAX_DIST / max_exact)
                         * (half - max_exact)).astype(np.int64)
    large = np.minimum(large, half - 1)
    return ret + np.where(n < max_exact, n, large)


def _bias_kernel(tab_ref, rng_ref, idx_ref, o_ref, p_ref, *, head_base, out_scale, chunk):
    t = pl.program_id(0)
    col = head_base + pl.program_id(1)
    rows, width = o_ref.shape[2:]
    slabs = width // LANES
    lo, hi = rng_ref[t, 0], rng_ref[t, 1]

    def one_chunk(ci, carry):
        rs = pl.ds(pl.multiple_of(ci * chunk, chunk), chunk)
        idx = idx_ref[0, rs, :]
        def bucket_group(g, acc):
            for b in range(BUCKET_GROUP):
                bb = g * BUCKET_GROUP + b
                acc = jnp.where(idx == bb, tab_ref[bb, col] * out_scale, acc)
            return acc

        p_ref[rs, :] = lax.fori_loop(lo // BUCKET_GROUP, hi // BUCKET_GROUP + 1, bucket_group,
                                     jnp.zeros((chunk, LANES), F32))
        return carry

    lax.fori_loop(0, p_ref.shape[0] // chunk, one_chunk, 0)
    for c in range(slabs):
        off = LANES * (slabs - 1 - c)
        o_ref[0, 0, :, c * LANES:(c + 1) * LANES] = p_ref[off:off + rows, :]


def _skewed_rel(rows, width, sign, base):
    slabs = width // LANES
    rho = np.arange(rows + LANES * (slabs - 1))[:, None]
    lane = np.arange(LANES)[None, :]
    return sign * (rho - LANES * (slabs - 1) - lane) + base


def _bias_tiles(t5_bias, idx, rows, width, n_heads, head_base, out_scale=1.0, chunk=128):
    nt, rp, _ = idx.shape
    ranges = np.stack([idx.reshape(nt, -1).min(axis=1), idx.reshape(nt, -1).max(axis=1)], axis=1).astype(np.int32)
    return pl.pallas_call(
        functools.partial(_bias_kernel, head_base=head_base, out_scale=out_scale, chunk=chunk),
        grid=(nt, n_heads),
        in_specs=[pl.BlockSpec(memory_space=pltpu.SMEM), pl.BlockSpec(memory_space=pltpu.SMEM),
                  pl.BlockSpec((1, rp, LANES), lambda t, h: (t, 0, 0))],
        out_specs=pl.BlockSpec((1, 1, rows, width), lambda t, h: (t, h, 0, 0)),
        out_shape=jax.ShapeDtypeStruct((nt, n_heads, rows, width), F32),
        scratch_shapes=[pltpu.VMEM((rp, LANES), F32)],
        compiler_params=_cparams(("parallel", "parallel")),
        name="t5_bias_tiles",
    )(t5_bias, jnp.asarray(ranges), jnp.asarray(idx))


def _diff_attn_kernel(qt_ref, k_ref, vt_ref, bias_ref, lam_ref, g_ref, o_ref,
                      qa_ref, qb_ref, m_ref, acc_ref, s_ref, *, lam_init, qc, far_lo, far_hi):
    j = pl.program_id(2)
    t = qt_ref.shape[1]

    @pl.when(j == 0)
    def _():
        qt = qt_ref[...]
        row = lax.broadcasted_iota(jnp.int32, qt.shape, 0)
        qa_ref[...] = jnp.where(row < HEAD_DIM, qt, jnp.zeros_like(qt))
        qb_ref[...] = jnp.where(row >= HEAD_DIM, qt, jnp.zeros_like(qt))
        m_ref[...] = jnp.full(m_ref.shape, -jnp.inf, F32)
        acc_ref[...] = jnp.zeros(acc_ref.shape, F32)

    starts = [sum(qc[:u]) for u in range(len(qc))]
    cuts = [slice(a, a + w) for a, w in zip(starts, qc)]
    chains = [(0, cs) for cs in cuts] + [(1, cs) for cs in reversed(cuts)]
    q_refs = (qa_ref, qb_ref)

    def step(far):
        if far:
            const = bias_ref[0, 0, 0:1, 0:1]
            scores = lambda c, cols: _dot(k_ref[...], q_refs[c][:, cols])
        else:
            const = 0.0
            scores = lambda c, cols: _dot(k_ref[...], q_refs[c][:, cols]) + bias_ref[0, 0, :, cols]
        width = lambda n: chains[n][1].stop - chains[n][1].start
        slots = s_ref.shape[0]
        for n in range(min(slots - 1, len(chains))):
            s_ref[n, :, :width(n)] = scores(*chains[n])
        for n, (c, cols) in enumerate(chains):
            ahead = n + slots - 1
            if ahead < len(chains):
                s_ref[ahead % slots, :, :width(ahead)] = scores(*chains[ahead])
            s = s_ref[n % slots, :, :width(n)]
            m_old = m_ref[c, :, cols]
            m_new = jnp.maximum(m_old, jnp.max(s, axis=0, keepdims=True) + const)
            alpha = jnp.exp2(m_old - m_new)
            p = jnp.exp2(s - (m_new - const)).astype(BF16)
            acc_ref[c, :, cols] = alpha * acc_ref[c, :, cols] + _dot(vt_ref[...], p)
            m_ref[c, :, cols] = m_new

    off = j - pl.program_id(1) * (t // k_ref.shape[0])
    is_far = jnp.logical_or(off <= far_lo, off >= far_hi)
    pl.when(is_far)(lambda: step(True))
    pl.when(jnp.logical_not(is_far))(lambda: step(False))

    @pl.when(j == pl.num_programs(2) - 1)
    def _():
        lp = lam_ref[...]
        lam = (jnp.exp(jnp.sum(lp[0:1] * lp[1:2], axis=-1, keepdims=True))
               - jnp.exp(jnp.sum(lp[2:3] * lp[3:4], axis=-1, keepdims=True)) + lam_init)
        a0 = acc_ref[0]
        a1 = acc_ref[1]
        o0 = a0[:DIFF_DV] / a0[DIFF_DV:DIFF_DV + 1]
        o1 = a1[:DIFF_DV] / a1[DIFF_DV:DIFF_DV + 1]
        att = o0 - lam * o1
        y = att * lax.rsqrt(jnp.mean(att * att, axis=0, keepdims=True) + RMS_EPS)
        o_ref[...] = y.T * g_ref[...] * (1.0 - lam_init)


def _diff_far_offsets(tq, tk):
    far_lo = (-T5_MAX_DIST - tk + 1) // tk
    far_hi = -(-(T5_MAX_DIST + tq - 1) // tk)
    return far_lo, far_hi


def _diff_attn(cqt, ck, cvt, bias_tiles, lam_p, subln_g, lam_init, t, tk, qc):
    s = ck.shape[0]
    far_lo, far_hi = _diff_far_offsets(t, tk)
    ratio = t // tk
    w = 2 * HEAD_DIM
    return pl.pallas_call(
        functools.partial(_diff_attn_kernel, lam_init=lam_init, qc=qc, far_lo=far_lo, far_hi=far_hi),
        grid=(DIFF_HEADS, s // t, s // tk),
        in_specs=[pl.BlockSpec((w, t), lambda h, i, j: (h, i)),
                  pl.BlockSpec((tk, w), lambda h, i, j: (j, h)),
                  pl.BlockSpec((DIFF_VT, tk), lambda h, i, j: (h, j)),
                  pl.BlockSpec((1, 1, tk, t),
                               lambda h, i, j: (jnp.clip(j - i * ratio, far_lo, far_hi) - far_lo, h, 0, 0)),
                  pl.BlockSpec((4, HEAD_DIM), lambda h, i, j: (0, 0)),
                  pl.BlockSpec((1, DIFF_DV), lambda h, i, j: (0, 0))],
        out_specs=pl.BlockSpec((t, DIFF_DV), lambda h, i, j: (i, h)),
        out_shape=jax.ShapeDtypeStruct((s, DIFF_HEADS * DIFF_DV), F32),
        scratch_shapes=[pltpu.VMEM((w, t), BF16), pltpu.VMEM((w, t), BF16),
                        pltpu.VMEM((2, 1, t), F32), pltpu.VMEM((2, DIFF_VT, t), F32),
                        pltpu.VMEM((2, tk, max(qc)), F32)],
        compiler_params=_cparams(("parallel", "parallel", "arbitrary")),
        name="diff_attn",
    )(cqt, ck, cvt, bias_tiles, lam_p, subln_g.reshape(1, DIFF_DV))


def _dil_kernel(q_ref, kp_ref, kc_ref, kn_ref, vp_ref, vc_ref, vn_ref, bias_ref, o_ref, lse_ref,
                qs_ref, ks_ref, vs_ref, os_ref, ls_ref, *, tq, dil, nt, m_len):
    n = pl.program_id(0)
    hd = DIL_HALF * dil
    body_rows = tq * dil * nt
    halves = DIL_HEADS * HEAD_DIM // LANES

    def put(dst, rows, src_ref):
        x = src_ref[...].astype(F32)
        for t in range(halves):
            dst[t, rows, :] = x[:, t * LANES:(t + 1) * LANES]

    def strided(src, r, count):
        return jnp.concatenate([src[t, pl.ds(r, count, stride=dil), :] for t in range(halves)], axis=1)

    put(qs_ref, slice(0, body_rows), q_ref)
    for dst, (p_ref, c_ref, n_ref) in ((ks_ref, (kp_ref, kc_ref, kn_ref)), (vs_ref, (vp_ref, vc_ref, vn_ref))):
        put(dst, slice(0, hd), p_ref)
        put(dst, slice(hd, hd + body_rows), c_ref)
        put(dst, slice(hd + body_rows, 2 * hd + body_rows), n_ref)
    tk = tq + 2 * DIL_HALF
    a = lax.broadcasted_iota(jnp.int32, (tq, tk), 0)
    c = lax.broadcasted_iota(jnp.int32, (tq, tk), 1)
    delta = c - DIL_HALF - a
    in_band = jnp.concatenate([jnp.abs(delta) <= DIL_HALF] * DIL_HEADS, axis=0)
    col = lax.broadcasted_iota(jnp.int32, (1, tk), 1)
    lane = lax.broadcasted_iota(jnp.int32, (tq, DIL_HEADS * HEAD_DIM), 1)
    head_masks = [(lane // HEAD_DIM) == h for h in range(DIL_HEADS)]

    def one_tile(it, carry):
        u = it // dil
        r = it % dil + u * (tq * dil)
        kpos = (n * nt + u) * tq - DIL_HALF + col
        valid = jnp.logical_and(in_band, jnp.logical_and(kpos >= 0, kpos < m_len))
        q = strided(qs_ref, r, tq).astype(BF16)
        k = strided(ks_ref, r, tk).astype(BF16)
        v = strided(vs_ref, r, tk).astype(BF16)
        qs = jnp.concatenate([jnp.where(hm, q, jnp.zeros_like(q)) for hm in head_masks], axis=0)
        s = _dot_nt(qs, k) + bias_ref[0].reshape(DIL_HEADS * tq, tk)
        s = jnp.where(valid, s, -1e30)
        m = jnp.max(s, axis=-1, keepdims=True)
        e = jnp.exp(s - m)
        l = jnp.sum(e, axis=-1, keepdims=True)
        oh = _dot((e / l).astype(BF16), v)
        lse = m + jnp.log(l)
        o = jnp.zeros(q.shape, F32)
        lse_o = jnp.zeros(q.shape, F32)
        for h, hm in enumerate(head_masks):
            rows = slice(h * tq, (h + 1) * tq)
            o = jnp.where(hm, oh[rows], o)
            lse_o = jnp.where(hm, lse[rows], lse_o)
        for t in range(halves):
            os_ref[t, pl.ds(r, tq, stride=dil), :] = o[:, t * LANES:(t + 1) * LANES]
            ls_ref[t, pl.ds(r, tq, stride=dil), :] = lse_o[:, t * LANES:(t + 1) * LANES]
        return carry

    lax.fori_loop(0, dil * nt, one_tile, 0, unroll=min(dil * nt, 4))
    o_ref[...] = jnp.concatenate([os_ref[t] for t in range(halves)], axis=1)
    lse_ref[...] = jnp.concatenate([ls_ref[t] for t in range(halves)], axis=1)


def _dil_attn(bq, bk, bv, bias, gi, dil, tq, nt):
    s = bq.shape[0]
    m_len = s // dil
    gw = DIL_HEADS * HEAD_DIM
    rows = tq * dil * nt
    hd = DIL_HALF * dil
    hb = rows // hd
    last = s // hd - 1
    qspec = pl.BlockSpec((rows, gw), lambda n: (n, gi))
    pspec = pl.BlockSpec((hd, gw), lambda n: (jnp.maximum(n * hb - 1, 0), gi))
    nspec = pl.BlockSpec((hd, gw), lambda n: (jnp.minimum((n + 1) * hb, last), gi))
    ospec = pl.BlockSpec((rows, gw), lambda n: (n, 0))
    return pl.pallas_call(
        functools.partial(_dil_kernel, tq=tq, dil=dil, nt=nt, m_len=m_len),
        grid=(s // rows,),
        in_specs=[qspec, pspec, qspec, nspec, pspec, qspec, nspec,
                  pl.BlockSpec((1, DIL_HEADS, tq, tq + 2 * DIL_HALF), lambda n: (0, 0, 0, 0))],
        out_specs=[ospec, ospec],
        out_shape=[jax.ShapeDtypeStruct((s, gw), F32)] * 2,
        scratch_shapes=[pltpu.VMEM((gw // LANES, rows, LANES), F32),
                        pltpu.VMEM((gw // LANES, rows + 2 * hd, LANES), F32),
                        pltpu.VMEM((gw // LANES, rows + 2 * hd, LANES), F32),
                        pltpu.VMEM((gw // LANES, rows, LANES), F32),
                        pltpu.VMEM((gw // LANES, rows, LANES), F32)],
        compiler_params=_cparams(("parallel",)),
        name=f"dil_attn_g{gi}",
    )(bq, bk, bk, bk, bv, bv, bv, bias)


def _gla_kernel(*refs, reverse, tb, finalize):
    if finalize:
        (q_ref, k_ref, v_ref, lr_ref, wg_ref, gb_ref, tri_ref, ofwd_ref, r_ref, ng_ref,
         o_ref, s_ref, oacc_ref) = refs
    else:
        q_ref, k_ref, v_ref, lr_ref, wg_ref, gb_ref, tri_ref, o_ref, s_ref = refs
        oacc_ref = o_ref
    cw = GLA_CHUNK
    qk = GLA_HEADS * GLA_DK
    vw = GLA_HEADS * GLA_DV

    @pl.when(pl.program_id(0) == 0)
    def _():
        s_ref[...] = jnp.zeros(s_ref.shape, F32)

    logits = _dot(lr_ref[...], wg_ref[...]) + gb_ref[...]
    g = (jnp.minimum(logits, 0.0) - jnp.log(1.0 + jnp.exp(-jnp.abs(logits)))) * (1.0 / GLA_TAU)
    ghi, glo = _split(g)
    tri = tri_ref[...]
    b = _dot(tri, ghi) + _dot(tri, glo)

    lane_q = lax.broadcasted_iota(jnp.int32, (cw, qk), 1)
    rr = lax.broadcasted_iota(jnp.int32, (GLA_HEADS * cw, cw), 0)
    cc = lax.broadcasted_iota(jnp.int32, (GLA_HEADS * cw, cw), 1)
    tt = rr % cw
    amask = (cc > tt) if reverse else (cc <= tt)
    srow = lax.broadcasted_iota(jnp.int32, (vw, qk), 0) // GLA_DV
    scol = lax.broadcasted_iota(jnp.int32, (vw, qk), 1) // GLA_DK
    bdmask = srow == scol

    n_chunks = tb // cw
    order = range(n_chunks - 1, -1, -1) if reverse else range(n_chunks)
    for ci in order:
        rows = slice(ci * cw, (ci + 1) * cw)
        bc = b[rows]
        b_end = bc[0:1] if reverse else bc[cw - 1:cw]
        k_c = k_ref[rows, :].astype(F32)
        v_c = v_ref[rows, :]
        kg_c = (k_c * jnp.exp(-bc)).astype(BF16)
        kdec = (k_c * jnp.exp(b_end - bc)).astype(BF16)
        qg_c = (q_ref[rows, :].astype(F32) * (GLA_DK ** -0.5) * jnp.exp(bc)).astype(BF16)
        qs = jnp.concatenate(
            [jnp.where((lane_q // GLA_DK) == h, qg_c, jnp.zeros_like(qg_c)) for h in range(GLA_HEADS)], axis=0)
        a = jnp.where(amask, _dot_nt(qs, kg_c), 0.0)
        obig = _dot(a.astype(BF16), v_c)
        o_intra = jnp.concatenate(
            [obig[h * cw:(h + 1) * cw, h * GLA_DV:(h + 1) * GLA_DV] for h in range(GLA_HEADS)], axis=1)
        state = s_ref[...]
        oacc_ref[rows, :] = o_intra + _dot_nt(qg_c, state.astype(BF16))
        ds = _dot_tn(v_c, kdec)
        s_ref[...] = jnp.where(bdmask, jnp.exp(b_end) * state + ds, 0.0)

    if finalize:
        o = ofwd_ref[...] + oacc_ref[...]
        r = r_ref[...].astype(F32)
        outs = []
        for h in range(GLA_HEADS):
            sl = slice(h * GLA_DV, (h + 1) * GLA_DV)
            outs.append(_rms(o[:, sl]) * ng_ref[...] * (r[:, sl] * _sigmoid(r[:, sl])))
        o_ref[...] = jnp.concatenate(outs, axis=1)


def _chunk_tri(tb, reverse):
    i = np.arange(tb)
    same = (i[:, None] // GLA_CHUNK) == (i[None, :] // GLA_CHUNK)
    tri = (i[None, :] >= i[:, None]) if reverse else (i[None, :] <= i[:, None])
    return jnp.asarray(same & tri, dtype=BF16)


def _gla_scan(z, wg, gb, tb, reverse, fin=None):
    s = z.shape[0]
    nb = s // tb
    qk = GLA_HEADS * GLA_DK
    vw = GLA_HEADS * GLA_DV
    blk = (lambda i: nb - 1 - i) if reverse else (lambda i: i)
    zspec = lambda w, off: pl.BlockSpec((tb, w), lambda i: (blk(i), off // w))
    cspec = lambda r, c: pl.BlockSpec((r, c), lambda i: (0, 0))
    in_specs = [zspec(qk, Z_AQ), zspec(qk, Z_AK), zspec(vw, Z_AV), zspec(LANES, Z_ALR),
                cspec(LANES, qk), cspec(1, qk), cspec(tb, tb)]
    args = [z, z, z, z, wg, gb, _chunk_tri(tb, reverse)]
    scratch = [pltpu.VMEM((vw, qk), F32)]
    if fin is not None:
        o_fwd, norm_g = fin
        in_specs += [pl.BlockSpec((tb, vw), lambda i: (blk(i), 0)), zspec(vw, Z_AR), cspec(1, GLA_DV)]
        args += [o_fwd, z, norm_g.reshape(1, GLA_DV)]
        scratch.append(pltpu.VMEM((tb, vw), F32))
    return pl.pallas_call(
        functools.partial(_gla_kernel, reverse=reverse, tb=tb, finalize=fin is not None),
        grid=(nb,),
        in_specs=in_specs,
        out_specs=pl.BlockSpec((tb, vw), lambda i: (blk(i), 0)),
        out_shape=jax.ShapeDtypeStruct((s, vw), F32),
        scratch_shapes=scratch,
        compiler_params=_cparams(("arbitrary",)),
        name="gla_bwd" if reverse else "gla_fwd",
    )(*args)


def _hy_pre_kernel(u_ref, up_ref, un_ref, w_ref, b_ref, x0_ref, zb_ref, *, tm):
    i = pl.program_id(0)
    u = u_ref[...].astype(F32)
    row = lax.broadcasted_iota(jnp.int32, u.shape, 0)
    prev_row = jnp.where(i == 0, 0.0, up_ref[...].astype(F32)[HALO_ROWS - 1:HALO_ROWS, :])
    next_row = jnp.where(i == pl.num_programs(0) - 1, 0.0, un_ref[...].astype(F32)[0:1, :])
    u_prev = jnp.where(row == 0, prev_row, pltpu.roll(u, 1, axis=0))
    u_next = jnp.where(row == tm - 1, next_row, pltpu.roll(u, tm - 1, axis=0))
    y = b_ref[...] + u_prev * w_ref[0:1] + u * w_ref[1:2] + u_next * w_ref[2:3]
    x0_ref[...] = y[:, :HY_WIDTH].astype(BF16)
    zb_ref[...] = (y[:, HY_WIDTH:2 * HY_WIDTH] * y[:, 2 * HY_WIDTH:]).astype(BF16)


def _hy_pre(z, conv_w, conv_b, tm):
    s = z.shape[0]
    w = 3 * HY_WIDTH
    nr = s // HALO_ROWS
    ospec = pl.BlockSpec((tm, HY_WIDTH), lambda i: (i, 0))
    return pl.pallas_call(
        functools.partial(_hy_pre_kernel, tm=tm),
        grid=(s // tm,),
        in_specs=[pl.BlockSpec((tm, w), lambda i: (i, 0)),
                  pl.BlockSpec((HALO_ROWS, w), lambda i: (jnp.maximum(i * (tm // HALO_ROWS) - 1, 0), 0)),
                  pl.BlockSpec((HALO_ROWS, w), lambda i: (jnp.minimum((i + 1) * (tm // HALO_ROWS), nr - 1), 0)),
                  pl.BlockSpec((3, w), lambda i: (0, 0)),
                  pl.BlockSpec((1, w), lambda i: (0, 0))],
        out_specs=[ospec, ospec],
        out_shape=[jax.ShapeDtypeStruct((s, HY_WIDTH), BF16)] * 2,
        compiler_params=_cparams(("parallel",)),
        name="hyena_pre",
    )(z, z, z, conv_w, conv_b.reshape(1, w))


def _hy_filter_kernel(emb_ref, w1_ref, b1_ref, f1_ref, w2_ref, b2_ref, f2_ref, w3_ref, dl_ref,
                      kb_ref, norm_ref, *, tl, seq):
    i = pl.program_id(0)
    half = tl // 2
    emb = emb_ref[...]
    x = jnp.concatenate([emb[:half], emb[half:]], axis=1)
    h = jnp.sin(f1_ref[...] * (_dot3(x, w1_ref[...]) + b1_ref[...]))
    h = jnp.sin(f2_ref[...] * (_dot3(h, w2_ref[...]) + b2_ref[...]))
    h = _dot3(h, w3_ref[...])
    back = i * tl >= seq
    pick = lambda y: jnp.where(back, y[:, HY_WIDTH:], y[:, :HY_WIDTH])
    h = jnp.concatenate([pick(h[:, :2 * HY_WIDTH]), pick(h[:, 2 * HY_WIDTH:])], axis=0)
    h = h * jnp.exp(-emb[:, 0:1] * dl_ref[...])
    row = i * tl + lax.broadcasted_iota(jnp.int32, h.shape, 0)
    h = jnp.where(row == seq, 0.0, h)
    kb_ref[...] = h.astype(BF16)

    @pl.when(i == 0)
    def _():
        norm_ref[...] = jnp.zeros(norm_ref.shape, F32)

    norm_ref[...] += jnp.sum(jnp.abs(h), axis=0, keepdims=True)


def _hy_positions(seq):
    t = np.linspace(0.0, 1.0, seq, dtype=np.float32)[:, None]
    bands = (HY_EMB - 1) // 2
    freqs = np.linspace(1e-4, bands - 1, bands, dtype=np.float32)[None]
    w = (np.float32(2.0 * math.pi) * np.arange(seq, dtype=np.float32)[:, None] / np.float32(seq))
    zf = np.concatenate([t, np.cos(freqs * w), -np.sin(freqs * w)], axis=-1).astype(np.float32)
    pos = np.concatenate([np.arange(seq), [0], np.arange(seq - 1, 0, -1)])
    return jnp.asarray(np.pad(zf[pos], ((0, 0), (0, LANES - HY_EMB))))


def _block_diag2(w):
    z = jnp.zeros_like(w)
    return jnp.concatenate([jnp.concatenate([w, z], axis=1), jnp.concatenate([z, w], axis=1)], axis=0)


def _hy_filter(emb, w1, b1, fr1, w2, b2, fr2, w3, tl):
    n = emb.shape[0]
    seq = n // 2
    w1p = jnp.pad(w1, ((0, LANES - HY_EMB), (0, 0)))
    twice = lambda v: jnp.concatenate([v, v])[None]
    min_decay = math.log(HY_DECAY_TARGET) / HY_SLOW_DECAY
    max_decay = math.log(HY_DECAY_TARGET) / HY_FAST_DECAY
    deltas = jnp.asarray(np.abs(np.linspace(min_decay, max_decay, HY_WIDTH, dtype=np.float32))[None])
    cs = lambda r, c: pl.BlockSpec((r, c), lambda i: (0, 0))
    return pl.pallas_call(
        functools.partial(_hy_filter_kernel, tl=tl, seq=seq),
        grid=(n // tl,),
        in_specs=[pl.BlockSpec((tl, LANES), lambda i: (i, 0)),
                  cs(2 * LANES, LANES), cs(1, LANES), cs(1, LANES),
                  cs(LANES, LANES), cs(1, LANES), cs(1, LANES),
                  cs(LANES, 4 * HY_WIDTH), cs(1, HY_WIDTH)],
        out_specs=[pl.BlockSpec((tl, HY_WIDTH), lambda i: (i, 0)), pl.BlockSpec((1, HY_WIDTH), lambda i: (0, 0))],
        out_shape=[jax.ShapeDtypeStruct((n, HY_WIDTH), BF16), jax.ShapeDtypeStruct((1, HY_WIDTH), F32)],
        compiler_params=_cparams(("arbitrary",)),
        name="hyena_filter",
    )(emb, _block_diag2(w1p), twice(b1), twice(fr1), _block_diag2(w2), twice(b2), twice(fr2),
      _block_diag2(w3), deltas)


def _dft_tables(n1):
    n = n1 * DFT_N2
    a = np.arange(n1, dtype=np.int64)
    ang1 = (2.0 * math.pi / n1) * ((a[:, None] * a[None, :]) % n1)
    c1, s1 = np.cos(ang1), np.sin(ang1)
    k1 = np.arange(n1, dtype=np.int64)[:, None, None]
    k2 = np.arange(DFT_N2, dtype=np.int64)[None, :, None]
    n2 = np.arange(DFT_N2, dtype=np.int64)[None, None, :]
    ang = (2.0 * math.pi / n) * ((n2 * (k1 + n1 * k2)) % n)
    gc, gs = np.cos(ang), np.sin(ang)
    tb = lambda x: jnp.asarray(np.ascontiguousarray(x).astype(BF16))
    gc, gs = tb(gc), tb(gs)
    gct, gst = jnp.swapaxes(gc, 1, 2), jnp.swapaxes(gs, 1, 2)
    blk = lambda a, b, c, d: jnp.concatenate([jnp.concatenate([a, b], axis=2), jnp.concatenate([c, d], axis=2)], axis=1)
    return dict(c1=tb(c1), s1=tb(s1), g2=blk(gc, gs, -gs, gc), h2=blk(gct, -gst, gst, gct))


def _dft1_kernel(c_ref, s_ref, x_ref, re_ref, im_ref):
    x = x_ref[...]
    re_ref[...] = _dot(c_ref[...], x).astype(BF16)
    im_ref[...] = (-_dot(s_ref[...], x)).astype(BF16)


def _dft1(x, c1, s1, cb):
    k1, w = x.shape
    n1 = c1.shape[0]
    ospec = pl.BlockSpec((n1, cb), lambda j: (0, j))
    return pl.pallas_call(
        _dft1_kernel,
        grid=(w // cb,),
        in_specs=[pl.BlockSpec((n1, k1), lambda j: (0, 0)), pl.BlockSpec((n1, k1), lambda j: (0, 0)),
                  pl.BlockSpec((k1, cb), lambda j: (0, j))],
        out_specs=[ospec, ospec],
        out_shape=[jax.ShapeDtypeStruct((n1, w), BF16)] * 2,
        compiler_params=_cparams(("parallel",)),
        name="dft_stage1",
    )(c1, s1, x)


def _stack(re, im):
    return jnp.concatenate([re, im], axis=0)


def _conv_mid_kernel(g2_ref, h2_ref, are_ref, aim_ref, fre_ref, fim_ref, norm_ref, bre_ref, bim_ref, *, kb):
    inv = 1.0 / norm_ref[...]
    for t in range(kb):
        g2 = g2_ref[t]
        x = _dot(g2, _stack(are_ref[t], aim_ref[t]))
        xre, xim = x[:DFT_N2], x[DFT_N2:]
        f = _dot(g2, _stack(fre_ref[t], fim_ref[t])) * inv
        kre, kim = f[:DFT_N2], f[DFT_N2:]
        yre = (xre * kre - xim * kim).astype(BF16)
        yim = (xre * kim + xim * kre).astype(BF16)
        b = _dot(h2_ref[t], _stack(yre, yim))
        bre_ref[t] = b[:DFT_N2].astype(BF16)
        bim_ref[t] = b[DFT_N2:].astype(BF16)


def _conv_mid(tabs, are, aim, fre, fim, norm, kb):
    n1, _, c = are.shape
    gspec = pl.BlockSpec((kb, 2 * DFT_N2, 2 * DFT_N2), lambda i: (i, 0, 0))
    aspec = pl.BlockSpec((kb, DFT_N2, c), lambda i: (i, 0, 0))
    return pl.pallas_call(
        functools.partial(_conv_mid_kernel, kb=kb),
        grid=(n1 // kb,),
        in_specs=[gspec, gspec, aspec, aspec, aspec, aspec, pl.BlockSpec((1, c), lambda i: (0, 0))],
        out_specs=[aspec, aspec],
        out_shape=[jax.ShapeDtypeStruct((n1, DFT_N2, c), BF16)] * 2,
        compiler_params=_cparams(("parallel",)),
        name="conv_spectral",
    )(tabs["g2"], tabs["h2"], are, aim, fre, fim, norm)


def _idft1_kernel(c_ref, s_ref, bre_ref, bim_ref, o_ref, *, inv_n):
    y = (_dot(c_ref[...], bre_ref[...]) - _dot(s_ref[...], bim_ref[...])) * inv_n
    o_ref[...] = y.astype(o_ref.dtype)


def _idft1(c1h, s1h, bre, bim, cb):
    ko, n1 = c1h.shape
    w = bre.shape[1]
    n = n1 * DFT_N2
    ospec = pl.BlockSpec((ko, cb), lambda j: (0, j))
    return pl.pallas_call(
        functools.partial(_idft1_kernel, inv_n=1.0 / n),
        grid=(w // cb,),
        in_specs=[pl.BlockSpec((ko, n1), lambda j: (0, 0)), pl.BlockSpec((ko, n1), lambda j: (0, 0)),
                  pl.BlockSpec((n1, cb), lambda j: (0, j)), pl.BlockSpec((n1, cb), lambda j: (0, j))],
        out_specs=ospec,
        out_shape=jax.ShapeDtypeStruct((ko, w), BF16),
        compiler_params=_cparams(("parallel",)),
        name="idft_stage1",
    )(c1h, s1h, bre, bim)


def _hyena(z_all, tabs, emb, p, i, tm, cb, kb):
    seq = z_all.shape[0]
    n1 = 2 * seq // DFT_N2
    c = HY_WIDTH
    x0, zb = _hy_pre(z_all, p["hy_conv_w"][i], p["hy_conv_b"][i], min(tm, seq))
    kern_b, norm = _hy_filter(emb, p["hy_w1"][i], p["hy_b1"][i], p["hy_freq1"][i], p["hy_w2"][i],
                                 p["hy_b2"][i], p["hy_freq2"][i], p["hy_w3"][i], min(1024, seq))
    wide = DFT_N2 * c
    fre, fim = _dft1(kern_b.reshape(n1, wide), tabs["c1"], tabs["s1"], cb)
    are, aim = _dft1(zb.reshape(n1 // 2, wide), tabs["c1"][:, :n1 // 2], tabs["s1"][:, :n1 // 2], cb)
    bre, bim = _conv_mid(tabs, are.reshape(n1, DFT_N2, c), aim.reshape(n1, DFT_N2, c),
                         fre.reshape(n1, DFT_N2, c), fim.reshape(n1, DFT_N2, c), norm, kb)
    y = _idft1(tabs["c1"][:n1 // 2], tabs["s1"][:n1 // 2], bre.reshape(n1, wide), bim.reshape(n1, wide), cb)
    return y.reshape(seq, c), zb, x0


def _combine_kernel(x_ref, ya_ref, o0_ref, o1_ref, o2_ref, l0_ref, l1_ref, l2_ref, yc_ref,
                    hy_ref, hz_ref, hx_ref, skip_ref,
                    g0_ref, g1_ref, g2_ref, g3_ref, pa_ref, pb_ref, pc_ref, pd_ref, wo_ref, o_ref):
    l0, l1, l2 = l0_ref[...], l1_ref[...], l2_ref[...]
    mx = jnp.maximum(jnp.maximum(l0, l1), l2)
    e0, e1, e2 = jnp.exp(l0 - mx), jnp.exp(l1 - mx), jnp.exp(l2 - mx)
    yb = (e0 * o0_ref[...] + e1 * o1_ref[...] + e2 * o2_ref[...]) / (e0 + e1 + e2)
    f32 = lambda ref: ref[...].astype(F32)
    yd = f32(hx_ref) * (f32(hy_ref) + f32(hz_ref) * skip_ref[...])
    gate = lambda ref: _sigmoid(f32(ref))
    m = (gate(g0_ref) * _dot(ya_ref[...].astype(BF16), pa_ref[...])
         + gate(g1_ref) * _dot(yb.astype(BF16), pb_ref[...])
         + gate(g2_ref) * _dot(yc_ref[...].astype(BF16), pc_ref[...])
         + gate(g3_ref) * _dot(yd.astype(BF16), pd_ref[...]))
    o_ref[...] = x_ref[...] + _dot(m.astype(BF16), wo_ref[...])


def _combine(x, z, ya, dil_outs, yc, hyena, skip, pa, pb, pc, pd, wo, tm):
    s, d = x.shape
    rs = lambda w: pl.BlockSpec((tm, w), lambda i: (i, 0))
    gs = lambda b: pl.BlockSpec((tm, d), lambda i: (i, Z_GATE // d + b))
    ws = lambda a: pl.BlockSpec(a.shape, lambda i: (0, 0))
    (o0, l0), (o1, l1), (o2, l2) = dil_outs
    gw = DIL_HEADS * HEAD_DIM
    hy, hz, hx = hyena
    return pl.pallas_call(
        _combine_kernel,
        grid=(s // tm,),
        in_specs=[rs(d), rs(ya.shape[1]), rs(gw), rs(gw), rs(gw), rs(gw), rs(gw), rs(gw),
                  rs(yc.shape[1]), rs(HY_WIDTH), rs(HY_WIDTH), rs(HY_WIDTH), ws(skip), gs(0), gs(1), gs(2), gs(3),
                  ws(pa), ws(pb), ws(pc), ws(pd), ws(wo)],
        out_specs=rs(d),
        out_shape=jax.ShapeDtypeStruct((s, d), F32),
        compiler_params=_cparams(("parallel",)),
        name="combine",
    )(x, ya, o0, o1, o2, l0, l1, l2, yc, hy, hz, hx, skip, z, z, z, z, pa, pb, pc, pd, wo)


def _mlp_kernel(x_ref, g_ref, w1_ref, w2_ref, o_ref, h_ref, acc_ref):
    j = pl.program_id(1)

    @pl.when(j == 0)
    def _():
        h_ref[...] = (_rms(x_ref[...]) * g_ref[...]).astype(BF16)
        acc_ref[...] = jnp.zeros(acc_ref.shape, F32)

    a = jnp.maximum(_dot(h_ref[...], w1_ref[...]), 0.0)
    acc_ref[...] += _dot((a * a).astype(BF16), w2_ref[...])

    @pl.when(j == pl.num_programs(1) - 1)
    def _():
        o_ref[...] = x_ref[...] + acc_ref[...]


def _mlp(x, g, w1, w2, tm, tf):
    s, d = x.shape
    ff = w1.shape[1]
    return pl.pallas_call(
        _mlp_kernel,
        grid=(s // tm, ff // tf),
        in_specs=[pl.BlockSpec((tm, d), lambda i, j: (i, 0)),
                  pl.BlockSpec((1, d), lambda i, j: (0, 0)),
                  pl.BlockSpec((d, tf), lambda i, j: (0, j)),
                  pl.BlockSpec((tf, d), lambda i, j: (j, 0))],
        out_specs=pl.BlockSpec((tm, d), lambda i, j: (i, 0)),
        out_shape=jax.ShapeDtypeStruct((s, d), F32),
        scratch_shapes=[pltpu.VMEM((tm, d), BF16), pltpu.VMEM((tm, d), F32)],
        compiler_params=_cparams(("parallel", "arbitrary")),
        name="mlp",
    )(x, g.reshape(1, d), w1, w2)


def _permute_w_in(w):
    d = w.shape[0]
    pieces = [w[:, _O_DU:_O_GATE], w[:, _O_B:_O_C], w[:, _O_AQ:_O_AK], w[:, _O_C:_O_DU],
              w[:, _O_AV:_O_AR], w[:, _O_AR:_O_ALR], w[:, _O_AK:_O_AV], w[:, _O_ALR:_O_B],
              jnp.zeros((d, Z_GATE - Z_ALR - 2 * GLA_RANK), w.dtype), w[:, _O_GATE:_O_END]]
    return jnp.concatenate(pieces, axis=1).astype(BF16)


def _dil_bias_idx(tq, dil):
    rel = _skewed_rel(tq, tq + 2 * DIL_HALF, -1, -DIL_HALF)
    return _t5_bucket(rel * dil)[None].astype(np.int32)


def _diff_bias_idx(tq, tk):
    far_lo, far_hi = _diff_far_offsets(tq, tk)
    return np.stack([_t5_bucket(_skewed_rel(tk, tq, 1, o * tk)) for o in range(far_lo, far_hi + 1)]).astype(np.int32)


def _forward(x, p, *, t_diff, tk_diff, qc_diff, tq_dil, nt_dil, tb_gla, tm_proj, tn_proj, tm_prep, tm_row, tm_mlp, tf_mlp, tm_hy, cb_dft, kb_dft):
    seq = x.shape[0]
    depth = p["w_in"].shape[0]
    n_dil_bias = DIL_GROUPS * DIL_HEADS
    tq_dil = [min(tq, seq // dil) for tq, (_, dil) in zip(tq_dil, DIL_PATTERNS)]
    dil_bias = [_bias_tiles(p["t5_bias"], _dil_bias_idx(tq, dil), tq, tq + 2 * DIL_HALF, DIL_HEADS, gi * DIL_HEADS)
                for gi, (tq, (_, dil)) in enumerate(zip(tq_dil, DIL_PATTERNS))]
    t_diff, tk_diff = min(t_diff, seq), min(tk_diff, seq)
    diff_bias = _bias_tiles(p["t5_bias"], _diff_bias_idx(t_diff, tk_diff), tk_diff, t_diff, DIFF_HEADS,
                            n_dil_bias, LOG2E)
    tabs = _dft_tables(2 * seq // DFT_N2)
    emb = _hy_positions(seq)
    rep = lambda g, n: jnp.tile(g, n)
    for i in range(depth):
        z = _norm_matmul(x, p["norm1_g"][i], _permute_w_in(p["w_in"][i]), tm_proj, tn_proj)
        wg = [jnp.zeros((LANES, GLA_HEADS * GLA_DK), F32).at[j * GLA_RANK:(j + 1) * GLA_RANK].set(
            p["gla_gate_w"][i, j]).astype(BF16) for j in range(2)]
        gb = p["gla_gate_b"][i]
        o_fwd = _gla_scan(z, wg[0], gb[0:1], tb_gla, False)
        ya = _gla_scan(z, wg[1], gb[1:2], tb_gla, True, fin=(o_fwd, p["gla_norm_g"][i]))
        bq, bk, bv, cq, ck, cv = _prep(
            z, jnp.repeat(p["dil_qnorm_g"][i], DIL_HEADS, axis=0).reshape(-1),
            jnp.repeat(p["dil_knorm_g"][i], DIL_HEADS, axis=0).reshape(-1),
            rep(p["diff_qnorm_g"][i], 2 * DIFF_HEADS), rep(p["diff_knorm_g"][i], 2 * DIFF_HEADS), tm_prep)
        dil_outs = [_dil_attn(bq, bk, bv, dil_bias[gi], gi, dil, tq_dil[gi], max(1, min(nt_dil[gi], seq // (tq_dil[gi] * dil))))
                    for gi, (_, dil) in enumerate(DIL_PATTERNS)]
        lam_init = 0.8 - 0.6 * math.exp(-0.3 * i)
        yc = _diff_attn(cq, ck, cv, diff_bias, p["diff_lambda"][i], p["diff_subln_g"][i], lam_init,
                        t_diff, tk_diff, qc_diff)
        yd = _hyena(z, tabs, emb, p, i, tm_hy, cb_dft, kb_dft)
        x = _combine(x, z, ya, dil_outs, yc, yd, p["hy_skip"][i][None],
                     p["proj_a"][i].astype(BF16), p["proj_b"][i].astype(BF16),
                     p["proj_c"][i].astype(BF16), p["proj_d"][i].astype(BF16), p["w_out"][i].astype(BF16),
                     tm_row)
        x = _mlp(x, p["norm2_g"][i], p["mlp_w1"][i].astype(BF16), p["mlp_w2"][i].astype(BF16), tm_mlp, tf_mlp)
    return x


def kernel(x, t5_bias, norm1_g, w_in, gla_gate_w, gla_gate_b, gla_norm_g, dil_qnorm_g, dil_knorm_g,
           diff_qnorm_g, diff_knorm_g, diff_lambda, diff_subln_g, hy_conv_w, hy_conv_b, hy_w1, hy_b1,
           hy_freq1, hy_w2, hy_b2, hy_freq2, hy_w3, hy_skip, proj_a, proj_b, proj_c, proj_d, w_out,
           norm2_g, mlp_w1, mlp_w2):
    p = dict(t5_bias=t5_bias, norm1_g=norm1_g, w_in=w_in, gla_gate_w=gla_gate_w, gla_gate_b=gla_gate_b,
             gla_norm_g=gla_norm_g, dil_qnorm_g=dil_qnorm_g, dil_knorm_g=dil_knorm_g,
             diff_qnorm_g=diff_qnorm_g, diff_knorm_g=diff_knorm_g, diff_lambda=diff_lambda,
             diff_subln_g=diff_subln_g, hy_conv_w=hy_conv_w, hy_conv_b=hy_conv_b, hy_w1=hy_w1, hy_b1=hy_b1,
             hy_freq1=hy_freq1, hy_w2=hy_w2, hy_b2=hy_b2, hy_freq2=hy_freq2, hy_w3=hy_w3, hy_skip=hy_skip,
             proj_a=proj_a, proj_b=proj_b, proj_c=proj_c, proj_d=proj_d, w_out=w_out, norm2_g=norm2_g,
             mlp_w1=mlp_w1, mlp_w2=mlp_w2)
    b, s, d = x.shape
    tiles = dict(
        t_diff=2048, tk_diff=1024, qc_diff=(256, 768, 768, 256),
        tq_dil=(256, 256, 128), nt_dil=(4, 1, 1),
        tb_gla=512, tm_proj=2048, tn_proj=1024, tm_prep=1024, tm_row=512, tm_mlp=1024, tf_mlp=2048,
        tm_hy=1024, cb_dft=8192, kb_dft=16)
    outs = [_forward(x[bi], p, **tiles) for bi in range(b)]
    return jnp.stack(outs)
```
